```python
import math
import jax
import jax.numpy as jnp
from jax import lax
import numpy as np

D_MODEL = 2048
BATCH = 4
SEQ = 2048
DEPTH = 1
DEC_BATCH = 8
DEC_SEQ = 4
PAST_LEN = 16384
PAGE_SIZE = 128

ML_H = 4
ML_W = D_MODEL // 2
ML_DH = ML_W // ML_H
SA_H = 4
SA_W = D_MODEL // 4
SA_DH = SA_W // SA_H
MEM_H = 4
MEM_W = D_MODEL // 4
MEM_DH = MEM_W // MEM_H
MIX_W = ML_W + SA_W + MEM_W
IDX_H = 8
IDX_D = 64
IDX_SCALE = (IDX_H * IDX_D) ** -0.5
TOPK_MAX = 256
N_MEM = 256
CHUNK = 64
Q_BLOCK = 128
ROPE_THETA = 10000.0
EPS = 1e-6
F32 = jnp.float32

IN_SPLITS = (
    ('ml_q', ML_W), ('ml_k', ML_W), ('ml_v', ML_W), ('ml_o', ML_W), ('ml_z', ML_W),
    ('ml_i', ML_H), ('ml_f', ML_H),
    ('sa_q', SA_W), ('sa_k', SA_W), ('sa_v', SA_W), ('sa_z', SA_W),
    ('idx_q', IDX_H * IDX_D), ('idx_k', IDX_D), ('idx_w', IDX_H),
    ('mem_q', MEM_W), ('mem_z', MEM_W),
)
IN_W = sum(w for _, w in IN_SPLITS)

kernel_name = 'hymba_mlstm_dsa_memory_decode_step'


def rmsnorm(x, g):
    xf = x.astype(F32)
    y = xf * lax.rsqrt(jnp.mean(xf * xf, axis=-1, keepdims=True) + EPS)
    return (y * g.astype(F32)).astype(x.dtype)


def rope(x, pos):
    half = x.shape[-1] // 2
    inv = ROPE_THETA ** (-jnp.arange(half, dtype=F32) / half)
    ang = pos.astype(F32)[:, None] * inv[None, :]
    cos = jnp.cos(ang)[None, :, None, :]
    sin = jnp.sin(ang)[None, :, None, :]
    x1 = x[..., :half].astype(F32)
    x2 = x[..., half:].astype(F32)
    return jnp.concatenate([x1 * cos - x2 * sin, x1 * sin + x2 * cos], axis=-1).astype(x.dtype)


def split_proj(h):
    out = {}
    off = 0
    for name, w in IN_SPLITS:
        out[name] = h[..., off:off + w]
        off += w
    return out


def mix_in(x, g_pre, w_in):
    return split_proj(rmsnorm(x, g_pre) @ w_in)


def mix_out(x, a, b, c, w_out, g_post):
    hcat = jnp.concatenate([a, b, c], axis=-1)
    return x + rmsnorm(hcat @ w_out, g_post)


def mlstm_chunkwise(q, k, v, ig, lf, C0, n0, m0):
    B, L, H, D = q.shape
    c = math.gcd(L, CHUNK)
    nc = L // c

    def to_chunks(a):
        a = a.reshape((B, nc, c) + a.shape[2:])
        a = jnp.moveaxis(a, 3, 2)
        return jnp.moveaxis(a, 1, 0)

    causal = jnp.tril(jnp.ones((c, c), dtype=bool))

    def step(carry, xs):
        C, n, m = carry
        qc, kc, vc, igc, lfc = xs
        b = jnp.cumsum(lfc, axis=-1)
        a = jnp.where(causal, b[..., :, None] - b[..., None, :] + igc[..., None, :], -jnp.inf)
        m_t = jnp.maximum(b + m[..., None], jnp.max(a, axis=-1))
        inter = jnp.exp(b + m[..., None] - m_t)
        dmat = jnp.exp(a - m_t[..., None])
        s = jnp.einsum('bhtd,bhsd->bhts', qc, kc) * dmat
        num = jnp.einsum('bhts,bhsv->bhtv', s, vc) + inter[..., None] * jnp.einsum('bhtk,bhkv->bhtv', qc, C)
        qn = jnp.sum(s, axis=-1) + inter * jnp.einsum('bhtk,bhk->bht', qc, n)
        h = num / jnp.maximum(jnp.abs(qn), jnp.exp(-m_t))[..., None]
        m_new = m_t[..., -1]
        w_end = jnp.exp(b[..., -1:] - b + igc - m_new[..., None])
        decay = jnp.exp(b[..., -1] + m - m_new)
        C_new = decay[..., None, None] * C + jnp.einsum('bhs,bhsk,bhsv->bhkv', w_end, kc, vc)
        n_new = decay[..., None] * n + jnp.einsum('bhs,bhsk->bhk', w_end, kc)
        return (C_new, n_new, m_new), h

    xs = (to_chunks(q), to_chunks(k), to_chunks(v), to_chunks(ig), to_chunks(lf))
    (C, n, m), hs = lax.scan(step, (C0, n0, m0), xs)
    h = jnp.swapaxes(jnp.moveaxis(hs, 0, 1), 2, 3).reshape(B, L, H, D)
    return h, C, n, m


def mlstm_branch(p, b_gates, g_head, C0, n0, m0, dtype):
    B, L, _ = p['ml_q'].shape
    shp = (B, L, ML_H, ML_DH)
    q = p['ml_q'].reshape(shp).astype(F32)
    k = p['ml_k'].reshape(shp).astype(F32) * (ML_DH ** -0.5)
    v = p['ml_v'].reshape(shp).astype(F32)
    bg = b_gates.astype(F32)
    ig = p['ml_i'].astype(F32) + bg[:ML_H]
    lf = jax.nn.log_sigmoid(p['ml_f'].astype(F32) + bg[ML_H:])
    h, C, n, m = mlstm_chunkwise(q, k, v, ig, lf, C0.astype(F32), n0.astype(F32), m0.astype(F32))
    h = h * lax.rsqrt(jnp.mean(h * h, axis=-1, keepdims=True) + EPS)
    h = h.reshape(B, L, ML_W) * g_head.astype(F32)
    out = h * jax.nn.sigmoid(p['ml_o'].astype(F32)) * jax.nn.silu(p['ml_z'].astype(F32))
    return out.astype(dtype), C, n, m


def dsa_project(p, pos):
    B, L, _ = p['sa_q'].shape
    shp = (B, L, SA_H, SA_DH)
    q = rope(p['sa_q'].reshape(shp), pos)
    k = rope(p['sa_k'].reshape(shp), pos)
    v = p['sa_v'].reshape(shp)
    qi = rope(p['idx_q'].reshape(B, L, IDX_H, IDX_D), pos)
    ki = rope(p['idx_k'][:, :, None, :], pos)[:, :, 0, :]
    return q, k, v, qi, p['idx_w'], ki


def select_keys(qi, wi, ki, valid, n_sel):
    dots = jnp.einsum('bqhd,bkd->bqhk', qi.astype(F32), ki.astype(F32))
    sc = jnp.einsum('bqh,bqhk->bqk', wi.astype(F32) * IDX_SCALE, jax.nn.relu(dots))
    sc = jnp.where(valid[None], sc, -jnp.inf)
    top, idx = lax.top_k(sc, n_sel)
    return idx, top > -jnp.inf


def attend_selected(q, ks, vs, ok):
    s = jnp.einsum('bqhd,bqnhd->bqhn', q.astype(F32), ks.astype(F32)) * (SA_DH ** -0.5)
    s = jnp.where(ok[:, :, None, :], s, -jnp.inf)
    a = jax.nn.softmax(s, axis=-1)
    return jnp.einsum('bqhn,bqnhd->bqhd', a, vs.astype(F32))


def gather_rows(a, idx):
    return jax.vmap(lambda ab, ib: ab[ib])(a, idx)


def dsa_prompt(p, dtype):
    B, L, _ = p['sa_q'].shape
    pos = jnp.arange(L, dtype=jnp.int32)
    q, k, v, qi, wi, ki = dsa_project(p, pos)
    n_sel = min(TOPK_MAX, L // 4)
    qb = math.gcd(L, Q_BLOCK)

    def block(s0):
        q_b = lax.dynamic_slice_in_dim(q, s0, qb, axis=1)
        qi_b = lax.dynamic_slice_in_dim(qi, s0, qb, axis=1)
        wi_b = lax.dynamic_slice_in_dim(wi, s0, qb, axis=1)
        valid = pos[None, :] <= (s0 + jnp.arange(qb, dtype=jnp.int32))[:, None]
        idx, ok = select_keys(qi_b, wi_b, ki, valid, n_sel)
        return attend_selected(q_b, gather_rows(k, idx), gather_rows(v, idx), ok)

    o = lax.map(block, jnp.arange(0, L, qb, dtype=jnp.int32))
    o = jnp.moveaxis(o, 0, 1).reshape(B, L, SA_W)
    out = o * jax.nn.silu(p['sa_z'].astype(F32))
    return out.astype(dtype), k, v, ki


def dsa_sample(p, cache_k, cache_v, cache_kidx, page_table, dtype):
    Bd, T, _ = p['sa_q'].shape
    P = page_table.shape[1] * PAGE_SIZE
    pos = P + jnp.arange(T, dtype=jnp.int32)
    q, k, v, qi, wi, ki = dsa_project(p, pos)
    ki_past = cache_kidx[page_table].reshape(Bd, P, IDX_D).astype(ki.dtype)
    ki_all = jnp.concatenate([ki_past, ki], axis=1)
    n_sel = min(TOPK_MAX, (P + T) // 4)
    valid = jnp.arange(P + T, dtype=jnp.int32)[None, :] <= pos[:, None]
    idx, ok = select_keys(qi, wi, ki_all, valid, n_sel)
    in_past = (idx < P)[..., None, None]
    pc = jnp.minimum(idx, P - 1)
    phys = jax.vmap(lambda pt, ib: pt[ib])(page_table, pc // PAGE_SIZE)
    slot = pc % PAGE_SIZE
    jn = jnp.clip(idx - P, 0, T - 1)
    ks = jnp.where(in_past, cache_k[phys, slot].astype(k.dtype), gather_rows(k, jn))
    vs = jnp.where(in_past, cache_v[phys, slot].astype(v.dtype), gather_rows(v, jn))
    o = attend_selected(q, ks, vs, ok).reshape(Bd, T, SA_W)
    out = o * jax.nn.silu(p['sa_z'].astype(F32))
    return out.astype(dtype), k, v, ki


def mem_kv(mem, g_mem, w_mem_k, w_mem_v):
    B, M, _ = mem.shape
    mn = rmsnorm(mem, g_mem)
    return (mn @ w_mem_k).reshape(B, M, MEM_H, MEM_DH), (mn @ w_mem_v).reshape(B, M, MEM_H, MEM_DH)


def mem_branch(p, mk, mv, dtype):
    B, L, _ = p['mem_q'].shape
    q = p['mem_q'].reshape(B, L, MEM_H, MEM_DH).astype(F32)
    s = jnp.einsum('blhd,bmhd->bhlm', q, mk.astype(F32)) * (MEM_DH ** -0.5)
    a = jax.nn.softmax(s, axis=-1)
    o = jnp.einsum('bhlm,bmhd->blhd', a, mv.astype(F32)).reshape(B, L, MEM_W)
    return (o * jax.nn.silu(p['mem_z'].astype(F32))).astype(dtype)


def layer(x_p, x_s, st_C, st_n, st_m, c_k, c_v, c_kidx, c_mk, c_mv, page_table, mem_prompt,
          g_pre, w_in, b_gates, g_head, w_mem_k, w_mem_v, g_mem, w_out, g_post):
    B = x_p.shape[0]
    pp = mix_in(x_p, g_pre, w_in)
    C0 = jnp.zeros((B, ML_H, ML_DH, ML_DH), F32)
    n0 = jnp.zeros((B, ML_H, ML_DH), F32)
    m0 = jnp.zeros((B, ML_H), F32)
    a_p, pC, pn, pm = mlstm_branch(pp, b_gates, g_head, C0, n0, m0, x_p.dtype)
    b_p, pk, pv, pki = dsa_prompt(pp, x_p.dtype)
    pmk, pmv = mem_kv(mem_prompt, g_mem, w_mem_k, w_mem_v)
    c_p = mem_branch(pp, pmk, pmv, x_p.dtype)
    y_p = mix_out(x_p, a_p, b_p, c_p, w_out, g_post)
    ps = mix_in(x_s, g_pre, w_in)
    a_s, sC, sn, sm = mlstm_branch(ps, b_gates, g_head, st_C, st_n, st_m, x_s.dtype)
    b_s, sk, sv, ski = dsa_sample(ps, c_k, c_v, c_kidx, page_table, x_s.dtype)
    c_s = mem_branch(ps, c_mk, c_mv, x_s.dtype)
    y_s = mix_out(x_s, a_s, b_s, c_s, w_out, g_post)
    return y_p, y_s, (pC, pn, pm, pk, pv, pki, pmk, pmv, sC, sn, sm, sk, sv, ski)


def setup_inputs(seed: int = 0) -> dict:
    key = jax.random.key(seed)
    ks = jax.random.split(key, 24)
    n_pages = PAST_LEN // PAGE_SIZE
    n_used = DEC_BATCH * n_pages
    n_pool = n_used + (n_used + 3) // 4

    def nrm(k, shape, s=1.0):
        return s * jax.random.normal(k, shape, F32)

    page_table = jax.random.permutation(ks[10], n_pool)[:n_used].reshape(DEC_BATCH, n_pages).astype(jnp.int32)
    b_gates = jnp.concatenate([nrm(ks[14], (DEPTH, ML_H), 0.1), 3.0 + nrm(ks[15], (DEPTH, ML_H), 0.5)], axis=-1)
    return {
        'x_prompt': nrm(ks[0], (BATCH, SEQ, D_MODEL)),
        'x_sample': nrm(ks[1], (DEC_BATCH, DEC_SEQ, D_MODEL)),
        'state_mlstm_C': nrm(ks[2], (DEPTH, DEC_BATCH, ML_H, ML_DH, ML_DH), 0.05),
        'state_mlstm_n': nrm(ks[3], (DEPTH, DEC_BATCH, ML_H, ML_DH), 0.5),
        'state_mlstm_m': nrm(ks[4], (DEPTH, DEC_BATCH, ML_H)),
        'cache_k': nrm(ks[5], (DEPTH, n_pool, PAGE_SIZE, SA_H, SA_DH)),
        'cache_v': nrm(ks[6], (DEPTH, n_pool, PAGE_SIZE, SA_H, SA_DH)),
        'cache_kidx': nrm(ks[7], (DEPTH, n_pool, PAGE_SIZE, IDX_D)),
        'cache_mem_k': nrm(ks[8], (DEPTH, DEC_BATCH, N_MEM, MEM_H, MEM_DH)),
        'cache_mem_v': nrm(ks[9], (DEPTH, DEC_BATCH, N_MEM, MEM_H, MEM_DH)),
        'page_table': page_table,
        'mem_prompt': nrm(ks[11], (BATCH, N_MEM, D_MODEL)),
        'g_pre': 1.0 + nrm(ks[12], (DEPTH, D_MODEL), 0.05),
        'w_in': nrm(ks[13], (DEPTH, D_MODEL, IN_W), D_MODEL ** -0.5),
        'b_gates': b_gates,
        'g_head': 1.0 + nrm(ks[16], (DEPTH, ML_W), 0.05),
        'w_mem_k': nrm(ks[17], (DEPTH, D_MODEL, MEM_W), D_MODEL ** -0.5),
        'w_mem_v': nrm(ks[18], (DEPTH, D_MODEL, MEM_W), D_MODEL ** -0.5),
        'g_mem': 1.0 + nrm(ks[19], (DEPTH, D_MODEL), 0.05),
        'w_out': nrm(ks[20], (DEPTH, MIX_W, D_MODEL), MIX_W ** -0.5),
        'g_post': 1.0 + nrm(ks[21], (DEPTH, D_MODEL), 0.05),
    }


def reference(x_prompt, x_sample, state_mlstm_C, state_mlstm_n, state_mlstm_m, cache_k, cache_v,
              cache_kidx, cache_mem_k, cache_mem_v, page_table, mem_prompt, g_pre, w_in, b_gates,
              g_head, w_mem_k, w_mem_v, g_mem, w_out, g_post):
    xp, xs = x_prompt, x_sample
    per_layer = []
    for l in range(DEPTH):
        xp, xs, new = layer(xp, xs, state_mlstm_C[l], state_mlstm_n[l], state_mlstm_m[l],
                            cache_k[l], cache_v[l], cache_kidx[l], cache_mem_k[l], cache_mem_v[l],
                            page_table, mem_prompt, g_pre[l], w_in[l], b_gates[l], g_head[l],
                            w_mem_k[l], w_mem_v[l], g_mem[l], w_out[l], g_post[l])
        per_layer.append(new)
    (p_C, p_n, p_m, p_k, p_v, p_kidx, p_mem_k, p_mem_v,
     s_C, s_n, s_m, s_k, s_v, s_kidx) = [jnp.stack(a) for a in zip(*per_layer)]
    return (xp, xs, p_C, p_n, p_m, p_k, p_v, p_kidx, p_mem_k, p_mem_v, s_C, s_n, s_m, s_k, s_v, s_kidx)
```

```python
import functools

import jax
import jax.numpy as jnp
from jax import lax
from jax.experimental import pallas as pl
from jax.experimental.pallas import tpu as pltpu

F32 = jnp.float32
BF16 = jnp.bfloat16

D_MODEL = 2048
ML_H = 4
ML_W = D_MODEL // 2
ML_DH = ML_W // ML_H
SA_H = 4
SA_W = D_MODEL // 4
SA_DH = SA_W // SA_H
MEM_H = 4
MEM_W = D_MODEL // 4
MEM_DH = MEM_W // MEM_H
IDX_H = 8
IDX_D = 64
IDX_SCALE = (IDX_H * IDX_D) ** -0.5
TOPK_MAX = 256
ROPE_THETA = 10000.0
EPS = 1e-6
PAGE_SIZE = 128

LANES = 128
CB = 512
N_MAIN_BLOCKS = 17
SMALL_IG = 64
SMALL_LF = 68
SMALL_W = 72
SAMPLE_ROWS = 16
VMEM_LIMIT = 56 * 1024 * 1024
NEG_INF = float("-inf")
POS_INF = float("inf")

_NT = (((1,), (1,)), ((), ()))
_TN = (((0,), (0,)), ((), ()))


def _cparams(sem):
    return pltpu.CompilerParams(dimension_semantics=sem, vmem_limit_bytes=VMEM_LIMIT)


def _sigmoid(x):
    return 1.0 / (1.0 + jnp.exp(-x))


def _silu(x):
    return x * _sigmoid(x)


def _log_sigmoid(x):
    return jnp.minimum(x, 0.0) - jnp.log1p(jnp.exp(-jnp.abs(x)))


def _split3(x):
    hi = x.astype(BF16)
    r = x - hi.astype(F32)
    mid = r.astype(BF16)
    lo = (r - mid.astype(F32)).astype(BF16)
    return hi, mid, lo


def _rope128(x, cos, sin_signed):
    return x * cos + pltpu.roll(x, 64, 1) * sin_signed


def _rope64(x, cos, sin_signed):
    lane = lax.broadcasted_iota(jnp.int32, x.shape, 1)
    first_half = (lane % 64) < 32
    partner = jnp.where(first_half, pltpu.roll(x, 96, 1), pltpu.roll(x, 32, 1))
    return x * cos + partner * sin_signed


def _proj_kernel(x_ref, g_ref, w_ref, ws_ref, wst_ref, c128_ref, s128_ref, c64_ref, s64_ref,
                 qkv_ref, oz_ref, saq_ref, k_ref, v_ref, saz_ref, memq_ref, memz_ref, idxq_ref,
                 kidx_ref, small_ref, smallt_ref, u_ref):
    j = pl.program_id(1)

    @pl.when(j == 0)
    def _():
        x = x_ref[...]
        u = x * lax.rsqrt(jnp.mean(x * x, axis=-1, keepdims=True) + EPS) * g_ref[...]
        ub = u.astype(BF16)
        u_ref[...] = ub
        sm = jnp.dot(ub, ws_ref[...], preferred_element_type=F32)
        small_ref[...] = sm
        kidx_ref[...] = _rope64(sm, c64_ref[...], s64_ref[...])[:, :IDX_D]
        smallt_ref[...] = lax.dot_general(wst_ref[...], ub, _NT, preferred_element_type=F32)

    acc = jnp.dot(u_ref[...], w_ref[...], preferred_element_type=F32)

    def rope_heads(fn, cos, sin):
        return jnp.concatenate(
            [fn(acc[:, h * LANES:(h + 1) * LANES], cos, sin) for h in range(CB // LANES)], axis=1)

    @pl.when(j < 6)
    def _():
        qkv_ref[...] = acc.astype(BF16)

    @pl.when((j >= 6) & (j < 10))
    def _():
        oz_ref[...] = acc

    @pl.when(j == 10)
    def _():
        saq_ref[...] = (rope_heads(_rope128, c128_ref[...], s128_ref[...]) * (SA_DH ** -0.5)).astype(BF16)

    @pl.when(j == 11)
    def _():
        k_ref[...] = rope_heads(_rope128, c128_ref[...], s128_ref[...])

    @pl.when(j == 12)
    def _():
        v_ref[...] = acc

    @pl.when(j == 13)
    def _():
        saz_ref[...] = acc

    @pl.when(j == 14)
    def _():
        memq_ref[...] = (acc * (MEM_DH ** -0.5)).astype(BF16)

    @pl.when(j == 15)
    def _():
        memz_ref[...] = acc

    @pl.when(j == 16)
    def _():
        idxq_ref[...] = rope_heads(_rope64, c64_ref[...], s64_ref[...]).astype(BF16)


def _project(x2d, g_pre, w_main, w_small, w_small_t, tabs, tm):
    rows = x2d.shape[0]
    c128, s128, c64, s64 = tabs
    ntab = c128.shape[0] // tm
    nb = N_MAIN_BLOCKS

    def span(a, b):
        return lambda i, j: (i, jnp.clip(j - a, 0, b - a - 1))

    row_only = lambda i, j: (i, 0)
    tab_map = lambda i, j: (i % ntab, 0)
    const = lambda i, j: (0, 0)
    out_shape = (
        jax.ShapeDtypeStruct((rows, 3 * ML_W), BF16),
        jax.ShapeDtypeStruct((rows, 2 * ML_W), F32),
        jax.ShapeDtypeStruct((rows, SA_W), BF16),
        jax.ShapeDtypeStruct((rows, SA_W), F32),
        jax.ShapeDtypeStruct((rows, SA_W), F32),
        jax.ShapeDtypeStruct((rows, SA_W), F32),
        jax.ShapeDtypeStruct((rows, MEM_W), BF16),
        jax.ShapeDtypeStruct((rows, MEM_W), F32),
        jax.ShapeDtypeStruct((rows, IDX_H * IDX_D), BF16),
        jax.ShapeDtypeStruct((rows, IDX_D), F32),
        jax.ShapeDtypeStruct((rows, LANES), F32),
        jax.ShapeDtypeStruct((LANES, rows), F32),
    )
    out_specs = (
        pl.BlockSpec((tm, CB), span(0, 6)),
        pl.BlockSpec((tm, CB), span(6, 10)),
        pl.BlockSpec((tm, CB), row_only),
        pl.BlockSpec((tm, CB), row_only),
        pl.BlockSpec((tm, CB), row_only),
        pl.BlockSpec((tm, CB), row_only),
        pl.BlockSpec((tm, CB), row_only),
        pl.BlockSpec((tm, CB), row_only),
        pl.BlockSpec((tm, CB), row_only),
        pl.BlockSpec((tm, IDX_D), row_only),
        pl.BlockSpec((tm, LANES), row_only),
        pl.BlockSpec((LANES, tm), lambda i, j: (0, i)),
    )
    in_specs = [
        pl.BlockSpec((tm, D_MODEL), row_only),
        pl.BlockSpec((1, D_MODEL), const),
        pl.BlockSpec((D_MODEL, CB), lambda i, j: (0, j)),
        pl.BlockSpec((D_MODEL, LANES), const),
        pl.BlockSpec((LANES, D_MODEL), const),
        pl.BlockSpec((tm, LANES), tab_map),
        pl.BlockSpec((tm, LANES), tab_map),
        pl.BlockSpec((tm, LANES), tab_map),
        pl.BlockSpec((tm, LANES), tab_map),
    ]
    return pl.pallas_call(
        _proj_kernel,
        out_shape=out_shape,
        grid=(rows // tm, nb),
        in_specs=in_specs,
        out_specs=out_specs,
        scratch_shapes=[pltpu.VMEM((tm, D_MODEL), BF16)],
        compiler_params=_cparams(("arbitrary", "arbitrary")),
        name="proj",
    )(x2d, g_pre.reshape(1, D_MODEL), w_main, w_small, w_small_t, c128, s128, c64, s64)


def _rope_tables(pos):
    def tab(half):
        inv = ROPE_THETA ** (-jnp.arange(half, dtype=F32) / half)
        ang = pos.astype(F32)[:, None] * inv[None, :]
        return jnp.cos(ang), jnp.sin(ang)

    c, s = tab(SA_DH // 2)
    c128 = jnp.concatenate([c, c], axis=1)
    s128 = jnp.concatenate([-s, s], axis=1)
    c, s = tab(IDX_D // 2)
    c64 = jnp.concatenate([c, c, c, c], axis=1)
    s64 = jnp.concatenate([-s, s, -s, s], axis=1)
    return c128, s128, c64, s64


def _relayout_w_in(w_in):
    off = {}
    o = 0
    for name, w in (('ml_q', ML_W), ('ml_k', ML_W), ('ml_v', ML_W), ('ml_o', ML_W), ('ml_z', ML_W),
                    ('ml_i', ML_H), ('ml_f', ML_H), ('sa_q', SA_W), ('sa_k', SA_W), ('sa_v', SA_W),
                    ('sa_z', SA_W), ('idx_q', IDX_H * IDX_D), ('idx_k', IDX_D), ('idx_w', IDX_H),
                    ('mem_q', MEM_W), ('mem_z', MEM_W)):
        off[name] = (o, w)
        o += w

    def col(name):
        a, w = off[name]
        return w_in[:, a:a + w]

    main = jnp.concatenate([
        col('ml_q'), col('ml_k') * (ML_DH ** -0.5), col('ml_v'), col('ml_o'), col('ml_z'),
        col('sa_q'), col('sa_k'), col('sa_v'), col('sa_z'), col('mem_q'), col('mem_z'), col('idx_q'),
    ], axis=1).astype(BF16)
    small = jnp.concatenate([
        col('idx_k'), col('ml_i'), col('ml_f'), col('idx_w'),
        jnp.zeros((D_MODEL, LANES - IDX_D - 2 * ML_H - IDX_H), F32)], axis=1).astype(BF16)
    return main, small, small.T


def _memkv_kernel(m_ref, g_ref, wk_ref, wv_ref, k_ref, v_ref):
    x = m_ref[...]
    u = (x * lax.rsqrt(jnp.mean(x * x, axis=-1, keepdims=True) + EPS) * g_ref[...]).astype(BF16)
    k_ref[...] = jnp.dot(u, wk_ref[...], preferred_element_type=F32)
    v_ref[...] = jnp.dot(u, wv_ref[...], preferred_element_type=F32)


def _memkv(mem2d, g_mem, wk, wv, n_mem):
    rows = mem2d.shape[0]
    row = lambda i: (i, 0)
    const = lambda i: (0, 0)
    return pl.pallas_call(
        _memkv_kernel,
        out_shape=(jax.ShapeDtypeStruct((rows, MEM_W), F32), jax.ShapeDtypeStruct((rows, MEM_W), F32)),
        grid=(rows // n_mem,),
        in_specs=[pl.BlockSpec((n_mem, D_MODEL), row), pl.BlockSpec((1, D_MODEL), const),
                  pl.BlockSpec((D_MODEL, MEM_W), const), pl.BlockSpec((D_MODEL, MEM_W), const)],
        out_specs=(pl.BlockSpec((n_mem, MEM_W), row), pl.BlockSpec((n_mem, MEM_W), row)),
        compiler_params=_cparams(("arbitrary",)),
        name="memkv",
    )(mem2d, g_mem.reshape(1, D_MODEL), wk.astype(BF16), wv.astype(BF16))


def _mlstm_kernel(qkv_ref, oz_ref, gc_ref, gt_ref, bcol_ref, brow_ref, gh_ref, c0_ref, n0_ref, m0_ref,
                  a_ref, cout_ref, nout_ref, mout_ref, c_s, n_s, m_s, *, c, n_pad):
    ci = pl.program_id(1)

    @pl.when(ci == 0)
    def _():
        c_s[...] = c0_ref[0]
        n_s[...] = n0_ref[0]
        m_s[...] = m0_ref[0]

    ri = lax.broadcasted_iota(jnp.int32, (c, c), 0)
    cj = lax.broadcasted_iota(jnp.int32, (c, c), 1)
    causal = cj <= ri
    tri = jnp.where(causal, 1.0, 0.0).astype(BF16)
    tri_t = jnp.where(ri <= cj, 1.0, 0.0).astype(BF16)

    g_c = gc_ref[...] + bcol_ref[...]
    pad_c = lax.broadcasted_iota(jnp.int32, (c, LANES), 0) < n_pad
    ig_c = jnp.where(pad_c, NEG_INF, g_c)
    lf_c = jnp.where(pad_c, 0.0, _log_sigmoid(g_c))
    b_c = sum(jnp.dot(tri, p, preferred_element_type=F32) for p in _split3(lf_c))
    g_r = gt_ref[0] + brow_ref[...]
    pad_r = lax.broadcasted_iota(jnp.int32, (SAMPLE_ROWS, c), 1) < n_pad
    ig_r = jnp.where(pad_r, NEG_INF, g_r)
    lf_r = jnp.where(pad_r, 0.0, _log_sigmoid(g_r))
    b_r = sum(jnp.dot(p, tri_t, preferred_element_type=F32) for p in _split3(lf_r))

    for h in range(ML_H):
        hs = slice(h * ML_DH, (h + 1) * ML_DH)
        m_prev = m_s[h:h + 1, 0:1]
        b_t = b_c[:, SMALL_LF + h:SMALL_LF + h + 1]
        igc = ig_c[:, SMALL_IG + h:SMALL_IG + h + 1]
        b_s = b_r[ML_H + h:ML_H + h + 1, :]
        igr = ig_r[h:h + 1, :]
        a = jnp.where(causal, b_t - b_s + igr, NEG_INF)
        bm = b_t + m_prev
        m_t = jnp.maximum(bm, jnp.max(a, axis=1, keepdims=True))
        inter = jnp.exp(bm - m_t)
        dmat = jnp.exp(a - m_t)
        q = qkv_ref[:, h * ML_DH:(h + 1) * ML_DH]
        k = qkv_ref[:, ML_W + h * ML_DH:ML_W + (h + 1) * ML_DH]
        v = qkv_ref[:, 2 * ML_W + h * ML_DH:2 * ML_W + (h + 1) * ML_DH]
        s = lax.dot_general(q, k, _NT, preferred_element_type=F32) * dmat
        c_h = c_s[h]
        n_h = n_s[h:h + 1, :]
        num = (jnp.dot(s.astype(BF16), v, preferred_element_type=F32)
               + inter * jnp.dot(q, c_h.astype(BF16), preferred_element_type=F32))
        qn = (jnp.sum(s, axis=1, keepdims=True)
              + inter * jnp.sum(q.astype(F32) * n_h, axis=1, keepdims=True))
        hh = num / jnp.maximum(jnp.abs(qn), jnp.exp(-m_t))
        hh = hh * lax.rsqrt(jnp.mean(hh * hh, axis=1, keepdims=True) + EPS)
        o = oz_ref[:, h * ML_DH:(h + 1) * ML_DH]
        z = oz_ref[:, ML_W + h * ML_DH:ML_W + (h + 1) * ML_DH]
        a_ref[:, hs] = (hh * gh_ref[:, hs] * _sigmoid(o) * _silu(z)).astype(BF16)

        m_new = m_t[c - 1:c, :]
        b_last = b_t[c - 1:c, :]
        w_end = jnp.exp(b_last - b_t + igc - m_new)
        decay = jnp.exp(b_last + m_prev - m_new)
        kw = k.astype(F32) * w_end
        c_s[h] = decay * c_h + lax.dot_general(kw.astype(BF16), v, _TN, preferred_element_type=F32)
        n_s[h:h + 1, :] = decay * n_h + jnp.sum(kw, axis=0, keepdims=True)
        m_s[h:h + 1, :] = jnp.broadcast_to(m_new, (1, LANES))

    @pl.when(ci == pl.num_programs(1) - 1)
    def _():
        cout_ref[0] = c_s[...]
        nout_ref[0] = n_s[...]
        mout_ref[0] = m_s[...]


def _mlstm(qkv, oz, small, small_t, b_gates, g_head, c0, n0, m0, nb, c, n_pad):
    rows = qkv.shape[0]
    nc = rows // (nb * c)
    bias_col = jnp.zeros((1, LANES), F32).at[0, SMALL_IG:SMALL_IG + 2 * ML_H].set(b_gates)
    bias_row = jnp.zeros((SAMPLE_ROWS, 1), F32).at[:2 * ML_H, 0].set(b_gates)
    m0b = jnp.zeros((nb, 8, LANES), F32).at[:, :ML_H, :].set(jnp.broadcast_to(m0[:, :, None], (nb, ML_H, LANES)))
    rowblk = lambda b, i: (b * nc + i, 0)
    const = lambda b, i: (0, 0)
    gates_t = small_t[SMALL_IG:SMALL_IG + SAMPLE_ROWS].reshape(SAMPLE_ROWS, rows // c, c).transpose(1, 0, 2)
    out_shape = (
        jax.ShapeDtypeStruct((rows, ML_W), BF16),
        jax.ShapeDtypeStruct((nb, ML_H, ML_DH, ML_DH), F32),
        jax.ShapeDtypeStruct((nb, ML_H, ML_DH), F32),
        jax.ShapeDtypeStruct((nb, 8, LANES), F32),
    )
    st4 = lambda b, i: (b, 0, 0, 0)
    st3 = lambda b, i: (b, 0, 0)
    a, c_out, n_out, m_out = pl.pallas_call(
        functools.partial(_mlstm_kernel, c=c, n_pad=n_pad),
        out_shape=out_shape,
        grid=(nb, nc),
        in_specs=[
            pl.BlockSpec((c, 3 * ML_W), rowblk),
            pl.BlockSpec((c, 2 * ML_W), rowblk),
            pl.BlockSpec((c, LANES), rowblk),
            pl.BlockSpec((1, SAMPLE_ROWS, c), lambda b, i: (b * nc + i, 0, 0)),
            pl.BlockSpec((1, LANES), const),
            pl.BlockSpec((SAMPLE_ROWS, 1), const),
            pl.BlockSpec((1, ML_W), const),
            pl.BlockSpec((1, ML_H, ML_DH, ML_DH), st4),
            pl.BlockSpec((1, ML_H, ML_DH), st3),
            pl.BlockSpec((1, 8, LANES), st3),
        ],
        out_specs=(
            pl.BlockSpec((c, ML_W), rowblk),
            pl.BlockSpec((1, ML_H, ML_DH, ML_DH), st4),
            pl.BlockSpec((1, ML_H, ML_DH), st3),
            pl.BlockSpec((1, 8, LANES), st3),
        ),
        scratch_shapes=[pltpu.VMEM((ML_H, ML_DH, ML_DH), F32), pltpu.VMEM((ML_H, ML_DH), F32),
                        pltpu.VMEM((8, LANES), F32)],
        compiler_params=_cparams(("arbitrary", "arbitrary")),
        name="mlstm",
    )(qkv, oz, small, gates_t, bias_col, bias_row, g_head.reshape(1, ML_W), c0, n0, m0b)
    return a, c_out, n_out, m_out[:, :ML_H, 0]


def _count(pred, axis):
    return jnp.sum(jnp.where(pred, 1.0, 0.0), axis=axis, keepdims=True)


def _kth_largest(x_ref, k, axis, n_bisect):
    kf = float(k)
    x = x_ref[...]
    hi = jnp.max(x, axis=axis, keepdims=True)
    lo = jnp.min(jnp.where(x == NEG_INF, POS_INF, x), axis=axis, keepdims=True)

    def bisect(_, carry):
        lo, hi = carry
        mid = 0.5 * (lo + hi)
        ge = _count(x_ref[...] >= mid, axis) >= kf
        return jnp.where(ge, mid, lo), jnp.where(ge, hi, mid)

    lo, hi = lax.fori_loop(0, n_bisect, bisect, (lo, hi))

    def cond(st):
        return st[4] < 0.5

    def body(st):
        lo, strict, thr, done, _ = st
        xx = x_ref[...]
        above = jnp.min(jnp.where(xx > lo, xx, POS_INF), axis=axis, keepdims=True)
        at_lo = (strict < 0.5) & (_count(xx == lo, axis) > 0.5)
        cmin = jnp.where(at_lo, lo, above)
        fin = (_count(xx > cmin, axis) < kf) | (cmin == POS_INF)
        active = done < 0.5
        thr = jnp.where(active, cmin, thr)
        lo = jnp.where(active, cmin, lo)
        done = jnp.where(fin, 1.0, done)
        return lo, jnp.ones_like(strict), thr, done, jnp.min(done)

    zeros = jnp.zeros_like(lo)
    _, _, thr, _, _ = lax.while_loop(cond, body, (lo, zeros, lo, zeros, jnp.float32(0.0)))
    return thr


def _dsa_kernel(q_ref, z_ref, idxq_ref, wt_ref, k_ref, v_ref, kidx_ref, o_ref,
                kb_s, vt_s, kib_s, x_s, sel_s, *, n_keys, qb, n_sel, n_bisect):
    j = pl.program_id(1)

    @pl.when(j == 0)
    def _():
        kb_s[...] = k_ref[...].astype(BF16)
        vt_s[...] = v_ref[...].T.astype(BF16)
        kib_s[...] = kidx_ref[...].astype(BF16)

    key = lax.broadcasted_iota(jnp.int32, (n_keys, qb), 0)
    qpos = j * qb + lax.broadcasted_iota(jnp.int32, (n_keys, qb), 1)
    valid = key <= qpos
    sc = jnp.zeros((n_keys, qb), F32)
    for h in range(IDX_H):
        d = lax.dot_general(kib_s[...], idxq_ref[:, h * IDX_D:(h + 1) * IDX_D], _NT,
                            preferred_element_type=F32)
        sc = sc + jnp.maximum(d, 0.0) * (wt_ref[h:h + 1, :] * IDX_SCALE)
    x_s[...] = jnp.where(valid, sc, NEG_INF)
    sel_s[...] = jnp.where(valid, 1.0, 0.0)

    @pl.when((j + 1) * qb > n_sel)
    def _():
        kf = float(n_sel)
        thr = _kth_largest(x_s, n_sel, 0, n_bisect)
        x = x_s[...]
        need = kf - _count(x > thr, 0)
        n_tie = _count(x == thr, 0)
        qrow = j * qb + lax.broadcasted_iota(jnp.int32, (1, qb), 1)
        small = (qrow + 1) <= n_sel
        sel_s[...] = jnp.where(small, jnp.where(valid, 1.0, 0.0), jnp.where(x >= thr, 1.0, 0.0))
        excess = jnp.max(jnp.where((n_tie > need) & jnp.logical_not(small), 1.0, 0.0))

        @pl.when(excess > 0.5)
        def _():
            tb = 256
            r_i = lax.broadcasted_iota(jnp.int32, (tb, tb), 0)
            c_i = lax.broadcasted_iota(jnp.int32, (tb, tb), 1)
            lower = jnp.where(c_i < r_i, 1.0, 0.0).astype(BF16)
            carry = jnp.zeros((1, qb), F32)
            for blk in range(n_keys // tb):
                rows = slice(blk * tb, (blk + 1) * tb)
                xb = x_s[rows, :]
                tie = jnp.where(xb == thr, 1.0, 0.0)
                rank = jnp.dot(lower, tie.astype(BF16), preferred_element_type=F32) + carry
                keep = (xb > thr) | ((xb == thr) & (rank < need))
                keyb = blk * tb + lax.broadcasted_iota(jnp.int32, (tb, qb), 0)
                qposb = j * qb + lax.broadcasted_iota(jnp.int32, (tb, qb), 1)
                smallb = jnp.where(keyb <= qposb, 1.0, 0.0)
                sel_s[rows, :] = jnp.where(small, smallb, jnp.where(keep, 1.0, 0.0))
                carry = carry + jnp.sum(tie, axis=0, keepdims=True)

    sel = sel_s[...] > 0.5
    for h in range(SA_H):
        hs = slice(h * SA_DH, (h + 1) * SA_DH)
        st = lax.dot_general(kb_s[:, hs], q_ref[:, hs], _NT, preferred_element_type=F32)
        st = jnp.where(sel, st, NEG_INF)
        mx = jnp.max(st, axis=0, keepdims=True)
        p = jnp.exp(st - mx)
        l = jnp.sum(p, axis=0, keepdims=True)
        ot = jnp.dot(vt_s[hs, :], p.astype(BF16), preferred_element_type=F32) / l
        o_ref[:, hs] = (ot.T * _silu(z_ref[:, hs])).astype(BF16)


def _dsa_prompt(saq, saz, idxq, small_t, k, v, kidx, nb, seq):
    rows = saq.shape[0]
    qb = min(seq, 128)
    nq = seq // qb
    n_sel = min(TOPK_MAX, seq // 4)
    qblk = lambda b, j: (b * nq + j, 0)
    per_b = lambda b, j: (b, 0)
    wt_blk = SMALL_W // 8
    return pl.pallas_call(
        functools.partial(_dsa_kernel, n_keys=seq, qb=qb, n_sel=n_sel, n_bisect=20),
        out_shape=jax.ShapeDtypeStruct((rows, SA_W), BF16),
        grid=(nb, nq),
        in_specs=[
            pl.BlockSpec((qb, SA_W), qblk),
            pl.BlockSpec((qb, SA_W), qblk),
            pl.BlockSpec((qb, IDX_H * IDX_D), qblk),
            pl.BlockSpec((8, qb), lambda b, j: (wt_blk, b * nq + j)),
            pl.BlockSpec((seq, SA_W), per_b),
            pl.BlockSpec((seq, SA_W), per_b),
            pl.BlockSpec((seq, IDX_D), per_b),
        ],
        out_specs=pl.BlockSpec((qb, SA_W), qblk),
        scratch_shapes=[pltpu.VMEM((seq, SA_W), BF16), pltpu.VMEM((SA_W, seq), BF16),
                        pltpu.VMEM((seq, IDX_D), BF16), pltpu.VMEM((seq, qb), F32),
                        pltpu.VMEM((seq, qb), F32)],
        compiler_params=_cparams(("arbitrary", "arbitrary")),
        name="dsa",
    )(saq, saz, idxq, small_t, k, v, kidx)


def _sel_kernel(pt_ref, idxq_ref, small_ref, kinew_ref, *rest, npg, n_past, n_sel, n_real, n_bisect):
    page_refs = rest[:npg]
    sel_ref = rest[npg]
    x_s = rest[npg + 1]
    g = pl.program_id(1)
    rws = SAMPLE_ROWS
    pk = n_past + LANES
    gk = npg * PAGE_SIZE

    qs = jnp.concatenate([idxq_ref[:, h * IDX_D:(h + 1) * IDX_D] for h in range(IDX_H)], axis=0)
    small = small_ref[...]

    def scores(keys_bf16):
        d = lax.dot_general(qs, keys_bf16, _NT, preferred_element_type=F32)
        sc = jnp.zeros((rws, keys_bf16.shape[0]), F32)
        for h in range(IDX_H):
            w = small[:, SMALL_W + h:SMALL_W + h + 1] * IDX_SCALE
            sc = sc + jnp.maximum(d[h * rws:(h + 1) * rws, :], 0.0) * w
        return sc

    kp = jnp.concatenate([r[0] for r in page_refs], axis=0).astype(BF16)
    x_s[:, pl.ds(pl.multiple_of(g * gk, LANES), gk)] = scores(kp)

    @pl.when(g == pl.num_programs(1) - 1)
    def _():
        knew = jnp.concatenate([kinew_ref[...], jnp.zeros((LANES - rws, IDX_D), F32)], axis=0).astype(BF16)
        row = lax.broadcasted_iota(jnp.int32, (rws, LANES), 0)
        col = lax.broadcasted_iota(jnp.int32, (rws, LANES), 1)
        ok = (col >= rws - n_real) & (col < rws) & (col <= row)
        x_s[:, n_past:pk] = jnp.where(ok, scores(knew), NEG_INF)

        kf = float(n_sel)
        thr = _kth_largest(x_s, n_sel, 1, n_bisect)
        x = x_s[...]
        need = kf - _count(x > thr, 1)
        n_tie = _count(x == thr, 1)
        sel_ref[0] = jnp.where(x >= thr, 1.0, 0.0)
        real = lax.broadcasted_iota(jnp.int32, (rws, 1), 0) >= rws - n_real
        excess = jnp.max(jnp.where((n_tie > need) & real, 1.0, 0.0))

        @pl.when(excess > 0.5)
        def _():
            r_i = lax.broadcasted_iota(jnp.int32, (LANES, LANES), 0)
            c_i = lax.broadcasted_iota(jnp.int32, (LANES, LANES), 1)
            upper = jnp.where(r_i < c_i, 1.0, 0.0).astype(BF16)

            def blk(i, carry):
                cols = pl.ds(pl.multiple_of(i * LANES, LANES), LANES)
                xb = x_s[:, cols]
                tie = jnp.where(xb == thr, 1.0, 0.0)
                rank = jnp.dot(tie.astype(BF16), upper, preferred_element_type=F32) + carry
                keep = (xb > thr) | ((xb == thr) & (rank < need))
                sel_ref[0, :, cols] = jnp.where(keep, 1.0, 0.0)
                return carry + jnp.sum(tie, axis=1, keepdims=True)

            lax.fori_loop(0, pk // LANES, blk, jnp.zeros((rws, 1), F32))


def _dsa_select(page_table, idxq, small, kidx_new, cache_kidx, n_sel, n_real):
    nreq, n_pages = page_table.shape
    n_past = n_pages * PAGE_SIZE
    pk = n_past + LANES
    npg = min(16, n_pages)
    req = lambda b, g, pt: (b, 0)

    def page_map(i):
        return lambda b, g, pt: (pt[b, g * npg + i], 0, 0)

    grid_spec = pltpu.PrefetchScalarGridSpec(
        num_scalar_prefetch=1,
        grid=(nreq, n_pages // npg),
        in_specs=[pl.BlockSpec((SAMPLE_ROWS, IDX_H * IDX_D), req),
                  pl.BlockSpec((SAMPLE_ROWS, LANES), req),
                  pl.BlockSpec((SAMPLE_ROWS, IDX_D), req)]
                 + [pl.BlockSpec((1, PAGE_SIZE, IDX_D), page_map(i)) for i in range(npg)],
        out_specs=pl.BlockSpec((1, SAMPLE_ROWS, pk), lambda b, g, pt: (b, 0, 0)),
        scratch_shapes=[pltpu.VMEM((SAMPLE_ROWS, pk), F32)],
    )
    return pl.pallas_call(
        functools.partial(_sel_kernel, npg=npg, n_past=n_past, n_sel=n_sel, n_real=n_real, n_bisect=20),
        out_shape=jax.ShapeDtypeStruct((nreq, SAMPLE_ROWS, pk), F32),
        grid_spec=grid_spec,
        compiler_params=_cparams(("arbitrary", "arbitrary")),
        name="dsa_sel",
    )(page_table, idxq, small, kidx_new, *([cache_kidx] * npg))


def _att_kernel(pt_ref, q_ref, z_ref, knew_ref, vnew_ref, sel_ref, seltail_ref, *rest, npg):
    k_refs = rest[:npg]
    v_refs = rest[npg:2 * npg]
    o_ref = rest[2 * npg]
    m_s, l_s, acc_s = rest[2 * npg + 1:]
    g = pl.program_id(1)
    rws = SAMPLE_ROWS
    floor = -1e30

    @pl.when(g == 0)
    def _():
        m_s[...] = jnp.full(m_s.shape, floor, F32)
        l_s[...] = jnp.zeros(l_s.shape, F32)
        acc_s[...] = jnp.zeros(acc_s.shape, F32)

    def update(kb, vb, sel):
        for h in range(SA_H):
            hs = slice(h * SA_DH, (h + 1) * SA_DH)
            s = lax.dot_general(q_ref[:, hs], kb[:, hs], _NT, preferred_element_type=F32)
            m_old = m_s[h][:, 0:1]
            m_new = jnp.maximum(m_old, jnp.max(jnp.where(sel, s, floor), axis=1, keepdims=True))
            alpha = jnp.exp(m_old - m_new)
            p = jnp.where(sel, jnp.exp(s - m_new), 0.0)
            l_s[h] = jnp.broadcast_to(alpha * l_s[h][:, 0:1] + jnp.sum(p, axis=1, keepdims=True), (rws, LANES))
            acc_s[:, hs] = alpha * acc_s[:, hs] + jnp.dot(p.astype(BF16), vb[:, hs], preferred_element_type=F32)
            m_s[h] = jnp.broadcast_to(m_new, (rws, LANES))

    kp = jnp.concatenate([r[0] for r in k_refs], axis=0).astype(BF16)
    vp = jnp.concatenate([r[0] for r in v_refs], axis=0).astype(BF16)
    update(kp, vp, sel_ref[0] > 0.5)

    @pl.when(g == pl.num_programs(1) - 1)
    def _():
        update(knew_ref[...].astype(BF16), vnew_ref[...].astype(BF16), seltail_ref[0][:, :rws] > 0.5)
        for h in range(SA_H):
            hs = slice(h * SA_DH, (h + 1) * SA_DH)
            o_ref[:, hs] = (acc_s[:, hs] / l_s[h][:, 0:1] * _silu(z_ref[:, hs])).astype(BF16)


def _dsa_attend(page_table, saq, saz, k_new, v_new, sel, cache_k, cache_v):
    nreq, n_pages = page_table.shape
    n_past = n_pages * PAGE_SIZE
    npg = min(8, n_pages)
    gk = npg * PAGE_SIZE
    req = lambda b, g, pt: (b, 0)

    def page_map(i):
        return lambda b, g, pt: (pt[b, g * npg + i], 0, 0)

    page_specs = [pl.BlockSpec((1, PAGE_SIZE, SA_W), page_map(i)) for i in range(npg)]
    grid_spec = pltpu.PrefetchScalarGridSpec(
        num_scalar_prefetch=1,
        grid=(nreq, n_pages // npg),
        in_specs=[pl.BlockSpec((SAMPLE_ROWS, SA_W), req), pl.BlockSpec((SAMPLE_ROWS, SA_W), req),
                  pl.BlockSpec((SAMPLE_ROWS, SA_W), req), pl.BlockSpec((SAMPLE_ROWS, SA_W), req),
                  pl.BlockSpec((1, SAMPLE_ROWS, gk), lambda b, g, pt: (b, 0, g)),
                  pl.BlockSpec((1, SAMPLE_ROWS, LANES), lambda b, g, pt: (b, 0, n_past // LANES))]
                 + page_specs + page_specs,
        out_specs=pl.BlockSpec((SAMPLE_ROWS, SA_W), req),
        scratch_shapes=[pltpu.VMEM((SA_H, SAMPLE_ROWS, LANES), F32), pltpu.VMEM((SA_H, SAMPLE_ROWS, LANES), F32),
                        pltpu.VMEM((SAMPLE_ROWS, SA_W), F32)],
    )
    return pl.pallas_call(
        functools.partial(_att_kernel, npg=npg),
        out_shape=jax.ShapeDtypeStruct((nreq * SAMPLE_ROWS, SA_W), BF16),
        grid_spec=grid_spec,
        compiler_params=_cparams(("arbitrary", "arbitrary")),
        name="dsa_att",
    )(page_table, saq, saz, k_new, v_new, sel, sel, *([cache_k] * npg), *([cache_v] * npg))


def _memattn_kernel(q_ref, z_ref, mk_ref, mv_ref, o_ref):
    mk = mk_ref[...].astype(BF16)
    mv = mv_ref[...].astype(BF16)
    for h in range(MEM_H):
        hs = slice(h * MEM_DH, (h + 1) * MEM_DH)
        s = lax.dot_general(q_ref[:, hs], mk[:, hs], _NT, preferred_element_type=F32)
        p = jnp.exp(s - jnp.max(s, axis=1, keepdims=True))
        l = jnp.sum(p, axis=1, keepdims=True)
        o = jnp.dot(p.astype(BF16), mv[:, hs], preferred_element_type=F32) / l
        o_ref[:, hs] = (o * _silu(z_ref[:, hs])).astype(BF16)


def _memattn(memq, memz, mk, mv, nb, tq):
    rows = memq.shape[0]
    n_mem = mk.shape[0] // nb
    nq = rows // (nb * tq)
    qblk = lambda b, i: (b * nq + i, 0)
    per_b = lambda b, i: (b, 0)
    return pl.pallas_call(
        _memattn_kernel,
        out_shape=jax.ShapeDtypeStruct((rows, MEM_W), BF16),
        grid=(nb, nq),
        in_specs=[pl.BlockSpec((tq, MEM_W), qblk), pl.BlockSpec((tq, MEM_W), qblk),
                  pl.BlockSpec((n_mem, MEM_W), per_b), pl.BlockSpec((n_mem, MEM_W), per_b)],
        out_specs=pl.BlockSpec((tq, MEM_W), qblk),
        compiler_params=_cparams(("arbitrary", "arbitrary")),
        name="memattn",
    )(memq, memz, mk, mv)


def _mixout_kernel(x_ref, a_ref, b_ref, c_ref, wa_ref, wb_ref, wc_ref, g_ref, y_ref):
    acc = (jnp.dot(a_ref[...], wa_ref[...], preferred_element_type=F32)
           + jnp.dot(b_ref[...], wb_ref[...], preferred_element_type=F32)
           + jnp.dot(c_ref[...], wc_ref[...], preferred_element_type=F32))
    y = acc * lax.rsqrt(jnp.mean(acc * acc, axis=-1, keepdims=True) + EPS) * g_ref[...]
    y_ref[...] = x_ref[...] + y


def _mixout(x2d, a, b, c, w_out, g_post, tm):
    rows = x2d.shape[0]
    wb16 = w_out.astype(BF16)
    row = lambda i: (i, 0)
    const = lambda i: (0, 0)
    return pl.pallas_call(
        _mixout_kernel,
        out_shape=jax.ShapeDtypeStruct((rows, D_MODEL), F32),
        grid=(rows // tm,),
        in_specs=[pl.BlockSpec((tm, D_MODEL), row), pl.BlockSpec((tm, ML_W), row),
                  pl.BlockSpec((tm, SA_W), row), pl.BlockSpec((tm, MEM_W), row),
                  pl.BlockSpec((ML_W, D_MODEL), const), pl.BlockSpec((SA_W, D_MODEL), const),
                  pl.BlockSpec((MEM_W, D_MODEL), const), pl.BlockSpec((1, D_MODEL), const)],
        out_specs=pl.BlockSpec((tm, D_MODEL), row),
        compiler_params=_cparams(("arbitrary",)),
        name="mixout",
    )(x2d, a, b, c, wb16[:ML_W], wb16[ML_W:ML_W + SA_W], wb16[ML_W + SA_W:], g_post.reshape(1, D_MODEL))


def _layer(x_p, x_s, st_c, st_n, st_m, c_k, c_v, c_kidx, c_mk, c_mv, page_table, mem_prompt,
           g_pre, w_in, b_gates, g_head, w_mem_k, w_mem_v, g_mem, w_out, g_post):
    nb, seq, _ = x_p.shape
    nreq, t_dec, _ = x_s.shape
    n_mem = mem_prompt.shape[1]
    n_past = page_table.shape[1] * PAGE_SIZE
    w_main, w_small, w_small_t = _relayout_w_in(w_in)

    tm = min(512, seq)
    tabs = _rope_tables(jnp.arange(seq, dtype=jnp.int32))
    (qkv, oz, saq, k, v, saz, memq, memz, idxq, kidx, small, small_t) = _project(
        x_p.reshape(nb * seq, D_MODEL), g_pre, w_main, w_small, w_small_t, tabs, tm)
    c0 = jnp.zeros((nb, ML_H, ML_DH, ML_DH), F32)
    n0 = jnp.zeros((nb, ML_H, ML_DH), F32)
    m0 = jnp.zeros((nb, ML_H), F32)
    a_p, p_c, p_n, p_m = _mlstm(qkv, oz, small, small_t, b_gates, g_head, c0, n0, m0, nb, min(256, seq), 0)
    b_p = _dsa_prompt(saq, saz, idxq, small_t, k, v, kidx, nb, seq)
    mk, mv = _memkv(mem_prompt.reshape(nb * n_mem, D_MODEL), g_mem, w_mem_k, w_mem_v, n_mem)
    c_p = _memattn(memq, memz, mk, mv, nb, min(256, seq))
    y_p = _mixout(x_p.reshape(nb * seq, D_MODEL), a_p, b_p, c_p, w_out, g_post, tm).reshape(nb, seq, D_MODEL)

    rws = SAMPLE_ROWS
    n_padrow = rws - t_dec
    xs_pad = jnp.concatenate([jnp.zeros((nreq, n_padrow, D_MODEL), F32), x_s], axis=1).reshape(nreq * rws, D_MODEL)
    pos_s = jnp.tile(jnp.concatenate([jnp.zeros((n_padrow,), jnp.int32),
                                      n_past + jnp.arange(t_dec, dtype=jnp.int32)]), nreq)
    tabs_s = _rope_tables(pos_s)
    (qkv_s, oz_s, saq_s, k_s, v_s, saz_s, memq_s, memz_s, idxq_s, kidx_s, small_s, small_t_s) = _project(
        xs_pad, g_pre, w_main, w_small, w_small_t, tabs_s, nreq * rws)
    a_s, s_c, s_n, s_m = _mlstm(qkv_s, oz_s, small_s, small_t_s, b_gates, g_head, st_c, st_n, st_m,
                                nreq, rws, n_padrow)
    n_sel = min(TOPK_MAX, (n_past + t_dec) // 4)
    sel = _dsa_select(page_table, idxq_s, small_s, kidx_s, c_kidx, n_sel, t_dec)
    b_s = _dsa_attend(page_table, saq_s, saz_s, k_s, v_s, sel,
                      c_k.reshape(c_k.shape[0], PAGE_SIZE, SA_W), c_v.reshape(c_v.shape[0], PAGE_SIZE, SA_W))
    c_s = _memattn(memq_s, memz_s, c_mk.reshape(nreq * n_mem, MEM_W), c_mv.reshape(nreq * n_mem, MEM_W), nreq, rws)
    y_s = _mixout(xs_pad, a_s, b_s, c_s, w_out, g_post, nreq * rws)

    def real(a2d):
        return a2d.reshape(nreq, rws, -1)[:, n_padrow:]

    new = (p_c, p_n, p_m,
           k.reshape(nb, seq, SA_H, SA_DH), v.reshape(nb, seq, SA_H, SA_DH), kidx.reshape(nb, seq, IDX_D),
           mk.reshape(nb, n_mem, MEM_H, MEM_DH), mv.reshape(nb, n_mem, MEM_H, MEM_DH),
           s_c, s_n, s_m,
           real(k_s).reshape(nreq, t_dec, SA_H, SA_DH), real(v_s).reshape(nreq, t_dec, SA_H, SA_DH), real(kidx_s))
    return y_p, real(y_s), new


def kernel(x_prompt, x_sample, state_mlstm_C, state_mlstm_n, state_mlstm_m, cache_k, cache_v, cache_kidx,
           cache_mem_k, cache_mem_v, page_table, mem_prompt, g_pre, w_in, b_gates, g_head, w_mem_k, w_mem_v,
           g_mem, w_out, g_post):
    xp, xs = x_prompt, x_sample
    per_layer = []
    for l in range(w_in.shape[0]):
        xp, xs, new = _layer(xp, xs, state_mlstm_C[l], state_mlstm_n[l], state_mlstm_m[l],
                             cache_k[l], cache_v[l], cache_kidx[l], cache_mem_k[l], cache_mem_v[l],
                             page_table, mem_prompt, g_pre[l], w_in[l], b_gates[l], g_head[l],
                             w_mem_k[l], w_mem_v[l], g_mem[l], w_out[l], g_post[l])
        per_layer.append(new)
    stacked = [jnp.stack(a) for a in zip(*per_layer)]
    return (xp, xs, *stacked)
```

```python
import functools

import jax
import jax.numpy as jnp
from jax import lax
from jax.experimental import pallas as pl
from jax.experimental.pallas import tpu as pltpu

F32 = jnp.float32
BF16 = jnp.bfloat16

D_MODEL = 2048
ML_H = 4
ML_W = D_MODEL // 2
ML_DH = ML_W // ML_H
SA_H = 4
SA_W = D_MODEL // 4
SA_DH = SA_W // SA_H
SA_H_LOG2 = 2
MEM_H = 4
MEM_W = D_MODEL // 4
MEM_DH = MEM_W // MEM_H
IDX_H = 8
IDX_D = 64
IDX_SCALE = (IDX_H * IDX_D) ** -0.5
TOPK_MAX = 256
ROPE_THETA = 10000.0
EPS = 1e-6
PAGE_SIZE = 128

LANES = 128
CB = 512
N_MAIN_BLOCKS = 17
SMALL_IG = 64
SMALL_LF = 68
SMALL_W = 72
LANES_LOG2 = 7
SAMPLE_ROWS = 16
SAMPLE_ROWS_LOG2 = 4
VMEM_LIMIT = 56 * 1024 * 1024
NEG_INF = float("-inf")
POS_INF = float("inf")

_NT = (((1,), (1,)), ((), ()))
_TN = (((0,), (0,)), ((), ()))


def _cparams(sem):
    return pltpu.CompilerParams(dimension_semantics=sem, vmem_limit_bytes=VMEM_LIMIT)


def _sigmoid(x):
    return 1.0 / (1.0 + jnp.exp(-x))


def _silu(x):
    return x * _sigmoid(x)


def _log_sigmoid(x):
    return jnp.minimum(x, 0.0) - jnp.log1p(jnp.exp(-jnp.abs(x)))


def _split3(x):
    hi = x.astype(BF16)
    r = x - hi.astype(F32)
    mid = r.astype(BF16)
    lo = (r - mid.astype(F32)).astype(BF16)
    return hi, mid, lo


def _rope128(x, cos, sin_signed):
    return x * cos + pltpu.roll(x, 64, 1) * sin_signed


def _rope64(x, cos, sin_signed):
    lane = lax.broadcasted_iota(jnp.int32, x.shape, 1)
    first_half = (lane % 64) < 32
    partner = jnp.where(first_half, pltpu.roll(x, 96, 1), pltpu.roll(x, 32, 1))
    return x * cos + partner * sin_signed


def _proj_kernel(x_ref, g_ref, w_ref, ws_ref, wst_ref, c128_ref, s128_ref, c64_ref, s64_ref,
                 qkv_ref, oz_ref, saq_ref, k_ref, v_ref, saz_ref, memq_ref, memz_ref, idxq_ref,
                 kidx_ref, small_ref, smallt_ref, u_ref):
    j = pl.program_id(1)

    @pl.when(j == 0)
    def _():
        x = x_ref[...]
        u = x * lax.rsqrt(jnp.mean(x * x, axis=-1, keepdims=True) + EPS) * g_ref[...]
        ub = u.astype(BF16)
        u_ref[...] = ub
        sm = jnp.dot(ub, ws_ref[...], preferred_element_type=F32)
        small_ref[...] = sm
        kidx_ref[...] = _rope64(sm, c64_ref[...], s64_ref[...])[:, :IDX_D]
        smallt_ref[...] = lax.dot_general(wst_ref[...], ub, _NT, preferred_element_type=F32)

    acc = jnp.dot(u_ref[...], w_ref[...], preferred_element_type=F32)

    def rope_heads(fn, cos, sin):
        return jnp.concatenate(
            [fn(acc[:, h * LANES:(h + 1) * LANES], cos, sin) for h in range(CB // LANES)], axis=1)

    @pl.when(j < 6)
    def _():
        qkv_ref[...] = acc.astype(BF16)

    @pl.when((j >= 6) & (j < 10))
    def _():
        oz_ref[...] = acc

    @pl.when(j == 10)
    def _():
        saq_ref[...] = (rope_heads(_rope128, c128_ref[...], s128_ref[...]) * (SA_DH ** -0.5)).astype(BF16)

    @pl.when(j == 11)
    def _():
        k_ref[...] = rope_heads(_rope128, c128_ref[...], s128_ref[...])

    @pl.when(j == 12)
    def _():
        v_ref[...] = acc

    @pl.when(j == 13)
    def _():
        saz_ref[...] = acc

    @pl.when(j == 14)
    def _():
        memq_ref[...] = (acc * (MEM_DH ** -0.5)).astype(BF16)

    @pl.when(j == 15)
    def _():
        memz_ref[...] = acc

    @pl.when(j == 16)
    def _():
        idxq_ref[...] = rope_heads(_rope64, c64_ref[...], s64_ref[...]).astype(BF16)


def _project(x2d, g_pre, w_main, w_small, w_small_t, tabs, tm):
    rows = x2d.shape[0]
    c128, s128, c64, s64 = tabs
    ntab = c128.shape[0] // tm
    nb = N_MAIN_BLOCKS

    def span(a, b):
        return lambda i, j: (i, jnp.clip(j - a, 0, b - a - 1))

    row_only = lambda i, j: (i, 0)
    tab_map = lambda i, j: (i % ntab, 0)
    const = lambda i, j: (0, 0)
    out_shape = (
        jax.ShapeDtypeStruct((rows, 3 * ML_W), BF16),
        jax.ShapeDtypeStruct((rows, 2 * ML_W), F32),
        jax.ShapeDtypeStruct((rows, SA_W), BF16),
        jax.ShapeDtypeStruct((rows, SA_W), F32),
        jax.ShapeDtypeStruct((rows, SA_W), F32),
        jax.ShapeDtypeStruct((rows, SA_W), F32),
        jax.ShapeDtypeStruct((rows, MEM_W), BF16),
        jax.ShapeDtypeStruct((rows, MEM_W), F32),
        jax.ShapeDtypeStruct((rows, IDX_H * IDX_D), BF16),
        jax.ShapeDtypeStruct((rows, IDX_D), F32),
        jax.ShapeDtypeStruct((rows, LANES), F32),
        jax.ShapeDtypeStruct((LANES, rows), F32),
    )
    out_specs = (
        pl.BlockSpec((tm, CB), span(0, 6)),
        pl.BlockSpec((tm, CB), span(6, 10)),
        pl.BlockSpec((tm, CB), row_only),
        pl.BlockSpec((tm, CB), row_only),
        pl.BlockSpec((tm, CB), row_only),
        pl.BlockSpec((tm, CB), row_only),
        pl.BlockSpec((tm, CB), row_only),
        pl.BlockSpec((tm, CB), row_only),
        pl.BlockSpec((tm, CB), row_only),
        pl.BlockSpec((tm, IDX_D), row_only),
        pl.BlockSpec((tm, LANES), row_only),
        pl.BlockSpec((LANES, tm), lambda i, j: (0, i)),
    )
    in_specs = [
        pl.BlockSpec((tm, D_MODEL), row_only),
        pl.BlockSpec((1, D_MODEL), const),
        pl.BlockSpec((D_MODEL, CB), lambda i, j: (0, j)),
        pl.BlockSpec((D_MODEL, LANES), const),
        pl.BlockSpec((LANES, D_MODEL), const),
        pl.BlockSpec((tm, LANES), tab_map),
        pl.BlockSpec((tm, LANES), tab_map),
        pl.BlockSpec((tm, LANES), tab_map),
        pl.BlockSpec((tm, LANES), tab_map),
    ]
    return pl.pallas_call(
        _proj_kernel,
        out_shape=out_shape,
        grid=(rows // tm, nb),
        in_specs=in_specs,
        out_specs=out_specs,
        scratch_shapes=[pltpu.VMEM((tm, D_MODEL), BF16)],
        compiler_params=_cparams(("arbitrary", "arbitrary")),
        name="proj",
    )(x2d, g_pre.reshape(1, D_MODEL), w_main, w_small, w_small_t, c128, s128, c64, s64)


def _rope_tables(pos):
    def tab(half):
        inv = ROPE_THETA ** (-jnp.arange(half, dtype=F32) / half)
        ang = pos.astype(F32)[:, None] * inv[None, :]
        return jnp.cos(ang), jnp.sin(ang)

    c, s = tab(SA_DH // 2)
    c128 = jnp.concatenate([c, c], axis=1)
    s128 = jnp.concatenate([-s, s], axis=1)
    c, s = tab(IDX_D // 2)
    c64 = jnp.concatenate([c, c, c, c], axis=1)
    s64 = jnp.concatenate([-s, s, -s, s], axis=1)
    return c128, s128, c64, s64


def _relayout_w_in(w_in):
    off = {}
    o = 0
    for name, w in (('ml_q', ML_W), ('ml_k', ML_W), ('ml_v', ML_W), ('ml_o', ML_W), ('ml_z', ML_W),
                    ('ml_i', ML_H), ('ml_f', ML_H), ('sa_q', SA_W), ('sa_k', SA_W), ('sa_v', SA_W),
                    ('sa_z', SA_W), ('idx_q', IDX_H * IDX_D), ('idx_k', IDX_D), ('idx_w', IDX_H),
                    ('mem_q', MEM_W), ('mem_z', MEM_W)):
        off[name] = (o, w)
        o += w

    def col(name):
        a, w = off[name]
        return w_in[:, a:a + w]

    main = jnp.concatenate([
        col('ml_q'), col('ml_k') * (ML_DH ** -0.5), col('ml_v'), col('ml_o'), col('ml_z'),
        col('sa_q'), col('sa_k'), col('sa_v'), col('sa_z'), col('mem_q'), col('mem_z'), col('idx_q'),
    ], axis=1).astype(BF16)
    small = jnp.concatenate([
        col('idx_k'), col('ml_i'), col('ml_f'), col('idx_w'),
        jnp.zeros((D_MODEL, LANES - IDX_D - 2 * ML_H - IDX_H), F32)], axis=1).astype(BF16)
    return main, small, small.T


def _memkv_kernel(m_ref, g_ref, wk_ref, wv_ref, k_ref, v_ref):
    x = m_ref[...]
    u = (x * lax.rsqrt(jnp.mean(x * x, axis=-1, keepdims=True) + EPS) * g_ref[...]).astype(BF16)
    k_ref[...] = jnp.dot(u, wk_ref[...], preferred_element_type=F32)
    v_ref[...] = jnp.dot(u, wv_ref[...], preferred_element_type=F32)


def _memkv(mem2d, g_mem, wk, wv, n_mem):
    rows = mem2d.shape[0]
    row = lambda i: (i, 0)
    const = lambda i: (0, 0)
    return pl.pallas_call(
        _memkv_kernel,
        out_shape=(jax.ShapeDtypeStruct((rows, MEM_W), F32), jax.ShapeDtypeStruct((rows, MEM_W), F32)),
        grid=(rows // n_mem,),
        in_specs=[pl.BlockSpec((n_mem, D_MODEL), row), pl.BlockSpec((1, D_MODEL), const),
                  pl.BlockSpec((D_MODEL, MEM_W), const), pl.BlockSpec((D_MODEL, MEM_W), const)],
        out_specs=(pl.BlockSpec((n_mem, MEM_W), row), pl.BlockSpec((n_mem, MEM_W), row)),
        compiler_params=_cparams(("arbitrary",)),
        name="memkv",
    )(mem2d, g_mem.reshape(1, D_MODEL), wk.astype(BF16), wv.astype(BF16))


def _mlstm_kernel(qkv_ref, oz_ref, gc_ref, gt_ref, bcol_ref, brow_ref, gh_ref, c0_ref, n0_ref, m0_ref,
                  a_ref, cout_ref, nout_ref, mout_ref, c_s, n_s, m_s, *, c, n_pad):
    ci = pl.program_id(1)

    @pl.when(ci == 0)
    def _():
        c_s[...] = c0_ref[0]
        n_s[...] = n0_ref[0]
        m_s[...] = m0_ref[0]

    ri = lax.broadcasted_iota(jnp.int32, (c, c), 0)
    cj = lax.broadcasted_iota(jnp.int32, (c, c), 1)
    causal = cj <= ri
    tri = jnp.where(causal, 1.0, 0.0).astype(BF16)
    tri_t = jnp.where(ri <= cj, 1.0, 0.0).astype(BF16)

    g_c = gc_ref[...] + bcol_ref[...]
    pad_c = lax.broadcasted_iota(jnp.int32, (c, LANES), 0) < n_pad
    ig_c = jnp.where(pad_c, NEG_INF, g_c)
    lf_c = jnp.where(pad_c, 0.0, _log_sigmoid(g_c))
    b_c = sum(jnp.dot(tri, p, preferred_element_type=F32) for p in _split3(lf_c))
    g_r = gt_ref[0] + brow_ref[...]
    pad_r = lax.broadcasted_iota(jnp.int32, (SAMPLE_ROWS, c), 1) < n_pad
    ig_r = jnp.where(pad_r, NEG_INF, g_r)
    lf_r = jnp.where(pad_r, 0.0, _log_sigmoid(g_r))
    b_r = sum(jnp.dot(p, tri_t, preferred_element_type=F32) for p in _split3(lf_r))

    for h in range(ML_H):
        hs = slice(h * ML_DH, (h + 1) * ML_DH)
        m_prev = m_s[h:h + 1, 0:1]
        b_t = b_c[:, SMALL_LF + h:SMALL_LF + h + 1]
        igc = ig_c[:, SMALL_IG + h:SMALL_IG + h + 1]
        b_s = b_r[ML_H + h:ML_H + h + 1, :]
        igr = ig_r[h:h + 1, :]
        a = jnp.where(causal, b_t - b_s + igr, NEG_INF)
        bm = b_t + m_prev
        m_t = jnp.maximum(bm, jnp.max(a, axis=1, keepdims=True))
        inter = jnp.exp(bm - m_t)
        dmat = jnp.exp(a - m_t)
        q = qkv_ref[:, h * ML_DH:(h + 1) * ML_DH]
        k = qkv_ref[:, ML_W + h * ML_DH:ML_W + (h + 1) * ML_DH]
        v = qkv_ref[:, 2 * ML_W + h * ML_DH:2 * ML_W + (h + 1) * ML_DH]
        s = lax.dot_general(q, k, _NT, preferred_element_type=F32) * dmat
        c_h = c_s[h]
        n_h = n_s[h:h + 1, :]
        num = (jnp.dot(s.astype(BF16), v, preferred_element_type=F32)
               + inter * jnp.dot(q, c_h.astype(BF16), preferred_element_type=F32))
        qn = (jnp.sum(s, axis=1, keepdims=True)
              + inter * jnp.sum(q.astype(F32) * n_h, axis=1, keepdims=True))
        hh = num / jnp.maximum(jnp.abs(qn), jnp.exp(-m_t))
        hh = hh * lax.rsqrt(jnp.mean(hh * hh, axis=1, keepdims=True) + EPS)
        o = oz_ref[:, h * ML_DH:(h + 1) * ML_DH]
        z = oz_ref[:, ML_W + h * ML_DH:ML_W + (h + 1) * ML_DH]
        a_ref[:, hs] = (hh * gh_ref[:, hs] * _sigmoid(o) * _silu(z)).astype(BF16)

        m_new = m_t[c - 1:c, :]
        b_last = b_t[c - 1:c, :]
        w_end = jnp.exp(b_last - b_t + igc - m_new)
        decay = jnp.exp(b_last + m_prev - m_new)
        kw = k.astype(F32) * w_end
        c_s[h] = decay * c_h + lax.dot_general(kw.astype(BF16), v, _TN, preferred_element_type=F32)
        n_s[h:h + 1, :] = decay * n_h + jnp.sum(kw, axis=0, keepdims=True)
        m_s[h:h + 1, :] = jnp.broadcast_to(m_new, (1, LANES))

    @pl.when(ci == pl.num_programs(1) - 1)
    def _():
        cout_ref[0] = c_s[...]
        nout_ref[0] = n_s[...]
        mout_ref[0] = m_s[...]


def _mlstm(qkv, oz, small, small_t, b_gates, g_head, c0, n0, m0, nb, c, n_pad):
    rows = qkv.shape[0]
    nc = rows // (nb * c)
    bias_col = jnp.zeros((1, LANES), F32).at[0, SMALL_IG:SMALL_IG + 2 * ML_H].set(b_gates)
    bias_row = jnp.zeros((SAMPLE_ROWS, 1), F32).at[:2 * ML_H, 0].set(b_gates)
    m0b = jnp.zeros((nb, 8, LANES), F32).at[:, :ML_H, :].set(jnp.broadcast_to(m0[:, :, None], (nb, ML_H, LANES)))
    rowblk = lambda b, i: (b * nc + i, 0)
    const = lambda b, i: (0, 0)
    gates_t = small_t[SMALL_IG:SMALL_IG + SAMPLE_ROWS].reshape(SAMPLE_ROWS, rows // c, c).transpose(1, 0, 2)
    out_shape = (
        jax.ShapeDtypeStruct((rows, ML_W), BF16),
        jax.ShapeDtypeStruct((nb, ML_H, ML_DH, ML_DH), F32),
        jax.ShapeDtypeStruct((nb, ML_H, ML_DH), F32),
        jax.ShapeDtypeStruct((nb, 8, LANES), F32),
    )
    st4 = lambda b, i: (b, 0, 0, 0)
    st3 = lambda b, i: (b, 0, 0)
    a, c_out, n_out, m_out = pl.pallas_call(
        functools.partial(_mlstm_kernel, c=c, n_pad=n_pad),
        out_shape=out_shape,
        grid=(nb, nc),
        in_specs=[
            pl.BlockSpec((c, 3 * ML_W), rowblk),
            pl.BlockSpec((c, 2 * ML_W), rowblk),
            pl.BlockSpec((c, LANES), rowblk),
            pl.BlockSpec((1, SAMPLE_ROWS, c), lambda b, i: (b * nc + i, 0, 0)),
            pl.BlockSpec((1, LANES), const),
            pl.BlockSpec((SAMPLE_ROWS, 1), const),
            pl.BlockSpec((1, ML_W), const),
            pl.BlockSpec((1, ML_H, ML_DH, ML_DH), st4),
            pl.BlockSpec((1, ML_H, ML_DH), st3),
            pl.BlockSpec((1, 8, LANES), st3),
        ],
        out_specs=(
            pl.BlockSpec((c, ML_W), rowblk),
            pl.BlockSpec((1, ML_H, ML_DH, ML_DH), st4),
            pl.BlockSpec((1, ML_H, ML_DH), st3),
            pl.BlockSpec((1, 8, LANES), st3),
        ),
        scratch_shapes=[pltpu.VMEM((ML_H, ML_DH, ML_DH), F32), pltpu.VMEM((ML_H, ML_DH), F32),
                        pltpu.VMEM((8, LANES), F32)],
        compiler_params=_cparams(("arbitrary", "arbitrary")),
        name="mlstm",
    )(qkv, oz, small, gates_t, bias_col, bias_row, g_head.reshape(1, ML_W), c0, n0, m0b)
    return a, c_out, n_out, m_out[:, :ML_H, 0]


_REDUCERS = {"sum": (jnp.sum, jnp.add), "max": (jnp.max, jnp.maximum), "min": (jnp.min, jnp.minimum)}
REDUCE_CHAINS = 8


def _reduce(x, axis, op):
    fn, combine = _REDUCERS[op]
    unit = 8 if axis == 0 else LANES
    n = x.shape[axis]
    units = n // unit
    if n % unit or units < 2 * REDUCE_CHAINS:
        return fn(x, axis=axis, keepdims=True)
    base, rem = divmod(units, REDUCE_CHAINS)
    parts, start = [], 0
    for i in range(REDUCE_CHAINS):
        size = (base + (1 if i < rem else 0)) * unit
        piece = x[start:start + size] if axis == 0 else x[:, start:start + size]
        parts.append(fn(piece, axis=axis, keepdims=True))
        start += size
    while len(parts) > 1:
        parts = [combine(parts[i], parts[i + 1]) for i in range(0, len(parts), 2)]
    return parts[0]


def _count(pred, axis):
    return _reduce(jnp.where(pred, 1.0, 0.0), axis, "sum")


def _kth_largest(x_ref, k, axis, n_bisect):
    kf = float(k)
    x = x_ref[...]
    hi = _reduce(x, axis, "max")
    lo = _reduce(jnp.where(x == NEG_INF, POS_INF, x), axis, "min")

    def bisect(_, carry):
        lo, hi = carry
        mid = 0.5 * (lo + hi)
        ge = _count(x_ref[...] >= mid, axis) >= kf
        return jnp.where(ge, mid, lo), jnp.where(ge, hi, mid)

    lo, hi = lax.fori_loop(0, n_bisect, bisect, (lo, hi))

    def cond(st):
        return st[4] < 0.5

    def body(st):
        lo, strict, thr, done, _ = st
        xx = x_ref[...]
        above = _reduce(jnp.where(xx > lo, xx, POS_INF), axis, "min")
        at_lo = (strict < 0.5) & (_count(xx == lo, axis) > 0.5)
        cmin = jnp.where(at_lo, lo, above)
        fin = (_count(xx > cmin, axis) < kf) | (cmin == POS_INF)
        active = done < 0.5
        thr = jnp.where(active, cmin, thr)
        lo = jnp.where(active, cmin, lo)
        done = jnp.where(fin, 1.0, done)
        return lo, jnp.ones_like(strict), thr, done, jnp.min(done)

    zeros = jnp.zeros_like(lo)
    _, _, thr, _, _ = lax.while_loop(cond, body, (lo, zeros, lo, zeros, jnp.float32(0.0)))
    return thr


def _dsa_kernel(q_ref, z_ref, idxq_ref, wt_ref, k_ref, v_ref, kidx_ref, o_ref,
                kb_s, vt_s, kib_s, x_s, sel_s, *, n_keys, qb, n_sel, n_bisect, key_step):
    j = pl.program_id(1)

    @pl.when(j == 0)
    def _():
        kb_s[...] = k_ref[...].astype(BF16)
        vt_s[...] = v_ref[...].T.astype(BF16)
        kib_s[...] = kidx_ref[...].astype(BF16)

    def attend(nk):
        xs = x_s.at[0:nk]
        ss = sel_s.at[0:nk]
        key = lax.broadcasted_iota(jnp.int32, (nk, qb), 0)
        qpos = j * qb + lax.broadcasted_iota(jnp.int32, (nk, qb), 1)
        valid = key <= qpos
        sc = jnp.zeros((nk, qb), F32)
        for h in range(IDX_H):
            d = lax.dot_general(kib_s[0:nk, :], idxq_ref[:, h * IDX_D:(h + 1) * IDX_D], _NT,
                                preferred_element_type=F32)
            sc = sc + jnp.maximum(d, 0.0) * (wt_ref[h:h + 1, :] * IDX_SCALE)
        xs[...] = jnp.where(valid, sc, NEG_INF)
        ss[...] = jnp.where(valid, 1.0, 0.0)

        @pl.when((j + 1) * qb > n_sel)
        def _():
            kf = float(n_sel)
            thr = _kth_largest(xs, n_sel, 0, n_bisect)
            x = xs[...]
            need = kf - _count(x > thr, 0)
            n_tie = _count(x == thr, 0)
            qrow = j * qb + lax.broadcasted_iota(jnp.int32, (1, qb), 1)
            small = (qrow + 1) <= n_sel
            ss[...] = jnp.where(small, jnp.where(valid, 1.0, 0.0), jnp.where(x >= thr, 1.0, 0.0))
            excess = jnp.max(jnp.where((n_tie > need) & jnp.logical_not(small), 1.0, 0.0))

            @pl.when(excess > 0.5)
            def _():
                tb = min(256, nk)
                r_i = lax.broadcasted_iota(jnp.int32, (tb, tb), 0)
                c_i = lax.broadcasted_iota(jnp.int32, (tb, tb), 1)
                lower = jnp.where(c_i < r_i, 1.0, 0.0).astype(BF16)
                carry = jnp.zeros((1, qb), F32)
                for blk in range(nk // tb):
                    rows = slice(blk * tb, (blk + 1) * tb)
                    xb = x_s[rows, :]
                    tie = jnp.where(xb == thr, 1.0, 0.0)
                    rank = jnp.dot(lower, tie.astype(BF16), preferred_element_type=F32) + carry
                    keep = (xb > thr) | ((xb == thr) & (rank < need))
                    keyb = blk * tb + lax.broadcasted_iota(jnp.int32, (tb, qb), 0)
                    qposb = j * qb + lax.broadcasted_iota(jnp.int32, (tb, qb), 1)
                    smallb = jnp.where(keyb <= qposb, 1.0, 0.0)
                    sel_s[rows, :] = jnp.where(small, smallb, jnp.where(keep, 1.0, 0.0))
                    carry = carry + jnp.sum(tie, axis=0, keepdims=True)

        sel = ss[...] > 0.5
        for h in range(SA_H):
            hs = slice(h * SA_DH, (h + 1) * SA_DH)
            st = lax.dot_general(kb_s[0:nk, hs], q_ref[:, hs], _NT, preferred_element_type=F32)
            st = jnp.where(sel, st, NEG_INF)
            mx = _reduce(st, 0, "max")
            p = jnp.exp(st - mx)
            l = _reduce(p, 0, "sum")
            ot = jnp.dot(vt_s[hs, 0:nk], p.astype(BF16), preferred_element_type=F32) / l
            o_ref[:, hs] = (ot.T * _silu(z_ref[:, hs])).astype(BF16)

    n_ext = n_keys // key_step
    for e in range(n_ext):
        nk = (e + 1) * key_step
        lo_j = e * key_step // qb
        hi_j = nk // qb

        @pl.when((j >= lo_j) & (j < hi_j))
        def _(nk=nk):
            attend(nk)


def _dsa_prompt(saq, saz, idxq, small_t, k, v, kidx, nb, seq):
    rows = saq.shape[0]
    qb = min(seq, 128)
    nq = seq // qb
    n_sel = min(TOPK_MAX, seq // 4)
    qblk = lambda b, j: (b * nq + j, 0)
    per_b = lambda b, j: (b, 0)
    wt_blk = SMALL_W // 8
    return pl.pallas_call(
        functools.partial(_dsa_kernel, n_keys=seq, qb=qb, n_sel=n_sel, n_bisect=20, key_step=min(512, seq)),
        out_shape=jax.ShapeDtypeStruct((rows, SA_W), BF16),
        grid=(nb, nq),
        in_specs=[
            pl.BlockSpec((qb, SA_W), qblk),
            pl.BlockSpec((qb, SA_W), qblk),
            pl.BlockSpec((qb, IDX_H * IDX_D), qblk),
            pl.BlockSpec((8, qb), lambda b, j: (wt_blk, b * nq + j)),
            pl.BlockSpec((seq, SA_W), per_b),
            pl.BlockSpec((seq, SA_W), per_b),
            pl.BlockSpec((seq, IDX_D), per_b),
        ],
        out_specs=pl.BlockSpec((qb, SA_W), qblk),
        scratch_shapes=[pltpu.VMEM((seq, SA_W), BF16), pltpu.VMEM((SA_W, seq), BF16),
                        pltpu.VMEM((seq, IDX_D), BF16), pltpu.VMEM((seq, qb), F32),
                        pltpu.VMEM((seq, qb), F32)],
        compiler_params=_cparams(("arbitrary", "arbitrary")),
        name="dsa",
    )(saq, saz, idxq, small_t, k, v, kidx)


def _sel_kernel(pt_ref, idxq_ref, small_ref, kinew_ref, *rest, npg, n_past, n_sel, n_real, n_bisect):
    page_refs = rest[:npg]
    selx_ref, seltail_ref, x_s, sel_s = rest[npg:]
    g = pl.program_id(1)
    rws = SAMPLE_ROWS
    pk = n_past + LANES
    gk = npg * PAGE_SIZE

    qs = jnp.concatenate([idxq_ref[:, h * IDX_D:(h + 1) * IDX_D] for h in range(IDX_H)], axis=0)
    small = small_ref[...]

    def scores(d):
        sc = jnp.zeros((rws, d.shape[1]), F32)
        for h in range(IDX_H):
            w = small[:, SMALL_W + h:SMALL_W + h + 1] * IDX_SCALE
            sc = sc + jnp.maximum(d[h * rws:(h + 1) * rws, :], 0.0) * w
        return sc

    kp_t = jnp.concatenate([r[0] for r in page_refs], axis=1).astype(BF16)
    x_s[:, pl.ds(pl.multiple_of(g * gk, LANES), gk)] = scores(jnp.dot(qs, kp_t, preferred_element_type=F32))

    @pl.when(g == pl.num_programs(1) - 1)
    def _():
        knew = jnp.concatenate([kinew_ref[...], jnp.zeros((LANES - rws, IDX_D), F32)], axis=0).astype(BF16)
        row = lax.broadcasted_iota(jnp.int32, (rws, LANES), 0)
        col = lax.broadcasted_iota(jnp.int32, (rws, LANES), 1)
        ok = (col >= rws - n_real) & (col < rws) & (col <= row)
        d_new = lax.dot_general(qs, knew, _NT, preferred_element_type=F32)
        x_s[:, n_past:pk] = jnp.where(ok, scores(d_new), NEG_INF)

        kf = float(n_sel)
        thr = _kth_largest(x_s, n_sel, 1, n_bisect)
        x = x_s[...]
        need = kf - _count(x > thr, 1)
        n_tie = _count(x == thr, 1)
        sel_s[...] = jnp.where(x >= thr, 1.0, 0.0)
        real = lax.broadcasted_iota(jnp.int32, (rws, 1), 0) >= rws - n_real
        excess = jnp.max(jnp.where((n_tie > need) & real, 1.0, 0.0))

        @pl.when(excess > 0.5)
        def _():
            r_i = lax.broadcasted_iota(jnp.int32, (LANES, LANES), 0)
            c_i = lax.broadcasted_iota(jnp.int32, (LANES, LANES), 1)
            upper = jnp.where(r_i < c_i, 1.0, 0.0).astype(BF16)

            def blk(i, carry):
                cols = pl.ds(pl.multiple_of(i * LANES, LANES), LANES)
                xb = x_s[:, cols]
                tie = jnp.where(xb == thr, 1.0, 0.0)
                rank = jnp.dot(tie.astype(BF16), upper, preferred_element_type=F32) + carry
                keep = (xb > thr) | ((xb == thr) & (rank < need))
                sel_s[:, cols] = jnp.where(keep, 1.0, 0.0)
                return carry + jnp.sum(tie, axis=1, keepdims=True)

            lax.fori_loop(0, pk // LANES, blk, jnp.zeros((rws, 1), F32))

        e_r = lax.broadcasted_iota(jnp.int32, (LANES, SA_H * LANES), 0)
        e_c = lax.broadcasted_iota(jnp.int32, (LANES, SA_H * LANES), 1)
        expand = jnp.where(jnp.right_shift(e_c, SA_H_LOG2) == e_r, 1.0, 0.0).astype(BF16)

        per_iter = 8 if (n_past // LANES) % 8 == 0 else 1

        def widen(i, carry):
            for u in range(per_iter):
                blk = i * per_iter + u
                cols = pl.ds(pl.multiple_of(blk * LANES, LANES), LANES)
                wide = pl.ds(pl.multiple_of(blk * SA_H * LANES, SA_H * LANES), SA_H * LANES)
                selx_ref[0, :, wide] = jnp.dot(sel_s[:, cols].astype(BF16), expand, preferred_element_type=F32)
            return carry

        lax.fori_loop(0, n_past // LANES // per_iter, widen, 0)
        seltail_ref[0] = sel_s[:, n_past:pk]


def _dsa_select(page_table, idxq, small, kidx_new, cache_kidx, n_sel, n_real):
    nreq, n_pages = page_table.shape
    n_past = n_pages * PAGE_SIZE
    pk = n_past + LANES
    npg = min(16, n_pages)
    req = lambda b, g, pt: (b, 0)

    def page_map(i):
        return lambda b, g, pt: (pt[b, g * npg + i], 0, 0)

    grid_spec = pltpu.PrefetchScalarGridSpec(
        num_scalar_prefetch=1,
        grid=(nreq, n_pages // npg),
        in_specs=[pl.BlockSpec((SAMPLE_ROWS, IDX_H * IDX_D), req),
                  pl.BlockSpec((SAMPLE_ROWS, LANES), req),
                  pl.BlockSpec((SAMPLE_ROWS, IDX_D), req)]
                 + [pl.BlockSpec((1, IDX_D, PAGE_SIZE), page_map(i)) for i in range(npg)],
        out_specs=(pl.BlockSpec((1, SAMPLE_ROWS, SA_H * n_past), lambda b, g, pt: (b, 0, 0)),
                   pl.BlockSpec((1, SAMPLE_ROWS, LANES), lambda b, g, pt: (b, 0, 0))),
        scratch_shapes=[pltpu.VMEM((SAMPLE_ROWS, pk), F32), pltpu.VMEM((SAMPLE_ROWS, pk), F32)],
    )
    return pl.pallas_call(
        functools.partial(_sel_kernel, npg=npg, n_past=n_past, n_sel=n_sel, n_real=n_real, n_bisect=20),
        out_shape=(jax.ShapeDtypeStruct((nreq, SAMPLE_ROWS, SA_H * n_past), F32),
                   jax.ShapeDtypeStruct((nreq, SAMPLE_ROWS, LANES), F32)),
        grid_spec=grid_spec,
        compiler_params=_cparams(("arbitrary", "arbitrary")),
        name="dsa_sel",
    )(page_table, idxq, small, kidx_new, *([cache_kidx] * npg))


def _att_kernel(pt_ref, q_ref, z_ref, knew_ref, vnew_ref, selx_ref, seltail_ref, *rest, npg):
    k_refs = rest[:npg]
    v_refs = rest[npg:2 * npg]
    o_ref = rest[2 * npg]
    m_s, l_s, acc_s = rest[2 * npg + 1:]
    g = pl.program_id(1)
    rws = SAMPLE_ROWS
    nrow = SA_H * rws
    floor = -1e30

    @pl.when(g == 0)
    def _():
        m_s[...] = jnp.full(m_s.shape, floor, F32)
        l_s[...] = jnp.zeros(l_s.shape, F32)
        acc_s[...] = jnp.zeros(acc_s.shape, F32)

    qs = jnp.concatenate([q_ref[:, h * SA_DH:(h + 1) * SA_DH] for h in range(SA_H)], axis=0)

    def update(kb, vb, flag, col_head):
        n = kb.shape[0]
        row_head = jnp.right_shift(lax.broadcasted_iota(jnp.int32, (nrow, n), 0), SAMPLE_ROWS_LOG2)
        keep = jnp.where(row_head == col_head, jnp.concatenate([flag] * SA_H, axis=0), 0.0) > 0.5
        s = lax.dot_general(qs, kb, _NT, preferred_element_type=F32)
        m_old = m_s[:, 0:1]
        m_new = jnp.maximum(m_old, _reduce(jnp.where(keep, s, floor), 1, "max"))
        alpha = jnp.exp(m_old - m_new)
        p = jnp.where(keep, jnp.exp(s - m_new), 0.0)
        l_s[...] = jnp.broadcast_to(alpha * l_s[:, 0:1] + _reduce(p, 1, "sum"), (nrow, LANES))
        acc_s[...] = alpha * acc_s[...] + jnp.dot(p.astype(BF16), vb, preferred_element_type=F32)
        m_s[...] = jnp.broadcast_to(m_new, (nrow, LANES))

    kp = jnp.concatenate([r[0] for r in k_refs], axis=0).astype(BF16)
    vp = jnp.concatenate([r[0] for r in v_refs], axis=0).astype(BF16)
    n = kp.shape[0]
    update(kp, vp, selx_ref[0], jnp.bitwise_and(lax.broadcasted_iota(jnp.int32, (nrow, n), 1), SA_H - 1))

    @pl.when(g == pl.num_programs(1) - 1)
    def _():
        zpad = jnp.zeros((LANES - rws, SA_DH), F32)

        def head_major(ref):
            return jnp.concatenate(
                [part for h in range(SA_H) for part in (ref[:, h * SA_DH:(h + 1) * SA_DH], zpad)],
                axis=0).astype(BF16)

        n_t = SA_H * LANES
        flag = jnp.concatenate([seltail_ref[0]] * SA_H, axis=1)
        col_head = jnp.right_shift(lax.broadcasted_iota(jnp.int32, (nrow, n_t), 1), LANES_LOG2)
        update(head_major(knew_ref), head_major(vnew_ref), flag, col_head)
        for h in range(SA_H):
            hs = slice(h * SA_DH, (h + 1) * SA_DH)
            rs = slice(h * rws, (h + 1) * rws)
            o_ref[:, hs] = (acc_s[rs, :] / l_s[rs, 0:1] * _silu(z_ref[:, hs])).astype(BF16)


def _dsa_attend(page_table, saq, saz, k_new, v_new, selx, seltail, cache_k, cache_v):
    nreq, n_pages = page_table.shape
    npg = min(8, n_pages)
    prow = PAGE_SIZE * SA_H
    req = lambda b, g, pt: (b, 0)

    def page_map(i):
        return lambda b, g, pt: (pt[b, g * npg + i], 0, 0)

    page_specs = [pl.BlockSpec((1, prow, SA_DH), page_map(i)) for i in range(npg)]
    grid_spec = pltpu.PrefetchScalarGridSpec(
        num_scalar_prefetch=1,
        grid=(nreq, n_pages // npg),
        in_specs=[pl.BlockSpec((SAMPLE_ROWS, SA_W), req), pl.BlockSpec((SAMPLE_ROWS, SA_W), req),
                  pl.BlockSpec((SAMPLE_ROWS, SA_W), req), pl.BlockSpec((SAMPLE_ROWS, SA_W), req),
                  pl.BlockSpec((1, SAMPLE_ROWS, npg * prow), lambda b, g, pt: (b, 0, g)),
                  pl.BlockSpec((1, SAMPLE_ROWS, LANES), lambda b, g, pt: (b, 0, 0))]
                 + page_specs + page_specs,
        out_specs=pl.BlockSpec((SAMPLE_ROWS, SA_W), req),
        scratch_shapes=[pltpu.VMEM((SA_H * SAMPLE_ROWS, LANES), F32), pltpu.VMEM((SA_H * SAMPLE_ROWS, LANES), F32),
                        pltpu.VMEM((SA_H * SAMPLE_ROWS, SA_DH), F32)],
    )
    return pl.pallas_call(
        functools.partial(_att_kernel, npg=npg),
        out_shape=jax.ShapeDtypeStruct((nreq * SAMPLE_ROWS, SA_W), BF16),
        grid_spec=grid_spec,
        compiler_params=_cparams(("arbitrary", "arbitrary")),
        name="dsa_att",
    )(page_table, saq, saz, k_new, v_new, selx, seltail, *([cache_k] * npg), *([cache_v] * npg))


def _memattn_kernel(q_ref, z_ref, mk_ref, mv_ref, o_ref):
    mk = mk_ref[...].astype(BF16)
    mv = mv_ref[...].astype(BF16)
    for h in range(MEM_H):
        hs = slice(h * MEM_DH, (h + 1) * MEM_DH)
        s = lax.dot_general(q_ref[:, hs], mk[:, hs], _NT, preferred_element_type=F32)
        p = jnp.exp(s - jnp.max(s, axis=1, keepdims=True))
        l = jnp.sum(p, axis=1, keepdims=True)
        o = jnp.dot(p.astype(BF16), mv[:, hs], preferred_element_type=F32) / l
        o_ref[:, hs] = (o * _silu(z_ref[:, hs])).astype(BF16)


def _memattn(memq, memz, mk, mv, nb, tq):
    rows = memq.shape[0]
    n_mem = mk.shape[0] // nb
    nq = rows // (nb * tq)
    qblk = lambda b, i: (b * nq + i, 0)
    per_b = lambda b, i: (b, 0)
    return pl.pallas_call(
        _memattn_kernel,
        out_shape=jax.ShapeDtypeStruct((rows, MEM_W), BF16),
        grid=(nb, nq),
        in_specs=[pl.BlockSpec((tq, MEM_W), qblk), pl.BlockSpec((tq, MEM_W), qblk),
                  pl.BlockSpec((n_mem, MEM_W), per_b), pl.BlockSpec((n_mem, MEM_W), per_b)],
        out_specs=pl.BlockSpec((tq, MEM_W), qblk),
        compiler_params=_cparams(("arbitrary", "arbitrary")),
        name="memattn",
    )(memq, memz, mk, mv)


def _mixout_kernel(x_ref, a_ref, b_ref, c_ref, wa_ref, wb_ref, wc_ref, g_ref, y_ref):
    acc = (jnp.dot(a_ref[...], wa_ref[...], preferred_element_type=F32)
           + jnp.dot(b_ref[...], wb_ref[...], preferred_element_type=F32)
           + jnp.dot(c_ref[...], wc_ref[...], preferred_element_type=F32))
    y = acc * lax.rsqrt(jnp.mean(acc * acc, axis=-1, keepdims=True) + EPS) * g_ref[...]
    y_ref[...] = x_ref[...] + y


def _mixout(x2d, a, b, c, w_out, g_post, tm):
    rows = x2d.shape[0]
    wb16 = w_out.astype(BF16)
    row = lambda i: (i, 0)
    const = lambda i: (0, 0)
    return pl.pallas_call(
        _mixout_kernel,
        out_shape=jax.ShapeDtypeStruct((rows, D_MODEL), F32),
        grid=(rows // tm,),
        in_specs=[pl.BlockSpec((tm, D_MODEL), row), pl.BlockSpec((tm, ML_W), row),
                  pl.BlockSpec((tm, SA_W), row), pl.BlockSpec((tm, MEM_W), row),
                  pl.BlockSpec((ML_W, D_MODEL), const), pl.BlockSpec((SA_W, D_MODEL), const),
                  pl.BlockSpec((MEM_W, D_MODEL), const), pl.BlockSpec((1, D_MODEL), const)],
        out_specs=pl.BlockSpec((tm, D_MODEL), row),
        compiler_params=_cparams(("arbitrary",)),
        name="mixout",
    )(x2d, a, b, c, wb16[:ML_W], wb16[ML_W:ML_W + SA_W], wb16[ML_W + SA_W:], g_post.reshape(1, D_MODEL))


def _layer(x_p, x_s, st_c, st_n, st_m, c_k, c_v, c_kidx, c_mk, c_mv, page_table, mem_prompt,
           g_pre, w_in, b_gates, g_head, w_mem_k, w_mem_v, g_mem, w_out, g_post):
    nb, seq, _ = x_p.shape
    nreq, t_dec, _ = x_s.shape
    n_mem = mem_prompt.shape[1]
    n_past = page_table.shape[1] * PAGE_SIZE
    w_main, w_small, w_small_t = _relayout_w_in(w_in)

    tm = min(512, seq)
    tabs = _rope_tables(jnp.arange(seq, dtype=jnp.int32))
    (qkv, oz, saq, k, v, saz, memq, memz, idxq, kidx, small, small_t) = _project(
        x_p.reshape(nb * seq, D_MODEL), g_pre, w_main, w_small, w_small_t, tabs, tm)
    c0 = jnp.zeros((nb, ML_H, ML_DH, ML_DH), F32)
    n0 = jnp.zeros((nb, ML_H, ML_DH), F32)
    m0 = jnp.zeros((nb, ML_H), F32)
    a_p, p_c, p_n, p_m = _mlstm(qkv, oz, small, small_t, b_gates, g_head, c0, n0, m0, nb, min(256, seq), 0)
    b_p = _dsa_prompt(saq, saz, idxq, small_t, k, v, kidx, nb, seq)
    mk, mv = _memkv(mem_prompt.reshape(nb * n_mem, D_MODEL), g_mem, w_mem_k, w_mem_v, n_mem)
    c_p = _memattn(memq, memz, mk, mv, nb, min(256, seq))
    y_p = _mixout(x_p.reshape(nb * seq, D_MODEL), a_p, b_p, c_p, w_out, g_post, tm).reshape(nb, seq, D_MODEL)

    rws = SAMPLE_ROWS
    n_padrow = rws - t_dec
    xs_pad = jnp.concatenate([jnp.zeros((nreq, n_padrow, D_MODEL), F32), x_s], axis=1).reshape(nreq * rws, D_MODEL)
    pos_s = jnp.tile(jnp.concatenate([jnp.zeros((n_padrow,), jnp.int32),
                                      n_past + jnp.arange(t_dec, dtype=jnp.int32)]), nreq)
    tabs_s = _rope_tables(pos_s)
    (qkv_s, oz_s, saq_s, k_s, v_s, saz_s, memq_s, memz_s, idxq_s, kidx_s, small_s, small_t_s) = _project(
        xs_pad, g_pre, w_main, w_small, w_small_t, tabs_s, nreq * rws)
    a_s, s_c, s_n, s_m = _mlstm(qkv_s, oz_s, small_s, small_t_s, b_gates, g_head, st_c, st_n, st_m,
                                nreq, rws, n_padrow)
    n_sel = min(TOPK_MAX, (n_past + t_dec) // 4)
    selx, seltail = _dsa_select(page_table, idxq_s, small_s, kidx_s, jnp.swapaxes(c_kidx, 1, 2), n_sel, t_dec)
    b_s = _dsa_attend(page_table, saq_s, saz_s, k_s, v_s, selx, seltail,
                      c_k.reshape(c_k.shape[0], PAGE_SIZE * SA_H, SA_DH),
                      c_v.reshape(c_v.shape[0], PAGE_SIZE * SA_H, SA_DH))
    c_s = _memattn(memq_s, memz_s, c_mk.reshape(nreq * n_mem, MEM_W), c_mv.reshape(nreq * n_mem, MEM_W), nreq, rws)
    y_s = _mixout(xs_pad, a_s, b_s, c_s, w_out, g_post, nreq * rws)

    def real(a2d):
        return a2d.reshape(nreq, rws, -1)[:, n_padrow:]

    new = (p_c, p_n, p_m,
           k.reshape(nb, seq, SA_H, SA_DH), v.reshape(nb, seq, SA_H, SA_DH), kidx.reshape(nb, seq, IDX_D),
           mk.reshape(nb, n_mem, MEM_H, MEM_DH), mv.reshape(nb, n_mem, MEM_H, MEM_DH),
           s_c, s_n, s_m,
           real(k_s).reshape(nreq, t_dec, SA_H, SA_DH), real(v_s).reshape(nreq, t_dec, SA_H, SA_DH), real(kidx_s))
    return y_p, real(y_s), new


def kernel(x_prompt, x_sample, state_mlstm_C, state_mlstm_n, state_mlstm_m, cache_k, cache_v, cache_kidx,
           cache_mem_k, cache_mem_v, page_table, mem_prompt, g_pre, w_in, b_gates, g_head, w_mem_k, w_mem_v,
           g_mem, w_out, g_post):
    xp, xs = x_prompt, x_sample
    per_layer = []
    for l in range(w_in.shape[0]):
        xp, xs, new = _layer(xp, xs, state_mlstm_C[l], state_mlstm_n[l], state_mlstm_m[l],
                             cache_k[l], cache_v[l], cache_kidx[l], cache_mem_k[l], cache_mem_v[l],
                             page_table, mem_prompt, g_pre[l], w_in[l], b_gates[l], g_head[l],
                             w_mem_k[l], w_mem_v[l], g_mem[l], w_out[l], g_post[l])
        per_layer.append(new)
    stacked = [jnp.stack(a) for a in zip(*per_layer)]
    return (xp, xs, *stacked)
```

```python
import functools

import jax
import jax.numpy as jnp
from jax import lax
from jax.experimental import pallas as pl
from jax.experimental.pallas import tpu as pltpu

F32 = jnp.float32
BF16 = jnp.bfloat16

D_MODEL = 2048
ML_H = 4
ML_W = D_MODEL // 2
ML_DH = ML_W // ML_H
SA_H = 4
SA_W = D_MODEL // 4
SA_DH = SA_W // SA_H
SA_H_LOG2 = 2
MEM_H = 4
MEM_W = D_MODEL // 4
MEM_DH = MEM_W // MEM_H
IDX_H = 8
IDX_D = 64
IDX_SCALE = (IDX_H * IDX_D) ** -0.5
TOPK_MAX = 256
ROPE_THETA = 10000.0
EPS = 1e-6
PAGE_SIZE = 128

LANES = 128
CB = 512
SMALL_IG = 64
SMALL_LF = 68
SMALL_W = 72
LANES_LOG2 = 7
SAMPLE_ROWS = 16
SAMPLE_ROWS_LOG2 = 4
VMEM_LIMIT = 56 * 1024 * 1024
NEG_INF = float("-inf")
POS_INF = float("inf")

_NT = (((1,), (1,)), ((), ()))
_TN = (((0,), (0,)), ((), ()))


def _cparams(sem):
    return pltpu.CompilerParams(dimension_semantics=sem, vmem_limit_bytes=VMEM_LIMIT)


def _sigmoid(x):
    return 1.0 / (1.0 + jnp.exp(-x))


def _silu(x):
    return x * _sigmoid(x)


def _log_sigmoid(x):
    return jnp.minimum(x, 0.0) - jnp.log1p(jnp.exp(-jnp.abs(x)))


def _split3(x):
    hi = x.astype(BF16)
    r = x - hi.astype(F32)
    mid = r.astype(BF16)
    lo = (r - mid.astype(F32)).astype(BF16)
    return hi, mid, lo


def _rope128(x, cos, sin_signed):
    return x * cos + pltpu.roll(x, 64, 1) * sin_signed


def _rope64(x, cos, sin_signed):
    lane = lax.broadcasted_iota(jnp.int32, x.shape, 1)
    first_half = (lane % 64) < 32
    partner = jnp.where(first_half, pltpu.roll(x, 96, 1), pltpu.roll(x, 32, 1))
    return x * cos + partner * sin_signed


def _normed(x_ref, g_ref):
    x = x_ref[...]
    return (x * lax.rsqrt(jnp.mean(x * x, axis=-1, keepdims=True) + EPS) * g_ref[...]).astype(BF16)


def _proj_ml_kernel(x_ref, g_ref, w_ref, qkv_ref, oz_ref, u_ref):
    u_ref[...] = _normed(x_ref, g_ref)
    n_qkv = 3 * ML_W // CB
    for cb in range(5 * ML_W // CB):
        acc = jnp.dot(u_ref[...], w_ref[:, cb * CB:(cb + 1) * CB], preferred_element_type=F32)
        if cb < n_qkv:
            qkv_ref[:, cb * CB:(cb + 1) * CB] = acc.astype(BF16)
        else:
            oz_ref[:, (cb - n_qkv) * CB:(cb - n_qkv + 1) * CB] = acc


def _proj_rest_kernel(x_ref, g_ref, w_ref, ws_ref, wst_ref, c128_ref, s128_ref, c64_ref, s64_ref,
                      saq_ref, k_ref, v_ref, saz_ref, memq_ref, memz_ref, idxq_ref,
                      kidx_ref, small_ref, smallt_ref, u_ref):
    u_ref[...] = _normed(x_ref, g_ref)
    sm = jnp.dot(u_ref[...], ws_ref[...], preferred_element_type=F32)
    small_ref[...] = sm
    kidx_ref[...] = _rope64(sm, c64_ref[...], s64_ref[...])[:, :IDX_D]
    smallt_ref[...] = lax.dot_general(wst_ref[...], u_ref[...], _NT, preferred_element_type=F32)

    def block(cb):
        return jnp.dot(u_ref[...], w_ref[:, cb * CB:(cb + 1) * CB], preferred_element_type=F32)

    def rope_heads(acc, fn, cos_ref, sin_ref):
        return jnp.concatenate(
            [fn(acc[:, h * LANES:(h + 1) * LANES], cos_ref[...], sin_ref[...]) for h in range(CB // LANES)], axis=1)

    saq_ref[...] = (rope_heads(block(0), _rope128, c128_ref, s128_ref) * (SA_DH ** -0.5)).astype(BF16)
    k_ref[...] = rope_heads(block(1), _rope128, c128_ref, s128_ref)
    v_ref[...] = block(2)
    saz_ref[...] = block(3)
    memq_ref[...] = (block(4) * (MEM_DH ** -0.5)).astype(BF16)
    memz_ref[...] = block(5)
    idxq_ref[...] = rope_heads(block(6), _rope64, c64_ref, s64_ref).astype(BF16)


def _project(x2d, g_pre, weights, tabs, tm):
    w_ml, w_rest, w_small, w_small_t = weights
    rows = x2d.shape[0]
    c128, s128, c64, s64 = tabs
    ntab = c128.shape[0] // tm
    n_ml = 5 * ML_W
    row_only = lambda i: (i, 0)
    tab_map = lambda i: (i % ntab, 0)
    const = lambda i: (0, 0)
    resident = pl.Buffered(1)
    g2d = g_pre.reshape(1, D_MODEL)

    qkv, oz = pl.pallas_call(
        _proj_ml_kernel,
        out_shape=(jax.ShapeDtypeStruct((rows, 3 * ML_W), BF16),
                   jax.ShapeDtypeStruct((rows, 2 * ML_W), F32)),
        grid=(rows // tm,),
        in_specs=[pl.BlockSpec((tm, D_MODEL), row_only),
                  pl.BlockSpec((1, D_MODEL), const),
                  pl.BlockSpec((D_MODEL, n_ml), const, pipeline_mode=resident)],
        out_specs=(pl.BlockSpec((tm, 3 * ML_W), row_only), pl.BlockSpec((tm, 2 * ML_W), row_only)),
        scratch_shapes=[pltpu.VMEM((tm, D_MODEL), BF16)],
        compiler_params=_cparams(("arbitrary",)),
        name="proj_ml",
    )(x2d, g2d, w_ml)

    out_shape = (
        jax.ShapeDtypeStruct((rows, SA_W), BF16),
        jax.ShapeDtypeStruct((rows, SA_W), F32),
        jax.ShapeDtypeStruct((rows, SA_W), F32),
        jax.ShapeDtypeStruct((rows, SA_W), F32),
        jax.ShapeDtypeStruct((rows, MEM_W), BF16),
        jax.ShapeDtypeStruct((rows, MEM_W), F32),
        jax.ShapeDtypeStruct((rows, IDX_H * IDX_D), BF16),
        jax.ShapeDtypeStruct((rows, IDX_D), F32),
        jax.ShapeDtypeStruct((rows, LANES), F32),
        jax.ShapeDtypeStruct((LANES, rows), F32),
    )
    out_specs = (
        pl.BlockSpec((tm, CB), row_only),
        pl.BlockSpec((tm, CB), row_only),
        pl.BlockSpec((tm, CB), row_only),
        pl.BlockSpec((tm, CB), row_only),
        pl.BlockSpec((tm, CB), row_only),
        pl.BlockSpec((tm, CB), row_only),
        pl.BlockSpec((tm, CB), row_only),
        pl.BlockSpec((tm, IDX_D), row_only),
        pl.BlockSpec((tm, LANES), row_only),
        pl.BlockSpec((LANES, tm), lambda i: (0, i)),
    )
    n_rest = w_rest.shape[1]
    in_specs = [
        pl.BlockSpec((tm, D_MODEL), row_only),
        pl.BlockSpec((1, D_MODEL), const),
        pl.BlockSpec((D_MODEL, n_rest), const, pipeline_mode=resident),
        pl.BlockSpec((D_MODEL, LANES), const, pipeline_mode=resident),
        pl.BlockSpec((LANES, D_MODEL), const, pipeline_mode=resident),
        pl.BlockSpec((tm, LANES), tab_map),
        pl.BlockSpec((tm, LANES), tab_map),
        pl.BlockSpec((tm, LANES), tab_map),
        pl.BlockSpec((tm, LANES), tab_map),
    ]
    rest = pl.pallas_call(
        _proj_rest_kernel,
        out_shape=out_shape,
        grid=(rows // tm,),
        in_specs=in_specs,
        out_specs=out_specs,
        scratch_shapes=[pltpu.VMEM((tm, D_MODEL), BF16)],
        compiler_params=_cparams(("arbitrary",)),
        name="proj_rest",
    )(x2d, g2d, w_rest, w_small, w_small_t, c128, s128, c64, s64)
    return (qkv, oz, *rest)


def _rope_tables(pos):
    def tab(half):
        inv = ROPE_THETA ** (-jnp.arange(half, dtype=F32) / half)
        ang = pos.astype(F32)[:, None] * inv[None, :]
        return jnp.cos(ang), jnp.sin(ang)

    c, s = tab(SA_DH // 2)
    c128 = jnp.concatenate([c, c], axis=1)
    s128 = jnp.concatenate([-s, s], axis=1)
    c, s = tab(IDX_D // 2)
    c64 = jnp.concatenate([c, c, c, c], axis=1)
    s64 = jnp.concatenate([-s, s, -s, s], axis=1)
    return c128, s128, c64, s64


def _relayout_w_in(w_in):
    off = {}
    o = 0
    for name, w in (('ml_q', ML_W), ('ml_k', ML_W), ('ml_v', ML_W), ('ml_o', ML_W), ('ml_z', ML_W),
                    ('ml_i', ML_H), ('ml_f', ML_H), ('sa_q', SA_W), ('sa_k', SA_W), ('sa_v', SA_W),
                    ('sa_z', SA_W), ('idx_q', IDX_H * IDX_D), ('idx_k', IDX_D), ('idx_w', IDX_H),
                    ('mem_q', MEM_W), ('mem_z', MEM_W)):
        off[name] = (o, w)
        o += w

    def col(name):
        a, w = off[name]
        return w_in[:, a:a + w]

    w_ml = jnp.concatenate([
        col('ml_q'), col('ml_k') * (ML_DH ** -0.5), col('ml_v'), col('ml_o'), col('ml_z')], axis=1).astype(BF16)
    w_rest = jnp.concatenate([
        col('sa_q'), col('sa_k'), col('sa_v'), col('sa_z'), col('mem_q'), col('mem_z'), col('idx_q'),
    ], axis=1).astype(BF16)
    small = jnp.concatenate([
        col('idx_k'), col('ml_i'), col('ml_f'), col('idx_w'),
        jnp.zeros((D_MODEL, LANES - IDX_D - 2 * ML_H - IDX_H), F32)], axis=1).astype(BF16)
    return (w_ml, w_rest, small, small.T)


def _memkv_kernel(m_ref, g_ref, wk_ref, wv_ref, k_ref, v_ref):
    x = m_ref[...]
    u = (x * lax.rsqrt(jnp.mean(x * x, axis=-1, keepdims=True) + EPS) * g_ref[...]).astype(BF16)
    k_ref[...] = jnp.dot(u, wk_ref[...], preferred_element_type=F32)
    v_ref[...] = jnp.dot(u, wv_ref[...], preferred_element_type=F32)


def _memkv(mem2d, g_mem, wk, wv, n_mem):
    rows = mem2d.shape[0]
    row = lambda i: (i, 0)
    const = lambda i: (0, 0)
    return pl.pallas_call(
        _memkv_kernel,
        out_shape=(jax.ShapeDtypeStruct((rows, MEM_W), F32), jax.ShapeDtypeStruct((rows, MEM_W), F32)),
        grid=(rows // n_mem,),
        in_specs=[pl.BlockSpec((n_mem, D_MODEL), row), pl.BlockSpec((1, D_MODEL), const),
                  pl.BlockSpec((D_MODEL, MEM_W), const), pl.BlockSpec((D_MODEL, MEM_W), const)],
        out_specs=(pl.BlockSpec((n_mem, MEM_W), row), pl.BlockSpec((n_mem, MEM_W), row)),
        compiler_params=_cparams(("arbitrary",)),
        name="memkv",
    )(mem2d, g_mem.reshape(1, D_MODEL), wk.astype(BF16), wv.astype(BF16))


def _mlstm_kernel(qkv_ref, oz_ref, gc_ref, gt_ref, bcol_ref, brow_ref, gh_ref, c0_ref, n0_ref, m0_ref,
                  a_ref, cout_ref, nout_ref, mout_ref, c_s, n_s, m_s, *, c, n_pad):
    ci = pl.program_id(1)

    @pl.when(ci == 0)
    def _():
        c_s[...] = c0_ref[0]
        n_s[...] = n0_ref[0]
        m_s[...] = m0_ref[0]

    ri = lax.broadcasted_iota(jnp.int32, (c, c), 0)
    cj = lax.broadcasted_iota(jnp.int32, (c, c), 1)
    causal = cj <= ri
    tri = jnp.where(causal, 1.0, 0.0).astype(BF16)
    tri_t = jnp.where(ri <= cj, 1.0, 0.0).astype(BF16)

    g_c = gc_ref[...] + bcol_ref[...]
    pad_c = lax.broadcasted_iota(jnp.int32, (c, LANES), 0) < n_pad
    ig_c = jnp.where(pad_c, NEG_INF, g_c)
    lf_c = jnp.where(pad_c, 0.0, _log_sigmoid(g_c))
    b_c = sum(jnp.dot(tri, p, preferred_element_type=F32) for p in _split3(lf_c))
    g_r = gt_ref[0] + brow_ref[...]
    pad_r = lax.broadcasted_iota(jnp.int32, (SAMPLE_ROWS, c), 1) < n_pad
    ig_r = jnp.where(pad_r, NEG_INF, g_r)
    lf_r = jnp.where(pad_r, 0.0, _log_sigmoid(g_r))
    b_r = sum(jnp.dot(p, tri_t, preferred_element_type=F32) for p in _split3(lf_r))

    for h in range(ML_H):
        hs = slice(h * ML_DH, (h + 1) * ML_DH)
        m_prev = m_s[h:h + 1, 0:1]
        b_t = b_c[:, SMALL_LF + h:SMALL_LF + h + 1]
        igc = ig_c[:, SMALL_IG + h:SMALL_IG + h + 1]
        b_s = b_r[ML_H + h:ML_H + h + 1, :]
        igr = ig_r[h:h + 1, :]
        a = jnp.where(causal, b_t - b_s + igr, NEG_INF)
        bm = b_t + m_prev
        m_t = jnp.maximum(bm, jnp.max(a, axis=1, keepdims=True))
        inter = jnp.exp(bm - m_t)
        dmat = jnp.exp(a - m_t)
        q = qkv_ref[:, h * ML_DH:(h + 1) * ML_DH]
        k = qkv_ref[:, ML_W + h * ML_DH:ML_W + (h + 1) * ML_DH]
        v = qkv_ref[:, 2 * ML_W + h * ML_DH:2 * ML_W + (h + 1) * ML_DH]
        s = lax.dot_general(q, k, _NT, preferred_element_type=F32) * dmat
        c_h = c_s[h]
        n_h = n_s[h:h + 1, :]
        num = (jnp.dot(s.astype(BF16), v, preferred_element_type=F32)
               + inter * jnp.dot(q, c_h.astype(BF16), preferred_element_type=F32))
        qn = (jnp.sum(s, axis=1, keepdims=True)
              + inter * jnp.sum(q.astype(F32) * n_h, axis=1, keepdims=True))
        hh = num / jnp.maximum(jnp.abs(qn), jnp.exp(-m_t))
        hh = hh * lax.rsqrt(jnp.mean(hh * hh, axis=1, keepdims=True) + EPS)
        o = oz_ref[:, h * ML_DH:(h + 1) * ML_DH]
        z = oz_ref[:, ML_W + h * ML_DH:ML_W + (h + 1) * ML_DH]
        a_ref[:, hs] = (hh * gh_ref[:, hs] * _sigmoid(o) * _silu(z)).astype(BF16)

        m_new = m_t[c - 1:c, :]
        b_last = b_t[c - 1:c, :]
        w_end = jnp.exp(b_last - b_t + igc - m_new)
        decay = jnp.exp(b_last + m_prev - m_new)
        kw = k.astype(F32) * w_end
        c_s[h] = decay * c_h + lax.dot_general(kw.astype(BF16), v, _TN, preferred_element_type=F32)
        n_s[h:h + 1, :] = decay * n_h + jnp.sum(kw, axis=0, keepdims=True)
        m_s[h:h + 1, :] = jnp.broadcast_to(m_new, (1, LANES))

    @pl.when(ci == pl.num_programs(1) - 1)
    def _():
        cout_ref[0] = c_s[...]
        nout_ref[0] = n_s[...]
        mout_ref[0] = m_s[...]


def _mlstm(qkv, oz, small, small_t, b_gates, g_head, c0, n0, m0, nb, c, n_pad):
    rows = qkv.shape[0]
    nc = rows // (nb * c)
    bias_col = jnp.zeros((1, LANES), F32).at[0, SMALL_IG:SMALL_IG + 2 * ML_H].set(b_gates)
    bias_row = jnp.zeros((SAMPLE_ROWS, 1), F32).at[:2 * ML_H, 0].set(b_gates)
    m0b = jnp.zeros((nb, 8, LANES), F32).at[:, :ML_H, :].set(jnp.broadcast_to(m0[:, :, None], (nb, ML_H, LANES)))
    rowblk = lambda b, i: (b * nc + i, 0)
    const = lambda b, i: (0, 0)
    gates_t = small_t[SMALL_IG:SMALL_IG + SAMPLE_ROWS].reshape(SAMPLE_ROWS, rows // c, c).transpose(1, 0, 2)
    out_shape = (
        jax.ShapeDtypeStruct((rows, ML_W), BF16),
        jax.ShapeDtypeStruct((nb, ML_H, ML_DH, ML_DH), F32),
        jax.ShapeDtypeStruct((nb, ML_H, ML_DH), F32),
        jax.ShapeDtypeStruct((nb, 8, LANES), F32),
    )
    st4 = lambda b, i: (b, 0, 0, 0)
    st3 = lambda b, i: (b, 0, 0)
    a, c_out, n_out, m_out = pl.pallas_call(
        functools.partial(_mlstm_kernel, c=c, n_pad=n_pad),
        out_shape=out_shape,
        grid=(nb, nc),
        in_specs=[
            pl.BlockSpec((c, 3 * ML_W), rowblk),
            pl.BlockSpec((c, 2 * ML_W), rowblk),
            pl.BlockSpec((c, LANES), rowblk),
            pl.BlockSpec((1, SAMPLE_ROWS, c), lambda b, i: (b * nc + i, 0, 0)),
            pl.BlockSpec((1, LANES), const),
            pl.BlockSpec((SAMPLE_ROWS, 1), const),
            pl.BlockSpec((1, ML_W), const),
            pl.BlockSpec((1, ML_H, ML_DH, ML_DH), st4),
            pl.BlockSpec((1, ML_H, ML_DH), st3),
            pl.BlockSpec((1, 8, LANES), st3),
        ],
        out_specs=(
            pl.BlockSpec((c, ML_W), rowblk),
            pl.BlockSpec((1, ML_H, ML_DH, ML_DH), st4),
            pl.BlockSpec((1, ML_H, ML_DH), st3),
            pl.BlockSpec((1, 8, LANES), st3),
        ),
        scratch_shapes=[pltpu.VMEM((ML_H, ML_DH, ML_DH), F32), pltpu.VMEM((ML_H, ML_DH), F32),
                        pltpu.VMEM((8, LANES), F32)],
        compiler_params=_cparams(("arbitrary", "arbitrary")),
        name="mlstm",
    )(qkv, oz, small, gates_t, bias_col, bias_row, g_head.reshape(1, ML_W), c0, n0, m0b)
    return a, c_out, n_out, m_out[:, :ML_H, 0]


_REDUCERS = {"sum": (jnp.sum, jnp.add), "max": (jnp.max, jnp.maximum), "min": (jnp.min, jnp.minimum)}
REDUCE_CHAINS = 8


def _reduce(x, axis, op):
    fn, combine = _REDUCERS[op]
    unit = 8 if axis == 0 else LANES
    n = x.shape[axis]
    units = n // unit
    if n % unit or units < 2 * REDUCE_CHAINS:
        return fn(x, axis=axis, keepdims=True)
    base, rem = divmod(units, REDUCE_CHAINS)
    parts, start = [], 0
    for i in range(REDUCE_CHAINS):
        size = (base + (1 if i < rem else 0)) * unit
        piece = x[start:start + size] if axis == 0 else x[:, start:start + size]
        parts.append(fn(piece, axis=axis, keepdims=True))
        start += size
    while len(parts) > 1:
        parts = [combine(parts[i], parts[i + 1]) for i in range(0, len(parts), 2)]
    return parts[0]


def _count(pred, axis):
    return _reduce(jnp.where(pred, 1.0, 0.0), axis, "sum")


def _kth_largest(x_ref, k, axis, n_bisect):
    kf = float(k)
    x = x_ref[...]
    hi = _reduce(x, axis, "max")
    lo = _reduce(jnp.where(x == NEG_INF, POS_INF, x), axis, "min")

    def bisect(_, carry):
        lo, hi = carry
        mid = 0.5 * (lo + hi)
        ge = _count(x_ref[...] >= mid, axis) >= kf
        return jnp.where(ge, mid, lo), jnp.where(ge, hi, mid)

    lo, hi = lax.fori_loop(0, n_bisect, bisect, (lo, hi))

    def cond(st):
        return st[4] < 0.5

    def body(st):
        lo, strict, thr, done, _ = st
        xx = x_ref[...]
        above = _reduce(jnp.where(xx > lo, xx, POS_INF), axis, "min")
        at_lo = (strict < 0.5) & (_count(xx == lo, axis) > 0.5)
        cmin = jnp.where(at_lo, lo, above)
        fin = (_count(xx > cmin, axis) < kf) | (cmin == POS_INF)
        active = done < 0.5
        thr = jnp.where(active, cmin, thr)
        lo = jnp.where(active, cmin, lo)
        done = jnp.where(fin, 1.0, done)
        return lo, jnp.ones_like(strict), thr, done, jnp.min(done)

    zeros = jnp.zeros_like(lo)
    _, _, thr, _, _ = lax.while_loop(cond, body, (lo, zeros, lo, zeros, jnp.float32(0.0)))
    return thr


def _dsa_kernel(q_ref, z_ref, idxq_ref, wt_ref, k_ref, v_ref, kidx_ref, o_ref,
                kb_s, vt_s, kib_s, x_s, sel_s, *, n_keys, qb, n_sel, n_bisect, key_step):
    j = pl.program_id(1)

    @pl.when(j == 0)
    def _():
        kb_s[...] = k_ref[...].astype(BF16)
        vt_s[...] = v_ref[...].T.astype(BF16)
        kib_s[...] = kidx_ref[...].astype(BF16)

    def attend(nk):
        xs = x_s.at[0:nk]
        ss = sel_s.at[0:nk]
        key = lax.broadcasted_iota(jnp.int32, (nk, qb), 0)
        qpos = j * qb + lax.broadcasted_iota(jnp.int32, (nk, qb), 1)
        valid = key <= qpos
        sc = jnp.zeros((nk, qb), F32)
        for h in range(IDX_H):
            d = lax.dot_general(kib_s[0:nk, :], idxq_ref[:, h * IDX_D:(h + 1) * IDX_D], _NT,
                                preferred_element_type=F32)
            sc = sc + jnp.maximum(d, 0.0) * (wt_ref[h:h + 1, :] * IDX_SCALE)
        xs[...] = jnp.where(valid, sc, NEG_INF)
        ss[...] = jnp.where(valid, 1.0, 0.0)

        @pl.when((j + 1) * qb > n_sel)
        def _():
            kf = float(n_sel)
            thr = _kth_largest(xs, n_sel, 0, n_bisect)
            x = xs[...]
            need = kf - _count(x > thr, 0)
            n_tie = _count(x == thr, 0)
            qrow = j * qb + lax.broadcasted_iota(jnp.int32, (1, qb), 1)
            small = (qrow + 1) <= n_sel
            ss[...] = jnp.where(small, jnp.where(valid, 1.0, 0.0), jnp.where(x >= thr, 1.0, 0.0))
            excess = jnp.max(jnp.where((n_tie > need) & jnp.logical_not(small), 1.0, 0.0))

            @pl.when(excess > 0.5)
            def _():
                tb = min(256, nk)
                r_i = lax.broadcasted_iota(jnp.int32, (tb, tb), 0)
                c_i = lax.broadcasted_iota(jnp.int32, (tb, tb), 1)
                lower = jnp.where(c_i < r_i, 1.0, 0.0).astype(BF16)
                carry = jnp.zeros((1, qb), F32)
                for blk in range(nk // tb):
                    rows = slice(blk * tb, (blk + 1) * tb)
                    xb = x_s[rows, :]
                    tie = jnp.where(xb == thr, 1.0, 0.0)
                    rank = jnp.dot(lower, tie.astype(BF16), preferred_element_type=F32) + carry
                    keep = (xb > thr) | ((xb == thr) & (rank < need))
                    keyb = blk * tb + lax.broadcasted_iota(jnp.int32, (tb, qb), 0)
                    qposb = j * qb + lax.broadcasted_iota(jnp.int32, (tb, qb), 1)
                    smallb = jnp.where(keyb <= qposb, 1.0, 0.0)
                    sel_s[rows, :] = jnp.where(small, smallb, jnp.where(keep, 1.0, 0.0))
                    carry = carry + jnp.sum(tie, axis=0, keepdims=True)

        sel = ss[...] > 0.5
        for h in range(SA_H):
            hs = slice(h * SA_DH, (h + 1) * SA_DH)
            st = lax.dot_general(kb_s[0:nk, hs], q_ref[:, hs], _NT, preferred_element_type=F32)
            st = jnp.where(sel, st, NEG_INF)
            mx = _reduce(st, 0, "max")
            p = jnp.exp(st - mx)
            l = _reduce(p, 0, "sum")
            ot = jnp.dot(vt_s[hs, 0:nk], p.astype(BF16), preferred_element_type=F32) / l
            o_ref[:, hs] = (ot.T * _silu(z_ref[:, hs])).astype(BF16)

    n_ext = n_keys // key_step
    for e in range(n_ext):
        nk = (e + 1) * key_step
        lo_j = e * key_step // qb
        hi_j = nk // qb

        @pl.when((j >= lo_j) & (j < hi_j))
        def _(nk=nk):
            attend(nk)


def _dsa_prompt(saq, saz, idxq, small_t, k, v, kidx, nb, seq):
    rows = saq.shape[0]
    qb = min(seq, 128)
    nq = seq // qb
    n_sel = min(TOPK_MAX, seq // 4)
    qblk = lambda b, j: (b * nq + j, 0)
    per_b = lambda b, j: (b, 0)
    wt_blk = SMALL_W // 8
    return pl.pallas_call(
        functools.partial(_dsa_kernel, n_keys=seq, qb=qb, n_sel=n_sel, n_bisect=20, key_step=min(512, seq)),
        out_shape=jax.ShapeDtypeStruct((rows, SA_W), BF16),
        grid=(nb, nq),
        in_specs=[
            pl.BlockSpec((qb, SA_W), qblk),
            pl.BlockSpec((qb, SA_W), qblk),
            pl.BlockSpec((qb, IDX_H * IDX_D), qblk),
            pl.BlockSpec((8, qb), lambda b, j: (wt_blk, b * nq + j)),
            pl.BlockSpec((seq, SA_W), per_b),
            pl.BlockSpec((seq, SA_W), per_b),
            pl.BlockSpec((seq, IDX_D), per_b),
        ],
        out_specs=pl.BlockSpec((qb, SA_W), qblk),
        scratch_shapes=[pltpu.VMEM((seq, SA_W), BF16), pltpu.VMEM((SA_W, seq), BF16),
                        pltpu.VMEM((seq, IDX_D), BF16), pltpu.VMEM((seq, qb), F32),
                        pltpu.VMEM((seq, qb), F32)],
        compiler_params=_cparams(("arbitrary", "arbitrary")),
        name="dsa",
    )(saq, saz, idxq, small_t, k, v, kidx)


def _sel_kernel(pt_ref, idxq_ref, small_ref, kinew_ref, *rest, npg, n_past, n_sel, n_real, n_bisect):
    page_refs = rest[:npg]
    selx_ref, seltail_ref, x_s, sel_s = rest[npg:]
    g = pl.program_id(1)
    rws = SAMPLE_ROWS
    pk = n_past + LANES
    gk = npg * PAGE_SIZE

    qs = jnp.concatenate([idxq_ref[:, h * IDX_D:(h + 1) * IDX_D] for h in range(IDX_H)], axis=0)
    small = small_ref[...]

    def scores(d):
        sc = jnp.zeros((rws, d.shape[1]), F32)
        for h in range(IDX_H):
            w = small[:, SMALL_W + h:SMALL_W + h + 1] * IDX_SCALE
            sc = sc + jnp.maximum(d[h * rws:(h + 1) * rws, :], 0.0) * w
        return sc

    kp_t = jnp.concatenate([r[0] for r in page_refs], axis=1).astype(BF16)
    x_s[:, pl.ds(pl.multiple_of(g * gk, LANES), gk)] = scores(jnp.dot(qs, kp_t, preferred_element_type=F32))

    @pl.when(g == pl.num_programs(1) - 1)
    def _():
        knew = jnp.concatenate([kinew_ref[...], jnp.zeros((LANES - rws, IDX_D), F32)], axis=0).astype(BF16)
        row = lax.broadcasted_iota(jnp.int32, (rws, LANES), 0)
        col = lax.broadcasted_iota(jnp.int32, (rws, LANES), 1)
        ok = (col >= rws - n_real) & (col < rws) & (col <= row)
        d_new = lax.dot_general(qs, knew, _NT, preferred_element_type=F32)
        x_s[:, n_past:pk] = jnp.where(ok, scores(d_new), NEG_INF)

        kf = float(n_sel)
        thr = _kth_largest(x_s, n_sel, 1, n_bisect)
        x = x_s[...]
        need = kf - _count(x > thr, 1)
        n_tie = _count(x == thr, 1)
        sel_s[...] = jnp.where(x >= thr, 1.0, 0.0)
        real = lax.broadcasted_iota(jnp.int32, (rws, 1), 0) >= rws - n_real
        excess = jnp.max(jnp.where((n_tie > need) & real, 1.0, 0.0))

        @pl.when(excess > 0.5)
        def _():
            r_i = lax.broadcasted_iota(jnp.int32, (LANES, LANES), 0)
            c_i = lax.broadcasted_iota(jnp.int32, (LANES, LANES), 1)
            upper = jnp.where(r_i < c_i, 1.0, 0.0).astype(BF16)

            def blk(i, carry):
                cols = pl.ds(pl.multiple_of(i * LANES, LANES), LANES)
                xb = x_s[:, cols]
                tie = jnp.where(xb == thr, 1.0, 0.0)
                rank = jnp.dot(tie.astype(BF16), upper, preferred_element_type=F32) + carry
                keep = (xb > thr) | ((xb == thr) & (rank < need))
                sel_s[:, cols] = jnp.where(keep, 1.0, 0.0)
                return carry + jnp.sum(tie, axis=1, keepdims=True)

            lax.fori_loop(0, pk // LANES, blk, jnp.zeros((rws, 1), F32))

        e_r = lax.broadcasted_iota(jnp.int32, (LANES, SA_H * LANES), 0)
        e_c = lax.broadcasted_iota(jnp.int32, (LANES, SA_H * LANES), 1)
        expand = jnp.where(jnp.right_shift(e_c, SA_H_LOG2) == e_r, 1.0, 0.0).astype(BF16)

        per_iter = 8 if (n_past // LANES) % 8 == 0 else 1

        def widen(i, carry):
            for u in range(per_iter):
                blk = i * per_iter + u
                cols = pl.ds(pl.multiple_of(blk * LANES, LANES), LANES)
                wide = pl.ds(pl.multiple_of(blk * SA_H * LANES, SA_H * LANES), SA_H * LANES)
                selx_ref[0, :, wide] = jnp.dot(sel_s[:, cols].astype(BF16), expand, preferred_element_type=F32)
            return carry

        lax.fori_loop(0, n_past // LANES // per_iter, widen, 0)
        seltail_ref[0] = sel_s[:, n_past:pk]


def _dsa_select(page_table, idxq, small, kidx_new, cache_kidx, n_sel, n_real):
    nreq, n_pages = page_table.shape
    n_past = n_pages * PAGE_SIZE
    pk = n_past + LANES
    npg = min(16, n_pages)
    req = lambda b, g, pt: (b, 0)

    def page_map(i):
        return lambda b, g, pt: (pt[b, g * npg + i], 0, 0)

    grid_spec = pltpu.PrefetchScalarGridSpec(
        num_scalar_prefetch=1,
        grid=(nreq, n_pages // npg),
        in_specs=[pl.BlockSpec((SAMPLE_ROWS, IDX_H * IDX_D), req),
                  pl.BlockSpec((SAMPLE_ROWS, LANES), req),
                  pl.BlockSpec((SAMPLE_ROWS, IDX_D), req)]
                 + [pl.BlockSpec((1, IDX_D, PAGE_SIZE), page_map(i)) for i in range(npg)],
        out_specs=(pl.BlockSpec((1, SAMPLE_ROWS, SA_H * n_past), lambda b, g, pt: (b, 0, 0)),
                   pl.BlockSpec((1, SAMPLE_ROWS, LANES), lambda b, g, pt: (b, 0, 0))),
        scratch_shapes=[pltpu.VMEM((SAMPLE_ROWS, pk), F32), pltpu.VMEM((SAMPLE_ROWS, pk), F32)],
    )
    return pl.pallas_call(
        functools.partial(_sel_kernel, npg=npg, n_past=n_past, n_sel=n_sel, n_real=n_real, n_bisect=20),
        out_shape=(jax.ShapeDtypeStruct((nreq, SAMPLE_ROWS, SA_H * n_past), F32),
                   jax.ShapeDtypeStruct((nreq, SAMPLE_ROWS, LANES), F32)),
        grid_spec=grid_spec,
        compiler_params=_cparams(("arbitrary", "arbitrary")),
        name="dsa_sel",
    )(page_table, idxq, small, kidx_new, *([cache_kidx] * npg))


def _att_kernel(pt_ref, q_ref, z_ref, knew_ref, vnew_ref, selx_ref, seltail_ref, *rest, npg):
    k_refs = rest[:npg]
    v_refs = rest[npg:2 * npg]
    o_ref = rest[2 * npg]
    m_s, l_s, acc_s = rest[2 * npg + 1:]
    g = pl.program_id(1)
    rws = SAMPLE_ROWS
    nrow = SA_H * rws
    floor = -1e30

    @pl.when(g == 0)
    def _():
        m_s[...] = jnp.full(m_s.shape, floor, F32)
        l_s[...] = jnp.zeros(l_s.shape, F32)
        acc_s[...] = jnp.zeros(acc_s.shape, F32)

    qs = jnp.concatenate([q_ref[:, h * SA_DH:(h + 1) * SA_DH] for h in range(SA_H)], axis=0)

    def update(kb, vb, flag, col_head):
        n = kb.shape[0]
        row_head = jnp.right_shift(lax.broadcasted_iota(jnp.int32, (nrow, n), 0), SAMPLE_ROWS_LOG2)
        keep = jnp.where(row_head == col_head, jnp.concatenate([flag] * SA_H, axis=0), 0.0) > 0.5
        s = lax.dot_general(qs, kb, _NT, preferred_element_type=F32)
        m_old = m_s[:, 0:1]
        m_new = jnp.maximum(m_old, _reduce(jnp.where(keep, s, floor), 1, "max"))
        alpha = jnp.exp(m_old - m_new)
        p = jnp.where(keep, jnp.exp(s - m_new), 0.0)
        l_s[...] = jnp.broadcast_to(alpha * l_s[:, 0:1] + _reduce(p, 1, "sum"), (nrow, LANES))
        acc_s[...] = alpha * acc_s[...] + jnp.dot(p.astype(BF16), vb, preferred_element_type=F32)
        m_s[...] = jnp.broadcast_to(m_new, (nrow, LANES))

    kp = jnp.concatenate([r[0] for r in k_refs], axis=0).astype(BF16)
    vp = jnp.concatenate([r[0] for r in v_refs], axis=0).astype(BF16)
    n = kp.shape[0]
    update(kp, vp, selx_ref[0], jnp.bitwise_and(lax.broadcasted_iota(jnp.int32, (nrow, n), 1), SA_H - 1))

    @pl.when(g == pl.num_programs(1) - 1)
    def _():
        zpad = jnp.zeros((LANES - rws, SA_DH), F32)

        def head_major(ref):
            return jnp.concatenate(
                [part for h in range(SA_H) for part in (ref[:, h * SA_DH:(h + 1) * SA_DH], zpad)],
                axis=0).astype(BF16)

        n_t = SA_H * LANES
        flag = jnp.concatenate([seltail_ref[0]] * SA_H, axis=1)
        col_head = jnp.right_shift(lax.broadcasted_iota(jnp.int32, (nrow, n_t), 1), LANES_LOG2)
        update(head_major(knew_ref), head_major(vnew_ref), flag, col_head)
        for h in range(SA_H):
            hs = slice(h * SA_DH, (h + 1) * SA_DH)
            rs = slice(h * rws, (h + 1) * rws)
            o_ref[:, hs] = (acc_s[rs, :] / l_s[rs, 0:1] * _silu(z_ref[:, hs])).astype(BF16)


def _dsa_attend(page_table, saq, saz, k_new, v_new, selx, seltail, cache_k, cache_v):
    nreq, n_pages = page_table.shape
    npg = min(8, n_pages)
    prow = PAGE_SIZE * SA_H
    req = lambda b, g, pt: (b, 0)

    def page_map(i):
        return lambda b, g, pt: (pt[b, g * npg + i], 0, 0)

    page_specs = [pl.BlockSpec((1, prow, SA_DH), page_map(i)) for i in range(npg)]
    grid_spec = pltpu.PrefetchScalarGridSpec(
        num_scalar_prefetch=1,
        grid=(nreq, n_pages // npg),
        in_specs=[pl.BlockSpec((SAMPLE_ROWS, SA_W), req), pl.BlockSpec((SAMPLE_ROWS, SA_W), req),
                  pl.BlockSpec((SAMPLE_ROWS, SA_W), req), pl.BlockSpec((SAMPLE_ROWS, SA_W), req),
                  pl.BlockSpec((1, SAMPLE_ROWS, npg * prow), lambda b, g, pt: (b, 0, g)),
                  pl.BlockSpec((1, SAMPLE_ROWS, LANES), lambda b, g, pt: (b, 0, 0))]
                 + page_specs + page_specs,
        out_specs=pl.BlockSpec((SAMPLE_ROWS, SA_W), req),
        scratch_shapes=[pltpu.VMEM((SA_H * SAMPLE_ROWS, LANES), F32), pltpu.VMEM((SA_H * SAMPLE_ROWS, LANES), F32),
                        pltpu.VMEM((SA_H * SAMPLE_ROWS, SA_DH), F32)],
    )
    return pl.pallas_call(
        functools.partial(_att_kernel, npg=npg),
        out_shape=jax.ShapeDtypeStruct((nreq * SAMPLE_ROWS, SA_W), BF16),
        grid_spec=grid_spec,
        compiler_params=_cparams(("arbitrary", "arbitrary")),
        name="dsa_att",
    )(page_table, saq, saz, k_new, v_new, selx, seltail, *([cache_k] * npg), *([cache_v] * npg))


def _memattn_kernel(q_ref, z_ref, mk_ref, mv_ref, o_ref):
    mk = mk_ref[...].astype(BF16)
    mv = mv_ref[...].astype(BF16)
    for h in range(MEM_H):
        hs = slice(h * MEM_DH, (h + 1) * MEM_DH)
        s = lax.dot_general(q_ref[:, hs], mk[:, hs], _NT, preferred_element_type=F32)
        p = jnp.exp(s - jnp.max(s, axis=1, keepdims=True))
        l = jnp.sum(p, axis=1, keepdims=True)
        o = jnp.dot(p.astype(BF16), mv[:, hs], preferred_element_type=F32) / l
        o_ref[:, hs] = (o * _silu(z_ref[:, hs])).astype(BF16)


def _memattn(memq, memz, mk, mv, nb, tq):
    rows = memq.shape[0]
    n_mem = mk.shape[0] // nb
    nq = rows // (nb * tq)
    qblk = lambda b, i: (b * nq + i, 0)
    per_b = lambda b, i: (b, 0)
    return pl.pallas_call(
        _memattn_kernel,
        out_shape=jax.ShapeDtypeStruct((rows, MEM_W), BF16),
        grid=(nb, nq),
        in_specs=[pl.BlockSpec((tq, MEM_W), qblk), pl.BlockSpec((tq, MEM_W), qblk),
                  pl.BlockSpec((n_mem, MEM_W), per_b), pl.BlockSpec((n_mem, MEM_W), per_b)],
        out_specs=pl.BlockSpec((tq, MEM_W), qblk),
        compiler_params=_cparams(("arbitrary", "arbitrary")),
        name="memattn",
    )(memq, memz, mk, mv)


def _mixout_kernel(x_ref, a_ref, b_ref, c_ref, wa_ref, wb_ref, wc_ref, g_ref, y_ref):
    acc = (jnp.dot(a_ref[...], wa_ref[...], preferred_element_type=F32)
           + jnp.dot(b_ref[...], wb_ref[...], preferred_element_type=F32)
           + jnp.dot(c_ref[...], wc_ref[...], preferred_element_type=F32))
    y = acc * lax.rsqrt(jnp.mean(acc * acc, axis=-1, keepdims=True) + EPS) * g_ref[...]
    y_ref[...] = x_ref[...] + y


def _mixout(x2d, a, b, c, w_out, g_post, tm):
    rows = x2d.shape[0]
    wb16 = w_out.astype(BF16)
    row = lambda i: (i, 0)
    const = lambda i: (0, 0)
    return pl.pallas_call(
        _mixout_kernel,
        out_shape=jax.ShapeDtypeStruct((rows, D_MODEL), F32),
        grid=(rows // tm,),
        in_specs=[pl.BlockSpec((tm, D_MODEL), row), pl.BlockSpec((tm, ML_W), row),
                  pl.BlockSpec((tm, SA_W), row), pl.BlockSpec((tm, MEM_W), row),
                  pl.BlockSpec((ML_W, D_MODEL), const), pl.BlockSpec((SA_W, D_MODEL), const),
                  pl.BlockSpec((MEM_W, D_MODEL), const), pl.BlockSpec((1, D_MODEL), const)],
        out_specs=pl.BlockSpec((tm, D_MODEL), row),
        compiler_params=_cparams(("arbitrary",)),
        name="mixout",
    )(x2d, a, b, c, wb16[:ML_W], wb16[ML_W:ML_W + SA_W], wb16[ML_W + SA_W:], g_post.reshape(1, D_MODEL))


def _layer(x_p, x_s, st_c, st_n, st_m, c_k, c_v, c_kidx, c_mk, c_mv, page_table, mem_prompt,
           g_pre, w_in, b_gates, g_head, w_mem_k, w_mem_v, g_mem, w_out, g_post):
    nb, seq, _ = x_p.shape
    nreq, t_dec, _ = x_s.shape
    n_mem = mem_prompt.shape[1]
    n_past = page_table.shape[1] * PAGE_SIZE
    weights = _relayout_w_in(w_in)

    tm = min(512, seq)
    tabs = _rope_tables(jnp.arange(seq, dtype=jnp.int32))
    (qkv, oz, saq, k, v, saz, memq, memz, idxq, kidx, small, small_t) = _project(
        x_p.reshape(nb * seq, D_MODEL), g_pre, weights, tabs, tm)
    c0 = jnp.zeros((nb, ML_H, ML_DH, ML_DH), F32)
    n0 = jnp.zeros((nb, ML_H, ML_DH), F32)
    m0 = jnp.zeros((nb, ML_H), F32)
    a_p, p_c, p_n, p_m = _mlstm(qkv, oz, small, small_t, b_gates, g_head, c0, n0, m0, nb, min(256, seq), 0)
    b_p = _dsa_prompt(saq, saz, idxq, small_t, k, v, kidx, nb, seq)
    mk, mv = _memkv(mem_prompt.reshape(nb * n_mem, D_MODEL), g_mem, w_mem_k, w_mem_v, n_mem)
    c_p = _memattn(memq, memz, mk, mv, nb, min(256, seq))
    y_p = _mixout(x_p.reshape(nb * seq, D_MODEL), a_p, b_p, c_p, w_out, g_post, tm).reshape(nb, seq, D_MODEL)

    rws = SAMPLE_ROWS
    n_padrow = rws - t_dec
    xs_pad = jnp.concatenate([jnp.zeros((nreq, n_padrow, D_MODEL), F32), x_s], axis=1).reshape(nreq * rws, D_MODEL)
    pos_s = jnp.tile(jnp.concatenate([jnp.zeros((n_padrow,), jnp.int32),
                                      n_past + jnp.arange(t_dec, dtype=jnp.int32)]), nreq)
    tabs_s = _rope_tables(pos_s)
    (qkv_s, oz_s, saq_s, k_s, v_s, saz_s, memq_s, memz_s, idxq_s, kidx_s, small_s, small_t_s) = _project(
        xs_pad, g_pre, weights, tabs_s, nreq * rws)
    a_s, s_c, s_n, s_m = _mlstm(qkv_s, oz_s, small_s, small_t_s, b_gates, g_head, st_c, st_n, st_m,
                                nreq, rws, n_padrow)
    n_sel = min(TOPK_MAX, (n_past + t_dec) // 4)
    selx, seltail = _dsa_select(page_table, idxq_s, small_s, kidx_s, jnp.swapaxes(c_kidx, 1, 2), n_sel, t_dec)
    b_s = _dsa_attend(page_table, saq_s, saz_s, k_s, v_s, selx, seltail,
                      c_k.reshape(c_k.shape[0], PAGE_SIZE * SA_H, SA_DH),
                      c_v.reshape(c_v.shape[0], PAGE_SIZE * SA_H, SA_DH))
    c_s = _memattn(memq_s, memz_s, c_mk.reshape(nreq * n_mem, MEM_W), c_mv.reshape(nreq * n_mem, MEM_W), nreq, rws)
    y_s = _mixout(xs_pad, a_s, b_s, c_s, w_out, g_post, nreq * rws)

    def real(a2d):
        return a2d.reshape(nreq, rws, -1)[:, n_padrow:]

    new = (p_c, p_n, p_m,
           k.reshape(nb, seq, SA_H, SA_DH), v.reshape(nb, seq, SA_H, SA_DH), kidx.reshape(nb, seq, IDX_D),
           mk.reshape(nb, n_mem, MEM_H, MEM_DH), mv.reshape(nb, n_mem, MEM_H, MEM_DH),
           s_c, s_n, s_m,
           real(k_s).reshape(nreq, t_dec, SA_H, SA_DH), real(v_s).reshape(nreq, t_dec, SA_H, SA_DH), real(kidx_s))
    return y_p, real(y_s), new


def kernel(x_prompt, x_sample, state_mlstm_C, state_mlstm_n, state_mlstm_m, cache_k, cache_v, cache_kidx,
           cache_mem_k, cache_mem_v, page_table, mem_prompt, g_pre, w_in, b_gates, g_head, w_mem_k, w_mem_v,
           g_mem, w_out, g_post):
    xp, xs = x_prompt, x_sample
    per_layer = []
    for l in range(w_in.shape[0]):
        xp, xs, new = _layer(xp, xs, state_mlstm_C[l], state_mlstm_n[l], state_mlstm_m[l],
                             cache_k[l], cache_v[l], cache_kidx[l], cache_mem_k[l], cache_mem_v[l],
                             page_table, mem_prompt, g_pre[l], w_in[l], b_gates[l], g_head[l],
                             w_mem_k[l], w_mem_v[l], g_mem[l], w_out[l], g_post[l])
        per_layer.append(new)
    stacked = [jnp.stack(a) for a in zip(*per_layer)]
    return (xp, xs, *stacked)
```

```python
import functools

import jax
import jax.numpy as jnp
from jax import lax
from jax.experimental import pallas as pl
from jax.experimental.pallas import tpu as pltpu

F32 = jnp.float32
BF16 = jnp.bfloat16

D_MODEL = 2048
ML_H = 4
ML_W = D_MODEL // 2
ML_DH = ML_W // ML_H
SA_H = 4
SA_W = D_MODEL // 4
SA_DH = SA_W // SA_H
MEM_H = 4
MEM_W = D_MODEL // 4
MEM_DH = MEM_W // MEM_H
IDX_H = 8
IDX_D = 64
IDX_SCALE = (IDX_H * IDX_D) ** -0.5
TOPK_MAX = 256
ROPE_THETA = 10000.0
EPS = 1e-6
PAGE_SIZE = 128

LANES = 128
CB = 512
SMALL_IG = 64
SMALL_LF = 68
SMALL_W = 72
SAMPLE_ROWS = 16
SEL_ROWS = 8
VMEM_LIMIT = 56 * 1024 * 1024
NEG_INF = float("-inf")
POS_INF = float("inf")

_NT = (((1,), (1,)), ((), ()))
_TN = (((0,), (0,)), ((), ()))


def _cparams(sem):
    return pltpu.CompilerParams(dimension_semantics=sem, vmem_limit_bytes=VMEM_LIMIT)


def _sigmoid(x):
    return 1.0 / (1.0 + jnp.exp(-x))


def _silu(x):
    return x * _sigmoid(x)


def _log_sigmoid(x):
    return jnp.minimum(x, 0.0) - jnp.log1p(jnp.exp(-jnp.abs(x)))


def _split3(x):
    hi = x.astype(BF16)
    r = x - hi.astype(F32)
    mid = r.astype(BF16)
    lo = (r - mid.astype(F32)).astype(BF16)
    return hi, mid, lo


def _rope128(x, cos, sin_signed):
    return x * cos + pltpu.roll(x, 64, 1) * sin_signed


def _rope64(x, cos, sin_signed):
    lane = lax.broadcasted_iota(jnp.int32, x.shape, 1)
    first_half = (lane % 64) < 32
    partner = jnp.where(first_half, pltpu.roll(x, 96, 1), pltpu.roll(x, 32, 1))
    return x * cos + partner * sin_signed


def _normed(x_ref, g_ref):
    x = x_ref[...]
    return (x * lax.rsqrt(jnp.mean(x * x, axis=-1, keepdims=True) + EPS) * g_ref[...]).astype(BF16)


def _proj_ml_kernel(x_ref, g_ref, w_ref, qkv_ref, oz_ref, u_ref):
    u_ref[...] = _normed(x_ref, g_ref)
    n_qkv = 3 * ML_W // CB
    for cb in range(5 * ML_W // CB):
        acc = jnp.dot(u_ref[...], w_ref[:, cb * CB:(cb + 1) * CB], preferred_element_type=F32)
        if cb < n_qkv:
            qkv_ref[:, cb * CB:(cb + 1) * CB] = acc.astype(BF16)
        else:
            oz_ref[:, (cb - n_qkv) * CB:(cb - n_qkv + 1) * CB] = acc


def _proj_rest_kernel(x_ref, g_ref, w_ref, ws_ref, wst_ref, c128_ref, s128_ref, c64_ref, s64_ref,
                      saq_ref, k_ref, v_ref, saz_ref, memq_ref, memz_ref, idxq_ref,
                      kidx_ref, small_ref, smallt_ref, u_ref):
    u_ref[...] = _normed(x_ref, g_ref)
    sm = jnp.dot(u_ref[...], ws_ref[...], preferred_element_type=F32)
    small_ref[...] = sm
    kidx_ref[...] = _rope64(sm, c64_ref[...], s64_ref[...])[:, :IDX_D]
    smallt_ref[...] = lax.dot_general(wst_ref[...], u_ref[...], _NT, preferred_element_type=F32)

    def block(cb):
        return jnp.dot(u_ref[...], w_ref[:, cb * CB:(cb + 1) * CB], preferred_element_type=F32)

    def rope_heads(acc, fn, cos_ref, sin_ref):
        return jnp.concatenate(
            [fn(acc[:, h * LANES:(h + 1) * LANES], cos_ref[...], sin_ref[...]) for h in range(CB // LANES)], axis=1)

    saq_ref[...] = (rope_heads(block(0), _rope128, c128_ref, s128_ref) * (SA_DH ** -0.5)).astype(BF16)
    k_ref[...] = rope_heads(block(1), _rope128, c128_ref, s128_ref)
    v_ref[...] = block(2)
    saz_ref[...] = block(3)
    memq_ref[...] = (block(4) * (MEM_DH ** -0.5)).astype(BF16)
    memz_ref[...] = block(5)
    idxq_ref[...] = rope_heads(block(6), _rope64, c64_ref, s64_ref).astype(BF16)


def _project(x2d, g_pre, weights, tabs, tm):
    w_ml, w_rest, w_small, w_small_t = weights
    rows = x2d.shape[0]
    c128, s128, c64, s64 = tabs
    ntab = c128.shape[0] // tm
    n_ml = 5 * ML_W
    row_only = lambda i: (i, 0)
    tab_map = lambda i: (i % ntab, 0)
    const = lambda i: (0, 0)
    resident = pl.Buffered(1)
    g2d = g_pre.reshape(1, D_MODEL)

    qkv, oz = pl.pallas_call(
        _proj_ml_kernel,
        out_shape=(jax.ShapeDtypeStruct((rows, 3 * ML_W), BF16),
                   jax.ShapeDtypeStruct((rows, 2 * ML_W), F32)),
        grid=(rows // tm,),
        in_specs=[pl.BlockSpec((tm, D_MODEL), row_only),
                  pl.BlockSpec((1, D_MODEL), const),
                  pl.BlockSpec((D_MODEL, n_ml), const, pipeline_mode=resident)],
        out_specs=(pl.BlockSpec((tm, 3 * ML_W), row_only), pl.BlockSpec((tm, 2 * ML_W), row_only)),
        scratch_shapes=[pltpu.VMEM((tm, D_MODEL), BF16)],
        compiler_params=_cparams(("arbitrary",)),
        name="proj_ml",
    )(x2d, g2d, w_ml)

    out_shape = (
        jax.ShapeDtypeStruct((rows, SA_W), BF16),
        jax.ShapeDtypeStruct((rows, SA_W), F32),
        jax.ShapeDtypeStruct((rows, SA_W), F32),
        jax.ShapeDtypeStruct((rows, SA_W), F32),
        jax.ShapeDtypeStruct((rows, MEM_W), BF16),
        jax.ShapeDtypeStruct((rows, MEM_W), F32),
        jax.ShapeDtypeStruct((rows, IDX_H * IDX_D), BF16),
        jax.ShapeDtypeStruct((rows, IDX_D), F32),
        jax.ShapeDtypeStruct((rows, LANES), F32),
        jax.ShapeDtypeStruct((LANES, rows), F32),
    )
    out_specs = (
        pl.BlockSpec((tm, CB), row_only),
        pl.BlockSpec((tm, CB), row_only),
        pl.BlockSpec((tm, CB), row_only),
        pl.BlockSpec((tm, CB), row_only),
        pl.BlockSpec((tm, CB), row_only),
        pl.BlockSpec((tm, CB), row_only),
        pl.BlockSpec((tm, CB), row_only),
        pl.BlockSpec((tm, IDX_D), row_only),
        pl.BlockSpec((tm, LANES), row_only),
        pl.BlockSpec((LANES, tm), lambda i: (0, i)),
    )
    n_rest = w_rest.shape[1]
    in_specs = [
        pl.BlockSpec((tm, D_MODEL), row_only),
        pl.BlockSpec((1, D_MODEL), const),
        pl.BlockSpec((D_MODEL, n_rest), const, pipeline_mode=resident),
        pl.BlockSpec((D_MODEL, LANES), const, pipeline_mode=resident),
        pl.BlockSpec((LANES, D_MODEL), const, pipeline_mode=resident),
        pl.BlockSpec((tm, LANES), tab_map),
        pl.BlockSpec((tm, LANES), tab_map),
        pl.BlockSpec((tm, LANES), tab_map),
        pl.BlockSpec((tm, LANES), tab_map),
    ]
    rest = pl.pallas_call(
        _proj_rest_kernel,
        out_shape=out_shape,
        grid=(rows // tm,),
        in_specs=in_specs,
        out_specs=out_specs,
        scratch_shapes=[pltpu.VMEM((tm, D_MODEL), BF16)],
        compiler_params=_cparams(("arbitrary",)),
        name="proj_rest",
    )(x2d, g2d, w_rest, w_small, w_small_t, c128, s128, c64, s64)
    return (qkv, oz, *rest)


def _rope_tables(pos):
    def tab(half):
        inv = ROPE_THETA ** (-jnp.arange(half, dtype=F32) / half)
        ang = pos.astype(F32)[:, None] * inv[None, :]
        return jnp.cos(ang), jnp.sin(ang)

    c, s = tab(SA_DH // 2)
    c128 = jnp.concatenate([c, c], axis=1)
    s128 = jnp.concatenate([-s, s], axis=1)
    c, s = tab(IDX_D // 2)
    c64 = jnp.concatenate([c, c, c, c], axis=1)
    s64 = jnp.concatenate([-s, s, -s, s], axis=1)
    return c128, s128, c64, s64


def _relayout_w_in(w_in):
    off = {}
    o = 0
    for name, w in (('ml_q', ML_W), ('ml_k', ML_W), ('ml_v', ML_W), ('ml_o', ML_W), ('ml_z', ML_W),
                    ('ml_i', ML_H), ('ml_f', ML_H), ('sa_q', SA_W), ('sa_k', SA_W), ('sa_v', SA_W),
                    ('sa_z', SA_W), ('idx_q', IDX_H * IDX_D), ('idx_k', IDX_D), ('idx_w', IDX_H),
                    ('mem_q', MEM_W), ('mem_z', MEM_W)):
        off[name] = (o, w)
        o += w

    def col(name):
        a, w = off[name]
        return w_in[:, a:a + w]

    w_ml = jnp.concatenate([
        col('ml_q'), col('ml_k') * (ML_DH ** -0.5), col('ml_v'), col('ml_o'), col('ml_z')], axis=1).astype(BF16)
    w_rest = jnp.concatenate([
        col('sa_q'), col('sa_k'), col('sa_v'), col('sa_z'), col('mem_q'), col('mem_z'), col('idx_q'),
    ], axis=1).astype(BF16)
    small = jnp.concatenate([
        col('idx_k'), col('ml_i'), col('ml_f'), col('idx_w'),
        jnp.zeros((D_MODEL, LANES - IDX_D - 2 * ML_H - IDX_H), F32)], axis=1).astype(BF16)
    return (w_ml, w_rest, small, small.T)


def _memkv_kernel(m_ref, g_ref, wk_ref, wv_ref, k_ref, v_ref):
    x = m_ref[...]
    u = (x * lax.rsqrt(jnp.mean(x * x, axis=-1, keepdims=True) + EPS) * g_ref[...]).astype(BF16)
    k_ref[...] = jnp.dot(u, wk_ref[...], preferred_element_type=F32)
    v_ref[...] = jnp.dot(u, wv_ref[...], preferred_element_type=F32)


def _memkv(mem2d, g_mem, wk, wv, n_mem):
    rows = mem2d.shape[0]
    row = lambda i: (i, 0)
    const = lambda i: (0, 0)
    return pl.pallas_call(
        _memkv_kernel,
        out_shape=(jax.ShapeDtypeStruct((rows, MEM_W), F32), jax.ShapeDtypeStruct((rows, MEM_W), F32)),
        grid=(rows // n_mem,),
        in_specs=[pl.BlockSpec((n_mem, D_MODEL), row), pl.BlockSpec((1, D_MODEL), const),
                  pl.BlockSpec((D_MODEL, MEM_W), const), pl.BlockSpec((D_MODEL, MEM_W), const)],
        out_specs=(pl.BlockSpec((n_mem, MEM_W), row), pl.BlockSpec((n_mem, MEM_W), row)),
        compiler_params=_cparams(("arbitrary",)),
        name="memkv",
    )(mem2d, g_mem.reshape(1, D_MODEL), wk.astype(BF16), wv.astype(BF16))


def _mlstm_kernel(qkv_ref, oz_ref, gc_ref, gt_ref, bcol_ref, brow_ref, gh_ref, c0_ref, n0_ref, m0_ref,
                  a_ref, cout_ref, nout_ref, mout_ref, c_s, n_s, m_s, *, c, n_pad):
    ci = pl.program_id(1)

    @pl.when(ci == 0)
    def _():
        c_s[...] = c0_ref[0]
        n_s[...] = n0_ref[0]
        m_s[...] = m0_ref[0]

    ri = lax.broadcasted_iota(jnp.int32, (c, c), 0)
    cj = lax.broadcasted_iota(jnp.int32, (c, c), 1)
    causal = cj <= ri
    tri = jnp.where(causal, 1.0, 0.0).astype(BF16)
    tri_t = jnp.where(ri <= cj, 1.0, 0.0).astype(BF16)

    g_c = gc_ref[...] + bcol_ref[...]
    pad_c = lax.broadcasted_iota(jnp.int32, (c, LANES), 0) < n_pad
    ig_c = jnp.where(pad_c, NEG_INF, g_c)
    lf_c = jnp.where(pad_c, 0.0, _log_sigmoid(g_c))
    b_c = sum(jnp.dot(tri, p, preferred_element_type=F32) for p in _split3(lf_c))
    g_r = gt_ref[0] + brow_ref[...]
    pad_r = lax.broadcasted_iota(jnp.int32, (SAMPLE_ROWS, c), 1) < n_pad
    ig_r = jnp.where(pad_r, NEG_INF, g_r)
    lf_r = jnp.where(pad_r, 0.0, _log_sigmoid(g_r))
    b_r = sum(jnp.dot(p, tri_t, preferred_element_type=F32) for p in _split3(lf_r))

    for h in range(ML_H):
        hs = slice(h * ML_DH, (h + 1) * ML_DH)
        m_prev = m_s[h:h + 1, 0:1]
        b_t = b_c[:, SMALL_LF + h:SMALL_LF + h + 1]
        igc = ig_c[:, SMALL_IG + h:SMALL_IG + h + 1]
        b_s = b_r[ML_H + h:ML_H + h + 1, :]
        igr = ig_r[h:h + 1, :]
        a = jnp.where(causal, b_t - b_s + igr, NEG_INF)
        bm = b_t + m_prev
        m_t = jnp.maximum(bm, jnp.max(a, axis=1, keepdims=True))
        inter = jnp.exp(bm - m_t)
        dmat = jnp.exp(a - m_t)
        q = qkv_ref[:, h * ML_DH:(h + 1) * ML_DH]
        k = qkv_ref[:, ML_W + h * ML_DH:ML_W + (h + 1) * ML_DH]
        v = qkv_ref[:, 2 * ML_W + h * ML_DH:2 * ML_W + (h + 1) * ML_DH]
        s = lax.dot_general(q, k, _NT, preferred_element_type=F32) * dmat
        c_h = c_s[h]
        n_h = n_s[h:h + 1, :]
        num = (jnp.dot(s.astype(BF16), v, preferred_element_type=F32)
               + inter * jnp.dot(q, c_h.astype(BF16), preferred_element_type=F32))
        qn = (jnp.sum(s, axis=1, keepdims=True)
              + inter * jnp.sum(q.astype(F32) * n_h, axis=1, keepdims=True))
        hh = num / jnp.maximum(jnp.abs(qn), jnp.exp(-m_t))
        hh = hh * lax.rsqrt(jnp.mean(hh * hh, axis=1, keepdims=True) + EPS)
        o = oz_ref[:, h * ML_DH:(h + 1) * ML_DH]
        z = oz_ref[:, ML_W + h * ML_DH:ML_W + (h + 1) * ML_DH]
        a_ref[:, hs] = (hh * gh_ref[:, hs] * _sigmoid(o) * _silu(z)).astype(BF16)

        m_new = m_t[c - 1:c, :]
        b_last = b_t[c - 1:c, :]
        w_end = jnp.exp(b_last - b_t + igc - m_new)
        decay = jnp.exp(b_last + m_prev - m_new)
        kw = k.astype(F32) * w_end
        c_s[h] = decay * c_h + lax.dot_general(kw.astype(BF16), v, _TN, preferred_element_type=F32)
        n_s[h:h + 1, :] = decay * n_h + jnp.sum(kw, axis=0, keepdims=True)
        m_s[h:h + 1, :] = jnp.broadcast_to(m_new, (1, LANES))

    @pl.when(ci == pl.num_programs(1) - 1)
    def _():
        cout_ref[0] = c_s[...]
        nout_ref[0] = n_s[...]
        mout_ref[0] = m_s[...]


def _mlstm(qkv, oz, small, small_t, b_gates, g_head, c0, n0, m0, nb, c, n_pad):
    rows = qkv.shape[0]
    nc = rows // (nb * c)
    bias_col = jnp.zeros((1, LANES), F32).at[0, SMALL_IG:SMALL_IG + 2 * ML_H].set(b_gates)
    bias_row = jnp.zeros((SAMPLE_ROWS, 1), F32).at[:2 * ML_H, 0].set(b_gates)
    m0b = jnp.zeros((nb, 8, LANES), F32).at[:, :ML_H, :].set(jnp.broadcast_to(m0[:, :, None], (nb, ML_H, LANES)))
    rowblk = lambda b, i: (b * nc + i, 0)
    const = lambda b, i: (0, 0)
    gates_t = small_t[SMALL_IG:SMALL_IG + SAMPLE_ROWS].reshape(SAMPLE_ROWS, rows // c, c).transpose(1, 0, 2)
    out_shape = (
        jax.ShapeDtypeStruct((rows, ML_W), BF16),
        jax.ShapeDtypeStruct((nb, ML_H, ML_DH, ML_DH), F32),
        jax.ShapeDtypeStruct((nb, ML_H, ML_DH), F32),
        jax.ShapeDtypeStruct((nb, 8, LANES), F32),
    )
    st4 = lambda b, i: (b, 0, 0, 0)
    st3 = lambda b, i: (b, 0, 0)
    a, c_out, n_out, m_out = pl.pallas_call(
        functools.partial(_mlstm_kernel, c=c, n_pad=n_pad),
        out_shape=out_shape,
        grid=(nb, nc),
        in_specs=[
            pl.BlockSpec((c, 3 * ML_W), rowblk),
            pl.BlockSpec((c, 2 * ML_W), rowblk),
            pl.BlockSpec((c, LANES), rowblk),
            pl.BlockSpec((1, SAMPLE_ROWS, c), lambda b, i: (b * nc + i, 0, 0)),
            pl.BlockSpec((1, LANES), const),
            pl.BlockSpec((SAMPLE_ROWS, 1), const),
            pl.BlockSpec((1, ML_W), const),
            pl.BlockSpec((1, ML_H, ML_DH, ML_DH), st4),
            pl.BlockSpec((1, ML_H, ML_DH), st3),
            pl.BlockSpec((1, 8, LANES), st3),
        ],
        out_specs=(
            pl.BlockSpec((c, ML_W), rowblk),
            pl.BlockSpec((1, ML_H, ML_DH, ML_DH), st4),
            pl.BlockSpec((1, ML_H, ML_DH), st3),
            pl.BlockSpec((1, 8, LANES), st3),
        ),
        scratch_shapes=[pltpu.VMEM((ML_H, ML_DH, ML_DH), F32), pltpu.VMEM((ML_H, ML_DH), F32),
                        pltpu.VMEM((8, LANES), F32)],
        compiler_params=_cparams(("arbitrary", "arbitrary")),
        name="mlstm",
    )(qkv, oz, small, gates_t, bias_col, bias_row, g_head.reshape(1, ML_W), c0, n0, m0b)
    return a, c_out, n_out, m_out[:, :ML_H, 0]


_REDUCERS = {"sum": (jnp.sum, jnp.add), "max": (jnp.max, jnp.maximum), "min": (jnp.min, jnp.minimum)}
REDUCE_CHAINS = 8


def _reduce(x, axis, op):
    fn, combine = _REDUCERS[op]
    unit = 8 if axis == 0 else LANES
    n = x.shape[axis]
    units = n // unit
    if n % unit or units < 2 * REDUCE_CHAINS:
        return fn(x, axis=axis, keepdims=True)
    base, rem = divmod(units, REDUCE_CHAINS)
    parts, start = [], 0
    for i in range(REDUCE_CHAINS):
        size = (base + (1 if i < rem else 0)) * unit
        piece = x[start:start + size] if axis == 0 else x[:, start:start + size]
        parts.append(fn(piece, axis=axis, keepdims=True))
        start += size
    while len(parts) > 1:
        parts = [combine(parts[i], parts[i + 1]) for i in range(0, len(parts), 2)]
    return parts[0]


def _count(pred, axis):
    return _reduce(jnp.where(pred, 1.0, 0.0), axis, "sum")


def _kth_largest(x_ref, k, axis, n_bisect):
    kf = float(k)
    x = x_ref[...]
    hi = _reduce(x, axis, "max")
    lo = _reduce(jnp.where(x == NEG_INF, POS_INF, x), axis, "min")

    def bisect(_, carry):
        lo, hi = carry
        mid = 0.5 * (lo + hi)
        ge = _count(x_ref[...] >= mid, axis) >= kf
        return jnp.where(ge, mid, lo), jnp.where(ge, hi, mid)

    lo, hi = lax.fori_loop(0, n_bisect, bisect, (lo, hi))

    def cond(st):
        return st[4] < 0.5

    def body(st):
        lo, strict, thr, done, _ = st
        xx = x_ref[...]
        above = _reduce(jnp.where(xx > lo, xx, POS_INF), axis, "min")
        at_lo = (strict < 0.5) & (_count(xx == lo, axis) > 0.5)
        cmin = jnp.where(at_lo, lo, above)
        fin = (_count(xx > cmin, axis) < kf) | (cmin == POS_INF)
        active = done < 0.5
        thr = jnp.where(active, cmin, thr)
        lo = jnp.where(active, cmin, lo)
        done = jnp.where(fin, 1.0, done)
        return lo, jnp.ones_like(strict), thr, done, jnp.min(done)

    zeros = jnp.zeros_like(lo)
    _, _, thr, _, _ = lax.while_loop(cond, body, (lo, zeros, lo, zeros, jnp.float32(0.0)))
    return thr


def _dsa_kernel(q_ref, z_ref, idxq_ref, wt_ref, k_ref, v_ref, kidx_ref, o_ref,
                kb_s, vt_s, kib_s, x_s, sel_s, *, n_keys, qb, n_sel, n_bisect, key_step):
    j = pl.program_id(1)

    @pl.when(j == 0)
    def _():
        kb_s[...] = k_ref[...].astype(BF16)
        vt_s[...] = v_ref[...].T.astype(BF16)
        kib_s[...] = kidx_ref[...].astype(BF16)

    def attend(nk):
        xs = x_s.at[0:nk]
        ss = sel_s.at[0:nk]
        key = lax.broadcasted_iota(jnp.int32, (nk, qb), 0)
        qpos = j * qb + lax.broadcasted_iota(jnp.int32, (nk, qb), 1)
        valid = key <= qpos
        sc = jnp.zeros((nk, qb), F32)
        for h in range(IDX_H):
            d = lax.dot_general(kib_s[0:nk, :], idxq_ref[:, h * IDX_D:(h + 1) * IDX_D], _NT,
                                preferred_element_type=F32)
            sc = sc + jnp.maximum(d, 0.0) * (wt_ref[h:h + 1, :] * IDX_SCALE)
        xs[...] = jnp.where(valid, sc, NEG_INF)
        ss[...] = jnp.where(valid, 1.0, 0.0)

        @pl.when((j + 1) * qb > n_sel)
        def _():
            kf = float(n_sel)
            thr = _kth_largest(xs, n_sel, 0, n_bisect)
            x = xs[...]
            need = kf - _count(x > thr, 0)
            n_tie = _count(x == thr, 0)
            qrow = j * qb + lax.broadcasted_iota(jnp.int32, (1, qb), 1)
            small = (qrow + 1) <= n_sel
            ss[...] = jnp.where(small, jnp.where(valid, 1.0, 0.0), jnp.where(x >= thr, 1.0, 0.0))
            excess = jnp.max(jnp.where((n_tie > need) & jnp.logical_not(small), 1.0, 0.0))

            @pl.when(excess > 0.5)
            def _():
                tb = min(256, nk)
                r_i = lax.broadcasted_iota(jnp.int32, (tb, tb), 0)
                c_i = lax.broadcasted_iota(jnp.int32, (tb, tb), 1)
                lower = jnp.where(c_i < r_i, 1.0, 0.0).astype(BF16)
                carry = jnp.zeros((1, qb), F32)
                for blk in range(nk // tb):
                    rows = slice(blk * tb, (blk + 1) * tb)
                    xb = x_s[rows, :]
                    tie = jnp.where(xb == thr, 1.0, 0.0)
                    rank = jnp.dot(lower, tie.astype(BF16), preferred_element_type=F32) + carry
                    keep = (xb > thr) | ((xb == thr) & (rank < need))
                    keyb = blk * tb + lax.broadcasted_iota(jnp.int32, (tb, qb), 0)
                    qposb = j * qb + lax.broadcasted_iota(jnp.int32, (tb, qb), 1)
                    smallb = jnp.where(keyb <= qposb, 1.0, 0.0)
                    sel_s[rows, :] = jnp.where(small, smallb, jnp.where(keep, 1.0, 0.0))
                    carry = carry + jnp.sum(tie, axis=0, keepdims=True)

        sel = ss[...] > 0.5
        for h in range(SA_H):
            hs = slice(h * SA_DH, (h + 1) * SA_DH)
            st = lax.dot_general(kb_s[0:nk, hs], q_ref[:, hs], _NT, preferred_element_type=F32)
            st = jnp.where(sel, st, NEG_INF)
            mx = _reduce(st, 0, "max")
            p = jnp.exp(st - mx)
            l = _reduce(p, 0, "sum")
            ot = jnp.dot(vt_s[hs, 0:nk], p.astype(BF16), preferred_element_type=F32) / l
            o_ref[:, hs] = (ot.T * _silu(z_ref[:, hs])).astype(BF16)

    n_ext = n_keys // key_step
    for e in range(n_ext):
        nk = (e + 1) * key_step
        lo_j = e * key_step // qb
        hi_j = nk // qb

        @pl.when((j >= lo_j) & (j < hi_j))
        def _(nk=nk):
            attend(nk)


def _dsa_prompt(saq, saz, idxq, small_t, k, v, kidx, nb, seq):
    rows = saq.shape[0]
    qb = min(seq, 128)
    nq = seq // qb
    n_sel = min(TOPK_MAX, seq // 4)
    qblk = lambda b, j: (b * nq + j, 0)
    per_b = lambda b, j: (b, 0)
    wt_blk = SMALL_W // 8
    return pl.pallas_call(
        functools.partial(_dsa_kernel, n_keys=seq, qb=qb, n_sel=n_sel, n_bisect=20, key_step=min(512, seq)),
        out_shape=jax.ShapeDtypeStruct((rows, SA_W), BF16),
        grid=(nb, nq),
        in_specs=[
            pl.BlockSpec((qb, SA_W), qblk),
            pl.BlockSpec((qb, SA_W), qblk),
            pl.BlockSpec((qb, IDX_H * IDX_D), qblk),
            pl.BlockSpec((8, qb), lambda b, j: (wt_blk, b * nq + j)),
            pl.BlockSpec((seq, SA_W), per_b),
            pl.BlockSpec((seq, SA_W), per_b),
            pl.BlockSpec((seq, IDX_D), per_b),
        ],
        out_specs=pl.BlockSpec((qb, SA_W), qblk),
        scratch_shapes=[pltpu.VMEM((seq, SA_W), BF16), pltpu.VMEM((SA_W, seq), BF16),
                        pltpu.VMEM((seq, IDX_D), BF16), pltpu.VMEM((seq, qb), F32),
                        pltpu.VMEM((seq, qb), F32)],
        compiler_params=_cparams(("arbitrary", "arbitrary")),
        name="dsa",
    )(saq, saz, idxq, small_t, k, v, kidx)


def _sel_kernel(pt_ref, idxq_ref, small_ref, kinew_ref, *rest, npg, n_past, n_sel, n_real, n_bisect):
    page_refs = rest[:npg]
    sel_ref, x_s = rest[npg:]
    g = pl.program_id(1)
    rws = SAMPLE_ROWS
    top = rws - SEL_ROWS
    pk = n_past + LANES
    gk = npg * PAGE_SIZE

    qs = jnp.concatenate([idxq_ref[:, h * IDX_D:(h + 1) * IDX_D] for h in range(IDX_H)], axis=0)
    small = small_ref[...]

    def scores(d):
        sc = jnp.zeros((SEL_ROWS, d.shape[1]), F32)
        for h in range(IDX_H):
            w = small[top:, SMALL_W + h:SMALL_W + h + 1] * IDX_SCALE
            sc = sc + jnp.maximum(d[h * rws + top:(h + 1) * rws, :], 0.0) * w
        return sc

    kp_t = jnp.concatenate([r[0] for r in page_refs], axis=1).astype(BF16)
    x_s[:, pl.ds(pl.multiple_of(g * gk, LANES), gk)] = scores(jnp.dot(qs, kp_t, preferred_element_type=F32))

    @pl.when(g == pl.num_programs(1) - 1)
    def _():
        knew = jnp.concatenate([kinew_ref[...], jnp.zeros((LANES - rws, IDX_D), F32)], axis=0).astype(BF16)
        row = top + lax.broadcasted_iota(jnp.int32, (SEL_ROWS, LANES), 0)
        col = lax.broadcasted_iota(jnp.int32, (SEL_ROWS, LANES), 1)
        ok = (col >= rws - n_real) & (col < rws) & (col <= row)
        d_new = lax.dot_general(qs, knew, _NT, preferred_element_type=F32)
        x_s[:, n_past:pk] = jnp.where(ok, scores(d_new), NEG_INF)

        kf = float(n_sel)
        thr = _kth_largest(x_s, n_sel, 1, n_bisect)
        x = x_s[...]
        need = kf - _count(x > thr, 1)
        n_tie = _count(x == thr, 1)
        sel_ref[0, 0:top, :] = jnp.ones((top, pk), F32)
        sel_ref[0, top:rws, :] = jnp.where(x >= thr, 1.0, 0.0)
        real = lax.broadcasted_iota(jnp.int32, (SEL_ROWS, 1), 0) >= SEL_ROWS - n_real
        excess = jnp.max(jnp.where((n_tie > need) & real, 1.0, 0.0))

        @pl.when(excess > 0.5)
        def _():
            r_i = lax.broadcasted_iota(jnp.int32, (LANES, LANES), 0)
            c_i = lax.broadcasted_iota(jnp.int32, (LANES, LANES), 1)
            upper = jnp.where(r_i < c_i, 1.0, 0.0).astype(BF16)

            def blk(i, carry):
                cols = pl.ds(pl.multiple_of(i * LANES, LANES), LANES)
                xb = x_s[:, cols]
                tie = jnp.where(xb == thr, 1.0, 0.0)
                rank = jnp.dot(tie.astype(BF16), upper, preferred_element_type=F32) + carry
                keep = (xb > thr) | ((xb == thr) & (rank < need))
                sel_ref[0, top:rws, cols] = jnp.where(keep, 1.0, 0.0)
                return carry + jnp.sum(tie, axis=1, keepdims=True)

            lax.fori_loop(0, pk // LANES, blk, jnp.zeros((SEL_ROWS, 1), F32))


def _dsa_select(page_table, idxq, small, kidx_new, cache_kidx, n_sel, n_real):
    nreq, n_pages = page_table.shape
    n_past = n_pages * PAGE_SIZE
    pk = n_past + LANES
    npg = min(16, n_pages)
    req = lambda b, g, pt: (b, 0)

    def page_map(i):
        return lambda b, g, pt: (pt[b, g * npg + i], 0, 0)

    grid_spec = pltpu.PrefetchScalarGridSpec(
        num_scalar_prefetch=1,
        grid=(nreq, n_pages // npg),
        in_specs=[pl.BlockSpec((SAMPLE_ROWS, IDX_H * IDX_D), req),
                  pl.BlockSpec((SAMPLE_ROWS, LANES), req),
                  pl.BlockSpec((SAMPLE_ROWS, IDX_D), req)]
                 + [pl.BlockSpec((1, IDX_D, PAGE_SIZE), page_map(i)) for i in range(npg)],
        out_specs=pl.BlockSpec((1, SAMPLE_ROWS, pk), lambda b, g, pt: (b, 0, 0)),
        scratch_shapes=[pltpu.VMEM((SEL_ROWS, pk), F32)],
    )
    assert n_real <= SEL_ROWS
    return pl.pallas_call(
        functools.partial(_sel_kernel, npg=npg, n_past=n_past, n_sel=n_sel, n_real=n_real, n_bisect=20),
        out_shape=jax.ShapeDtypeStruct((nreq, SAMPLE_ROWS, pk), F32),
        grid_spec=grid_spec,
        compiler_params=_cparams(("arbitrary", "arbitrary")),
        name="dsa_sel",
    )(page_table, idxq, small, kidx_new, *([cache_kidx] * npg))


def _att_kernel(pt_ref, q_ref, z_ref, knew_ref, vnew_ref, sel_ref, seltail_ref, *rest, npg):
    k_refs = rest[:npg]
    v_refs = rest[npg:2 * npg]
    o_ref = rest[2 * npg]
    m_s, l_s, acc_s = rest[2 * npg + 1:]
    g = pl.program_id(1)
    rws = SAMPLE_ROWS
    floor = -1e30

    @pl.when(g == 0)
    def _():
        m_s[...] = jnp.full(m_s.shape, floor, F32)
        l_s[...] = jnp.zeros(l_s.shape, F32)
        acc_s[...] = jnp.zeros(acc_s.shape, F32)

    heads = range(SA_H)
    hsl = [slice(h * SA_DH, (h + 1) * SA_DH) for h in heads]

    def update(kbs, vbs, keep):
        m_old = [m_s[h][:, 0:1] for h in heads]
        l_old = [l_s[h][:, 0:1] for h in heads]
        acc_old = [acc_s[:, hsl[h]] for h in heads]
        s = [lax.dot_general(q_ref[:, hsl[h]], kbs[h], _NT, preferred_element_type=F32) for h in heads]
        m_new = [jnp.maximum(m_old[h], jnp.max(jnp.where(keep, s[h], floor), axis=1, keepdims=True)) for h in heads]
        p = [jnp.where(keep, jnp.exp(s[h] - m_new[h]), 0.0) for h in heads]
        pv = [jnp.dot(p[h].astype(BF16), vbs[h], preferred_element_type=F32) for h in heads]
        alpha = [jnp.exp(m_old[h] - m_new[h]) for h in heads]
        l_new = [alpha[h] * l_old[h] + jnp.sum(p[h], axis=1, keepdims=True) for h in heads]
        for h in heads:
            acc_s[:, hsl[h]] = alpha[h] * acc_old[h] + pv[h]
            l_s[h] = jnp.broadcast_to(l_new[h], (rws, LANES))
            m_s[h] = jnp.broadcast_to(m_new[h], (rws, LANES))

    def head_rows(refs, h):
        return jnp.concatenate([r[0, pl.ds(h, PAGE_SIZE, stride=SA_H), :] for r in refs], axis=0).astype(BF16)

    update([head_rows(k_refs, h) for h in heads], [head_rows(v_refs, h) for h in heads], sel_ref[0] > 0.5)

    @pl.when(g == pl.num_programs(1) - 1)
    def _():
        update([knew_ref[:, hsl[h]].astype(BF16) for h in heads], [vnew_ref[:, hsl[h]].astype(BF16) for h in heads],
               seltail_ref[0][:, :rws] > 0.5)
        for h in heads:
            o_ref[:, hsl[h]] = (acc_s[:, hsl[h]] / l_s[h][:, 0:1] * _silu(z_ref[:, hsl[h]])).astype(BF16)


def _dsa_attend(page_table, saq, saz, k_new, v_new, sel, cache_k, cache_v):
    nreq, n_pages = page_table.shape
    n_past = n_pages * PAGE_SIZE
    npg = min(8, n_pages)
    prow = PAGE_SIZE * SA_H
    req = lambda b, g, pt: (b, 0)

    def page_map(i):
        return lambda b, g, pt: (pt[b, g * npg + i], 0, 0)

    page_specs = [pl.BlockSpec((1, prow, SA_DH), page_map(i)) for i in range(npg)]
    grid_spec = pltpu.PrefetchScalarGridSpec(
        num_scalar_prefetch=1,
        grid=(nreq, n_pages // npg),
        in_specs=[pl.BlockSpec((SAMPLE_ROWS, SA_W), req), pl.BlockSpec((SAMPLE_ROWS, SA_W), req),
                  pl.BlockSpec((SAMPLE_ROWS, SA_W), req), pl.BlockSpec((SAMPLE_ROWS, SA_W), req),
                  pl.BlockSpec((1, SAMPLE_ROWS, npg * PAGE_SIZE), lambda b, g, pt: (b, 0, g)),
                  pl.BlockSpec((1, SAMPLE_ROWS, LANES), lambda b, g, pt: (b, 0, n_past // LANES))]
                 + page_specs + page_specs,
        out_specs=pl.BlockSpec((SAMPLE_ROWS, SA_W), req),
        scratch_shapes=[pltpu.VMEM((SA_H, SAMPLE_ROWS, LANES), F32), pltpu.VMEM((SA_H, SAMPLE_ROWS, LANES), F32),
                        pltpu.VMEM((SAMPLE_ROWS, SA_W), F32)],
    )
    return pl.pallas_call(
        functools.partial(_att_kernel, npg=npg),
        out_shape=jax.ShapeDtypeStruct((nreq * SAMPLE_ROWS, SA_W), BF16),
        grid_spec=grid_spec,
        compiler_params=_cparams(("arbitrary", "arbitrary")),
        name="dsa_att",
    )(page_table, saq, saz, k_new, v_new, sel, sel, *([cache_k] * npg), *([cache_v] * npg))


def _memattn_kernel(q_ref, z_ref, mk_ref, mv_ref, o_ref):
    mk = mk_ref[...].astype(BF16)
    mv = mv_ref[...].astype(BF16)
    for h in range(MEM_H):
        hs = slice(h * MEM_DH, (h + 1) * MEM_DH)
        s = lax.dot_general(q_ref[:, hs], mk[:, hs], _NT, preferred_element_type=F32)
        p = jnp.exp(s - jnp.max(s, axis=1, keepdims=True))
        l = jnp.sum(p, axis=1, keepdims=True)
        o = jnp.dot(p.astype(BF16), mv[:, hs], preferred_element_type=F32) / l
        o_ref[:, hs] = (o * _silu(z_ref[:, hs])).astype(BF16)


def _memattn(memq, memz, mk, mv, nb, tq):
    rows = memq.shape[0]
    n_mem = mk.shape[0] // nb
    nq = rows // (nb * tq)
    qblk = lambda b, i: (b * nq + i, 0)
    per_b = lambda b, i: (b, 0)
    return pl.pallas_call(
        _memattn_kernel,
        out_shape=jax.ShapeDtypeStruct((rows, MEM_W), BF16),
        grid=(nb, nq),
        in_specs=[pl.BlockSpec((tq, MEM_W), qblk), pl.BlockSpec((tq, MEM_W), qblk),
                  pl.BlockSpec((n_mem, MEM_W), per_b), pl.BlockSpec((n_mem, MEM_W), per_b)],
        out_specs=pl.BlockSpec((tq, MEM_W), qblk),
        compiler_params=_cparams(("arbitrary", "arbitrary")),
        name="memattn",
    )(memq, memz, mk, mv)


def _mixout_kernel(x_ref, a_ref, b_ref, c_ref, wa_ref, wb_ref, wc_ref, g_ref, y_ref):
    acc = (jnp.dot(a_ref[...], wa_ref[...], preferred_element_type=F32)
           + jnp.dot(b_ref[...], wb_ref[...], preferred_element_type=F32)
           + jnp.dot(c_ref[...], wc_ref[...], preferred_element_type=F32))
    y = acc * lax.rsqrt(jnp.mean(acc * acc, axis=-1, keepdims=True) + EPS) * g_ref[...]
    y_ref[...] = x_ref[...] + y


def _mixout(x2d, a, b, c, w_out, g_post, tm):
    rows = x2d.shape[0]
    wb16 = w_out.astype(BF16)
    row = lambda i: (i, 0)
    const = lambda i: (0, 0)
    return pl.pallas_call(
        _mixout_kernel,
        out_shape=jax.ShapeDtypeStruct((rows, D_MODEL), F32),
        grid=(rows // tm,),
        in_specs=[pl.BlockSpec((tm, D_MODEL), row), pl.BlockSpec((tm, ML_W), row),
                  pl.BlockSpec((tm, SA_W), row), pl.BlockSpec((tm, MEM_W), row),
                  pl.BlockSpec((ML_W, D_MODEL), const), pl.BlockSpec((SA_W, D_MODEL), const),
                  pl.BlockSpec((MEM_W, D_MODEL), const), pl.BlockSpec((1, D_MODEL), const)],
        out_specs=pl.BlockSpec((tm, D_MODEL), row),
        compiler_params=_cparams(("arbitrary",)),
        name="mixout",
    )(x2d, a, b, c, wb16[:ML_W], wb16[ML_W:ML_W + SA_W], wb16[ML_W + SA_W:], g_post.reshape(1, D_MODEL))


def _layer(x_p, x_s, st_c, st_n, st_m, c_k, c_v, c_kidx, c_mk, c_mv, page_table, mem_prompt,
           g_pre, w_in, b_gates, g_head, w_mem_k, w_mem_v, g_mem, w_out, g_post):
    nb, seq, _ = x_p.shape
    nreq, t_dec, _ = x_s.shape
    n_mem = mem_prompt.shape[1]
    n_past = page_table.shape[1] * PAGE_SIZE
    weights = _relayout_w_in(w_in)

    tm = min(512, seq)
    tabs = _rope_tables(jnp.arange(seq, dtype=jnp.int32))
    (qkv, oz, saq, k, v, saz, memq, memz, idxq, kidx, small, small_t) = _project(
        x_p.reshape(nb * seq, D_MODEL), g_pre, weights, tabs, tm)
    c0 = jnp.zeros((nb, ML_H, ML_DH, ML_DH), F32)
    n0 = jnp.zeros((nb, ML_H, ML_DH), F32)
    m0 = jnp.zeros((nb, ML_H), F32)
    a_p, p_c, p_n, p_m = _mlstm(qkv, oz, small, small_t, b_gates, g_head, c0, n0, m0, nb, min(256, seq), 0)
    b_p = _dsa_prompt(saq, saz, idxq, small_t, k, v, kidx, nb, seq)
    mk, mv = _memkv(mem_prompt.reshape(nb * n_mem, D_MODEL), g_mem, w_mem_k, w_mem_v, n_mem)
    c_p = _memattn(memq, memz, mk, mv, nb, min(256, seq))
    y_p = _mixout(x_p.reshape(nb * seq, D_MODEL), a_p, b_p, c_p, w_out, g_post, tm).reshape(nb, seq, D_MODEL)

    rws = SAMPLE_ROWS
    n_padrow = rws - t_dec
    xs_pad = jnp.concatenate([jnp.zeros((nreq, n_padrow, D_MODEL), F32), x_s], axis=1).reshape(nreq * rws, D_MODEL)
    pos_s = jnp.tile(jnp.concatenate([jnp.zeros((n_padrow,), jnp.int32),
                                      n_past + jnp.arange(t_dec, dtype=jnp.int32)]), nreq)
    tabs_s = _rope_tables(pos_s)
    (qkv_s, oz_s, saq_s, k_s, v_s, saz_s, memq_s, memz_s, idxq_s, kidx_s, small_s, small_t_s) = _project(
        xs_pad, g_pre, weights, tabs_s, nreq * rws)
    a_s, s_c, s_n, s_m = _mlstm(qkv_s, oz_s, small_s, small_t_s, b_gates, g_head, st_c, st_n, st_m,
                                nreq, rws, n_padrow)
    n_sel = min(TOPK_MAX, (n_past + t_dec) // 4)
    sel = _dsa_select(page_table, idxq_s, small_s, kidx_s, jnp.swapaxes(c_kidx, 1, 2), n_sel, t_dec)
    b_s = _dsa_attend(page_table, saq_s, saz_s, k_s, v_s, sel,
                      c_k.reshape(c_k.shape[0], PAGE_SIZE * SA_H, SA_DH),
                      c_v.reshape(c_v.shape[0], PAGE_SIZE * SA_H, SA_DH))
    c_s = _memattn(memq_s, memz_s, c_mk.reshape(nreq * n_mem, MEM_W), c_mv.reshape(nreq * n_mem, MEM_W), nreq, rws)
    y_s = _mixout(xs_pad, a_s, b_s, c_s, w_out, g_post, nreq * rws)

    def real(a2d):
        return a2d.reshape(nreq, rws, -1)[:, n_padrow:]

    new = (p_c, p_n, p_m,
           k.reshape(nb, seq, SA_H, SA_DH), v.reshape(nb, seq, SA_H, SA_DH), kidx.reshape(nb, seq, IDX_D),
           mk.reshape(nb, n_mem, MEM_H, MEM_DH), mv.reshape(nb, n_mem, MEM_H, MEM_DH),
           s_c, s_n, s_m,
           real(k_s).reshape(nreq, t_dec, SA_H, SA_DH), real(v_s).reshape(nreq, t_dec, SA_H, SA_DH), real(kidx_s))
    return y_p, real(y_s), new


def kernel(x_prompt, x_sample, state_mlstm_C, state_mlstm_n, state_mlstm_m, cache_k, cache_v, cache_kidx,
           cache_mem_k, cache_mem_v, page_table, mem_prompt, g_pre, w_in, b_gates, g_head, w_mem_k, w_mem_v,
           g_mem, w_out, g_post):
    xp, xs = x_prompt, x_sample
    per_layer = []
    for l in range(w_in.shape[0]):
        xp, xs, new = _layer(xp, xs, state_mlstm_C[l], state_mlstm_n[l], state_mlstm_m[l],
                             cache_k[l], cache_v[l], cache_kidx[l], cache_mem_k[l], cache_mem_v[l],
                             page_table, mem_prompt, g_pre[l], w_in[l], b_gates[l], g_head[l],
                             w_mem_k[l], w_mem_v[l], g_mem[l], w_out[l], g_post[l])
        per_layer.append(new)
    stacked = [jnp.stack(a) for a in zip(*per_layer)]
    return (xp, xs, *stacked)
```

```python
import functools

import jax
import jax.numpy as jnp
from jax import lax
from jax.experimental import pallas as pl
from jax.experimental.pallas import tpu as pltpu

F32 = jnp.float32
BF16 = jnp.bfloat16

D_MODEL = 2048
ML_H = 4
ML_W = D_MODEL // 2
ML_DH = ML_W // ML_H
SA_H = 4
SA_W = D_MODEL // 4
SA_DH = SA_W // SA_H
MEM_H = 4
MEM_W = D_MODEL // 4
MEM_DH = MEM_W // MEM_H
IDX_H = 8
IDX_D = 64
IDX_SCALE = (IDX_H * IDX_D) ** -0.5
TOPK_MAX = 256
ROPE_THETA = 10000.0
LOG2E = 1.4426950408889634
EPS = 1e-6
PAGE_SIZE = 128

LANES = 128
CB = 512
SMALL_IG = 64
SMALL_LF = 68
SMALL_W = 72
SAMPLE_ROWS = 16
SEL_ROWS = 8
VMEM_LIMIT = 56 * 1024 * 1024
NEG_INF = float("-inf")
POS_INF = float("inf")

_NT = (((1,), (1,)), ((), ()))
_TN = (((0,), (0,)), ((), ()))


def _cparams(sem):
    return pltpu.CompilerParams(dimension_semantics=sem, vmem_limit_bytes=VMEM_LIMIT)


def _sigmoid(x):
    return 1.0 / (1.0 + jnp.exp(-x))


def _silu(x):
    return x * _sigmoid(x)


def _log_sigmoid(x):
    return jnp.minimum(x, 0.0) - jnp.log1p(jnp.exp(-jnp.abs(x)))


def _split3(x):
    hi = x.astype(BF16)
    r = x - hi.astype(F32)
    mid = r.astype(BF16)
    lo = (r - mid.astype(F32)).astype(BF16)
    return hi, mid, lo


def _rope128(x, cos, sin_signed):
    return x * cos + pltpu.roll(x, 64, 1) * sin_signed


def _rope64(x, cos, sin_signed):
    lane = lax.broadcasted_iota(jnp.int32, x.shape, 1)
    first_half = (lane % 64) < 32
    partner = jnp.where(first_half, pltpu.roll(x, 96, 1), pltpu.roll(x, 32, 1))
    return x * cos + partner * sin_signed


def _normed(x_ref, g_ref):
    x = x_ref[...]
    return (x * lax.rsqrt(jnp.mean(x * x, axis=-1, keepdims=True) + EPS) * g_ref[...]).astype(BF16)


def _proj_ml_kernel(x_ref, g_ref, w_ref, qkv_ref, oz_ref, u_ref):
    u_ref[...] = _normed(x_ref, g_ref)
    n_qkv = 3 * ML_W // CB
    for cb in range(5 * ML_W // CB):
        acc = lax.dot_general(u_ref[...], w_ref[cb * CB:(cb + 1) * CB, :], _NT, preferred_element_type=F32)
        if ML_W <= cb * CB < 2 * ML_W:
            acc = acc * (ML_DH ** -0.5)
        if cb < n_qkv:
            qkv_ref[:, cb * CB:(cb + 1) * CB] = acc.astype(BF16)
        else:
            oz_ref[:, (cb - n_qkv) * CB:(cb - n_qkv + 1) * CB] = acc


def _proj_rest_kernel(x_ref, g_ref, wsa_ref, wmem_ref, ws_ref, wst_ref, c128_ref, s128_ref, c64_ref, s64_ref,
                      saq_ref, k_ref, v_ref, saz_ref, memq_ref, memz_ref, idxq_ref,
                      kidx_ref, small_ref, smallt_ref, u_ref):
    u_ref[...] = _normed(x_ref, g_ref)
    sm = jnp.dot(u_ref[...], ws_ref[...], preferred_element_type=F32)
    small_ref[...] = sm
    kidx_ref[...] = _rope64(sm, c64_ref[...], s64_ref[...])[:, :IDX_D]
    smallt_ref[...] = lax.dot_general(wst_ref[...], u_ref[...], _NT, preferred_element_type=F32)

    def block(cb, w_ref=wsa_ref):
        return lax.dot_general(u_ref[...], w_ref[cb * CB:(cb + 1) * CB, :], _NT, preferred_element_type=F32)

    def rope_heads(acc, fn, cos_ref, sin_ref):
        return jnp.concatenate(
            [fn(acc[:, h * LANES:(h + 1) * LANES], cos_ref[...], sin_ref[...]) for h in range(CB // LANES)], axis=1)

    saq_ref[...] = (rope_heads(block(0), _rope128, c128_ref, s128_ref) * (SA_DH ** -0.5 * LOG2E)).astype(BF16)
    k_ref[...] = rope_heads(block(1), _rope128, c128_ref, s128_ref)
    v_ref[...] = block(2)
    saz_ref[...] = block(3)
    idxq_ref[...] = rope_heads(block(4), _rope64, c64_ref, s64_ref).astype(BF16)
    memq_ref[...] = (block(0, wmem_ref) * (MEM_DH ** -0.5)).astype(BF16)
    memz_ref[...] = block(1, wmem_ref)


def _project(x2d, g_pre, weights, tabs, tm):
    w_ml, w_sa, w_mem, w_small, w_small_t = weights
    rows = x2d.shape[0]
    c128, s128, c64, s64 = tabs
    ntab = c128.shape[0] // tm
    n_ml = 5 * ML_W
    row_only = lambda i: (i, 0)
    tab_map = lambda i: (i % ntab, 0)
    const = lambda i: (0, 0)
    resident = pl.Buffered(1)
    g2d = g_pre.reshape(1, D_MODEL)

    qkv, oz = pl.pallas_call(
        _proj_ml_kernel,
        out_shape=(jax.ShapeDtypeStruct((rows, 3 * ML_W), BF16),
                   jax.ShapeDtypeStruct((rows, 2 * ML_W), F32)),
        grid=(rows // tm,),
        in_specs=[pl.BlockSpec((tm, D_MODEL), row_only),
                  pl.BlockSpec((1, D_MODEL), const),
                  pl.BlockSpec((n_ml, D_MODEL), const, pipeline_mode=resident)],
        out_specs=(pl.BlockSpec((tm, 3 * ML_W), row_only), pl.BlockSpec((tm, 2 * ML_W), row_only)),
        scratch_shapes=[pltpu.VMEM((tm, D_MODEL), BF16)],
        compiler_params=_cparams(("arbitrary",)),
        name="proj_ml",
    )(x2d, g2d, w_ml)

    out_shape = (
        jax.ShapeDtypeStruct((rows, SA_W), BF16),
        jax.ShapeDtypeStruct((rows, SA_W), F32),
        jax.ShapeDtypeStruct((rows, SA_W), F32),
        jax.ShapeDtypeStruct((rows, SA_W), F32),
        jax.ShapeDtypeStruct((rows, MEM_W), BF16),
        jax.ShapeDtypeStruct((rows, MEM_W), F32),
        jax.ShapeDtypeStruct((rows, IDX_H * IDX_D), BF16),
        jax.ShapeDtypeStruct((rows, IDX_D), F32),
        jax.ShapeDtypeStruct((rows, LANES), F32),
        jax.ShapeDtypeStruct((LANES, rows), F32),
    )
    out_specs = (
        pl.BlockSpec((tm, CB), row_only),
        pl.BlockSpec((tm, CB), row_only),
        pl.BlockSpec((tm, CB), row_only),
        pl.BlockSpec((tm, CB), row_only),
        pl.BlockSpec((tm, CB), row_only),
        pl.BlockSpec((tm, CB), row_only),
        pl.BlockSpec((tm, CB), row_only),
        pl.BlockSpec((tm, IDX_D), row_only),
        pl.BlockSpec((tm, LANES), row_only),
        pl.BlockSpec((LANES, tm), lambda i: (0, i)),
    )
    in_specs = [
        pl.BlockSpec((tm, D_MODEL), row_only),
        pl.BlockSpec((1, D_MODEL), const),
        pl.BlockSpec(w_sa.shape, const, pipeline_mode=resident),
        pl.BlockSpec(w_mem.shape, const, pipeline_mode=resident),
        pl.BlockSpec((D_MODEL, LANES), const, pipeline_mode=resident),
        pl.BlockSpec((LANES, D_MODEL), const, pipeline_mode=resident),
        pl.BlockSpec((tm, LANES), tab_map),
        pl.BlockSpec((tm, LANES), tab_map),
        pl.BlockSpec((tm, LANES), tab_map),
        pl.BlockSpec((tm, LANES), tab_map),
    ]
    rest = pl.pallas_call(
        _proj_rest_kernel,
        out_shape=out_shape,
        grid=(rows // tm,),
        in_specs=in_specs,
        out_specs=out_specs,
        scratch_shapes=[pltpu.VMEM((tm, D_MODEL), BF16)],
        compiler_params=_cparams(("arbitrary",)),
        name="proj_rest",
    )(x2d, g2d, w_sa, w_mem, w_small, w_small_t, c128, s128, c64, s64)
    return (qkv, oz, *rest)


def _rope_tables(pos):
    def tab(half):
        inv = ROPE_THETA ** (-jnp.arange(half, dtype=F32) / half)
        ang = pos.astype(F32)[:, None] * inv[None, :]
        return jnp.cos(ang), jnp.sin(ang)

    c, s = tab(SA_DH // 2)
    c128 = jnp.concatenate([c, c], axis=1)
    s128 = jnp.concatenate([-s, s], axis=1)
    c, s = tab(IDX_D // 2)
    c64 = jnp.concatenate([c, c, c, c], axis=1)
    s64 = jnp.concatenate([-s, s, -s, s], axis=1)
    return c128, s128, c64, s64


def _relayout_w_in(w_in):
    off = {}
    o = 0
    for name, w in (('ml_q', ML_W), ('ml_k', ML_W), ('ml_v', ML_W), ('ml_o', ML_W), ('ml_z', ML_W),
                    ('ml_i', ML_H), ('ml_f', ML_H), ('sa_q', SA_W), ('sa_k', SA_W), ('sa_v', SA_W),
                    ('sa_z', SA_W), ('idx_q', IDX_H * IDX_D), ('idx_k', IDX_D), ('idx_w', IDX_H),
                    ('mem_q', MEM_W), ('mem_z', MEM_W)):
        off[name] = (o, w)
        o += w

    w_t = w_in.T

    def col(name):
        a, w = off[name]
        return w_t[a:a + w]

    def span(first, last):
        return w_t[off[first][0]:off[last][0] + off[last][1]].astype(BF16)

    w_ml = span('ml_q', 'ml_z')
    w_sa = span('sa_q', 'idx_q')
    w_mem = span('mem_q', 'mem_z')
    small_t = jnp.concatenate([
        col('idx_k'), col('ml_i'), col('ml_f'), col('idx_w'),
        jnp.zeros((LANES - IDX_D - 2 * ML_H - IDX_H, D_MODEL), F32)], axis=0).astype(BF16)
    return (w_ml, w_sa, w_mem, small_t.T, small_t)


def _memkv_kernel(m_ref, g_ref, wk_ref, wv_ref, k_ref, v_ref):
    x = m_ref[...]
    u = (x * lax.rsqrt(jnp.mean(x * x, axis=-1, keepdims=True) + EPS) * g_ref[...]).astype(BF16)
    k_ref[...] = jnp.dot(u, wk_ref[...], preferred_element_type=F32)
    v_ref[...] = jnp.dot(u, wv_ref[...], preferred_element_type=F32)


def _memkv(mem2d, g_mem, wk, wv, n_mem):
    rows = mem2d.shape[0]
    row = lambda i: (i, 0)
    const = lambda i: (0, 0)
    return pl.pallas_call(
        _memkv_kernel,
        out_shape=(jax.ShapeDtypeStruct((rows, MEM_W), F32), jax.ShapeDtypeStruct((rows, MEM_W), F32)),
        grid=(rows // n_mem,),
        in_specs=[pl.BlockSpec((n_mem, D_MODEL), row), pl.BlockSpec((1, D_MODEL), const),
                  pl.BlockSpec((D_MODEL, MEM_W), const), pl.BlockSpec((D_MODEL, MEM_W), const)],
        out_specs=(pl.BlockSpec((n_mem, MEM_W), row), pl.BlockSpec((n_mem, MEM_W), row)),
        compiler_params=_cparams(("arbitrary",)),
        name="memkv",
    )(mem2d, g_mem.reshape(1, D_MODEL), wk.astype(BF16), wv.astype(BF16))


def _mlstm_kernel(qkv_ref, oz_ref, gc_ref, gt_ref, bcol_ref, brow_ref, gh_ref, c0_ref, n0_ref, m0_ref,
                  a_ref, cout_ref, nout_ref, mout_ref, c_s, n_s, m_s, *, c, n_pad):
    ci = pl.program_id(1)

    @pl.when(ci == 0)
    def _():
        c_s[...] = c0_ref[0]
        n_s[...] = n0_ref[0]
        m_s[...] = m0_ref[0]

    ri = lax.broadcasted_iota(jnp.int32, (c, c), 0)
    cj = lax.broadcasted_iota(jnp.int32, (c, c), 1)
    causal = cj <= ri
    tri = jnp.where(causal, 1.0, 0.0).astype(BF16)
    tri_t = jnp.where(ri <= cj, 1.0, 0.0).astype(BF16)

    g_c = gc_ref[...] + bcol_ref[...]
    pad_c = lax.broadcasted_iota(jnp.int32, (c, LANES), 0) < n_pad
    ig_c = jnp.where(pad_c, NEG_INF, g_c)
    lf_c = jnp.where(pad_c, 0.0, _log_sigmoid(g_c))
    b_c = sum(jnp.dot(tri, p, preferred_element_type=F32) for p in _split3(lf_c))
    g_r = gt_ref[0] + brow_ref[...]
    pad_r = lax.broadcasted_iota(jnp.int32, (SAMPLE_ROWS, c), 1) < n_pad
    ig_r = jnp.where(pad_r, NEG_INF, g_r)
    lf_r = jnp.where(pad_r, 0.0, _log_sigmoid(g_r))
    b_r = sum(jnp.dot(p, tri_t, preferred_element_type=F32) for p in _split3(lf_r))

    for h in range(ML_H):
        hs = slice(h * ML_DH, (h + 1) * ML_DH)
        m_prev = m_s[h:h + 1, 0:1]
        b_t = b_c[:, SMALL_LF + h:SMALL_LF + h + 1]
        igc = ig_c[:, SMALL_IG + h:SMALL_IG + h + 1]
        b_s = b_r[ML_H + h:ML_H + h + 1, :]
        igr = ig_r[h:h + 1, :]
        a = jnp.where(causal, b_t - b_s + igr, NEG_INF)
        bm = b_t + m_prev
        m_t = jnp.maximum(bm, jnp.max(a, axis=1, keepdims=True))
        inter = jnp.exp(bm - m_t)
        dmat = jnp.exp(a - m_t)
        q = qkv_ref[:, h * ML_DH:(h + 1) * ML_DH]
        k = qkv_ref[:, ML_W + h * ML_DH:ML_W + (h + 1) * ML_DH]
        v = qkv_ref[:, 2 * ML_W + h * ML_DH:2 * ML_W + (h + 1) * ML_DH]
        s = lax.dot_general(q, k, _NT, preferred_element_type=F32) * dmat
        c_h = c_s[h]
        n_h = n_s[h:h + 1, :]
        num = (jnp.dot(s.astype(BF16), v, preferred_element_type=F32)
               + inter * jnp.dot(q, c_h.astype(BF16), preferred_element_type=F32))
        qn = (jnp.sum(s, axis=1, keepdims=True)
              + inter * jnp.sum(q.astype(F32) * n_h, axis=1, keepdims=True))
        hh = num / jnp.maximum(jnp.abs(qn), jnp.exp(-m_t))
        hh = hh * lax.rsqrt(jnp.mean(hh * hh, axis=1, keepdims=True) + EPS)
        o = oz_ref[:, h * ML_DH:(h + 1) * ML_DH]
        z = oz_ref[:, ML_W + h * ML_DH:ML_W + (h + 1) * ML_DH]
        a_ref[:, hs] = (hh * gh_ref[:, hs] * _sigmoid(o) * _silu(z)).astype(BF16)

        m_new = m_t[c - 1:c, :]
        b_last = b_t[c - 1:c, :]
        w_end = jnp.exp(b_last - b_t + igc - m_new)
        decay = jnp.exp(b_last + m_prev - m_new)
        kw = k.astype(F32) * w_end
        c_s[h] = decay * c_h + lax.dot_general(kw.astype(BF16), v, _TN, preferred_element_type=F32)
        n_s[h:h + 1, :] = decay * n_h + jnp.sum(kw, axis=0, keepdims=True)
        m_s[h:h + 1, :] = jnp.broadcast_to(m_new, (1, LANES))

    @pl.when(ci == pl.num_programs(1) - 1)
    def _():
        cout_ref[0] = c_s[...]
        nout_ref[0] = n_s[...]
        mout_ref[0] = m_s[...]


def _mlstm(qkv, oz, small, small_t, b_gates, g_head, c0, n0, m0, nb, c, n_pad):
    rows = qkv.shape[0]
    nc = rows // (nb * c)
    bias_col = jnp.zeros((1, LANES), F32).at[0, SMALL_IG:SMALL_IG + 2 * ML_H].set(b_gates)
    bias_row = jnp.zeros((SAMPLE_ROWS, 1), F32).at[:2 * ML_H, 0].set(b_gates)
    m0b = jnp.zeros((nb, 8, LANES), F32).at[:, :ML_H, :].set(jnp.broadcast_to(m0[:, :, None], (nb, ML_H, LANES)))
    rowblk = lambda b, i: (b * nc + i, 0)
    const = lambda b, i: (0, 0)
    gates_t = small_t[SMALL_IG:SMALL_IG + SAMPLE_ROWS].reshape(SAMPLE_ROWS, rows // c, c).transpose(1, 0, 2)
    out_shape = (
        jax.ShapeDtypeStruct((rows, ML_W), BF16),
        jax.ShapeDtypeStruct((nb, ML_H, ML_DH, ML_DH), F32),
        jax.ShapeDtypeStruct((nb, ML_H, ML_DH), F32),
        jax.ShapeDtypeStruct((nb, 8, LANES), F32),
    )
    st4 = lambda b, i: (b, 0, 0, 0)
    st3 = lambda b, i: (b, 0, 0)
    a, c_out, n_out, m_out = pl.pallas_call(
        functools.partial(_mlstm_kernel, c=c, n_pad=n_pad),
        out_shape=out_shape,
        grid=(nb, nc),
        in_specs=[
            pl.BlockSpec((c, 3 * ML_W), rowblk),
            pl.BlockSpec((c, 2 * ML_W), rowblk),
            pl.BlockSpec((c, LANES), rowblk),
            pl.BlockSpec((1, SAMPLE_ROWS, c), lambda b, i: (b * nc + i, 0, 0)),
            pl.BlockSpec((1, LANES), const),
            pl.BlockSpec((SAMPLE_ROWS, 1), const),
            pl.BlockSpec((1, ML_W), const),
            pl.BlockSpec((1, ML_H, ML_DH, ML_DH), st4),
            pl.BlockSpec((1, ML_H, ML_DH), st3),
            pl.BlockSpec((1, 8, LANES), st3),
        ],
        out_specs=(
            pl.BlockSpec((c, ML_W), rowblk),
            pl.BlockSpec((1, ML_H, ML_DH, ML_DH), st4),
            pl.BlockSpec((1, ML_H, ML_DH), st3),
            pl.BlockSpec((1, 8, LANES), st3),
        ),
        scratch_shapes=[pltpu.VMEM((ML_H, ML_DH, ML_DH), F32), pltpu.VMEM((ML_H, ML_DH), F32),
                        pltpu.VMEM((8, LANES), F32)],
        compiler_params=_cparams(("arbitrary", "arbitrary")),
        name="mlstm",
    )(qkv, oz, small, gates_t, bias_col, bias_row, g_head.reshape(1, ML_W), c0, n0, m0b)
    return a, c_out, n_out, m_out[:, :ML_H, 0]


_REDUCERS = {"sum": (jnp.sum, jnp.add), "max": (jnp.max, jnp.maximum), "min": (jnp.min, jnp.minimum)}
REDUCE_CHAINS = 8


def _reduce(x, axis, op):
    fn, combine = _REDUCERS[op]
    unit = 8 if axis == 0 else LANES
    n = x.shape[axis]
    units = n // unit
    if n % unit or units < 2 * REDUCE_CHAINS:
        return fn(x, axis=axis, keepdims=True)
    base, rem = divmod(units, REDUCE_CHAINS)
    parts, start = [], 0
    for i in range(REDUCE_CHAINS):
        size = (base + (1 if i < rem else 0)) * unit
        piece = x[start:start + size] if axis == 0 else x[:, start:start + size]
        parts.append(fn(piece, axis=axis, keepdims=True))
        start += size
    while len(parts) > 1:
        parts = [combine(parts[i], parts[i + 1]) for i in range(0, len(parts), 2)]
    return parts[0]


def _count(pred, axis):
    return _reduce(jnp.where(pred, 1.0, 0.0), axis, "sum")


def _kth_largest(x_ref, k, axis, n_bisect):
    kf = float(k)
    x = x_ref[...]
    hi = _reduce(x, axis, "max")
    lo = _reduce(jnp.where(x == NEG_INF, POS_INF, x), axis, "min")

    def bisect(_, carry):
        lo, hi = carry
        mid = 0.5 * (lo + hi)
        ge = _count(x_ref[...] >= mid, axis) >= kf
        return jnp.where(ge, mid, lo), jnp.where(ge, hi, mid)

    lo, hi = lax.fori_loop(0, n_bisect, bisect, (lo, hi))

    def cond(st):
        return st[4] < 0.5

    def body(st):
        lo, strict, thr, done, _ = st
        xx = x_ref[...]
        above = _reduce(jnp.where(xx > lo, xx, POS_INF), axis, "min")
        at_lo = (strict < 0.5) & (_count(xx == lo, axis) > 0.5)
        cmin = jnp.where(at_lo, lo, above)
        fin = (_count(xx > cmin, axis) < kf) | (cmin == POS_INF)
        active = done < 0.5
        thr = jnp.where(active, cmin, thr)
        lo = jnp.where(active, cmin, lo)
        done = jnp.where(fin, 1.0, done)
        return lo, jnp.ones_like(strict), thr, done, jnp.min(done)

    zeros = jnp.zeros_like(lo)
    _, _, thr, _, _ = lax.while_loop(cond, body, (lo, zeros, lo, zeros, jnp.float32(0.0)))
    return thr


def _dsa_kernel(q_ref, z_ref, idxq_ref, wt_ref, k_ref, v_ref, kidx_ref, o_ref,
                kb_s, vt_s, kib_s, x_s, sel_s, *, n_keys, qb, n_sel, n_bisect, key_step):
    j = pl.program_id(1)

    @pl.when(j == 0)
    def _():
        kb_s[...] = k_ref[...].astype(BF16)
        vt_s[...] = v_ref[...].T.astype(BF16)
        kib_s[...] = kidx_ref[...].astype(BF16)

    def attend(nk):
        xs = x_s.at[0:nk]
        ss = sel_s.at[0:nk]
        key = lax.broadcasted_iota(jnp.int32, (nk, qb), 0)
        qpos = j * qb + lax.broadcasted_iota(jnp.int32, (nk, qb), 1)
        valid = key <= qpos
        sc = jnp.zeros((nk, qb), F32)
        for h in range(IDX_H):
            d = lax.dot_general(kib_s[0:nk, :], idxq_ref[:, h * IDX_D:(h + 1) * IDX_D], _NT,
                                preferred_element_type=F32)
            sc = sc + jnp.maximum(d, 0.0) * (wt_ref[h:h + 1, :] * IDX_SCALE)
        xs[...] = jnp.where(valid, sc, NEG_INF)
        ss[...] = jnp.where(valid, 1.0, 0.0)

        @pl.when((j + 1) * qb > n_sel)
        def _():
            kf = float(n_sel)
            thr = _kth_largest(xs, n_sel, 0, n_bisect)
            x = xs[...]
            need = kf - _count(x > thr, 0)
            n_tie = _count(x == thr, 0)
            qrow = j * qb + lax.broadcasted_iota(jnp.int32, (1, qb), 1)
            small = (qrow + 1) <= n_sel
            ss[...] = jnp.where(small, jnp.where(valid, 1.0, 0.0), jnp.where(x >= thr, 1.0, 0.0))
            excess = jnp.max(jnp.where((n_tie > need) & jnp.logical_not(small), 1.0, 0.0))

            @pl.when(excess > 0.5)
            def _():
                tb = min(256, nk)
                r_i = lax.broadcasted_iota(jnp.int32, (tb, tb), 0)
                c_i = lax.broadcasted_iota(jnp.int32, (tb, tb), 1)
                lower = jnp.where(c_i < r_i, 1.0, 0.0).astype(BF16)
                carry = jnp.zeros((1, qb), F32)
                for blk in range(nk // tb):
                    rows = slice(blk * tb, (blk + 1) * tb)
                    xb = x_s[rows, :]
                    tie = jnp.where(xb == thr, 1.0, 0.0)
                    rank = jnp.dot(lower, tie.astype(BF16), preferred_element_type=F32) + carry
                    keep = (xb > thr) | ((xb == thr) & (rank < need))
                    keyb = blk * tb + lax.broadcasted_iota(jnp.int32, (tb, qb), 0)
                    qposb = j * qb + lax.broadcasted_iota(jnp.int32, (tb, qb), 1)
                    smallb = jnp.where(keyb <= qposb, 1.0, 0.0)
                    sel_s[rows, :] = jnp.where(small, smallb, jnp.where(keep, 1.0, 0.0))
                    carry = carry + jnp.sum(tie, axis=0, keepdims=True)

        sel = ss[...] > 0.5
        for h in range(SA_H):
            hs = slice(h * SA_DH, (h + 1) * SA_DH)
            st = lax.dot_general(kb_s[0:nk, hs], q_ref[:, hs], _NT, preferred_element_type=F32)
            st = jnp.where(sel, st, NEG_INF)
            mx = _reduce(st, 0, "max")
            p = jnp.exp2(st - mx)
            l = _reduce(p, 0, "sum")
            ot = jnp.dot(vt_s[hs, 0:nk], p.astype(BF16), preferred_element_type=F32) / l
            o_ref[:, hs] = (ot.T * _silu(z_ref[:, hs])).astype(BF16)

    n_ext = n_keys // key_step
    for e in range(n_ext):
        nk = (e + 1) * key_step
        lo_j = e * key_step // qb
        hi_j = nk // qb

        @pl.when((j >= lo_j) & (j < hi_j))
        def _(nk=nk):
            attend(nk)


def _dsa_prompt(saq, saz, idxq, small_t, k, v, kidx, nb, seq):
    rows = saq.shape[0]
    qb = min(seq, 128)
    nq = seq // qb
    n_sel = min(TOPK_MAX, seq // 4)
    qblk = lambda b, j: (b * nq + j, 0)
    per_b = lambda b, j: (b, 0)
    wt_blk = SMALL_W // 8
    return pl.pallas_call(
        functools.partial(_dsa_kernel, n_keys=seq, qb=qb, n_sel=n_sel, n_bisect=20, key_step=min(256, seq)),
        out_shape=jax.ShapeDtypeStruct((rows, SA_W), BF16),
        grid=(nb, nq),
        in_specs=[
            pl.BlockSpec((qb, SA_W), qblk),
            pl.BlockSpec((qb, SA_W), qblk),
            pl.BlockSpec((qb, IDX_H * IDX_D), qblk),
            pl.BlockSpec((8, qb), lambda b, j: (wt_blk, b * nq + j)),
            pl.BlockSpec((seq, SA_W), per_b),
            pl.BlockSpec((seq, SA_W), per_b),
            pl.BlockSpec((seq, IDX_D), per_b),
        ],
        out_specs=pl.BlockSpec((qb, SA_W), qblk),
        scratch_shapes=[pltpu.VMEM((seq, SA_W), BF16), pltpu.VMEM((SA_W, seq), BF16),
                        pltpu.VMEM((seq, IDX_D), BF16), pltpu.VMEM((seq, qb), F32),
                        pltpu.VMEM((seq, qb), F32)],
        compiler_params=_cparams(("arbitrary", "arbitrary")),
        name="dsa",
    )(saq, saz, idxq, small_t, k, v, kidx)


def _sel_kernel(pt_ref, idxq_ref, small_ref, kinew_ref, *rest, npg, n_past, n_sel, n_real, n_bisect):
    page_refs = rest[:npg]
    sel_ref, x_s = rest[npg:]
    g = pl.program_id(1)
    rws = SAMPLE_ROWS
    top = rws - SEL_ROWS
    pk = n_past + LANES
    gk = npg * PAGE_SIZE

    qs = jnp.concatenate([idxq_ref[:, h * IDX_D:(h + 1) * IDX_D] for h in range(IDX_H)], axis=0)
    small = small_ref[...]

    def scores(d):
        sc = jnp.zeros((SEL_ROWS, d.shape[1]), F32)
        for h in range(IDX_H):
            w = small[top:, SMALL_W + h:SMALL_W + h + 1] * IDX_SCALE
            sc = sc + jnp.maximum(d[h * rws + top:(h + 1) * rws, :], 0.0) * w
        return sc

    kp_t = jnp.concatenate([r[0] for r in page_refs], axis=1).astype(BF16)
    x_s[:, pl.ds(pl.multiple_of(g * gk, LANES), gk)] = scores(jnp.dot(qs, kp_t, preferred_element_type=F32))

    @pl.when(g == pl.num_programs(1) - 1)
    def _():
        knew = jnp.concatenate([kinew_ref[...], jnp.zeros((LANES - rws, IDX_D), F32)], axis=0).astype(BF16)
        row = top + lax.broadcasted_iota(jnp.int32, (SEL_ROWS, LANES), 0)
        col = lax.broadcasted_iota(jnp.int32, (SEL_ROWS, LANES), 1)
        ok = (col >= rws - n_real) & (col < rws) & (col <= row)
        d_new = lax.dot_general(qs, knew, _NT, preferred_element_type=F32)
        x_s[:, n_past:pk] = jnp.where(ok, scores(d_new), NEG_INF)

        kf = float(n_sel)
        thr = _kth_largest(x_s, n_sel, 1, n_bisect)
        x = x_s[...]
        need = kf - _count(x > thr, 1)
        n_tie = _count(x == thr, 1)
        sel_ref[0, 0:top, :] = jnp.ones((top, pk), F32)
        sel_ref[0, top:rws, :] = jnp.where(x >= thr, 1.0, 0.0)
        real = lax.broadcasted_iota(jnp.int32, (SEL_ROWS, 1), 0) >= SEL_ROWS - n_real
        excess = jnp.max(jnp.where((n_tie > need) & real, 1.0, 0.0))

        @pl.when(excess > 0.5)
        def _():
            r_i = lax.broadcasted_iota(jnp.int32, (LANES, LANES), 0)
            c_i = lax.broadcasted_iota(jnp.int32, (LANES, LANES), 1)
            upper = jnp.where(r_i < c_i, 1.0, 0.0).astype(BF16)

            def blk(i, carry):
                cols = pl.ds(pl.multiple_of(i * LANES, LANES), LANES)
                xb = x_s[:, cols]
                tie = jnp.where(xb == thr, 1.0, 0.0)
                rank = jnp.dot(tie.astype(BF16), upper, preferred_element_type=F32) + carry
                keep = (xb > thr) | ((xb == thr) & (rank < need))
                sel_ref[0, top:rws, cols] = jnp.where(keep, 1.0, 0.0)
                return carry + jnp.sum(tie, axis=1, keepdims=True)

            lax.fori_loop(0, pk // LANES, blk, jnp.zeros((SEL_ROWS, 1), F32))


def _dsa_select(page_table, idxq, small, kidx_new, cache_kidx, n_sel, n_real):
    nreq, n_pages = page_table.shape
    n_past = n_pages * PAGE_SIZE
    pk = n_past + LANES
    npg = min(16, n_pages)
    req = lambda b, g, pt: (b, 0)

    def page_map(i):
        return lambda b, g, pt: (pt[b, g * npg + i], 0, 0)

    grid_spec = pltpu.PrefetchScalarGridSpec(
        num_scalar_prefetch=1,
        grid=(nreq, n_pages // npg),
        in_specs=[pl.BlockSpec((SAMPLE_ROWS, IDX_H * IDX_D), req),
                  pl.BlockSpec((SAMPLE_ROWS, LANES), req),
                  pl.BlockSpec((SAMPLE_ROWS, IDX_D), req)]
                 + [pl.BlockSpec((1, IDX_D, PAGE_SIZE), page_map(i)) for i in range(npg)],
        out_specs=pl.BlockSpec((1, SAMPLE_ROWS, pk), lambda b, g, pt: (b, 0, 0)),
        scratch_shapes=[pltpu.VMEM((SEL_ROWS, pk), F32)],
    )
    assert n_real <= SEL_ROWS
    return pl.pallas_call(
        functools.partial(_sel_kernel, npg=npg, n_past=n_past, n_sel=n_sel, n_real=n_real, n_bisect=20),
        out_shape=jax.ShapeDtypeStruct((nreq, SAMPLE_ROWS, pk), F32),
        grid_spec=grid_spec,
        compiler_params=_cparams(("arbitrary", "arbitrary")),
        name="dsa_sel",
    )(page_table, idxq, small, kidx_new, *([cache_kidx] * npg))


def _att_kernel(pt_ref, q_ref, z_ref, knew_ref, vnew_ref, sel_ref, seltail_ref, *rest, npg):
    k_refs = rest[:npg]
    v_refs = rest[npg:2 * npg]
    o_ref = rest[2 * npg]
    m_s, l_s, acc_s = rest[2 * npg + 1:]
    g = pl.program_id(1)
    rws = SAMPLE_ROWS
    floor = -1e30

    @pl.when(g == 0)
    def _():
        m_s[...] = jnp.full(m_s.shape, floor, F32)
        l_s[...] = jnp.zeros(l_s.shape, F32)
        acc_s[...] = jnp.zeros(acc_s.shape, F32)

    heads = range(SA_H)
    hsl = [slice(h * SA_DH, (h + 1) * SA_DH) for h in heads]

    def update(kbs, vbs, keep):
        m_old = [m_s[h][:, 0:1] for h in heads]
        l_old = [l_s[h][:, 0:1] for h in heads]
        acc_old = [acc_s[:, hsl[h]] for h in heads]
        s = [lax.dot_general(q_ref[:, hsl[h]], kbs[h], _NT, preferred_element_type=F32) for h in heads]
        m_new = [jnp.maximum(m_old[h], jnp.max(jnp.where(keep, s[h], floor), axis=1, keepdims=True)) for h in heads]
        p = [jnp.where(keep, jnp.exp2(s[h] - m_new[h]), 0.0) for h in heads]
        pv = [jnp.dot(p[h].astype(BF16), vbs[h], preferred_element_type=F32) for h in heads]
        alpha = [jnp.exp2(m_old[h] - m_new[h]) for h in heads]
        l_new = [alpha[h] * l_old[h] + jnp.sum(p[h], axis=1, keepdims=True) for h in heads]
        for h in heads:
            acc_s[:, hsl[h]] = alpha[h] * acc_old[h] + pv[h]
            l_s[h] = jnp.broadcast_to(l_new[h], (rws, LANES))
            m_s[h] = jnp.broadcast_to(m_new[h], (rws, LANES))

    def head_rows(refs, h):
        return jnp.concatenate([r[0, pl.ds(h, PAGE_SIZE, stride=SA_H), :] for r in refs], axis=0).astype(BF16)

    update([head_rows(k_refs, h) for h in heads], [head_rows(v_refs, h) for h in heads], sel_ref[0] > 0.5)

    @pl.when(g == pl.num_programs(1) - 1)
    def _():
        update([knew_ref[:, hsl[h]].astype(BF16) for h in heads], [vnew_ref[:, hsl[h]].astype(BF16) for h in heads],
               seltail_ref[0][:, :rws] > 0.5)
        for h in heads:
            o_ref[:, hsl[h]] = (acc_s[:, hsl[h]] / l_s[h][:, 0:1] * _silu(z_ref[:, hsl[h]])).astype(BF16)


def _dsa_attend(page_table, saq, saz, k_new, v_new, sel, cache_k, cache_v):
    nreq, n_pages = page_table.shape
    n_past = n_pages * PAGE_SIZE
    npg = min(8, n_pages)
    prow = PAGE_SIZE * SA_H
    req = lambda b, g, pt: (b, 0)

    def page_map(i):
        return lambda b, g, pt: (pt[b, g * npg + i], 0, 0)

    page_specs = [pl.BlockSpec((1, prow, SA_DH), page_map(i)) for i in range(npg)]
    grid_spec = pltpu.PrefetchScalarGridSpec(
        num_scalar_prefetch=1,
        grid=(nreq, n_pages // npg),
        in_specs=[pl.BlockSpec((SAMPLE_ROWS, SA_W), req), pl.BlockSpec((SAMPLE_ROWS, SA_W), req),
                  pl.BlockSpec((SAMPLE_ROWS, SA_W), req), pl.BlockSpec((SAMPLE_ROWS, SA_W), req),
                  pl.BlockSpec((1, SAMPLE_ROWS, npg * PAGE_SIZE), lambda b, g, pt: (b, 0, g)),
                  pl.BlockSpec((1, SAMPLE_ROWS, LANES), lambda b, g, pt: (b, 0, n_past // LANES))]
                 + page_specs + page_specs,
        out_specs=pl.BlockSpec((SAMPLE_ROWS, SA_W), req),
        scratch_shapes=[pltpu.VMEM((SA_H, SAMPLE_ROWS, LANES), F32), pltpu.VMEM((SA_H, SAMPLE_ROWS, LANES), F32),
                        pltpu.VMEM((SAMPLE_ROWS, SA_W), F32)],
    )
    return pl.pallas_call(
        functools.partial(_att_kernel, npg=npg),
        out_shape=jax.ShapeDtypeStruct((nreq * SAMPLE_ROWS, SA_W), BF16),
        grid_spec=grid_spec,
        compiler_params=_cparams(("arbitrary", "arbitrary")),
        name="dsa_att",
    )(page_table, saq, saz, k_new, v_new, sel, sel, *([cache_k] * npg), *([cache_v] * npg))


def _memattn_kernel(q_ref, z_ref, mk_ref, mv_ref, o_ref):
    mk = mk_ref[...].astype(BF16)
    mv = mv_ref[...].astype(BF16)
    for h in range(MEM_H):
        hs = slice(h * MEM_DH, (h + 1) * MEM_DH)
        s = lax.dot_general(q_ref[:, hs], mk[:, hs], _NT, preferred_element_type=F32)
        p = jnp.exp(s - jnp.max(s, axis=1, keepdims=True))
        l = jnp.sum(p, axis=1, keepdims=True)
        o = jnp.dot(p.astype(BF16), mv[:, hs], preferred_element_type=F32) / l
        o_ref[:, hs] = (o * _silu(z_ref[:, hs])).astype(BF16)


def _memattn(memq, memz, mk, mv, nb, tq):
    rows = memq.shape[0]
    n_mem = mk.shape[0] // nb
    nq = rows // (nb * tq)
    qblk = lambda b, i: (b * nq + i, 0)
    per_b = lambda b, i: (b, 0)
    return pl.pallas_call(
        _memattn_kernel,
        out_shape=jax.ShapeDtypeStruct((rows, MEM_W), BF16),
        grid=(nb, nq),
        in_specs=[pl.BlockSpec((tq, MEM_W), qblk), pl.BlockSpec((tq, MEM_W), qblk),
                  pl.BlockSpec((n_mem, MEM_W), per_b), pl.BlockSpec((n_mem, MEM_W), per_b)],
        out_specs=pl.BlockSpec((tq, MEM_W), qblk),
        compiler_params=_cparams(("arbitrary", "arbitrary")),
        name="memattn",
    )(memq, memz, mk, mv)


def _mixout_kernel(x_ref, a_ref, b_ref, c_ref, wa_ref, wb_ref, wc_ref, g_ref, y_ref):
    acc = (jnp.dot(a_ref[...], wa_ref[...], preferred_element_type=F32)
           + jnp.dot(b_ref[...], wb_ref[...], preferred_element_type=F32)
           + jnp.dot(c_ref[...], wc_ref[...], preferred_element_type=F32))
    y = acc * lax.rsqrt(jnp.mean(acc * acc, axis=-1, keepdims=True) + EPS) * g_ref[...]
    y_ref[...] = x_ref[...] + y


def _mixout(x2d, a, b, c, w_out, g_post, tm):
    rows = x2d.shape[0]
    wb16 = w_out.astype(BF16)
    row = lambda i: (i, 0)
    const = lambda i: (0, 0)
    return pl.pallas_call(
        _mixout_kernel,
        out_shape=jax.ShapeDtypeStruct((rows, D_MODEL), F32),
        grid=(rows // tm,),
        in_specs=[pl.BlockSpec((tm, D_MODEL), row), pl.BlockSpec((tm, ML_W), row),
                  pl.BlockSpec((tm, SA_W), row), pl.BlockSpec((tm, MEM_W), row),
                  pl.BlockSpec((ML_W, D_MODEL), const), pl.BlockSpec((SA_W, D_MODEL), const),
                  pl.BlockSpec((MEM_W, D_MODEL), const), pl.BlockSpec((1, D_MODEL), const)],
        out_specs=pl.BlockSpec((tm, D_MODEL), row),
        compiler_params=_cparams(("arbitrary",)),
        name="mixout",
    )(x2d, a, b, c, wb16[:ML_W], wb16[ML_W:ML_W + SA_W], wb16[ML_W + SA_W:], g_post.reshape(1, D_MODEL))


def _layer(x_p, x_s, st_c, st_n, st_m, c_k, c_v, c_kidx, c_mk, c_mv, page_table, mem_prompt,
           g_pre, w_in, b_gates, g_head, w_mem_k, w_mem_v, g_mem, w_out, g_post):
    nb, seq, _ = x_p.shape
    nreq, t_dec, _ = x_s.shape
    n_mem = mem_prompt.shape[1]
    n_past = page_table.shape[1] * PAGE_SIZE
    weights = _relayout_w_in(w_in)

    tm = min(512, seq)
    tabs = _rope_tables(jnp.arange(seq, dtype=jnp.int32))
    (qkv, oz, saq, k, v, saz, memq, memz, idxq, kidx, small, small_t) = _project(
        x_p.reshape(nb * seq, D_MODEL), g_pre, weights, tabs, tm)
    c0 = jnp.zeros((nb, ML_H, ML_DH, ML_DH), F32)
    n0 = jnp.zeros((nb, ML_H, ML_DH), F32)
    m0 = jnp.zeros((nb, ML_H), F32)
    a_p, p_c, p_n, p_m = _mlstm(qkv, oz, small, small_t, b_gates, g_head, c0, n0, m0, nb, min(256, seq), 0)
    b_p = _dsa_prompt(saq, saz, idxq, small_t, k, v, kidx, nb, seq)
    mk, mv = _memkv(mem_prompt.reshape(nb * n_mem, D_MODEL), g_mem, w_mem_k, w_mem_v, n_mem)
    c_p = _memattn(memq, memz, mk, mv, nb, min(256, seq))
    y_p = _mixout(x_p.reshape(nb * seq, D_MODEL), a_p, b_p, c_p, w_out, g_post, tm).reshape(nb, seq, D_MODEL)

    rws = SAMPLE_ROWS
    n_padrow = rws - t_dec
    xs_pad = jnp.concatenate([jnp.zeros((nreq, n_padrow, D_MODEL), F32), x_s], axis=1).reshape(nreq * rws, D_MODEL)
    pos_s = jnp.tile(jnp.concatenate([jnp.zeros((n_padrow,), jnp.int32),
                                      n_past + jnp.arange(t_dec, dtype=jnp.int32)]), nreq)
    tabs_s = _rope_tables(pos_s)
    (qkv_s, oz_s, saq_s, k_s, v_s, saz_s, memq_s, memz_s, idxq_s, kidx_s, small_s, small_t_s) = _project(
        xs_pad, g_pre, weights, tabs_s, nreq * rws)
    a_s, s_c, s_n, s_m = _mlstm(qkv_s, oz_s, small_s, small_t_s, b_gates, g_head, st_c, st_n, st_m,
                                nreq, rws, n_padrow)
    n_sel = min(TOPK_MAX, (n_past + t_dec) // 4)
    sel = _dsa_select(page_table, idxq_s, small_s, kidx_s, jnp.swapaxes(c_kidx, 1, 2), n_sel, t_dec)
    b_s = _dsa_attend(page_table, saq_s, saz_s, k_s, v_s, sel,
                      c_k.reshape(c_k.shape[0], PAGE_SIZE * SA_H, SA_DH),
                      c_v.reshape(c_v.shape[0], PAGE_SIZE * SA_H, SA_DH))
    c_s = _memattn(memq_s, memz_s, c_mk.reshape(nreq * n_mem, MEM_W), c_mv.reshape(nreq * n_mem, MEM_W), nreq, rws)
    y_s = _mixout(xs_pad, a_s, b_s, c_s, w_out, g_post, nreq * rws)

    def real(a2d):
        return a2d.reshape(nreq, rws, -1)[:, n_padrow:]

    new = (p_c, p_n, p_m,
           k.reshape(nb, seq, SA_H, SA_DH), v.reshape(nb, seq, SA_H, SA_DH), kidx.reshape(nb, seq, IDX_D),
           mk.reshape(nb, n_mem, MEM_H, MEM_DH), mv.reshape(nb, n_mem, MEM_H, MEM_DH),
           s_c, s_n, s_m,
           real(k_s).reshape(nreq, t_dec, SA_H, SA_DH), real(v_s).reshape(nreq, t_dec, SA_H, SA_DH), real(kidx_s))
    return y_p, real(y_s), new


def kernel(x_prompt, x_sample, state_mlstm_C, state_mlstm_n, state_mlstm_m, cache_k, cache_v, cache_kidx,
           cache_mem_k, cache_mem_v, page_table, mem_prompt, g_pre, w_in, b_gates, g_head, w_mem_k, w_mem_v,
           g_mem, w_out, g_post):
    xp, xs = x_prompt, x_sample
    per_layer = []
    for l in range(w_in.shape[0]):
        xp, xs, new = _layer(xp, xs, state_mlstm_C[l], state_mlstm_n[l], state_mlstm_m[l],
                             cache_k[l], cache_v[l], cache_kidx[l], cache_mem_k[l], cache_mem_v[l],
                             page_table, mem_prompt, g_pre[l], w_in[l], b_gates[l], g_head[l],
                             w_mem_k[l], w_mem_v[l], g_mem[l], w_out[l], g_post[l])
        per_layer.append(new)
    stacked = [jnp.stack(a) for a in zip(*per_layer)]
    return (xp, xs, *stacked)
```

```python
import functools

import jax
import jax.numpy as jnp
from jax import lax
from jax.experimental import pallas as pl
from jax.experimental.pallas import tpu as pltpu

F32 = jnp.float32
BF16 = jnp.bfloat16

D_MODEL = 2048
ML_H = 4
ML_W = D_MODEL // 2
ML_DH = ML_W // ML_H
SA_H = 4
SA_W = D_MODEL // 4
SA_DH = SA_W // SA_H
MEM_H = 4
MEM_W = D_MODEL // 4
MEM_DH = MEM_W // MEM_H
IDX_H = 8
IDX_D = 64
IDX_SCALE = (IDX_H * IDX_D) ** -0.5
TOPK_MAX = 256
ROPE_THETA = 10000.0
LOG2E = 1.4426950408889634
EPS = 1e-6
PAGE_SIZE = 128

LANES = 128
CB = 512
SMALL_IG = 64
SMALL_LF = 68
SMALL_W = 72
SAMPLE_ROWS = 16
SEL_ROWS = 8
SEL_PAGES_PER_STEP = 32
ATT_PAGES_PER_STEP = 16
ML_CHUNK = 256
VMEM_LIMIT = 56 * 1024 * 1024
NEG_INF = float("-inf")
POS_INF = float("inf")

_NT = (((1,), (1,)), ((), ()))
_TN = (((0,), (0,)), ((), ()))


def _cparams(sem):
    return pltpu.CompilerParams(dimension_semantics=sem, vmem_limit_bytes=VMEM_LIMIT)


def _sigmoid(x):
    return 1.0 / (1.0 + jnp.exp(-x))


def _silu(x):
    return x * _sigmoid(x)


def _log_sigmoid(x):
    return jnp.minimum(x, 0.0) - jnp.log1p(jnp.exp(-jnp.abs(x)))


def _split3(x):
    hi = x.astype(BF16)
    r = x - hi.astype(F32)
    mid = r.astype(BF16)
    lo = (r - mid.astype(F32)).astype(BF16)
    return hi, mid, lo


def _rope128(x, cos, sin_signed):
    return x * cos + pltpu.roll(x, 64, 1) * sin_signed


def _rope64(x, cos, sin_signed):
    lane = lax.broadcasted_iota(jnp.int32, x.shape, 1)
    first_half = (lane % 64) < 32
    partner = jnp.where(first_half, pltpu.roll(x, 96, 1), pltpu.roll(x, 32, 1))
    return x * cos + partner * sin_signed


def _normed(x_ref, g_ref):
    x = x_ref[...]
    return (x * lax.rsqrt(jnp.mean(x * x, axis=-1, keepdims=True) + EPS) * g_ref[...]).astype(BF16)


def _proj_ml_kernel(x_ref, g_ref, w_ref, qkv_ref, oz_ref, u_ref):
    u_ref[...] = _normed(x_ref, g_ref)
    n_qkv = 3 * ML_W // CB
    for cb in range(5 * ML_W // CB):
        acc = lax.dot_general(u_ref[...], w_ref[cb * CB:(cb + 1) * CB, :], _NT, preferred_element_type=F32)
        if ML_W <= cb * CB < 2 * ML_W:
            acc = acc * (ML_DH ** -0.5)
        if cb < n_qkv:
            qkv_ref[:, cb * CB:(cb + 1) * CB] = acc.astype(BF16)
        else:
            oz_ref[:, (cb - n_qkv) * CB:(cb - n_qkv + 1) * CB] = acc


def _proj_rest_kernel(x_ref, g_ref, wsa_ref, wmem_ref, ws_ref, wst_ref, c128_ref, s128_ref, c64_ref, s64_ref,
                      saq_ref, k_ref, v_ref, saz_ref, memq_ref, memz_ref, idxq_ref,
                      kidx_ref, small_ref, smallt_ref, u_ref):
    u_ref[...] = _normed(x_ref, g_ref)
    sm = jnp.dot(u_ref[...], ws_ref[...], preferred_element_type=F32)
    small_ref[...] = sm
    kidx_ref[...] = _rope64(sm, c64_ref[...], s64_ref[...])[:, :IDX_D]
    smallt_ref[...] = lax.dot_general(wst_ref[...], u_ref[...], _NT, preferred_element_type=F32)

    def block(cb, w_ref=wsa_ref):
        return lax.dot_general(u_ref[...], w_ref[cb * CB:(cb + 1) * CB, :], _NT, preferred_element_type=F32)

    def rope_heads(acc, fn, cos_ref, sin_ref):
        return jnp.concatenate(
            [fn(acc[:, h * LANES:(h + 1) * LANES], cos_ref[...], sin_ref[...]) for h in range(CB // LANES)], axis=1)

    saq_ref[...] = (rope_heads(block(0), _rope128, c128_ref, s128_ref) * (SA_DH ** -0.5 * LOG2E)).astype(BF16)
    tm = x_ref.shape[0]
    k_acc = block(1)
    v_acc = block(2)
    for h in range(SA_H):
        lanes = slice(h * SA_DH, (h + 1) * SA_DH)
        k_ref[pl.ds(h, tm, stride=SA_H), :] = _rope128(k_acc[:, lanes], c128_ref[...], s128_ref[...])
        v_ref[pl.ds(h, tm, stride=SA_H), :] = v_acc[:, lanes]
    saz_ref[...] = block(3)
    idxq_ref[...] = rope_heads(block(4), _rope64, c64_ref, s64_ref).astype(BF16)
    memq_ref[...] = (block(0, wmem_ref) * (MEM_DH ** -0.5)).astype(BF16)
    memz_ref[...] = block(1, wmem_ref)


def _project(x2d, g_pre, weights, tabs, tm):
    w_ml, w_sa, w_mem, w_small, w_small_t = weights
    rows = x2d.shape[0]
    c128, s128, c64, s64 = tabs
    ntab = c128.shape[0] // tm
    n_ml = 5 * ML_W
    row_only = lambda i: (i, 0)
    tab_map = lambda i: (i % ntab, 0)
    const = lambda i: (0, 0)
    resident = pl.Buffered(1)
    g2d = g_pre.reshape(1, D_MODEL)

    qkv, oz = pl.pallas_call(
        _proj_ml_kernel,
        out_shape=(jax.ShapeDtypeStruct((rows, 3 * ML_W), BF16),
                   jax.ShapeDtypeStruct((rows, 2 * ML_W), F32)),
        grid=(rows // tm,),
        in_specs=[pl.BlockSpec((tm, D_MODEL), row_only),
                  pl.BlockSpec((1, D_MODEL), const),
                  pl.BlockSpec((n_ml, D_MODEL), const, pipeline_mode=resident)],
        out_specs=(pl.BlockSpec((tm, 3 * ML_W), row_only), pl.BlockSpec((tm, 2 * ML_W), row_only)),
        scratch_shapes=[pltpu.VMEM((tm, D_MODEL), BF16)],
        compiler_params=_cparams(("arbitrary",)),
        name="proj_ml",
    )(x2d, g2d, w_ml)

    out_shape = (
        jax.ShapeDtypeStruct((rows, SA_W), BF16),
        jax.ShapeDtypeStruct((rows * SA_H, SA_DH), F32),
        jax.ShapeDtypeStruct((rows * SA_H, SA_DH), F32),
        jax.ShapeDtypeStruct((rows, SA_W), F32),
        jax.ShapeDtypeStruct((rows, MEM_W), BF16),
        jax.ShapeDtypeStruct((rows, MEM_W), F32),
        jax.ShapeDtypeStruct((rows, IDX_H * IDX_D), BF16),
        jax.ShapeDtypeStruct((rows, IDX_D), F32),
        jax.ShapeDtypeStruct((rows, LANES), F32),
        jax.ShapeDtypeStruct((LANES, rows), F32),
    )
    out_specs = (
        pl.BlockSpec((tm, CB), row_only),
        pl.BlockSpec((tm * SA_H, SA_DH), row_only),
        pl.BlockSpec((tm * SA_H, SA_DH), row_only),
        pl.BlockSpec((tm, CB), row_only),
        pl.BlockSpec((tm, CB), row_only),
        pl.BlockSpec((tm, CB), row_only),
        pl.BlockSpec((tm, CB), row_only),
        pl.BlockSpec((tm, IDX_D), row_only),
        pl.BlockSpec((tm, LANES), row_only),
        pl.BlockSpec((LANES, tm), lambda i: (0, i)),
    )
    in_specs = [
        pl.BlockSpec((tm, D_MODEL), row_only),
        pl.BlockSpec((1, D_MODEL), const),
        pl.BlockSpec(w_sa.shape, const, pipeline_mode=resident),
        pl.BlockSpec(w_mem.shape, const, pipeline_mode=resident),
        pl.BlockSpec((D_MODEL, LANES), const, pipeline_mode=resident),
        pl.BlockSpec((LANES, D_MODEL), const, pipeline_mode=resident),
        pl.BlockSpec((tm, LANES), tab_map),
        pl.BlockSpec((tm, LANES), tab_map),
        pl.BlockSpec((tm, LANES), tab_map),
        pl.BlockSpec((tm, LANES), tab_map),
    ]
    rest = pl.pallas_call(
        _proj_rest_kernel,
        out_shape=out_shape,
        grid=(rows // tm,),
        in_specs=in_specs,
        out_specs=out_specs,
        scratch_shapes=[pltpu.VMEM((tm, D_MODEL), BF16)],
        compiler_params=_cparams(("arbitrary",)),
        name="proj_rest",
    )(x2d, g2d, w_sa, w_mem, w_small, w_small_t, c128, s128, c64, s64)
    return (qkv, oz, *rest)


def _rope_tables(pos):
    def tab(half):
        inv = ROPE_THETA ** (-jnp.arange(half, dtype=F32) / half)
        ang = pos.astype(F32)[:, None] * inv[None, :]
        return jnp.cos(ang), jnp.sin(ang)

    c, s = tab(SA_DH // 2)
    c128 = jnp.concatenate([c, c], axis=1)
    s128 = jnp.concatenate([-s, s], axis=1)
    c, s = tab(IDX_D // 2)
    c64 = jnp.concatenate([c, c, c, c], axis=1)
    s64 = jnp.concatenate([-s, s, -s, s], axis=1)
    return c128, s128, c64, s64


def _relayout_w_in(w_in):
    off = {}
    o = 0
    for name, w in (('ml_q', ML_W), ('ml_k', ML_W), ('ml_v', ML_W), ('ml_o', ML_W), ('ml_z', ML_W),
                    ('ml_i', ML_H), ('ml_f', ML_H), ('sa_q', SA_W), ('sa_k', SA_W), ('sa_v', SA_W),
                    ('sa_z', SA_W), ('idx_q', IDX_H * IDX_D), ('idx_k', IDX_D), ('idx_w', IDX_H),
                    ('mem_q', MEM_W), ('mem_z', MEM_W)):
        off[name] = (o, w)
        o += w

    w_t = w_in.T

    def col(name):
        a, w = off[name]
        return w_t[a:a + w]

    def span(first, last):
        return w_t[off[first][0]:off[last][0] + off[last][1]].astype(BF16)

    w_ml = span('ml_q', 'ml_z')
    w_sa = span('sa_q', 'idx_q')
    w_mem = span('mem_q', 'mem_z')
    small_t = jnp.concatenate([
        col('idx_k'), col('ml_i'), col('ml_f'), col('idx_w'),
        jnp.zeros((LANES - IDX_D - 2 * ML_H - IDX_H, D_MODEL), F32)], axis=0).astype(BF16)
    return (w_ml, w_sa, w_mem, small_t.T, small_t)


def _memkv_kernel(m_ref, g_ref, wk_ref, wv_ref, k_ref, v_ref):
    x = m_ref[...]
    u = (x * lax.rsqrt(jnp.mean(x * x, axis=-1, keepdims=True) + EPS) * g_ref[...]).astype(BF16)
    k_ref[...] = jnp.dot(u, wk_ref[...], preferred_element_type=F32)
    v_ref[...] = jnp.dot(u, wv_ref[...], preferred_element_type=F32)


def _memkv(mem2d, g_mem, wk, wv, n_mem):
    rows = mem2d.shape[0]
    row = lambda i: (i, 0)
    const = lambda i: (0, 0)
    return pl.pallas_call(
        _memkv_kernel,
        out_shape=(jax.ShapeDtypeStruct((rows, MEM_W), F32), jax.ShapeDtypeStruct((rows, MEM_W), F32)),
        grid=(rows // n_mem,),
        in_specs=[pl.BlockSpec((n_mem, D_MODEL), row), pl.BlockSpec((1, D_MODEL), const),
                  pl.BlockSpec((D_MODEL, MEM_W), const), pl.BlockSpec((D_MODEL, MEM_W), const)],
        out_specs=(pl.BlockSpec((n_mem, MEM_W), row), pl.BlockSpec((n_mem, MEM_W), row)),
        compiler_params=_cparams(("arbitrary",)),
        name="memkv",
    )(mem2d, g_mem.reshape(1, D_MODEL), wk.astype(BF16), wv.astype(BF16))


def _mlstm_kernel(qkv_ref, oz_ref, gc_ref, gt_ref, bcol_ref, brow_ref, gh_ref, c0_ref, n0_ref, m0_ref,
                  a_ref, cout_ref, nout_ref, mout_ref, c_s, n_s, m_s, *, c, n_pad):
    ci = pl.program_id(1)

    @pl.when(ci == 0)
    def _():
        c_s[...] = c0_ref[0]
        n_s[...] = n0_ref[0]
        m_s[...] = m0_ref[0]

    ri = lax.broadcasted_iota(jnp.int32, (c, c), 0)
    cj = lax.broadcasted_iota(jnp.int32, (c, c), 1)
    causal = cj <= ri
    tri = jnp.where(causal, 1.0, 0.0).astype(BF16)
    tri_t = jnp.where(ri <= cj, 1.0, 0.0).astype(BF16)

    g_c = gc_ref[...] + bcol_ref[...]
    pad_c = lax.broadcasted_iota(jnp.int32, (c, LANES), 0) < n_pad
    ig_c = jnp.where(pad_c, NEG_INF, g_c)
    lf_c = jnp.where(pad_c, 0.0, _log_sigmoid(g_c))
    b_c = sum(jnp.dot(tri, p, preferred_element_type=F32) for p in _split3(lf_c))
    g_r = gt_ref[0] + brow_ref[...]
    pad_r = lax.broadcasted_iota(jnp.int32, (SAMPLE_ROWS, c), 1) < n_pad
    ig_r = jnp.where(pad_r, NEG_INF, g_r)
    lf_r = jnp.where(pad_r, 0.0, _log_sigmoid(g_r))
    b_r = sum(jnp.dot(p, tri_t, preferred_element_type=F32) for p in _split3(lf_r))

    for h in range(ML_H):
        hs = slice(h * ML_DH, (h + 1) * ML_DH)
        m_prev = m_s[h:h + 1, 0:1]
        b_t = b_c[:, SMALL_LF + h:SMALL_LF + h + 1]
        igc = ig_c[:, SMALL_IG + h:SMALL_IG + h + 1]
        b_s = b_r[ML_H + h:ML_H + h + 1, :]
        igr = ig_r[h:h + 1, :]
        a = jnp.where(causal, b_t - b_s + igr, NEG_INF)
        bm = b_t + m_prev
        m_t = jnp.maximum(bm, jnp.max(a, axis=1, keepdims=True))
        inter = jnp.exp(bm - m_t)
        dmat = jnp.exp(a - m_t)
        q = qkv_ref[:, h * ML_DH:(h + 1) * ML_DH]
        k = qkv_ref[:, ML_W + h * ML_DH:ML_W + (h + 1) * ML_DH]
        v = qkv_ref[:, 2 * ML_W + h * ML_DH:2 * ML_W + (h + 1) * ML_DH]
        s = lax.dot_general(q, k, _NT, preferred_element_type=F32) * dmat
        c_h = c_s[h]
        n_h = n_s[h:h + 1, :]
        num = (jnp.dot(s.astype(BF16), v, preferred_element_type=F32)
               + inter * jnp.dot(q, c_h.astype(BF16), preferred_element_type=F32))
        qn = (jnp.sum(s, axis=1, keepdims=True)
              + inter * jnp.sum(q.astype(F32) * n_h, axis=1, keepdims=True))
        hh = num / jnp.maximum(jnp.abs(qn), jnp.exp(-m_t))
        hh = hh * lax.rsqrt(jnp.mean(hh * hh, axis=1, keepdims=True) + EPS)
        o = oz_ref[:, h * ML_DH:(h + 1) * ML_DH]
        z = oz_ref[:, ML_W + h * ML_DH:ML_W + (h + 1) * ML_DH]
        gate = z / ((1.0 + jnp.exp(-o)) * (1.0 + jnp.exp(-z)))
        a_ref[:, hs] = (hh * gh_ref[:, hs] * gate).astype(BF16)

        m_new = m_t[c - 1:c, :]
        b_last = b_t[c - 1:c, :]
        w_end = jnp.exp(b_last - b_t + igc - m_new)
        decay = jnp.exp(b_last + m_prev - m_new)
        kw = k.astype(F32) * w_end
        c_s[h] = decay * c_h + lax.dot_general(kw.astype(BF16), v, _TN, preferred_element_type=F32)
        n_s[h:h + 1, :] = decay * n_h + jnp.sum(kw, axis=0, keepdims=True)
        m_s[h:h + 1, :] = jnp.broadcast_to(m_new, (1, LANES))

    @pl.when(ci == pl.num_programs(1) - 1)
    def _():
        cout_ref[0] = c_s[...]
        nout_ref[0] = n_s[...]
        mout_ref[0] = m_s[...]


def _mlstm(qkv, oz, small, small_t, b_gates, g_head, c0, n0, m0, nb, c, n_pad):
    rows = qkv.shape[0]
    nc = rows // (nb * c)
    bias_col = jnp.zeros((1, LANES), F32).at[0, SMALL_IG:SMALL_IG + 2 * ML_H].set(b_gates)
    bias_row = jnp.zeros((SAMPLE_ROWS, 1), F32).at[:2 * ML_H, 0].set(b_gates)
    m0b = jnp.zeros((nb, 8, LANES), F32).at[:, :ML_H, :].set(jnp.broadcast_to(m0[:, :, None], (nb, ML_H, LANES)))
    rowblk = lambda b, i: (b * nc + i, 0)
    const = lambda b, i: (0, 0)
    gates_t = small_t[SMALL_IG:SMALL_IG + SAMPLE_ROWS].reshape(SAMPLE_ROWS, rows // c, c).transpose(1, 0, 2)
    out_shape = (
        jax.ShapeDtypeStruct((rows, ML_W), BF16),
        jax.ShapeDtypeStruct((nb, ML_H, ML_DH, ML_DH), F32),
        jax.ShapeDtypeStruct((nb, ML_H, ML_DH), F32),
        jax.ShapeDtypeStruct((nb, 8, LANES), F32),
    )
    st4 = lambda b, i: (b, 0, 0, 0)
    st3 = lambda b, i: (b, 0, 0)
    a, c_out, n_out, m_out = pl.pallas_call(
        functools.partial(_mlstm_kernel, c=c, n_pad=n_pad),
        out_shape=out_shape,
        grid=(nb, nc),
        in_specs=[
            pl.BlockSpec((c, 3 * ML_W), rowblk),
            pl.BlockSpec((c, 2 * ML_W), rowblk),
            pl.BlockSpec((c, LANES), rowblk),
            pl.BlockSpec((1, SAMPLE_ROWS, c), lambda b, i: (b * nc + i, 0, 0)),
            pl.BlockSpec((1, LANES), const),
            pl.BlockSpec((SAMPLE_ROWS, 1), const),
            pl.BlockSpec((1, ML_W), const),
            pl.BlockSpec((1, ML_H, ML_DH, ML_DH), st4),
            pl.BlockSpec((1, ML_H, ML_DH), st3),
            pl.BlockSpec((1, 8, LANES), st3),
        ],
        out_specs=(
            pl.BlockSpec((c, ML_W), rowblk),
            pl.BlockSpec((1, ML_H, ML_DH, ML_DH), st4),
            pl.BlockSpec((1, ML_H, ML_DH), st3),
            pl.BlockSpec((1, 8, LANES), st3),
        ),
        scratch_shapes=[pltpu.VMEM((ML_H, ML_DH, ML_DH), F32), pltpu.VMEM((ML_H, ML_DH), F32),
                        pltpu.VMEM((8, LANES), F32)],
        compiler_params=_cparams(("arbitrary", "arbitrary")),
        name="mlstm",
    )(qkv, oz, small, gates_t, bias_col, bias_row, g_head.reshape(1, ML_W), c0, n0, m0b)
    return a, c_out, n_out, m_out[:, :ML_H, 0]


_REDUCERS = {"sum": (jnp.sum, jnp.add), "max": (jnp.max, jnp.maximum), "min": (jnp.min, jnp.minimum)}
REDUCE_CHAINS = 8


def _reduce(x, axis, op):
    fn, combine = _REDUCERS[op]
    unit = 8 if axis == 0 else LANES
    n = x.shape[axis]
    units = n // unit
    if n % unit or units < 2 * REDUCE_CHAINS:
        return fn(x, axis=axis, keepdims=True)
    base, rem = divmod(units, REDUCE_CHAINS)
    parts, start = [], 0
    for i in range(REDUCE_CHAINS):
        size = (base + (1 if i < rem else 0)) * unit
        piece = x[start:start + size] if axis == 0 else x[:, start:start + size]
        parts.append(fn(piece, axis=axis, keepdims=True))
        start += size
    while len(parts) > 1:
        parts = [combine(parts[i], parts[i + 1]) for i in range(0, len(parts), 2)]
    return parts[0]


def _count(pred, axis):
    return _reduce(jnp.where(pred, 1.0, 0.0), axis, "sum")


def _kth_largest(x_ref, k, axis, n_bisect):
    kf = float(k)
    x = x_ref[...]
    hi = _reduce(x, axis, "max")
    lo = _reduce(jnp.where(x == NEG_INF, POS_INF, x), axis, "min")

    def bisect(_, carry):
        lo, hi = carry
        mid = 0.5 * (lo + hi)
        ge = _count(x_ref[...] >= mid, axis) >= kf
        return jnp.where(ge, mid, lo), jnp.where(ge, hi, mid)

    lo, hi = lax.fori_loop(0, n_bisect, bisect, (lo, hi))

    def finished(cmin, xx):
        return jnp.where((_count(xx > cmin, axis) < kf) | (cmin == POS_INF), 1.0, 0.0)

    xx = x_ref[...]
    thr = _reduce(jnp.where(xx >= lo, xx, POS_INF), axis, "min")
    done = finished(thr, xx)

    def cond(st):
        return st[2] < 0.5

    def body(st):
        thr, done, _ = st
        xx = x_ref[...]
        cmin = _reduce(jnp.where(xx > thr, xx, POS_INF), axis, "min")
        thr = jnp.where(done < 0.5, cmin, thr)
        done = jnp.maximum(done, finished(thr, xx))
        return thr, done, jnp.min(done)

    thr, _, _ = lax.while_loop(cond, body, (thr, done, jnp.min(done)))
    return thr


def _dsa_kernel(q_ref, z_ref, idxq_ref, wt_ref, k_ref, v_ref, kidx_ref, o_ref,
                kb_s, vt_s, kib_s, x_s, sel_s, *, n_keys, qb, n_sel, n_bisect, key_step):
    j = pl.program_id(1)

    @pl.when(j == 0)
    def _():
        for h in range(SA_H):
            lanes = slice(h * SA_DH, (h + 1) * SA_DH)
            kb_s[:, lanes] = k_ref[pl.ds(h, n_keys, stride=SA_H), :].astype(BF16)
            vt_s[lanes, :] = v_ref[pl.ds(h, n_keys, stride=SA_H), :].T.astype(BF16)
        kib_s[...] = kidx_ref[...].astype(BF16)

    def attend(nk):
        xs = x_s.at[0:nk]
        ss = sel_s.at[0:nk]
        key = lax.broadcasted_iota(jnp.int32, (nk, qb), 0)
        qpos = j * qb + lax.broadcasted_iota(jnp.int32, (nk, qb), 1)
        valid = key <= qpos
        sc = jnp.zeros((nk, qb), F32)
        for h in range(IDX_H):
            d = lax.dot_general(kib_s[0:nk, :], idxq_ref[:, h * IDX_D:(h + 1) * IDX_D], _NT,
                                preferred_element_type=F32)
            sc = sc + jnp.maximum(d, 0.0) * (wt_ref[h:h + 1, :] * IDX_SCALE)
        xs[...] = jnp.where(valid, sc, NEG_INF)
        ss[...] = jnp.where(valid, 1.0, 0.0)

        @pl.when((j + 1) * qb > n_sel)
        def _():
            kf = float(n_sel)
            thr = _kth_largest(xs, n_sel, 0, n_bisect)
            x = xs[...]
            need = kf - _count(x > thr, 0)
            n_tie = _count(x == thr, 0)
            qrow = j * qb + lax.broadcasted_iota(jnp.int32, (1, qb), 1)
            small = (qrow + 1) <= n_sel
            ss[...] = jnp.where(small, jnp.where(valid, 1.0, 0.0), jnp.where(x >= thr, 1.0, 0.0))
            excess = jnp.max(jnp.where((n_tie > need) & jnp.logical_not(small), 1.0, 0.0))

            @pl.when(excess > 0.5)
            def _():
                tb = min(256, nk)
                r_i = lax.broadcasted_iota(jnp.int32, (tb, tb), 0)
                c_i = lax.broadcasted_iota(jnp.int32, (tb, tb), 1)
                lower = jnp.where(c_i < r_i, 1.0, 0.0).astype(BF16)
                carry = jnp.zeros((1, qb), F32)
                for blk in range(nk // tb):
                    rows = slice(blk * tb, (blk + 1) * tb)
                    xb = x_s[rows, :]
                    tie = jnp.where(xb == thr, 1.0, 0.0)
                    rank = jnp.dot(lower, tie.astype(BF16), preferred_element_type=F32) + carry
                    keep = (xb > thr) | ((xb == thr) & (rank < need))
                    keyb = blk * tb + lax.broadcasted_iota(jnp.int32, (tb, qb), 0)
                    qposb = j * qb + lax.broadcasted_iota(jnp.int32, (tb, qb), 1)
                    smallb = jnp.where(keyb <= qposb, 1.0, 0.0)
                    sel_s[rows, :] = jnp.where(small, smallb, jnp.where(keep, 1.0, 0.0))
                    carry = carry + jnp.sum(tie, axis=0, keepdims=True)

        sel = ss[...] > 0.5
        for h in range(SA_H):
            hs = slice(h * SA_DH, (h + 1) * SA_DH)
            st = lax.dot_general(kb_s[0:nk, hs], q_ref[:, hs], _NT, preferred_element_type=F32)
            st = jnp.where(sel, st, NEG_INF)
            mx = _reduce(st, 0, "max")
            p = jnp.exp2(st - mx)
            l = _reduce(p, 0, "sum")
            ot = jnp.dot(vt_s[hs, 0:nk], p.astype(BF16), preferred_element_type=F32) / l
            o_ref[:, hs] = (ot.T * _silu(z_ref[:, hs])).astype(BF16)

    n_ext = n_keys // key_step
    for e in range(n_ext):
        nk = (e + 1) * key_step
        lo_j = e * key_step // qb
        hi_j = nk // qb

        @pl.when((j >= lo_j) & (j < hi_j))
        def _(nk=nk):
            attend(nk)


def _dsa_prompt(saq, saz, idxq, small_t, k, v, kidx, nb, seq):
    rows = saq.shape[0]
    qb = min(seq, 128)
    nq = seq // qb
    n_sel = min(TOPK_MAX, seq // 4)
    qblk = lambda b, j: (b * nq + j, 0)
    per_b = lambda b, j: (b, 0)
    wt_blk = SMALL_W // 8
    return pl.pallas_call(
        functools.partial(_dsa_kernel, n_keys=seq, qb=qb, n_sel=n_sel, n_bisect=20, key_step=min(256, seq)),
        out_shape=jax.ShapeDtypeStruct((rows, SA_W), BF16),
        grid=(nb, nq),
        in_specs=[
            pl.BlockSpec((qb, SA_W), qblk),
            pl.BlockSpec((qb, SA_W), qblk),
            pl.BlockSpec((qb, IDX_H * IDX_D), qblk),
            pl.BlockSpec((8, qb), lambda b, j: (wt_blk, b * nq + j)),
            pl.BlockSpec((seq * SA_H, SA_DH), per_b),
            pl.BlockSpec((seq * SA_H, SA_DH), per_b),
            pl.BlockSpec((seq, IDX_D), per_b),
        ],
        out_specs=pl.BlockSpec((qb, SA_W), qblk),
        scratch_shapes=[pltpu.VMEM((seq, SA_W), BF16), pltpu.VMEM((SA_W, seq), BF16),
                        pltpu.VMEM((seq, IDX_D), BF16), pltpu.VMEM((seq, qb), F32),
                        pltpu.VMEM((seq, qb), F32)],
        compiler_params=_cparams(("arbitrary", "arbitrary")),
        name="dsa",
    )(saq, saz, idxq, small_t, k, v, kidx)


def _sel_kernel(pt_ref, idxq_ref, small_ref, kinew_ref, *rest, npg, n_past, n_sel, n_real, n_bisect):
    page_refs = rest[:npg]
    sel_ref, x_s = rest[npg:]
    g = pl.program_id(1)
    rws = SAMPLE_ROWS
    top = rws - SEL_ROWS
    pk = n_past + LANES
    gk = npg * PAGE_SIZE

    qs = jnp.concatenate([idxq_ref[:, h * IDX_D:(h + 1) * IDX_D] for h in range(IDX_H)], axis=0)
    small = small_ref[...]

    def scores(d):
        sc = jnp.zeros((SEL_ROWS, d.shape[1]), F32)
        for h in range(IDX_H):
            w = small[top:, SMALL_W + h:SMALL_W + h + 1] * IDX_SCALE
            sc = sc + jnp.maximum(d[h * rws + top:(h + 1) * rws, :], 0.0) * w
        return sc

    kp_t = jnp.concatenate([r[0] for r in page_refs], axis=1).astype(BF16)
    x_s[:, pl.ds(pl.multiple_of(g * gk, LANES), gk)] = scores(jnp.dot(qs, kp_t, preferred_element_type=F32))

    @pl.when(g == pl.num_programs(1) - 1)
    def _():
        knew = jnp.concatenate([kinew_ref[...], jnp.zeros((LANES - rws, IDX_D), F32)], axis=0).astype(BF16)
        row = top + lax.broadcasted_iota(jnp.int32, (SEL_ROWS, LANES), 0)
        col = lax.broadcasted_iota(jnp.int32, (SEL_ROWS, LANES), 1)
        ok = (col >= rws - n_real) & (col < rws) & (col <= row)
        d_new = lax.dot_general(qs, knew, _NT, preferred_element_type=F32)
        x_s[:, n_past:pk] = jnp.where(ok, scores(d_new), NEG_INF)

        kf = float(n_sel)
        thr = _kth_largest(x_s, n_sel, 1, n_bisect)
        x = x_s[...]
        need = kf - _count(x > thr, 1)
        n_tie = _count(x == thr, 1)
        sel_ref[0, 0:top, :] = jnp.ones((top, pk), F32)
        sel_ref[0, top:rws, :] = jnp.where(x >= thr, 1.0, 0.0)
        real = lax.broadcasted_iota(jnp.int32, (SEL_ROWS, 1), 0) >= SEL_ROWS - n_real
        excess = jnp.max(jnp.where((n_tie > need) & real, 1.0, 0.0))

        @pl.when(excess > 0.5)
        def _():
            r_i = lax.broadcasted_iota(jnp.int32, (LANES, LANES), 0)
            c_i = lax.broadcasted_iota(jnp.int32, (LANES, LANES), 1)
            upper = jnp.where(r_i < c_i, 1.0, 0.0).astype(BF16)

            def blk(i, carry):
                cols = pl.ds(pl.multiple_of(i * LANES, LANES), LANES)
                xb = x_s[:, cols]
                tie = jnp.where(xb == thr, 1.0, 0.0)
                rank = jnp.dot(tie.astype(BF16), upper, preferred_element_type=F32) + carry
                keep = (xb > thr) | ((xb == thr) & (rank < need))
                sel_ref[0, top:rws, cols] = jnp.where(keep, 1.0, 0.0)
                return carry + jnp.sum(tie, axis=1, keepdims=True)

            lax.fori_loop(0, pk // LANES, blk, jnp.zeros((SEL_ROWS, 1), F32))


def _dsa_select(page_table, idxq, small, kidx_new, cache_kidx, n_sel, n_real):
    nreq, n_pages = page_table.shape
    n_past = n_pages * PAGE_SIZE
    pk = n_past + LANES
    npg = min(SEL_PAGES_PER_STEP, n_pages)
    req = lambda b, g, pt: (b, 0)

    def page_map(i):
        return lambda b, g, pt: (pt[b, g * npg + i], 0, 0)

    grid_spec = pltpu.PrefetchScalarGridSpec(
        num_scalar_prefetch=1,
        grid=(nreq, n_pages // npg),
        in_specs=[pl.BlockSpec((SAMPLE_ROWS, IDX_H * IDX_D), req),
                  pl.BlockSpec((SAMPLE_ROWS, LANES), req),
                  pl.BlockSpec((SAMPLE_ROWS, IDX_D), req)]
                 + [pl.BlockSpec((1, IDX_D, PAGE_SIZE), page_map(i)) for i in range(npg)],
        out_specs=pl.BlockSpec((1, SAMPLE_ROWS, pk), lambda b, g, pt: (b, 0, 0)),
        scratch_shapes=[pltpu.VMEM((SEL_ROWS, pk), F32)],
    )
    assert n_real <= SEL_ROWS
    return pl.pallas_call(
        functools.partial(_sel_kernel, npg=npg, n_past=n_past, n_sel=n_sel, n_real=n_real, n_bisect=20),
        out_shape=jax.ShapeDtypeStruct((nreq, SAMPLE_ROWS, pk), F32),
        grid_spec=grid_spec,
        compiler_params=_cparams(("arbitrary", "arbitrary")),
        name="dsa_sel",
    )(page_table, idxq, small, kidx_new, *([cache_kidx] * npg))


def _att_kernel(pt_ref, q_ref, z_ref, knew_ref, vnew_ref, sel_ref, seltail_ref, *rest, npg):
    k_refs = rest[:npg]
    v_refs = rest[npg:2 * npg]
    o_ref = rest[2 * npg]
    m_s, l_s, acc_s = rest[2 * npg + 1:]
    g = pl.program_id(1)
    rws = SAMPLE_ROWS
    floor = -1e30

    @pl.when(g == 0)
    def _():
        m_s[...] = jnp.full(m_s.shape, floor, F32)
        l_s[...] = jnp.zeros(l_s.shape, F32)
        acc_s[...] = jnp.zeros(acc_s.shape, F32)

    heads = range(SA_H)
    hsl = [slice(h * SA_DH, (h + 1) * SA_DH) for h in heads]

    def update(kbs, vbs, keep):
        m_old = [m_s[h][:, 0:1] for h in heads]
        l_old = [l_s[h][:, 0:1] for h in heads]
        acc_old = [acc_s[:, hsl[h]] for h in heads]
        s = [lax.dot_general(q_ref[:, hsl[h]], kbs[h], _NT, preferred_element_type=F32) for h in heads]
        m_new = [jnp.maximum(m_old[h], jnp.max(jnp.where(keep, s[h], floor), axis=1, keepdims=True)) for h in heads]
        p = [jnp.where(keep, jnp.exp2(s[h] - m_new[h]), 0.0) for h in heads]
        pv = [jnp.dot(p[h].astype(BF16), vbs[h], preferred_element_type=F32) for h in heads]
        alpha = [jnp.exp2(m_old[h] - m_new[h]) for h in heads]
        l_new = [alpha[h] * l_old[h] + jnp.sum(p[h], axis=1, keepdims=True) for h in heads]
        for h in heads:
            acc_s[:, hsl[h]] = alpha[h] * acc_old[h] + pv[h]
            l_s[h] = jnp.broadcast_to(l_new[h], (rws, LANES))
            m_s[h] = jnp.broadcast_to(m_new[h], (rws, LANES))

    def head_rows(refs, h):
        return jnp.concatenate([r[0, pl.ds(h, PAGE_SIZE, stride=SA_H), :] for r in refs], axis=0).astype(BF16)

    update([head_rows(k_refs, h) for h in heads], [head_rows(v_refs, h) for h in heads], sel_ref[0] > 0.5)

    @pl.when(g == pl.num_programs(1) - 1)
    def _():
        update([knew_ref[pl.ds(h, rws, stride=SA_H), :].astype(BF16) for h in heads],
               [vnew_ref[pl.ds(h, rws, stride=SA_H), :].astype(BF16) for h in heads],
               seltail_ref[0][:, :rws] > 0.5)
        for h in heads:
            o_ref[:, hsl[h]] = (acc_s[:, hsl[h]] / l_s[h][:, 0:1] * _silu(z_ref[:, hsl[h]])).astype(BF16)


def _dsa_attend(page_table, saq, saz, k_new, v_new, sel, cache_k, cache_v):
    nreq, n_pages = page_table.shape
    n_past = n_pages * PAGE_SIZE
    npg = min(ATT_PAGES_PER_STEP, n_pages)
    prow = PAGE_SIZE * SA_H
    req = lambda b, g, pt: (b, 0)

    def page_map(i):
        return lambda b, g, pt: (pt[b, g * npg + i], 0, 0)

    page_specs = [pl.BlockSpec((1, prow, SA_DH), page_map(i)) for i in range(npg)]
    grid_spec = pltpu.PrefetchScalarGridSpec(
        num_scalar_prefetch=1,
        grid=(nreq, n_pages // npg),
        in_specs=[pl.BlockSpec((SAMPLE_ROWS, SA_W), req), pl.BlockSpec((SAMPLE_ROWS, SA_W), req),
                  pl.BlockSpec((SAMPLE_ROWS * SA_H, SA_DH), req), pl.BlockSpec((SAMPLE_ROWS * SA_H, SA_DH), req),
                  pl.BlockSpec((1, SAMPLE_ROWS, npg * PAGE_SIZE), lambda b, g, pt: (b, 0, g)),
                  pl.BlockSpec((1, SAMPLE_ROWS, LANES), lambda b, g, pt: (b, 0, n_past // LANES))]
                 + page_specs + page_specs,
        out_specs=pl.BlockSpec((SAMPLE_ROWS, SA_W), req),
        scratch_shapes=[pltpu.VMEM((SA_H, SAMPLE_ROWS, LANES), F32), pltpu.VMEM((SA_H, SAMPLE_ROWS, LANES), F32),
                        pltpu.VMEM((SAMPLE_ROWS, SA_W), F32)],
    )
    return pl.pallas_call(
        functools.partial(_att_kernel, npg=npg),
        out_shape=jax.ShapeDtypeStruct((nreq * SAMPLE_ROWS, SA_W), BF16),
        grid_spec=grid_spec,
        compiler_params=_cparams(("arbitrary", "arbitrary")),
        name="dsa_att",
    )(page_table, saq, saz, k_new, v_new, sel, sel, *([cache_k] * npg), *([cache_v] * npg))


def _memattn_kernel(q_ref, z_ref, mk_ref, mv_ref, o_ref):
    mk = mk_ref[...].astype(BF16)
    mv = mv_ref[...].astype(BF16)
    for h in range(MEM_H):
        hs = slice(h * MEM_DH, (h + 1) * MEM_DH)
        s = lax.dot_general(q_ref[:, hs], mk[:, hs], _NT, preferred_element_type=F32)
        p = jnp.exp(s - jnp.max(s, axis=1, keepdims=True))
        l = jnp.sum(p, axis=1, keepdims=True)
        o = jnp.dot(p.astype(BF16), mv[:, hs], preferred_element_type=F32) / l
        o_ref[:, hs] = (o * _silu(z_ref[:, hs])).astype(BF16)


def _memattn(memq, memz, mk, mv, nb, tq):
    rows = memq.shape[0]
    n_mem = mk.shape[0] // nb
    nq = rows // (nb * tq)
    qblk = lambda b, i: (b * nq + i, 0)
    per_b = lambda b, i: (b, 0)
    return pl.pallas_call(
        _memattn_kernel,
        out_shape=jax.ShapeDtypeStruct((rows, MEM_W), BF16),
        grid=(nb, nq),
        in_specs=[pl.BlockSpec((tq, MEM_W), qblk), pl.BlockSpec((tq, MEM_W), qblk),
                  pl.BlockSpec((n_mem, MEM_W), per_b), pl.BlockSpec((n_mem, MEM_W), per_b)],
        out_specs=pl.BlockSpec((tq, MEM_W), qblk),
        compiler_params=_cparams(("arbitrary", "arbitrary")),
        name="memattn",
    )(memq, memz, mk, mv)


def _mixout_kernel(x_ref, a_ref, b_ref, c_ref, wa_ref, wb_ref, wc_ref, g_ref, y_ref):
    acc = (jnp.dot(a_ref[...], wa_ref[...], preferred_element_type=F32)
           + jnp.dot(b_ref[...], wb_ref[...], preferred_element_type=F32)
           + jnp.dot(c_ref[...], wc_ref[...], preferred_element_type=F32))
    y = acc * lax.rsqrt(jnp.mean(acc * acc, axis=-1, keepdims=True) + EPS) * g_ref[...]
    y_ref[...] = x_ref[...] + y


def _mixout(x2d, a, b, c, w_out, g_post, tm):
    rows = x2d.shape[0]
    wb16 = w_out.astype(BF16)
    row = lambda i: (i, 0)
    const = lambda i: (0, 0)
    return pl.pallas_call(
        _mixout_kernel,
        out_shape=jax.ShapeDtypeStruct((rows, D_MODEL), F32),
        grid=(rows // tm,),
        in_specs=[pl.BlockSpec((tm, D_MODEL), row), pl.BlockSpec((tm, ML_W), row),
                  pl.BlockSpec((tm, SA_W), row), pl.BlockSpec((tm, MEM_W), row),
                  pl.BlockSpec((ML_W, D_MODEL), const), pl.BlockSpec((SA_W, D_MODEL), const),
                  pl.BlockSpec((MEM_W, D_MODEL), const), pl.BlockSpec((1, D_MODEL), const)],
        out_specs=pl.BlockSpec((tm, D_MODEL), row),
        compiler_params=_cparams(("arbitrary",)),
        name="mixout",
    )(x2d, a, b, c, wb16[:ML_W], wb16[ML_W:ML_W + SA_W], wb16[ML_W + SA_W:], g_post.reshape(1, D_MODEL))


def _layer(x_p, x_s, st_c, st_n, st_m, c_k, c_v, c_kidx, c_mk, c_mv, page_table, mem_prompt,
           g_pre, w_in, b_gates, g_head, w_mem_k, w_mem_v, g_mem, w_out, g_post):
    nb, seq, _ = x_p.shape
    nreq, t_dec, _ = x_s.shape
    n_mem = mem_prompt.shape[1]
    n_past = page_table.shape[1] * PAGE_SIZE
    weights = _relayout_w_in(w_in)

    tm = min(512, seq)
    tabs = _rope_tables(jnp.arange(seq, dtype=jnp.int32))
    (qkv, oz, saq, k, v, saz, memq, memz, idxq, kidx, small, small_t) = _project(
        x_p.reshape(nb * seq, D_MODEL), g_pre, weights, tabs, tm)
    c0 = jnp.zeros((nb, ML_H, ML_DH, ML_DH), F32)
    n0 = jnp.zeros((nb, ML_H, ML_DH), F32)
    m0 = jnp.zeros((nb, ML_H), F32)
    a_p, p_c, p_n, p_m = _mlstm(qkv, oz, small, small_t, b_gates, g_head, c0, n0, m0, nb, min(ML_CHUNK, seq), 0)
    b_p = _dsa_prompt(saq, saz, idxq, small_t, k, v, kidx, nb, seq)
    mk, mv = _memkv(mem_prompt.reshape(nb * n_mem, D_MODEL), g_mem, w_mem_k, w_mem_v, n_mem)
    c_p = _memattn(memq, memz, mk, mv, nb, min(256, seq))
    y_p = _mixout(x_p.reshape(nb * seq, D_MODEL), a_p, b_p, c_p, w_out, g_post, tm).reshape(nb, seq, D_MODEL)

    rws = SAMPLE_ROWS
    n_padrow = rws - t_dec
    xs_pad = jnp.concatenate([jnp.zeros((nreq, n_padrow, D_MODEL), F32), x_s], axis=1).reshape(nreq * rws, D_MODEL)
    pos_s = jnp.tile(jnp.concatenate([jnp.zeros((n_padrow,), jnp.int32),
                                      n_past + jnp.arange(t_dec, dtype=jnp.int32)]), nreq)
    tabs_s = _rope_tables(pos_s)
    (qkv_s, oz_s, saq_s, k_s, v_s, saz_s, memq_s, memz_s, idxq_s, kidx_s, small_s, small_t_s) = _project(
        xs_pad, g_pre, weights, tabs_s, nreq * rws)
    a_s, s_c, s_n, s_m = _mlstm(qkv_s, oz_s, small_s, small_t_s, b_gates, g_head, st_c, st_n, st_m,
                                nreq, rws, n_padrow)
    n_sel = min(TOPK_MAX, (n_past + t_dec) // 4)
    sel = _dsa_select(page_table, idxq_s, small_s, kidx_s, jnp.swapaxes(c_kidx, 1, 2), n_sel, t_dec)
    b_s = _dsa_attend(page_table, saq_s, saz_s, k_s, v_s, sel,
                      c_k.reshape(c_k.shape[0], PAGE_SIZE * SA_H, SA_DH),
                      c_v.reshape(c_v.shape[0], PAGE_SIZE * SA_H, SA_DH))
    c_s = _memattn(memq_s, memz_s, c_mk.reshape(nreq * n_mem, MEM_W), c_mv.reshape(nreq * n_mem, MEM_W), nreq, rws)
    y_s = _mixout(xs_pad, a_s, b_s, c_s, w_out, g_post, nreq * rws)

    def real(a2d):
        return a2d.reshape(nreq, rws, -1)[:, n_padrow:]

    new = (p_c, p_n, p_m,
           k.reshape(nb, seq, SA_H, SA_DH), v.reshape(nb, seq, SA_H, SA_DH), kidx.reshape(nb, seq, IDX_D),
           mk.reshape(nb, n_mem, MEM_H, MEM_DH), mv.reshape(nb, n_mem, MEM_H, MEM_DH),
           s_c, s_n, s_m,
           real(k_s).reshape(nreq, t_dec, SA_H, SA_DH), real(v_s).reshape(nreq, t_dec, SA_H, SA_DH), real(kidx_s))
    return y_p, real(y_s), new


def kernel(x_prompt, x_sample, state_mlstm_C, state_mlstm_n, state_mlstm_m, cache_k, cache_v, cache_kidx,
           cache_mem_k, cache_mem_v, page_table, mem_prompt, g_pre, w_in, b_gates, g_head, w_mem_k, w_mem_v,
           g_mem, w_out, g_post):
    xp, xs = x_prompt, x_sample
    per_layer = []
    for l in range(w_in.shape[0]):
        xp, xs, new = _layer(xp, xs, state_mlstm_C[l], state_mlstm_n[l], state_mlstm_m[l],
                             cache_k[l], cache_v[l], cache_kidx[l], cache_mem_k[l], cache_mem_v[l],
                             page_table, mem_prompt, g_pre[l], w_in[l], b_gates[l], g_head[l],
                             w_mem_k[l], w_mem_v[l], g_mem[l], w_out[l], g_post[l])
        per_layer.append(new)
    stacked = [jnp.stack(a) for a in zip(*per_layer)]
    return (xp, xs, *stacked)
```

```python
import functools

import jax
import jax.numpy as jnp
from jax import lax
from jax.experimental import pallas as pl
from jax.experimental.pallas import tpu as pltpu

F32 = jnp.float32
BF16 = jnp.bfloat16

D_MODEL = 2048
ML_H = 4
ML_W = D_MODEL // 2
ML_DH = ML_W // ML_H
SA_H = 4
SA_W = D_MODEL // 4
SA_DH = SA_W // SA_H
MEM_H = 4
MEM_W = D_MODEL // 4
MEM_DH = MEM_W // MEM_H
IDX_H = 8
IDX_D = 64
IDX_SCALE = (IDX_H * IDX_D) ** -0.5
TOPK_MAX = 256
ROPE_THETA = 10000.0
LOG2E = 1.4426950408889634
EPS = 1e-6
PAGE_SIZE = 128

LANES = 128
CB = 512
SMALL_IG = 64
SMALL_LF = 68
SMALL_W = 72
SAMPLE_ROWS = 16
SEL_ROWS = 8
SEL_PAGES_PER_STEP = 32
ATT_PAGES_PER_STEP = 16
ML_CHUNK = 256
IDX_KEY_CHUNK = 256
VMEM_LIMIT = 56 * 1024 * 1024
NEG_INF = float("-inf")
POS_INF = float("inf")

_NT = (((1,), (1,)), ((), ()))
_TN = (((0,), (0,)), ((), ()))


def _cparams(sem):
    return pltpu.CompilerParams(dimension_semantics=sem, vmem_limit_bytes=VMEM_LIMIT)


def _sigmoid(x):
    return 1.0 / (1.0 + jnp.exp(-x))


def _silu(x):
    return x * _sigmoid(x)


def _log_sigmoid(x):
    return jnp.minimum(x, 0.0) - jnp.log1p(jnp.exp(-jnp.abs(x)))


def _split3(x):
    hi = x.astype(BF16)
    r = x - hi.astype(F32)
    mid = r.astype(BF16)
    lo = (r - mid.astype(F32)).astype(BF16)
    return hi, mid, lo


def _rope128(x, cos, sin_signed):
    return x * cos + pltpu.roll(x, 64, 1) * sin_signed


def _rope64(x, cos, sin_signed):
    lane = lax.broadcasted_iota(jnp.int32, x.shape, 1)
    first_half = (lane % 64) < 32
    partner = jnp.where(first_half, pltpu.roll(x, 96, 1), pltpu.roll(x, 32, 1))
    return x * cos + partner * sin_signed


def _normed(x_ref, g_ref):
    x = x_ref[...]
    return (x * lax.rsqrt(jnp.mean(x * x, axis=-1, keepdims=True) + EPS) * g_ref[...]).astype(BF16)


def _proj_ml_kernel(x_ref, g_ref, w_ref, qkv_ref, oz_ref, u_ref):
    u_ref[...] = _normed(x_ref, g_ref)
    n_qkv = 3 * ML_W // CB
    for cb in range(5 * ML_W // CB):
        acc = lax.dot_general(u_ref[...], w_ref[cb * CB:(cb + 1) * CB, :], _NT, preferred_element_type=F32)
        if ML_W <= cb * CB < 2 * ML_W:
            acc = acc * (ML_DH ** -0.5)
        if cb < n_qkv:
            qkv_ref[:, cb * CB:(cb + 1) * CB] = acc.astype(BF16)
        else:
            oz_ref[:, (cb - n_qkv) * CB:(cb - n_qkv + 1) * CB] = acc


def _proj_rest_kernel(x_ref, g_ref, wsa_ref, wmem_ref, ws_ref, wst_ref, c128_ref, s128_ref, c64_ref, s64_ref,
                      saq_ref, k_ref, v_ref, saz_ref, memq_ref, memz_ref, idxq_ref,
                      kidx_ref, small_ref, smallt_ref, u_ref):
    u_ref[...] = _normed(x_ref, g_ref)
    sm = jnp.dot(u_ref[...], ws_ref[...], preferred_element_type=F32)
    small_ref[...] = sm
    kidx_ref[...] = _rope64(sm, c64_ref[...], s64_ref[...])[:, :IDX_D]
    smallt_ref[...] = lax.dot_general(wst_ref[...], u_ref[...], _NT, preferred_element_type=F32)

    def block(cb, w_ref=wsa_ref):
        return lax.dot_general(u_ref[...], w_ref[cb * CB:(cb + 1) * CB, :], _NT, preferred_element_type=F32)

    def rope_heads(acc, fn, cos_ref, sin_ref):
        return jnp.concatenate(
            [fn(acc[:, h * LANES:(h + 1) * LANES], cos_ref[...], sin_ref[...]) for h in range(CB // LANES)], axis=1)

    saq_ref[...] = (rope_heads(block(0), _rope128, c128_ref, s128_ref) * (SA_DH ** -0.5 * LOG2E)).astype(BF16)
    tm = x_ref.shape[0]
    k_acc = block(1)
    v_acc = block(2)
    for h in range(SA_H):
        lanes = slice(h * SA_DH, (h + 1) * SA_DH)
        k_ref[pl.ds(h, tm, stride=SA_H), :] = _rope128(k_acc[:, lanes], c128_ref[...], s128_ref[...])
        v_ref[pl.ds(h, tm, stride=SA_H), :] = v_acc[:, lanes]
    saz_ref[...] = block(3)
    idxq_ref[...] = rope_heads(block(4), _rope64, c64_ref, s64_ref).astype(BF16)
    memq_ref[...] = (block(0, wmem_ref) * (MEM_DH ** -0.5)).astype(BF16)
    memz_ref[...] = block(1, wmem_ref)


def _project(x2d, g_pre, weights, tabs, tm):
    w_ml, w_sa, w_mem, w_small, w_small_t = weights
    rows = x2d.shape[0]
    c128, s128, c64, s64 = tabs
    ntab = c128.shape[0] // tm
    n_ml = 5 * ML_W
    row_only = lambda i: (i, 0)
    tab_map = lambda i: (i % ntab, 0)
    const = lambda i: (0, 0)
    resident = pl.Buffered(1)
    g2d = g_pre.reshape(1, D_MODEL)

    qkv, oz = pl.pallas_call(
        _proj_ml_kernel,
        out_shape=(jax.ShapeDtypeStruct((rows, 3 * ML_W), BF16),
                   jax.ShapeDtypeStruct((rows, 2 * ML_W), F32)),
        grid=(rows // tm,),
        in_specs=[pl.BlockSpec((tm, D_MODEL), row_only),
                  pl.BlockSpec((1, D_MODEL), const),
                  pl.BlockSpec((n_ml, D_MODEL), const, pipeline_mode=resident)],
        out_specs=(pl.BlockSpec((tm, 3 * ML_W), row_only), pl.BlockSpec((tm, 2 * ML_W), row_only)),
        scratch_shapes=[pltpu.VMEM((tm, D_MODEL), BF16)],
        compiler_params=_cparams(("arbitrary",)),
        name="proj_ml",
    )(x2d, g2d, w_ml)

    out_shape = (
        jax.ShapeDtypeStruct((rows, SA_W), BF16),
        jax.ShapeDtypeStruct((rows * SA_H, SA_DH), F32),
        jax.ShapeDtypeStruct((rows * SA_H, SA_DH), F32),
        jax.ShapeDtypeStruct((rows, SA_W), F32),
        jax.ShapeDtypeStruct((rows, MEM_W), BF16),
        jax.ShapeDtypeStruct((rows, MEM_W), F32),
        jax.ShapeDtypeStruct((rows, IDX_H * IDX_D), BF16),
        jax.ShapeDtypeStruct((rows, IDX_D), F32),
        jax.ShapeDtypeStruct((rows, LANES), F32),
        jax.ShapeDtypeStruct((LANES, rows), F32),
    )
    out_specs = (
        pl.BlockSpec((tm, CB), row_only),
        pl.BlockSpec((tm * SA_H, SA_DH), row_only),
        pl.BlockSpec((tm * SA_H, SA_DH), row_only),
        pl.BlockSpec((tm, CB), row_only),
        pl.BlockSpec((tm, CB), row_only),
        pl.BlockSpec((tm, CB), row_only),
        pl.BlockSpec((tm, CB), row_only),
        pl.BlockSpec((tm, IDX_D), row_only),
        pl.BlockSpec((tm, LANES), row_only),
        pl.BlockSpec((LANES, tm), lambda i: (0, i)),
    )
    in_specs = [
        pl.BlockSpec((tm, D_MODEL), row_only),
        pl.BlockSpec((1, D_MODEL), const),
        pl.BlockSpec(w_sa.shape, const, pipeline_mode=resident),
        pl.BlockSpec(w_mem.shape, const, pipeline_mode=resident),
        pl.BlockSpec((D_MODEL, LANES), const, pipeline_mode=resident),
        pl.BlockSpec((LANES, D_MODEL), const, pipeline_mode=resident),
        pl.BlockSpec((tm, LANES), tab_map),
        pl.BlockSpec((tm, LANES), tab_map),
        pl.BlockSpec((tm, LANES), tab_map),
        pl.BlockSpec((tm, LANES), tab_map),
    ]
    rest = pl.pallas_call(
        _proj_rest_kernel,
        out_shape=out_shape,
        grid=(rows // tm,),
        in_specs=in_specs,
        out_specs=out_specs,
        scratch_shapes=[pltpu.VMEM((tm, D_MODEL), BF16)],
        compiler_params=_cparams(("arbitrary",)),
        name="proj_rest",
    )(x2d, g2d, w_sa, w_mem, w_small, w_small_t, c128, s128, c64, s64)
    return (qkv, oz, *rest)


def _rope_tables(pos):
    def tab(half):
        inv = ROPE_THETA ** (-jnp.arange(half, dtype=F32) / half)
        ang = pos.astype(F32)[:, None] * inv[None, :]
        return jnp.cos(ang), jnp.sin(ang)

    c, s = tab(SA_DH // 2)
    c128 = jnp.concatenate([c, c], axis=1)
    s128 = jnp.concatenate([-s, s], axis=1)
    c, s = tab(IDX_D // 2)
    c64 = jnp.concatenate([c, c, c, c], axis=1)
    s64 = jnp.concatenate([-s, s, -s, s], axis=1)
    return c128, s128, c64, s64


def _relayout_w_in(w_in):
    off = {}
    o = 0
    for name, w in (('ml_q', ML_W), ('ml_k', ML_W), ('ml_v', ML_W), ('ml_o', ML_W), ('ml_z', ML_W),
                    ('ml_i', ML_H), ('ml_f', ML_H), ('sa_q', SA_W), ('sa_k', SA_W), ('sa_v', SA_W),
                    ('sa_z', SA_W), ('idx_q', IDX_H * IDX_D), ('idx_k', IDX_D), ('idx_w', IDX_H),
                    ('mem_q', MEM_W), ('mem_z', MEM_W)):
        off[name] = (o, w)
        o += w

    w_t = w_in.T

    def col(name):
        a, w = off[name]
        return w_t[a:a + w]

    def span(first, last):
        return w_t[off[first][0]:off[last][0] + off[last][1]].astype(BF16)

    w_ml = w_t.astype(BF16)
    w_sa = span('sa_q', 'idx_q')
    w_mem = span('mem_q', 'mem_z')
    small_t = jnp.concatenate([
        col('idx_k'), col('ml_i'), col('ml_f'), col('idx_w'),
        jnp.zeros((LANES - IDX_D - 2 * ML_H - IDX_H, D_MODEL), F32)], axis=0).astype(BF16)
    return (w_ml, w_sa, w_mem, small_t.T, small_t)


def _memkv_kernel(m_ref, g_ref, wk_ref, wv_ref, k_ref, v_ref):
    x = m_ref[...]
    u = (x * lax.rsqrt(jnp.mean(x * x, axis=-1, keepdims=True) + EPS) * g_ref[...]).astype(BF16)
    k_ref[...] = jnp.dot(u, wk_ref[...], preferred_element_type=F32)
    v_ref[...] = jnp.dot(u, wv_ref[...], preferred_element_type=F32)


def _memkv(mem2d, g_mem, wk, wv, n_mem):
    rows = mem2d.shape[0]
    row = lambda i: (i, 0)
    const = lambda i: (0, 0)
    return pl.pallas_call(
        _memkv_kernel,
        out_shape=(jax.ShapeDtypeStruct((rows, MEM_W), F32), jax.ShapeDtypeStruct((rows, MEM_W), F32)),
        grid=(rows // n_mem,),
        in_specs=[pl.BlockSpec((n_mem, D_MODEL), row), pl.BlockSpec((1, D_MODEL), const),
                  pl.BlockSpec((D_MODEL, MEM_W), const), pl.BlockSpec((D_MODEL, MEM_W), const)],
        out_specs=(pl.BlockSpec((n_mem, MEM_W), row), pl.BlockSpec((n_mem, MEM_W), row)),
        compiler_params=_cparams(("arbitrary",)),
        name="memkv",
    )(mem2d, g_mem.reshape(1, D_MODEL), wk.astype(BF16), wv.astype(BF16))


def _mlstm_kernel(qkv_ref, oz_ref, gc_ref, gt_ref, bcol_ref, brow_ref, gh_ref, c0_ref, n0_ref, m0_ref,
                  a_ref, cout_ref, nout_ref, mout_ref, c_s, n_s, m_s, *, c, n_pad):
    ci = pl.program_id(1)

    @pl.when(ci == 0)
    def _():
        c_s[...] = c0_ref[0]
        n_s[...] = n0_ref[0]
        m_s[...] = m0_ref[0]

    ri = lax.broadcasted_iota(jnp.int32, (c, c), 0)
    cj = lax.broadcasted_iota(jnp.int32, (c, c), 1)
    causal = cj <= ri
    tri = jnp.where(causal, 1.0, 0.0).astype(BF16)
    tri_t = jnp.where(ri <= cj, 1.0, 0.0).astype(BF16)

    g_c = gc_ref[...] + bcol_ref[...]
    pad_c = lax.broadcasted_iota(jnp.int32, (c, LANES), 0) < n_pad
    ig_c = jnp.where(pad_c, NEG_INF, g_c)
    lf_c = jnp.where(pad_c, 0.0, _log_sigmoid(g_c))
    b_c = sum(jnp.dot(tri, p, preferred_element_type=F32) for p in _split3(lf_c))
    g_r = gt_ref[0] + brow_ref[...]
    pad_r = lax.broadcasted_iota(jnp.int32, (SAMPLE_ROWS, c), 1) < n_pad
    ig_r = jnp.where(pad_r, NEG_INF, g_r)
    lf_r = jnp.where(pad_r, 0.0, _log_sigmoid(g_r))
    b_r = sum(jnp.dot(p, tri_t, preferred_element_type=F32) for p in _split3(lf_r))

    m_all = m_s[...]
    n_all = n_s[...]
    c_all = [c_s[h] for h in range(ML_H)]
    new_state = []
    for h in range(ML_H):
        hs = slice(h * ML_DH, (h + 1) * ML_DH)
        m_prev = m_all[h:h + 1, 0:1]
        b_t = b_c[:, SMALL_LF + h:SMALL_LF + h + 1]
        igc = ig_c[:, SMALL_IG + h:SMALL_IG + h + 1]
        b_s = b_r[ML_H + h:ML_H + h + 1, :]
        igr = ig_r[h:h + 1, :]
        a = jnp.where(causal, b_t - b_s + igr, NEG_INF)
        bm = b_t + m_prev
        m_t = jnp.maximum(bm, jnp.max(a, axis=1, keepdims=True))
        inter = jnp.exp(bm - m_t)
        dmat = jnp.exp(a - m_t)
        q = qkv_ref[:, h * ML_DH:(h + 1) * ML_DH]
        k = qkv_ref[:, ML_W + h * ML_DH:ML_W + (h + 1) * ML_DH]
        v = qkv_ref[:, 2 * ML_W + h * ML_DH:2 * ML_W + (h + 1) * ML_DH]
        s = lax.dot_general(q, k, _NT, preferred_element_type=F32) * dmat
        c_h = c_all[h]
        n_h = n_all[h:h + 1, :]
        num = (jnp.dot(s.astype(BF16), v, preferred_element_type=F32)
               + inter * jnp.dot(q, c_h.astype(BF16), preferred_element_type=F32))
        qn = (jnp.sum(s, axis=1, keepdims=True)
              + inter * jnp.sum(q.astype(F32) * n_h, axis=1, keepdims=True))
        hh = num / jnp.maximum(jnp.abs(qn), jnp.exp(-m_t))
        hh = hh * lax.rsqrt(jnp.mean(hh * hh, axis=1, keepdims=True) + EPS)
        o = oz_ref[:, h * ML_DH:(h + 1) * ML_DH]
        z = oz_ref[:, ML_W + h * ML_DH:ML_W + (h + 1) * ML_DH]
        gate = z / ((1.0 + jnp.exp(-o)) * (1.0 + jnp.exp(-z)))
        a_ref[:, hs] = (hh * gh_ref[:, hs] * gate).astype(BF16)

        m_new = m_t[c - 1:c, :]
        b_last = b_t[c - 1:c, :]
        w_end = jnp.exp(b_last - b_t + igc - m_new)
        decay = jnp.exp(b_last + m_prev - m_new)
        kw = k.astype(F32) * w_end
        new_state.append((decay * c_h + lax.dot_general(kw.astype(BF16), v, _TN, preferred_element_type=F32),
                          decay * n_h + jnp.sum(kw, axis=0, keepdims=True),
                          jnp.broadcast_to(m_new, (1, LANES))))

    for h, (c_new, n_new, m_new) in enumerate(new_state):
        c_s[h] = c_new
        n_s[h:h + 1, :] = n_new
        m_s[h:h + 1, :] = m_new

    @pl.when(ci == pl.num_programs(1) - 1)
    def _():
        cout_ref[0] = c_s[...]
        nout_ref[0] = n_s[...]
        mout_ref[0] = m_s[...]


def _mlstm(qkv, oz, small, small_t, b_gates, g_head, c0, n0, m0, nb, c, n_pad):
    rows = qkv.shape[0]
    nc = rows // (nb * c)
    bias_col = jnp.zeros((1, LANES), F32).at[0, SMALL_IG:SMALL_IG + 2 * ML_H].set(b_gates)
    bias_row = jnp.zeros((SAMPLE_ROWS, 1), F32).at[:2 * ML_H, 0].set(b_gates)
    m0b = jnp.zeros((nb, 8, LANES), F32).at[:, :ML_H, :].set(jnp.broadcast_to(m0[:, :, None], (nb, ML_H, LANES)))
    rowblk = lambda b, i: (b * nc + i, 0)
    const = lambda b, i: (0, 0)
    gates_t = small_t[SMALL_IG:SMALL_IG + SAMPLE_ROWS].reshape(SAMPLE_ROWS, rows // c, c).transpose(1, 0, 2)
    out_shape = (
        jax.ShapeDtypeStruct((rows, ML_W), BF16),
        jax.ShapeDtypeStruct((nb, ML_H, ML_DH, ML_DH), F32),
        jax.ShapeDtypeStruct((nb, ML_H, ML_DH), F32),
        jax.ShapeDtypeStruct((nb, 8, LANES), F32),
    )
    st4 = lambda b, i: (b, 0, 0, 0)
    st3 = lambda b, i: (b, 0, 0)
    a, c_out, n_out, m_out = pl.pallas_call(
        functools.partial(_mlstm_kernel, c=c, n_pad=n_pad),
        out_shape=out_shape,
        grid=(nb, nc),
        in_specs=[
            pl.BlockSpec((c, 3 * ML_W), rowblk),
            pl.BlockSpec((c, 2 * ML_W), rowblk),
            pl.BlockSpec((c, LANES), rowblk),
            pl.BlockSpec((1, SAMPLE_ROWS, c), lambda b, i: (b * nc + i, 0, 0)),
            pl.BlockSpec((1, LANES), const),
            pl.BlockSpec((SAMPLE_ROWS, 1), const),
            pl.BlockSpec((1, ML_W), const),
            pl.BlockSpec((1, ML_H, ML_DH, ML_DH), st4),
            pl.BlockSpec((1, ML_H, ML_DH), st3),
            pl.BlockSpec((1, 8, LANES), st3),
        ],
        out_specs=(
            pl.BlockSpec((c, ML_W), rowblk),
            pl.BlockSpec((1, ML_H, ML_DH, ML_DH), st4),
            pl.BlockSpec((1, ML_H, ML_DH), st3),
            pl.BlockSpec((1, 8, LANES), st3),
        ),
        scratch_shapes=[pltpu.VMEM((ML_H, ML_DH, ML_DH), F32), pltpu.VMEM((ML_H, ML_DH), F32),
                        pltpu.VMEM((8, LANES), F32)],
        compiler_params=_cparams(("arbitrary", "arbitrary")),
        name="mlstm",
    )(qkv, oz, small, gates_t, bias_col, bias_row, g_head.reshape(1, ML_W), c0, n0, m0b)
    return a, c_out, n_out, m_out[:, :ML_H, 0]


_REDUCERS = {"sum": (jnp.sum, jnp.add), "max": (jnp.max, jnp.maximum), "min": (jnp.min, jnp.minimum)}
REDUCE_CHAINS = 8


def _reduce(x, axis, op):
    fn, combine = _REDUCERS[op]
    unit = 8 if axis == 0 else LANES
    n = x.shape[axis]
    units = n // unit
    if n % unit or units < 2 * REDUCE_CHAINS:
        return fn(x, axis=axis, keepdims=True)
    base, rem = divmod(units, REDUCE_CHAINS)
    parts, start = [], 0
    for i in range(REDUCE_CHAINS):
        size = (base + (1 if i < rem else 0)) * unit
        piece = x[start:start + size] if axis == 0 else x[:, start:start + size]
        parts.append(fn(piece, axis=axis, keepdims=True))
        start += size
    while len(parts) > 1:
        parts = [combine(parts[i], parts[i + 1]) for i in range(0, len(parts), 2)]
    return parts[0]


def _count(pred, axis):
    return _reduce(jnp.where(pred, 1.0, 0.0), axis, "sum")


def _kth_largest(x_ref, k, axis, n_bisect):
    kf = float(k)
    x = x_ref[...]
    hi = _reduce(x, axis, "max")
    lo = _reduce(jnp.where(x == NEG_INF, POS_INF, x), axis, "min")

    def bisect(_, carry):
        lo, hi = carry
        mid = 0.5 * (lo + hi)
        ge = _count(x_ref[...] >= mid, axis) >= kf
        return jnp.where(ge, mid, lo), jnp.where(ge, hi, mid)

    lo, hi = lax.fori_loop(0, n_bisect, bisect, (lo, hi))

    def finished(cmin, xx):
        return jnp.where((_count(xx > cmin, axis) < kf) | (cmin == POS_INF), 1.0, 0.0)

    xx = x_ref[...]
    thr = _reduce(jnp.where(xx >= lo, xx, POS_INF), axis, "min")
    done = finished(thr, xx)

    def cond(st):
        return st[2] < 0.5

    def body(st):
        thr, done, _ = st
        xx = x_ref[...]
        cmin = _reduce(jnp.where(xx > thr, xx, POS_INF), axis, "min")
        thr = jnp.where(done < 0.5, cmin, thr)
        done = jnp.maximum(done, finished(thr, xx))
        return thr, done, jnp.min(done)

    thr, _, _ = lax.while_loop(cond, body, (thr, done, jnp.min(done)))
    return thr


def _dsa_kernel(q_ref, z_ref, idxq_ref, wt_ref, k_ref, v_ref, kidx_ref, o_ref,
                kb_s, vt_s, kib_s, x_s, sel_s, *, n_keys, qb, n_sel, n_bisect, key_step):
    j = pl.program_id(1)

    @pl.when(j == 0)
    def _():
        for h in range(SA_H):
            lanes = slice(h * SA_DH, (h + 1) * SA_DH)
            kb_s[:, lanes] = k_ref[pl.ds(h, n_keys, stride=SA_H), :].astype(BF16)
            vt_s[lanes, :] = v_ref[pl.ds(h, n_keys, stride=SA_H), :].T.astype(BF16)
        kib_s[...] = kidx_ref[...].astype(BF16)

    def attend(nk):
        xs = x_s.at[0:nk]
        ss = sel_s.at[0:nk]
        key = lax.broadcasted_iota(jnp.int32, (nk, qb), 0)
        qpos = j * qb + lax.broadcasted_iota(jnp.int32, (nk, qb), 1)
        valid = key <= qpos
        qcat = jnp.concatenate([idxq_ref[:, h * IDX_D:(h + 1) * IDX_D] for h in range(IDX_H)], axis=0)
        w_rows = [wt_ref[h:h + 1, :] * IDX_SCALE for h in range(IDX_H)]
        kc = min(IDX_KEY_CHUNK, nk)
        for c0 in range(0, nk, kc):
            d = lax.dot_general(kib_s[c0:c0 + kc, :], qcat, _NT, preferred_element_type=F32)
            sc = jnp.zeros((kc, qb), F32)
            for h in range(IDX_H):
                sc = sc + jnp.maximum(d[:, h * qb:(h + 1) * qb], 0.0) * w_rows[h]
            ok = (c0 + lax.broadcasted_iota(jnp.int32, (kc, qb), 0)) <= (
                j * qb + lax.broadcasted_iota(jnp.int32, (kc, qb), 1))
            x_s[c0:c0 + kc, :] = jnp.where(ok, sc, NEG_INF)
            sel_s[c0:c0 + kc, :] = jnp.where(ok, 1.0, 0.0)

        @pl.when((j + 1) * qb > n_sel)
        def _():
            kf = float(n_sel)
            thr = _kth_largest(xs, n_sel, 0, n_bisect)
            x = xs[...]
            need = kf - _count(x > thr, 0)
            n_tie = _count(x == thr, 0)
            qrow = j * qb + lax.broadcasted_iota(jnp.int32, (1, qb), 1)
            small = (qrow + 1) <= n_sel
            ss[...] = jnp.where(small, jnp.where(valid, 1.0, 0.0), jnp.where(x >= thr, 1.0, 0.0))
            excess = jnp.max(jnp.where((n_tie > need) & jnp.logical_not(small), 1.0, 0.0))

            @pl.when(excess > 0.5)
            def _():
                tb = min(256, nk)
                r_i = lax.broadcasted_iota(jnp.int32, (tb, tb), 0)
                c_i = lax.broadcasted_iota(jnp.int32, (tb, tb), 1)
                lower = jnp.where(c_i < r_i, 1.0, 0.0).astype(BF16)
                carry = jnp.zeros((1, qb), F32)
                for blk in range(nk // tb):
                    rows = slice(blk * tb, (blk + 1) * tb)
                    xb = x_s[rows, :]
                    tie = jnp.where(xb == thr, 1.0, 0.0)
                    rank = jnp.dot(lower, tie.astype(BF16), preferred_element_type=F32) + carry
                    keep = (xb > thr) | ((xb == thr) & (rank < need))
                    keyb = blk * tb + lax.broadcasted_iota(jnp.int32, (tb, qb), 0)
                    qposb = j * qb + lax.broadcasted_iota(jnp.int32, (tb, qb), 1)
                    smallb = jnp.where(keyb <= qposb, 1.0, 0.0)
                    sel_s[rows, :] = jnp.where(small, smallb, jnp.where(keep, 1.0, 0.0))
                    carry = carry + jnp.sum(tie, axis=0, keepdims=True)

        sel = ss[...] > 0.5
        heads = range(SA_H)
        hsl = [slice(h * SA_DH, (h + 1) * SA_DH) for h in heads]
        st = [jnp.where(sel, lax.dot_general(kb_s[0:nk, hsl[h]], q_ref[:, hsl[h]], _NT,
                                             preferred_element_type=F32), NEG_INF) for h in heads]
        mx = [_reduce(st[h], 0, "max") for h in heads]
        p = [jnp.exp2(st[h] - mx[h]) for h in heads]
        l = [_reduce(p[h], 0, "sum") for h in heads]
        ot = [jnp.dot(vt_s[hsl[h], 0:nk], p[h].astype(BF16), preferred_element_type=F32) / l[h]
              for h in heads]
        for h in heads:
            o_ref[:, hsl[h]] = (ot[h].T * _silu(z_ref[:, hsl[h]])).astype(BF16)

    n_ext = n_keys // key_step
    for e in range(n_ext):
        nk = (e + 1) * key_step
        lo_j = e * key_step // qb
        hi_j = nk // qb

        @pl.when((j >= lo_j) & (j < hi_j))
        def _(nk=nk):
            attend(nk)


def _dsa_prompt(saq, saz, idxq, small_t, k, v, kidx, nb, seq):
    rows = saq.shape[0]
    qb = min(seq, 128)
    nq = seq // qb
    n_sel = min(TOPK_MAX, seq // 4)
    qblk = lambda b, j: (b * nq + j, 0)
    per_b = lambda b, j: (b, 0)
    wt_blk = SMALL_W // 8
    return pl.pallas_call(
        functools.partial(_dsa_kernel, n_keys=seq, qb=qb, n_sel=n_sel, n_bisect=20, key_step=min(256, seq)),
        out_shape=jax.ShapeDtypeStruct((rows, SA_W), BF16),
        grid=(nb, nq),
        in_specs=[
            pl.BlockSpec((qb, SA_W), qblk),
            pl.BlockSpec((qb, SA_W), qblk),
            pl.BlockSpec((qb, IDX_H * IDX_D), qblk),
            pl.BlockSpec((8, qb), lambda b, j: (wt_blk, b * nq + j)),
            pl.BlockSpec((seq * SA_H, SA_DH), per_b),
            pl.BlockSpec((seq * SA_H, SA_DH), per_b),
            pl.BlockSpec((seq, IDX_D), per_b),
        ],
        out_specs=pl.BlockSpec((qb, SA_W), qblk),
        scratch_shapes=[pltpu.VMEM((seq, SA_W), BF16), pltpu.VMEM((SA_W, seq), BF16),
                        pltpu.VMEM((seq, IDX_D), BF16), pltpu.VMEM((seq, qb), F32),
                        pltpu.VMEM((seq, qb), F32)],
        compiler_params=_cparams(("arbitrary", "arbitrary")),
        name="dsa",
    )(saq, saz, idxq, small_t, k, v, kidx)


def _sel_kernel(pt_ref, idxq_ref, small_ref, kinew_ref, *rest, npg, n_past, n_sel, n_real, n_bisect):
    page_refs = rest[:npg]
    sel_ref, x_s = rest[npg:]
    g = pl.program_id(1)
    rws = SAMPLE_ROWS
    top = rws - SEL_ROWS
    pk = n_past + LANES
    gk = npg * PAGE_SIZE

    qs = jnp.concatenate([idxq_ref[:, h * IDX_D:(h + 1) * IDX_D] for h in range(IDX_H)], axis=0)
    small = small_ref[...]

    def scores(d):
        sc = jnp.zeros((SEL_ROWS, d.shape[1]), F32)
        for h in range(IDX_H):
            w = small[top:, SMALL_W + h:SMALL_W + h + 1] * IDX_SCALE
            sc = sc + jnp.maximum(d[h * rws + top:(h + 1) * rws, :], 0.0) * w
        return sc

    kp_t = jnp.concatenate([r[0] for r in page_refs], axis=1).astype(BF16)
    x_s[:, pl.ds(pl.multiple_of(g * gk, LANES), gk)] = scores(jnp.dot(qs, kp_t, preferred_element_type=F32))

    @pl.when(g == pl.num_programs(1) - 1)
    def _():
        knew = jnp.concatenate([kinew_ref[...], jnp.zeros((LANES - rws, IDX_D), F32)], axis=0).astype(BF16)
        row = top + lax.broadcasted_iota(jnp.int32, (SEL_ROWS, LANES), 0)
        col = lax.broadcasted_iota(jnp.int32, (SEL_ROWS, LANES), 1)
        ok = (col >= rws - n_real) & (col < rws) & (col <= row)
        d_new = lax.dot_general(qs, knew, _NT, preferred_element_type=F32)
        x_s[:, n_past:pk] = jnp.where(ok, scores(d_new), NEG_INF)

        kf = float(n_sel)
        thr = _kth_largest(x_s, n_sel, 1, n_bisect)
        x = x_s[...]
        need = kf - _count(x > thr, 1)
        n_tie = _count(x == thr, 1)
        sel_ref[0, 0:top, :] = jnp.ones((top, pk), F32)
        sel_ref[0, top:rws, :] = jnp.where(x >= thr, 1.0, 0.0)
        real = lax.broadcasted_iota(jnp.int32, (SEL_ROWS, 1), 0) >= SEL_ROWS - n_real
        excess = jnp.max(jnp.where((n_tie > need) & real, 1.0, 0.0))

        @pl.when(excess > 0.5)
        def _():
            r_i = lax.broadcasted_iota(jnp.int32, (LANES, LANES), 0)
            c_i = lax.broadcasted_iota(jnp.int32, (LANES, LANES), 1)
            upper = jnp.where(r_i < c_i, 1.0, 0.0).astype(BF16)

            def blk(i, carry):
                cols = pl.ds(pl.multiple_of(i * LANES, LANES), LANES)
                xb = x_s[:, cols]
                tie = jnp.where(xb == thr, 1.0, 0.0)
                rank = jnp.dot(tie.astype(BF16), upper, preferred_element_type=F32) + carry
                keep = (xb > thr) | ((xb == thr) & (rank < need))
                sel_ref[0, top:rws, cols] = jnp.where(keep, 1.0, 0.0)
                return carry + jnp.sum(tie, axis=1, keepdims=True)

            lax.fori_loop(0, pk // LANES, blk, jnp.zeros((SEL_ROWS, 1), F32))


def _dsa_select(page_table, idxq, small, kidx_new, cache_kidx, n_sel, n_real):
    nreq, n_pages = page_table.shape
    n_past = n_pages * PAGE_SIZE
    pk = n_past + LANES
    npg = min(SEL_PAGES_PER_STEP, n_pages)
    req = lambda b, g, pt: (b, 0)

    def page_map(i):
        return lambda b, g, pt: (pt[b, g * npg + i], 0, 0)

    grid_spec = pltpu.PrefetchScalarGridSpec(
        num_scalar_prefetch=1,
        grid=(nreq, n_pages // npg),
        in_specs=[pl.BlockSpec((SAMPLE_ROWS, IDX_H * IDX_D), req),
                  pl.BlockSpec((SAMPLE_ROWS, LANES), req),
                  pl.BlockSpec((SAMPLE_ROWS, IDX_D), req)]
                 + [pl.BlockSpec((1, IDX_D, PAGE_SIZE), page_map(i)) for i in range(npg)],
        out_specs=pl.BlockSpec((1, SAMPLE_ROWS, pk), lambda b, g, pt: (b, 0, 0)),
        scratch_shapes=[pltpu.VMEM((SEL_ROWS, pk), F32)],
    )
    assert n_real <= SEL_ROWS
    return pl.pallas_call(
        functools.partial(_sel_kernel, npg=npg, n_past=n_past, n_sel=n_sel, n_real=n_real, n_bisect=20),
        out_shape=jax.ShapeDtypeStruct((nreq, SAMPLE_ROWS, pk), F32),
        grid_spec=grid_spec,
        compiler_params=_cparams(("arbitrary", "arbitrary")),
        name="dsa_sel",
    )(page_table, idxq, small, kidx_new, *([cache_kidx] * npg))


def _att_kernel(pt_ref, q_ref, z_ref, knew_ref, vnew_ref, sel_ref, seltail_ref, *rest, npg):
    k_refs = rest[:npg]
    v_refs = rest[npg:2 * npg]
    o_ref = rest[2 * npg]
    m_s, l_s, acc_s = rest[2 * npg + 1:]
    g = pl.program_id(1)
    rws = SAMPLE_ROWS
    floor = -1e30

    @pl.when(g == 0)
    def _():
        m_s[...] = jnp.full(m_s.shape, floor, F32)
        l_s[...] = jnp.zeros(l_s.shape, F32)
        acc_s[...] = jnp.zeros(acc_s.shape, F32)

    heads = range(SA_H)
    hsl = [slice(h * SA_DH, (h + 1) * SA_DH) for h in heads]

    def update(kbs, vbs, keep):
        m_old = [m_s[h][:, 0:1] for h in heads]
        l_old = [l_s[h][:, 0:1] for h in heads]
        acc_old = [acc_s[:, hsl[h]] for h in heads]
        s = [lax.dot_general(q_ref[:, hsl[h]], kbs[h], _NT, preferred_element_type=F32) for h in heads]
        m_new = [jnp.maximum(m_old[h], jnp.max(jnp.where(keep, s[h], floor), axis=1, keepdims=True)) for h in heads]
        p = [jnp.where(keep, jnp.exp2(s[h] - m_new[h]), 0.0) for h in heads]
        pv = [jnp.dot(p[h].astype(BF16), vbs[h], preferred_element_type=F32) for h in heads]
        alpha = [jnp.exp2(m_old[h] - m_new[h]) for h in heads]
        l_new = [alpha[h] * l_old[h] + jnp.sum(p[h], axis=1, keepdims=True) for h in heads]
        for h in heads:
            acc_s[:, hsl[h]] = alpha[h] * acc_old[h] + pv[h]
            l_s[h] = jnp.broadcast_to(l_new[h], (rws, LANES))
            m_s[h] = jnp.broadcast_to(m_new[h], (rws, LANES))

    def head_rows(refs, h):
        return jnp.concatenate([r[0, pl.ds(h, PAGE_SIZE, stride=SA_H), :] for r in refs], axis=0).astype(BF16)

    update([head_rows(k_refs, h) for h in heads], [head_rows(v_refs, h) for h in heads], sel_ref[0] > 0.5)

    @pl.when(g == pl.num_programs(1) - 1)
    def _():
        update([knew_ref[pl.ds(h, rws, stride=SA_H), :].astype(BF16) for h in heads],
               [vnew_ref[pl.ds(h, rws, stride=SA_H), :].astype(BF16) for h in heads],
               seltail_ref[0][:, :rws] > 0.5)
        for h in heads:
            o_ref[:, hsl[h]] = (acc_s[:, hsl[h]] / l_s[h][:, 0:1] * _silu(z_ref[:, hsl[h]])).astype(BF16)


def _dsa_attend(page_table, saq, saz, k_new, v_new, sel, cache_k, cache_v):
    nreq, n_pages = page_table.shape
    n_past = n_pages * PAGE_SIZE
    npg = min(ATT_PAGES_PER_STEP, n_pages)
    prow = PAGE_SIZE * SA_H
    req = lambda b, g, pt: (b, 0)

    def page_map(i):
        return lambda b, g, pt: (pt[b, g * npg + i], 0, 0)

    page_specs = [pl.BlockSpec((1, prow, SA_DH), page_map(i)) for i in range(npg)]
    grid_spec = pltpu.PrefetchScalarGridSpec(
        num_scalar_prefetch=1,
        grid=(nreq, n_pages // npg),
        in_specs=[pl.BlockSpec((SAMPLE_ROWS, SA_W), req), pl.BlockSpec((SAMPLE_ROWS, SA_W), req),
                  pl.BlockSpec((SAMPLE_ROWS * SA_H, SA_DH), req), pl.BlockSpec((SAMPLE_ROWS * SA_H, SA_DH), req),
                  pl.BlockSpec((1, SAMPLE_ROWS, npg * PAGE_SIZE), lambda b, g, pt: (b, 0, g)),
                  pl.BlockSpec((1, SAMPLE_ROWS, LANES), lambda b, g, pt: (b, 0, n_past // LANES))]
                 + page_specs + page_specs,
        out_specs=pl.BlockSpec((SAMPLE_ROWS, SA_W), req),
        scratch_shapes=[pltpu.VMEM((SA_H, SAMPLE_ROWS, LANES), F32), pltpu.VMEM((SA_H, SAMPLE_ROWS, LANES), F32),
                        pltpu.VMEM((SAMPLE_ROWS, SA_W), F32)],
    )
    return pl.pallas_call(
        functools.partial(_att_kernel, npg=npg),
        out_shape=jax.ShapeDtypeStruct((nreq * SAMPLE_ROWS, SA_W), BF16),
        grid_spec=grid_spec,
        compiler_params=_cparams(("arbitrary", "arbitrary")),
        name="dsa_att",
    )(page_table, saq, saz, k_new, v_new, sel, sel, *([cache_k] * npg), *([cache_v] * npg))


def _memattn_kernel(q_ref, z_ref, mk_ref, mv_ref, o_ref):
    mk = mk_ref[...].astype(BF16)
    mv = mv_ref[...].astype(BF16)
    for h in range(MEM_H):
        hs = slice(h * MEM_DH, (h + 1) * MEM_DH)
        s = lax.dot_general(q_ref[:, hs], mk[:, hs], _NT, preferred_element_type=F32)
        p = jnp.exp(s - jnp.max(s, axis=1, keepdims=True))
        l = jnp.sum(p, axis=1, keepdims=True)
        o = jnp.dot(p.astype(BF16), mv[:, hs], preferred_element_type=F32) / l
        o_ref[:, hs] = (o * _silu(z_ref[:, hs])).astype(BF16)


def _memattn(memq, memz, mk, mv, nb, tq):
    rows = memq.shape[0]
    n_mem = mk.shape[0] // nb
    nq = rows // (nb * tq)
    qblk = lambda b, i: (b * nq + i, 0)
    per_b = lambda b, i: (b, 0)
    return pl.pallas_call(
        _memattn_kernel,
        out_shape=jax.ShapeDtypeStruct((rows, MEM_W), BF16),
        grid=(nb, nq),
        in_specs=[pl.BlockSpec((tq, MEM_W), qblk), pl.BlockSpec((tq, MEM_W), qblk),
                  pl.BlockSpec((n_mem, MEM_W), per_b), pl.BlockSpec((n_mem, MEM_W), per_b)],
        out_specs=pl.BlockSpec((tq, MEM_W), qblk),
        compiler_params=_cparams(("arbitrary", "arbitrary")),
        name="memattn",
    )(memq, memz, mk, mv)


def _mixout_kernel(x_ref, a_ref, b_ref, c_ref, wa_ref, wb_ref, wc_ref, g_ref, y_ref):
    acc = (jnp.dot(a_ref[...], wa_ref[...], preferred_element_type=F32)
           + jnp.dot(b_ref[...], wb_ref[...], preferred_element_type=F32)
           + jnp.dot(c_ref[...], wc_ref[...], preferred_element_type=F32))
    y = acc * lax.rsqrt(jnp.mean(acc * acc, axis=-1, keepdims=True) + EPS) * g_ref[...]
    y_ref[...] = x_ref[...] + y


def _mixout(x2d, a, b, c, w_out, g_post, tm):
    rows = x2d.shape[0]
    wb16 = w_out.astype(BF16)
    row = lambda i: (i, 0)
    const = lambda i: (0, 0)
    return pl.pallas_call(
        _mixout_kernel,
        out_shape=jax.ShapeDtypeStruct((rows, D_MODEL), F32),
        grid=(rows // tm,),
        in_specs=[pl.BlockSpec((tm, D_MODEL), row), pl.BlockSpec((tm, ML_W), row),
                  pl.BlockSpec((tm, SA_W), row), pl.BlockSpec((tm, MEM_W), row),
                  pl.BlockSpec((ML_W, D_MODEL), const), pl.BlockSpec((SA_W, D_MODEL), const),
                  pl.BlockSpec((MEM_W, D_MODEL), const), pl.BlockSpec((1, D_MODEL), const)],
        out_specs=pl.BlockSpec((tm, D_MODEL), row),
        compiler_params=_cparams(("arbitrary",)),
        name="mixout",
    )(x2d, a, b, c, wb16[:ML_W], wb16[ML_W:ML_W + SA_W], wb16[ML_W + SA_W:], g_post.reshape(1, D_MODEL))


def _layer(x_p, x_s, st_c, st_n, st_m, c_k, c_v, c_kidx, c_mk, c_mv, page_table, mem_prompt,
           g_pre, w_in, b_gates, g_head, w_mem_k, w_mem_v, g_mem, w_out, g_post):
    nb, seq, _ = x_p.shape
    nreq, t_dec, _ = x_s.shape
    n_mem = mem_prompt.shape[1]
    n_past = page_table.shape[1] * PAGE_SIZE
    weights = _relayout_w_in(w_in)

    tm = min(512, seq)
    tabs = _rope_tables(jnp.arange(seq, dtype=jnp.int32))
    (qkv, oz, saq, k, v, saz, memq, memz, idxq, kidx, small, small_t) = _project(
        x_p.reshape(nb * seq, D_MODEL), g_pre, weights, tabs, tm)
    c0 = jnp.zeros((nb, ML_H, ML_DH, ML_DH), F32)
    n0 = jnp.zeros((nb, ML_H, ML_DH), F32)
    m0 = jnp.zeros((nb, ML_H), F32)
    a_p, p_c, p_n, p_m = _mlstm(qkv, oz, small, small_t, b_gates, g_head, c0, n0, m0, nb, min(ML_CHUNK, seq), 0)
    b_p = _dsa_prompt(saq, saz, idxq, small_t, k, v, kidx, nb, seq)
    mk, mv = _memkv(mem_prompt.reshape(nb * n_mem, D_MODEL), g_mem, w_mem_k, w_mem_v, n_mem)
    c_p = _memattn(memq, memz, mk, mv, nb, min(256, seq))
    y_p = _mixout(x_p.reshape(nb * seq, D_MODEL), a_p, b_p, c_p, w_out, g_post, tm).reshape(nb, seq, D_MODEL)

    rws = SAMPLE_ROWS
    n_padrow = rws - t_dec
    xs_pad = jnp.concatenate([jnp.zeros((nreq, n_padrow, D_MODEL), F32), x_s], axis=1).reshape(nreq * rws, D_MODEL)
    pos_s = jnp.tile(jnp.concatenate([jnp.zeros((n_padrow,), jnp.int32),
                                      n_past + jnp.arange(t_dec, dtype=jnp.int32)]), nreq)
    tabs_s = _rope_tables(pos_s)
    (qkv_s, oz_s, saq_s, k_s, v_s, saz_s, memq_s, memz_s, idxq_s, kidx_s, small_s, small_t_s) = _project(
        xs_pad, g_pre, weights, tabs_s, nreq * rws)
    a_s, s_c, s_n, s_m = _mlstm(qkv_s, oz_s, small_s, small_t_s, b_gates, g_head, st_c, st_n, st_m,
                                nreq, rws, n_padrow)
    n_sel = min(TOPK_MAX, (n_past + t_dec) // 4)
    sel = _dsa_select(page_table, idxq_s, small_s, kidx_s, jnp.swapaxes(c_kidx, 1, 2), n_sel, t_dec)
    b_s = _dsa_attend(page_table, saq_s, saz_s, k_s, v_s, sel,
                      c_k.reshape(c_k.shape[0], PAGE_SIZE * SA_H, SA_DH),
                      c_v.reshape(c_v.shape[0], PAGE_SIZE * SA_H, SA_DH))
    c_s = _memattn(memq_s, memz_s, c_mk.reshape(nreq * n_mem, MEM_W), c_mv.reshape(nreq * n_mem, MEM_W), nreq, rws)
    y_s = _mixout(xs_pad, a_s, b_s, c_s, w_out, g_post, nreq * rws)

    def real(a2d):
        return a2d.reshape(nreq, rws, -1)[:, n_padrow:]

    new = (p_c, p_n, p_m,
           k.reshape(nb, seq, SA_H, SA_DH), v.reshape(nb, seq, SA_H, SA_DH), kidx.reshape(nb, seq, IDX_D),
           mk.reshape(nb, n_mem, MEM_H, MEM_DH), mv.reshape(nb, n_mem, MEM_H, MEM_DH),
           s_c, s_n, s_m,
           real(k_s).reshape(nreq, t_dec, SA_H, SA_DH), real(v_s).reshape(nreq, t_dec, SA_H, SA_DH), real(kidx_s))
    return y_p, real(y_s), new


def kernel(x_prompt, x_sample, state_mlstm_C, state_mlstm_n, state_mlstm_m, cache_k, cache_v, cache_kidx,
           cache_mem_k, cache_mem_v, page_table, mem_prompt, g_pre, w_in, b_gates, g_head, w_mem_k, w_mem_v,
           g_mem, w_out, g_post):
    xp, xs = x_prompt, x_sample
    per_layer = []
    for l in range(w_in.shape[0]):
        xp, xs, new = _layer(xp, xs, state_mlstm_C[l], state_mlstm_n[l], state_mlstm_m[l],
                             cache_k[l], cache_v[l], cache_kidx[l], cache_mem_k[l], cache_mem_v[l],
                             page_table, mem_prompt, g_pre[l], w_in[l], b_gates[l], g_head[l],
                             w_mem_k[l], w_mem_v[l], g_mem[l], w_out[l], g_post[l])
        per_layer.append(new)
    stacked = [jnp.stack(a) for a in zip(*per_layer)]
    return (xp, xs, *stacked)
```

```python
import functools

import jax
import jax.numpy as jnp
from jax import lax
from jax.experimental import pallas as pl
from jax.experimental.pallas import tpu as pltpu

F32 = jnp.float32
BF16 = jnp.bfloat16

D_MODEL = 2048
ML_H = 4
ML_W = D_MODEL // 2
ML_DH = ML_W // ML_H
SA_H = 4
SA_W = D_MODEL // 4
SA_DH = SA_W // SA_H
MEM_H = 4
MEM_W = D_MODEL // 4
MEM_DH = MEM_W // MEM_H
IDX_H = 8
IDX_D = 64
IDX_SCALE = (IDX_H * IDX_D) ** -0.5
TOPK_MAX = 256
ROPE_THETA = 10000.0
LOG2E = 1.4426950408889634
EPS = 1e-6
PAGE_SIZE = 128

LANES = 128
CB = 512
SMALL_IG = 64
SMALL_LF = 68
SMALL_W = 72
SAMPLE_ROWS = 16
SEL_ROWS = 8
SEL_PAGES_PER_STEP = 32
ATT_PAGES_PER_STEP = 16
ML_CHUNK = 256
IDX_KEY_CHUNK = 256
VMEM_LIMIT = 56 * 1024 * 1024
NEG_INF = float("-inf")
POS_INF = float("inf")

_NT = (((1,), (1,)), ((), ()))
_TN = (((0,), (0,)), ((), ()))


def _cparams(sem):
    return pltpu.CompilerParams(dimension_semantics=sem, vmem_limit_bytes=VMEM_LIMIT)


def _sigmoid(x):
    return 1.0 / (1.0 + jnp.exp(-x))


def _silu(x):
    return x * _sigmoid(x)


def _log_sigmoid(x):
    return jnp.minimum(x, 0.0) - jnp.log1p(jnp.exp(-jnp.abs(x)))


def _split3(x):
    hi = x.astype(BF16)
    r = x - hi.astype(F32)
    mid = r.astype(BF16)
    lo = (r - mid.astype(F32)).astype(BF16)
    return hi, mid, lo


def _rope128(x, cos, sin_signed):
    return x * cos + pltpu.roll(x, 64, 1) * sin_signed


def _rope64(x, cos, sin_signed):
    lane = lax.broadcasted_iota(jnp.int32, x.shape, 1)
    first_half = (lane % 64) < 32
    partner = jnp.where(first_half, pltpu.roll(x, 96, 1), pltpu.roll(x, 32, 1))
    return x * cos + partner * sin_signed


def _normed(x_ref, g_ref):
    x = x_ref[...]
    return (x * lax.rsqrt(jnp.mean(x * x, axis=-1, keepdims=True) + EPS) * g_ref[...]).astype(BF16)


def _proj_ml_kernel(x_ref, g_ref, w_ref, qkv_ref, oz_ref, u_ref):
    u_ref[...] = _normed(x_ref, g_ref)
    n_qkv = 3 * ML_W // CB
    for cb in range(5 * ML_W // CB):
        acc = lax.dot_general(u_ref[...], w_ref[cb * CB:(cb + 1) * CB, :], _NT, preferred_element_type=F32)
        if ML_W <= cb * CB < 2 * ML_W:
            acc = acc * (ML_DH ** -0.5)
        if cb < n_qkv:
            qkv_ref[:, cb * CB:(cb + 1) * CB] = acc.astype(BF16)
        else:
            oz_ref[:, (cb - n_qkv) * CB:(cb - n_qkv + 1) * CB] = acc


def _proj_rest_kernel(x_ref, g_ref, wsa_ref, wmem_ref, ws_ref, wst_ref, c128_ref, s128_ref, c64_ref, s64_ref,
                      saq_ref, k_ref, v_ref, saz_ref, memq_ref, memz_ref, idxq_ref,
                      kidx_ref, small_ref, smallt_ref, u_ref):
    u_ref[...] = _normed(x_ref, g_ref)
    sm = jnp.dot(u_ref[...], ws_ref[...], preferred_element_type=F32)
    small_ref[...] = sm
    kidx_ref[...] = _rope64(sm, c64_ref[...], s64_ref[...])[:, :IDX_D]
    smallt_ref[...] = lax.dot_general(wst_ref[...], u_ref[...], _NT, preferred_element_type=F32)

    def block(cb, w_ref=wsa_ref):
        return lax.dot_general(u_ref[...], w_ref[cb * CB:(cb + 1) * CB, :], _NT, preferred_element_type=F32)

    def rope_heads(acc, fn, cos_ref, sin_ref):
        return jnp.concatenate(
            [fn(acc[:, h * LANES:(h + 1) * LANES], cos_ref[...], sin_ref[...]) for h in range(CB // LANES)], axis=1)

    saq_ref[...] = (rope_heads(block(0), _rope128, c128_ref, s128_ref) * (SA_DH ** -0.5 * LOG2E)).astype(BF16)
    tm = x_ref.shape[0]
    k_acc = block(1)
    v_acc = block(2)
    for h in range(SA_H):
        lanes = slice(h * SA_DH, (h + 1) * SA_DH)
        k_ref[pl.ds(h, tm, stride=SA_H), :] = _rope128(k_acc[:, lanes], c128_ref[...], s128_ref[...])
        v_ref[pl.ds(h, tm, stride=SA_H), :] = v_acc[:, lanes]
    saz_ref[...] = block(3)
    idxq_ref[...] = rope_heads(block(4), _rope64, c64_ref, s64_ref).astype(BF16)
    memq_ref[...] = (block(0, wmem_ref) * (MEM_DH ** -0.5)).astype(BF16)
    memz_ref[...] = block(1, wmem_ref)


def _project(x2d, g_pre, weights, tabs, tm):
    w_ml, w_sa, w_mem, w_small, w_small_t = weights
    rows = x2d.shape[0]
    c128, s128, c64, s64 = tabs
    ntab = c128.shape[0] // tm
    n_ml = 5 * ML_W
    row_only = lambda i: (i, 0)
    tab_map = lambda i: (i % ntab, 0)
    const = lambda i: (0, 0)
    resident = pl.Buffered(1)
    g2d = g_pre.reshape(1, D_MODEL)

    qkv, oz = pl.pallas_call(
        _proj_ml_kernel,
        out_shape=(jax.ShapeDtypeStruct((rows, 3 * ML_W), BF16),
                   jax.ShapeDtypeStruct((rows, 2 * ML_W), F32)),
        grid=(rows // tm,),
        in_specs=[pl.BlockSpec((tm, D_MODEL), row_only),
                  pl.BlockSpec((1, D_MODEL), const),
                  pl.BlockSpec((n_ml, D_MODEL), const, pipeline_mode=resident)],
        out_specs=(pl.BlockSpec((tm, 3 * ML_W), row_only), pl.BlockSpec((tm, 2 * ML_W), row_only)),
        scratch_shapes=[pltpu.VMEM((tm, D_MODEL), BF16)],
        compiler_params=_cparams(("arbitrary",)),
        name="proj_ml",
    )(x2d, g2d, w_ml)

    out_shape = (
        jax.ShapeDtypeStruct((rows, SA_W), BF16),
        jax.ShapeDtypeStruct((rows * SA_H, SA_DH), F32),
        jax.ShapeDtypeStruct((rows * SA_H, SA_DH), F32),
        jax.ShapeDtypeStruct((rows, SA_W), F32),
        jax.ShapeDtypeStruct((rows, MEM_W), BF16),
        jax.ShapeDtypeStruct((rows, MEM_W), F32),
        jax.ShapeDtypeStruct((rows, IDX_H * IDX_D), BF16),
        jax.ShapeDtypeStruct((rows, IDX_D), F32),
        jax.ShapeDtypeStruct((rows, LANES), F32),
        jax.ShapeDtypeStruct((LANES, rows), F32),
    )
    out_specs = (
        pl.BlockSpec((tm, CB), row_only),
        pl.BlockSpec((tm * SA_H, SA_DH), row_only),
        pl.BlockSpec((tm * SA_H, SA_DH), row_only),
        pl.BlockSpec((tm, CB), row_only),
        pl.BlockSpec((tm, CB), row_only),
        pl.BlockSpec((tm, CB), row_only),
        pl.BlockSpec((tm, CB), row_only),
        pl.BlockSpec((tm, IDX_D), row_only),
        pl.BlockSpec((tm, LANES), row_only),
        pl.BlockSpec((LANES, tm), lambda i: (0, i)),
    )
    in_specs = [
        pl.BlockSpec((tm, D_MODEL), row_only),
        pl.BlockSpec((1, D_MODEL), const),
        pl.BlockSpec(w_sa.shape, const, pipeline_mode=resident),
        pl.BlockSpec(w_mem.shape, const, pipeline_mode=resident),
        pl.BlockSpec((D_MODEL, LANES), const, pipeline_mode=resident),
        pl.BlockSpec((LANES, D_MODEL), const, pipeline_mode=resident),
        pl.BlockSpec((tm, LANES), tab_map),
        pl.BlockSpec((tm, LANES), tab_map),
        pl.BlockSpec((tm, LANES), tab_map),
        pl.BlockSpec((tm, LANES), tab_map),
    ]
    rest = pl.pallas_call(
        _proj_rest_kernel,
        out_shape=out_shape,
        grid=(rows // tm,),
        in_specs=in_specs,
        out_specs=out_specs,
        scratch_shapes=[pltpu.VMEM((tm, D_MODEL), BF16)],
        compiler_params=_cparams(("arbitrary",)),
        name="proj_rest",
    )(x2d, g2d, w_sa, w_mem, w_small, w_small_t, c128, s128, c64, s64)
    return (qkv, oz, *rest)


def _rope_tables(pos):
    def tab(half):
        inv = ROPE_THETA ** (-jnp.arange(half, dtype=F32) / half)
        ang = pos.astype(F32)[:, None] * inv[None, :]
        return jnp.cos(ang), jnp.sin(ang)

    c, s = tab(SA_DH // 2)
    c128 = jnp.concatenate([c, c], axis=1)
    s128 = jnp.concatenate([-s, s], axis=1)
    c, s = tab(IDX_D // 2)
    c64 = jnp.concatenate([c, c, c, c], axis=1)
    s64 = jnp.concatenate([-s, s, -s, s], axis=1)
    return c128, s128, c64, s64


def _relayout_w_in(w_in):
    off = {}
    o = 0
    for name, w in (('ml_q', ML_W), ('ml_k', ML_W), ('ml_v', ML_W), ('ml_o', ML_W), ('ml_z', ML_W),
                    ('ml_i', ML_H), ('ml_f', ML_H), ('sa_q', SA_W), ('sa_k', SA_W), ('sa_v', SA_W),
                    ('sa_z', SA_W), ('idx_q', IDX_H * IDX_D), ('idx_k', IDX_D), ('idx_w', IDX_H),
                    ('mem_q', MEM_W), ('mem_z', MEM_W)):
        off[name] = (o, w)
        o += w

    w_t = w_in.T

    def col(name):
        a, w = off[name]
        return w_t[a:a + w]

    def span(first, last):
        return w_t[off[first][0]:off[last][0] + off[last][1]].astype(BF16)

    w_ml = w_t.astype(BF16)
    w_sa = span('sa_q', 'idx_q')
    w_mem = span('mem_q', 'mem_z')
    small_t = jnp.concatenate([
        col('idx_k'), col('ml_i'), col('ml_f'), col('idx_w'),
        jnp.zeros((LANES - IDX_D - 2 * ML_H - IDX_H, D_MODEL), F32)], axis=0).astype(BF16)
    return (w_ml, w_sa, w_mem, small_t.T, small_t)


def _memkv_kernel(m_ref, g_ref, wk_ref, wv_ref, k_ref, v_ref):
    x = m_ref[...]
    u = (x * lax.rsqrt(jnp.mean(x * x, axis=-1, keepdims=True) + EPS) * g_ref[...]).astype(BF16)
    k_ref[...] = jnp.dot(u, wk_ref[...], preferred_element_type=F32)
    v_ref[...] = jnp.dot(u, wv_ref[...], preferred_element_type=F32)


def _memkv(mem2d, g_mem, wk, wv, n_mem):
    rows = mem2d.shape[0]
    row = lambda i: (i, 0)
    const = lambda i: (0, 0)
    return pl.pallas_call(
        _memkv_kernel,
        out_shape=(jax.ShapeDtypeStruct((rows, MEM_W), F32), jax.ShapeDtypeStruct((rows, MEM_W), F32)),
        grid=(rows // n_mem,),
        in_specs=[pl.BlockSpec((n_mem, D_MODEL), row), pl.BlockSpec((1, D_MODEL), const),
                  pl.BlockSpec((D_MODEL, MEM_W), const), pl.BlockSpec((D_MODEL, MEM_W), const)],
        out_specs=(pl.BlockSpec((n_mem, MEM_W), row), pl.BlockSpec((n_mem, MEM_W), row)),
        compiler_params=_cparams(("arbitrary",)),
        name="memkv",
    )(mem2d, g_mem.reshape(1, D_MODEL), wk.astype(BF16), wv.astype(BF16))


def _mlstm_kernel(qkv_ref, oz_ref, gc_ref, gt_ref, bcol_ref, brow_ref, gh_ref, c0_ref, n0_ref, m0_ref,
                  a_ref, cout_ref, nout_ref, mout_ref, c_s, n_s, m_s, *, c, n_pad):
    ci = pl.program_id(1)

    @pl.when(ci == 0)
    def _():
        c_s[...] = c0_ref[0]
        n_s[...] = n0_ref[0]
        m_s[...] = m0_ref[0]

    ri = lax.broadcasted_iota(jnp.int32, (c, c), 0)
    cj = lax.broadcasted_iota(jnp.int32, (c, c), 1)
    causal = cj <= ri
    tri = jnp.where(causal, 1.0, 0.0).astype(BF16)
    tri_t = jnp.where(ri <= cj, 1.0, 0.0).astype(BF16)

    g_c = gc_ref[...] + bcol_ref[...]
    pad_c = lax.broadcasted_iota(jnp.int32, (c, LANES), 0) < n_pad
    ig_c = jnp.where(pad_c, NEG_INF, g_c)
    lf_c = jnp.where(pad_c, 0.0, _log_sigmoid(g_c))
    b_c = sum(jnp.dot(tri, p, preferred_element_type=F32) for p in _split3(lf_c))
    g_r = gt_ref[0] + brow_ref[...]
    pad_r = lax.broadcasted_iota(jnp.int32, (SAMPLE_ROWS, c), 1) < n_pad
    ig_r = jnp.where(pad_r, NEG_INF, g_r)
    lf_r = jnp.where(pad_r, 0.0, _log_sigmoid(g_r))
    b_r = sum(jnp.dot(p, tri_t, preferred_element_type=F32) for p in _split3(lf_r))

    m_all = m_s[...]
    n_all = n_s[...]
    c_all = [c_s[h] for h in range(ML_H)]
    new_state = []
    for h in range(ML_H):
        hs = slice(h * ML_DH, (h + 1) * ML_DH)
        m_prev = m_all[h:h + 1, 0:1]
        b_t = b_c[:, SMALL_LF + h:SMALL_LF + h + 1]
        igc = ig_c[:, SMALL_IG + h:SMALL_IG + h + 1]
        b_s = b_r[ML_H + h:ML_H + h + 1, :]
        igr = ig_r[h:h + 1, :]
        a = jnp.where(causal, b_t - b_s + igr, NEG_INF)
        bm = b_t + m_prev
        m_t = jnp.maximum(bm, jnp.max(a, axis=1, keepdims=True))
        inter = jnp.exp(bm - m_t)
        dmat = jnp.exp(a - m_t)
        q = qkv_ref[:, h * ML_DH:(h + 1) * ML_DH]
        k = qkv_ref[:, ML_W + h * ML_DH:ML_W + (h + 1) * ML_DH]
        v = qkv_ref[:, 2 * ML_W + h * ML_DH:2 * ML_W + (h + 1) * ML_DH]
        s = lax.dot_general(q, k, _NT, preferred_element_type=F32) * dmat
        c_h = c_all[h]
        n_h = n_all[h:h + 1, :]
        num = (jnp.dot(s.astype(BF16), v, preferred_element_type=F32)
               + inter * jnp.dot(q, c_h.astype(BF16), preferred_element_type=F32))
        qn = (jnp.sum(s, axis=1, keepdims=True)
              + inter * jnp.sum(q.astype(F32) * n_h, axis=1, keepdims=True))
        hh = num / jnp.maximum(jnp.abs(qn), jnp.exp(-m_t))
        hh = hh * lax.rsqrt(jnp.mean(hh * hh, axis=1, keepdims=True) + EPS)
        o = oz_ref[:, h * ML_DH:(h + 1) * ML_DH]
        z = oz_ref[:, ML_W + h * ML_DH:ML_W + (h + 1) * ML_DH]
        gate = z / ((1.0 + jnp.exp(-o)) * (1.0 + jnp.exp(-z)))
        a_ref[:, hs] = (hh * gh_ref[:, hs] * gate).astype(BF16)

        m_new = m_t[c - 1:c, :]
        b_last = b_t[c - 1:c, :]
        w_end = jnp.exp(b_last - b_t + igc - m_new)
        decay = jnp.exp(b_last + m_prev - m_new)
        kw = k.astype(F32) * w_end
        new_state.append((decay * c_h + lax.dot_general(kw.astype(BF16), v, _TN, preferred_element_type=F32),
                          decay * n_h + jnp.sum(kw, axis=0, keepdims=True),
                          jnp.broadcast_to(m_new, (1, LANES))))

    for h, (c_new, n_new, m_new) in enumerate(new_state):
        c_s[h] = c_new
        n_s[h:h + 1, :] = n_new
        m_s[h:h + 1, :] = m_new

    @pl.when(ci == pl.num_programs(1) - 1)
    def _():
        cout_ref[0] = c_s[...]
        nout_ref[0] = n_s[...]
        mout_ref[0] = m_s[...]


def _mlstm(qkv, oz, small, small_t, b_gates, g_head, c0, n0, m0, nb, c, n_pad):
    rows = qkv.shape[0]
    nc = rows // (nb * c)
    bias_col = jnp.zeros((1, LANES), F32).at[0, SMALL_IG:SMALL_IG + 2 * ML_H].set(b_gates)
    bias_row = jnp.zeros((SAMPLE_ROWS, 1), F32).at[:2 * ML_H, 0].set(b_gates)
    m0b = jnp.zeros((nb, 8, LANES), F32).at[:, :ML_H, :].set(jnp.broadcast_to(m0[:, :, None], (nb, ML_H, LANES)))
    rowblk = lambda b, i: (b * nc + i, 0)
    const = lambda b, i: (0, 0)
    gates_t = small_t[SMALL_IG:SMALL_IG + SAMPLE_ROWS].reshape(SAMPLE_ROWS, rows // c, c).transpose(1, 0, 2)
    out_shape = (
        jax.ShapeDtypeStruct((rows, ML_W), BF16),
        jax.ShapeDtypeStruct((nb, ML_H, ML_DH, ML_DH), F32),
        jax.ShapeDtypeStruct((nb, ML_H, ML_DH), F32),
        jax.ShapeDtypeStruct((nb, 8, LANES), F32),
    )
    st4 = lambda b, i: (b, 0, 0, 0)
    st3 = lambda b, i: (b, 0, 0)
    a, c_out, n_out, m_out = pl.pallas_call(
        functools.partial(_mlstm_kernel, c=c, n_pad=n_pad),
        out_shape=out_shape,
        grid=(nb, nc),
        in_specs=[
            pl.BlockSpec((c, 3 * ML_W), rowblk),
            pl.BlockSpec((c, 2 * ML_W), rowblk),
            pl.BlockSpec((c, LANES), rowblk),
            pl.BlockSpec((1, SAMPLE_ROWS, c), lambda b, i: (b * nc + i, 0, 0)),
            pl.BlockSpec((1, LANES), const),
            pl.BlockSpec((SAMPLE_ROWS, 1), const),
            pl.BlockSpec((1, ML_W), const),
            pl.BlockSpec((1, ML_H, ML_DH, ML_DH), st4),
            pl.BlockSpec((1, ML_H, ML_DH), st3),
            pl.BlockSpec((1, 8, LANES), st3),
        ],
        out_specs=(
            pl.BlockSpec((c, ML_W), rowblk),
            pl.BlockSpec((1, ML_H, ML_DH, ML_DH), st4),
            pl.BlockSpec((1, ML_H, ML_DH), st3),
            pl.BlockSpec((1, 8, LANES), st3),
        ),
        scratch_shapes=[pltpu.VMEM((ML_H, ML_DH, ML_DH), F32), pltpu.VMEM((ML_H, ML_DH), F32),
                        pltpu.VMEM((8, LANES), F32)],
        compiler_params=_cparams(("arbitrary", "arbitrary")),
        name="mlstm",
    )(qkv, oz, small, gates_t, bias_col, bias_row, g_head.reshape(1, ML_W), c0, n0, m0b)
    return a, c_out, n_out, m_out[:, :ML_H, 0]


_REDUCERS = {"sum": (jnp.sum, jnp.add), "max": (jnp.max, jnp.maximum), "min": (jnp.min, jnp.minimum)}
REDUCE_CHAINS = 8


def _reduce(x, axis, op):
    fn, combine = _REDUCERS[op]
    unit = 8 if axis == 0 else LANES
    n = x.shape[axis]
    units = n // unit
    if n % unit or units < 2 * REDUCE_CHAINS:
        return fn(x, axis=axis, keepdims=True)
    base, rem = divmod(units, REDUCE_CHAINS)
    parts, start = [], 0
    for i in range(REDUCE_CHAINS):
        size = (base + (1 if i < rem else 0)) * unit
        piece = x[start:start + size] if axis == 0 else x[:, start:start + size]
        parts.append(fn(piece, axis=axis, keepdims=True))
        start += size
    while len(parts) > 1:
        parts = [combine(parts[i], parts[i + 1]) for i in range(0, len(parts), 2)]
    return parts[0]


def _count(pred, axis):
    return _reduce(jnp.where(pred, 1.0, 0.0), axis, "sum")


def _kth_largest(x_ref, k, axis, n_bisect):
    kf = float(k)
    x = x_ref[...]
    hi = _reduce(x, axis, "max")
    lo = _reduce(jnp.where(x == NEG_INF, POS_INF, x), axis, "min")

    def bisect(_, carry):
        lo, hi = carry
        mid = 0.5 * (lo + hi)
        ge = _count(x_ref[...] >= mid, axis) >= kf
        return jnp.where(ge, mid, lo), jnp.where(ge, hi, mid)

    lo, hi = lax.fori_loop(0, n_bisect, bisect, (lo, hi))

    def finished(cmin, xx):
        return jnp.where((_count(xx > cmin, axis) < kf) | (cmin == POS_INF), 1.0, 0.0)

    xx = x_ref[...]
    thr = _reduce(jnp.where(xx >= lo, xx, POS_INF), axis, "min")
    done = finished(thr, xx)

    def cond(st):
        return st[2] < 0.5

    def body(st):
        thr, done, _ = st
        xx = x_ref[...]
        cmin = _reduce(jnp.where(xx > thr, xx, POS_INF), axis, "min")
        thr = jnp.where(done < 0.5, cmin, thr)
        done = jnp.maximum(done, finished(thr, xx))
        return thr, done, jnp.min(done)

    thr, _, _ = lax.while_loop(cond, body, (thr, done, jnp.min(done)))
    return thr


def _dsa_kernel(q_ref, z_ref, idxq_ref, wt_ref, k_ref, v_ref, kidx_ref, o_ref,
                kb_s, vt_s, kib_s, x_s, sel_s, *, n_keys, qb, n_sel, n_bisect, key_step):
    j = pl.program_id(1)

    @pl.when(j == 0)
    def _():
        for h in range(SA_H):
            lanes = slice(h * SA_DH, (h + 1) * SA_DH)
            kb_s[:, lanes] = k_ref[pl.ds(h, n_keys, stride=SA_H), :].astype(BF16)
            vt_s[lanes, :] = v_ref[pl.ds(h, n_keys, stride=SA_H), :].T.astype(BF16)
        kib_s[...] = kidx_ref[...].astype(BF16)

    def attend(nk):
        xs = x_s.at[0:nk]
        ss = sel_s.at[0:nk]
        key = lax.broadcasted_iota(jnp.int32, (nk, qb), 0)
        qpos = j * qb + lax.broadcasted_iota(jnp.int32, (nk, qb), 1)
        valid = key <= qpos
        qcat = jnp.concatenate([idxq_ref[:, h * IDX_D:(h + 1) * IDX_D] for h in range(IDX_H)], axis=0)
        w_rows = [wt_ref[h:h + 1, :] * IDX_SCALE for h in range(IDX_H)]
        kc = min(IDX_KEY_CHUNK, nk)
        for c0 in range(0, nk, kc):
            d = lax.dot_general(kib_s[c0:c0 + kc, :], qcat, _NT, preferred_element_type=F32)
            sc = jnp.zeros((kc, qb), F32)
            for h in range(IDX_H):
                sc = sc + jnp.maximum(d[:, h * qb:(h + 1) * qb], 0.0) * w_rows[h]
            ok = (c0 + lax.broadcasted_iota(jnp.int32, (kc, qb), 0)) <= (
                j * qb + lax.broadcasted_iota(jnp.int32, (kc, qb), 1))
            x_s[c0:c0 + kc, :] = jnp.where(ok, sc, NEG_INF)
            sel_s[c0:c0 + kc, :] = jnp.where(ok, 1.0, 0.0)

        @pl.when((j + 1) * qb > n_sel)
        def _():
            kf = float(n_sel)
            thr = _kth_largest(xs, n_sel, 0, n_bisect)
            x = xs[...]
            need = kf - _count(x > thr, 0)
            n_tie = _count(x == thr, 0)
            qrow = j * qb + lax.broadcasted_iota(jnp.int32, (1, qb), 1)
            small = (qrow + 1) <= n_sel
            ss[...] = jnp.where(small, jnp.where(valid, 1.0, 0.0), jnp.where(x >= thr, 1.0, 0.0))
            excess = jnp.max(jnp.where((n_tie > need) & jnp.logical_not(small), 1.0, 0.0))

            @pl.when(excess > 0.5)
            def _():
                tb = min(256, nk)
                r_i = lax.broadcasted_iota(jnp.int32, (tb, tb), 0)
                c_i = lax.broadcasted_iota(jnp.int32, (tb, tb), 1)
                lower = jnp.where(c_i < r_i, 1.0, 0.0).astype(BF16)
                carry = jnp.zeros((1, qb), F32)
                for blk in range(nk // tb):
                    rows = slice(blk * tb, (blk + 1) * tb)
                    xb = x_s[rows, :]
                    tie = jnp.where(xb == thr, 1.0, 0.0)
                    rank = jnp.dot(lower, tie.astype(BF16), preferred_element_type=F32) + carry
                    keep = (xb > thr) | ((xb == thr) & (rank < need))
                    keyb = blk * tb + lax.broadcasted_iota(jnp.int32, (tb, qb), 0)
                    qposb = j * qb + lax.broadcasted_iota(jnp.int32, (tb, qb), 1)
                    smallb = jnp.where(keyb <= qposb, 1.0, 0.0)
                    sel_s[rows, :] = jnp.where(small, smallb, jnp.where(keep, 1.0, 0.0))
                    carry = carry + jnp.sum(tie, axis=0, keepdims=True)

        sel = ss[...] > 0.5
        heads = range(SA_H)
        hsl = [slice(h * SA_DH, (h + 1) * SA_DH) for h in heads]
        st = [jnp.where(sel, lax.dot_general(kb_s[0:nk, hsl[h]], q_ref[:, hsl[h]], _NT,
                                             preferred_element_type=F32), NEG_INF) for h in heads]
        mx = [_reduce(st[h], 0, "max") for h in heads]
        p = [jnp.exp2(st[h] - mx[h]) for h in heads]
        l = [_reduce(p[h], 0, "sum") for h in heads]
        ot = [jnp.dot(vt_s[hsl[h], 0:nk], p[h].astype(BF16), preferred_element_type=F32) / l[h]
              for h in heads]
        for h in heads:
            o_ref[:, hsl[h]] = (ot[h].T * _silu(z_ref[:, hsl[h]])).astype(BF16)

    n_ext = n_keys // key_step
    for e in range(n_ext):
        nk = (e + 1) * key_step
        lo_j = e * key_step // qb
        hi_j = nk // qb

        @pl.when((j >= lo_j) & (j < hi_j))
        def _(nk=nk):
            attend(nk)


def _dsa_prompt(saq, saz, idxq, small_t, k, v, kidx, nb, seq):
    rows = saq.shape[0]
    qb = min(seq, 128)
    nq = seq // qb
    n_sel = min(TOPK_MAX, seq // 4)
    qblk = lambda b, j: (b * nq + j, 0)
    per_b = lambda b, j: (b, 0)
    wt_blk = SMALL_W // 8
    return pl.pallas_call(
        functools.partial(_dsa_kernel, n_keys=seq, qb=qb, n_sel=n_sel, n_bisect=20, key_step=min(256, seq)),
        out_shape=jax.ShapeDtypeStruct((rows, SA_W), BF16),
        grid=(nb, nq),
        in_specs=[
            pl.BlockSpec((qb, SA_W), qblk),
            pl.BlockSpec((qb, SA_W), qblk),
            pl.BlockSpec((qb, IDX_H * IDX_D), qblk),
            pl.BlockSpec((8, qb), lambda b, j: (wt_blk, b * nq + j)),
            pl.BlockSpec((seq * SA_H, SA_DH), per_b),
            pl.BlockSpec((seq * SA_H, SA_DH), per_b),
            pl.BlockSpec((seq, IDX_D), per_b),
        ],
        out_specs=pl.BlockSpec((qb, SA_W), qblk),
        scratch_shapes=[pltpu.VMEM((seq, SA_W), BF16), pltpu.VMEM((SA_W, seq), BF16),
                        pltpu.VMEM((seq, IDX_D), BF16), pltpu.VMEM((seq, qb), F32),
                        pltpu.VMEM((seq, qb), F32)],
        compiler_params=_cparams(("arbitrary", "arbitrary")),
        name="dsa",
    )(saq, saz, idxq, small_t, k, v, kidx)


def _sel_kernel(pt_ref, idxq_ref, small_ref, kinew_ref, *rest, npg, n_past, n_sel, n_real, n_bisect):
    page_refs = rest[:npg]
    sel_ref, x_s = rest[npg:]
    g = pl.program_id(1)
    rws = SAMPLE_ROWS
    top = rws - SEL_ROWS
    pk = n_past + LANES
    gk = npg * PAGE_SIZE

    qs = jnp.concatenate([idxq_ref[:, h * IDX_D:(h + 1) * IDX_D] for h in range(IDX_H)], axis=0)
    small = small_ref[...]

    def scores(d):
        sc = jnp.zeros((SEL_ROWS, d.shape[1]), F32)
        for h in range(IDX_H):
            w = small[top:, SMALL_W + h:SMALL_W + h + 1] * IDX_SCALE
            sc = sc + jnp.maximum(d[h * rws + top:(h + 1) * rws, :], 0.0) * w
        return sc

    kp_t = jnp.concatenate([r[0] for r in page_refs], axis=1).astype(BF16)
    x_s[:, pl.ds(pl.multiple_of(g * gk, LANES), gk)] = scores(jnp.dot(qs, kp_t, preferred_element_type=F32))

    @pl.when(g == pl.num_programs(1) - 1)
    def _():
        knew = jnp.concatenate([kinew_ref[...], jnp.zeros((LANES - rws, IDX_D), F32)], axis=0).astype(BF16)
        row = top + lax.broadcasted_iota(jnp.int32, (SEL_ROWS, LANES), 0)
        col = lax.broadcasted_iota(jnp.int32, (SEL_ROWS, LANES), 1)
        ok = (col >= rws - n_real) & (col < rws) & (col <= row)
        d_new = lax.dot_general(qs, knew, _NT, preferred_element_type=F32)
        x_s[:, n_past:pk] = jnp.where(ok, scores(d_new), NEG_INF)

        kf = float(n_sel)
        thr = _kth_largest(x_s, n_sel, 1, n_bisect)
        x = x_s[...]
        need = kf - _count(x > thr, 1)
        n_tie = _count(x == thr, 1)
        sel_ref[0, 0:top, :] = jnp.ones((top, pk), F32)
        sel_ref[0, top:rws, :] = jnp.where(x >= thr, 1.0, 0.0)
        real = lax.broadcasted_iota(jnp.int32, (SEL_ROWS, 1), 0) >= SEL_ROWS - n_real
        excess = jnp.max(jnp.where((n_tie > need) & real, 1.0, 0.0))

        @pl.when(excess > 0.5)
        def _():
            r_i = lax.broadcasted_iota(jnp.int32, (LANES, LANES), 0)
            c_i = lax.broadcasted_iota(jnp.int32, (LANES, LANES), 1)
            upper = jnp.where(r_i < c_i, 1.0, 0.0).astype(BF16)

            def blk(i, carry):
                cols = pl.ds(pl.multiple_of(i * LANES, LANES), LANES)
                xb = x_s[:, cols]
                tie = jnp.where(xb == thr, 1.0, 0.0)
                rank = jnp.dot(tie.astype(BF16), upper, preferred_element_type=F32) + carry
                keep = (xb > thr) | ((xb == thr) & (rank < need))
                sel_ref[0, top:rws, cols] = jnp.where(keep, 1.0, 0.0)
                return carry + jnp.sum(tie, axis=1, keepdims=True)

            lax.fori_loop(0, pk // LANES, blk, jnp.zeros((SEL_ROWS, 1), F32))


def _dsa_select(page_table, idxq, small, kidx_new, cache_kidx, n_sel, n_real):
    nreq, n_pages = page_table.shape
    n_past = n_pages * PAGE_SIZE
    pk = n_past + LANES
    npg = min(SEL_PAGES_PER_STEP, n_pages)
    req = lambda b, g, pt: (b, 0)

    def page_map(i):
        return lambda b, g, pt: (pt[b, g * npg + i], 0, 0)

    grid_spec = pltpu.PrefetchScalarGridSpec(
        num_scalar_prefetch=1,
        grid=(nreq, n_pages // npg),
        in_specs=[pl.BlockSpec((SAMPLE_ROWS, IDX_H * IDX_D), req),
                  pl.BlockSpec((SAMPLE_ROWS, LANES), req),
                  pl.BlockSpec((SAMPLE_ROWS, IDX_D), req)]
                 + [pl.BlockSpec((1, IDX_D, PAGE_SIZE), page_map(i)) for i in range(npg)],
        out_specs=pl.BlockSpec((1, SAMPLE_ROWS, pk), lambda b, g, pt: (b, 0, 0)),
        scratch_shapes=[pltpu.VMEM((SEL_ROWS, pk), F32)],
    )
    assert n_real <= SEL_ROWS
    return pl.pallas_call(
        functools.partial(_sel_kernel, npg=npg, n_past=n_past, n_sel=n_sel, n_real=n_real, n_bisect=20),
        out_shape=jax.ShapeDtypeStruct((nreq, SAMPLE_ROWS, pk), F32),
        grid_spec=grid_spec,
        compiler_params=_cparams(("arbitrary", "arbitrary")),
        name="dsa_sel",
    )(page_table, idxq, small, kidx_new, *([cache_kidx] * npg))


def _att_kernel(pt_ref, q_ref, z_ref, knew_ref, vnew_ref, sel_ref, seltail_ref, *rest, npg):
    k_refs = rest[:npg]
    v_refs = rest[npg:2 * npg]
    o_ref = rest[2 * npg]
    m_s, l_s, acc_s = rest[2 * npg + 1:]
    g = pl.program_id(1)
    rws = SAMPLE_ROWS
    floor = -1e30

    @pl.when(g == 0)
    def _():
        m_s[...] = jnp.full(m_s.shape, floor, F32)
        l_s[...] = jnp.zeros(l_s.shape, F32)
        acc_s[...] = jnp.zeros(acc_s.shape, F32)

    heads = range(SA_H)
    hsl = [slice(h * SA_DH, (h + 1) * SA_DH) for h in heads]

    def update(kbs, vbs, keep):
        m_old = [m_s[h][:, 0:1] for h in heads]
        l_old = [l_s[h][:, 0:1] for h in heads]
        acc_old = [acc_s[:, hsl[h]] for h in heads]
        s = [lax.dot_general(q_ref[:, hsl[h]], kbs[h], _NT, preferred_element_type=F32) for h in heads]
        m_new = [jnp.maximum(m_old[h], jnp.max(jnp.where(keep, s[h], floor), axis=1, keepdims=True)) for h in heads]
        p = [jnp.where(keep, jnp.exp2(s[h] - m_new[h]), 0.0) for h in heads]
        pv = [jnp.dot(p[h].astype(BF16), vbs[h], preferred_element_type=F32) for h in heads]
        alpha = [jnp.exp2(m_old[h] - m_new[h]) for h in heads]
        l_new = [alpha[h] * l_old[h] + jnp.sum(p[h], axis=1, keepdims=True) for h in heads]
        for h in heads:
            acc_s[:, hsl[h]] = alpha[h] * acc_old[h] + pv[h]
            l_s[h] = jnp.broadcast_to(l_new[h], (rws, LANES))
            m_s[h] = jnp.broadcast_to(m_new[h], (rws, LANES))

    def head_rows(refs, h):
        return jnp.concatenate([r[0, pl.ds(h, PAGE_SIZE, stride=SA_H), :] for r in refs], axis=0).astype(BF16)

    update([head_rows(k_refs, h) for h in heads], [head_rows(v_refs, h) for h in heads], sel_ref[0] > 0.5)

    @pl.when(g == pl.num_programs(1) - 1)
    def _():
        update([knew_ref[pl.ds(h, rws, stride=SA_H), :].astype(BF16) for h in heads],
               [vnew_ref[pl.ds(h, rws, stride=SA_H), :].astype(BF16) for h in heads],
               seltail_ref[0][:, :rws] > 0.5)
        for h in heads:
            o_ref[:, hsl[h]] = (acc_s[:, hsl[h]] / l_s[h][:, 0:1] * _silu(z_ref[:, hsl[h]])).astype(BF16)


def _dsa_attend(page_table, saq, saz, k_new, v_new, sel, cache_k, cache_v):
    nreq, n_pages = page_table.shape
    n_past = n_pages * PAGE_SIZE
    npg = min(ATT_PAGES_PER_STEP, n_pages)
    prow = PAGE_SIZE * SA_H
    req = lambda b, g, pt: (b, 0)

    def page_map(i):
        return lambda b, g, pt: (pt[b, g * npg + i], 0, 0)

    page_specs = [pl.BlockSpec((1, prow, SA_DH), page_map(i)) for i in range(npg)]
    grid_spec = pltpu.PrefetchScalarGridSpec(
        num_scalar_prefetch=1,
        grid=(nreq, n_pages // npg),
        in_specs=[pl.BlockSpec((SAMPLE_ROWS, SA_W), req), pl.BlockSpec((SAMPLE_ROWS, SA_W), req),
                  pl.BlockSpec((SAMPLE_ROWS * SA_H, SA_DH), req), pl.BlockSpec((SAMPLE_ROWS * SA_H, SA_DH), req),
                  pl.BlockSpec((1, SAMPLE_ROWS, npg * PAGE_SIZE), lambda b, g, pt: (b, 0, g)),
                  pl.BlockSpec((1, SAMPLE_ROWS, LANES), lambda b, g, pt: (b, 0, n_past // LANES))]
                 + page_specs + page_specs,
        out_specs=pl.BlockSpec((SAMPLE_ROWS, SA_W), req),
        scratch_shapes=[pltpu.VMEM((SA_H, SAMPLE_ROWS, LANES), F32), pltpu.VMEM((SA_H, SAMPLE_ROWS, LANES), F32),
                        pltpu.VMEM((SAMPLE_ROWS, SA_W), F32)],
    )
    return pl.pallas_call(
        functools.partial(_att_kernel, npg=npg),
        out_shape=jax.ShapeDtypeStruct((nreq * SAMPLE_ROWS, SA_W), BF16),
        grid_spec=grid_spec,
        compiler_params=_cparams(("arbitrary", "arbitrary")),
        name="dsa_att",
    )(page_table, saq, saz, k_new, v_new, sel, sel, *([cache_k] * npg), *([cache_v] * npg))


def _mem_attend(q_ref, z_ref, mk_ref, mv_ref):
    mk = mk_ref[...].astype(BF16)
    mv = mv_ref[...].astype(BF16)
    outs = []
    for h in range(MEM_H):
        hs = slice(h * MEM_DH, (h + 1) * MEM_DH)
        s = lax.dot_general(q_ref[:, hs], mk[:, hs], _NT, preferred_element_type=F32)
        p = jnp.exp(s - jnp.max(s, axis=1, keepdims=True))
        l = jnp.sum(p, axis=1, keepdims=True)
        o = jnp.dot(p.astype(BF16), mv[:, hs], preferred_element_type=F32) / l
        outs.append((o * _silu(z_ref[:, hs])).astype(BF16))
    return jnp.concatenate(outs, axis=1)


def _memattn_kernel(q_ref, z_ref, mk_ref, mv_ref, o_ref):
    o_ref[...] = _mem_attend(q_ref, z_ref, mk_ref, mv_ref)


def _memattn(memq, memz, mk, mv, nb, tq):
    rows = memq.shape[0]
    n_mem = mk.shape[0] // nb
    nq = rows // (nb * tq)
    qblk = lambda b, i: (b * nq + i, 0)
    per_b = lambda b, i: (b, 0)
    return pl.pallas_call(
        _memattn_kernel,
        out_shape=jax.ShapeDtypeStruct((rows, MEM_W), BF16),
        grid=(nb, nq),
        in_specs=[pl.BlockSpec((tq, MEM_W), qblk), pl.BlockSpec((tq, MEM_W), qblk),
                  pl.BlockSpec((n_mem, MEM_W), per_b), pl.BlockSpec((n_mem, MEM_W), per_b)],
        out_specs=pl.BlockSpec((tq, MEM_W), qblk),
        compiler_params=_cparams(("arbitrary", "arbitrary")),
        name="memattn",
    )(memq, memz, mk, mv)


def _mixout_kernel(x_ref, a_ref, b_ref, c_ref, wa_ref, wb_ref, wc_ref, g_ref, y_ref):
    acc = (jnp.dot(a_ref[...], wa_ref[...], preferred_element_type=F32)
           + jnp.dot(b_ref[...], wb_ref[...], preferred_element_type=F32)
           + jnp.dot(c_ref[...], wc_ref[...], preferred_element_type=F32))
    y = acc * lax.rsqrt(jnp.mean(acc * acc, axis=-1, keepdims=True) + EPS) * g_ref[...]
    y_ref[...] = x_ref[...] + y


def _mixout_mem_kernel(x_ref, a_ref, b_ref, q_ref, z_ref, mk_ref, mv_ref, wa_ref, wb_ref, wc_ref, g_ref, y_ref):
    c = _mem_attend(q_ref, z_ref, mk_ref, mv_ref)
    acc = (jnp.dot(a_ref[...], wa_ref[...], preferred_element_type=F32)
           + jnp.dot(b_ref[...], wb_ref[...], preferred_element_type=F32)
           + jnp.dot(c, wc_ref[...], preferred_element_type=F32))
    y = acc * lax.rsqrt(jnp.mean(acc * acc, axis=-1, keepdims=True) + EPS) * g_ref[...]
    y_ref[...] = x_ref[...] + y


def _mixout_mem(x2d, a, b, memq, memz, mk, mv, w_out, g_post, tm, seq):
    rows = x2d.shape[0]
    n_mem = mk.shape[0] // (rows // seq)
    wb16 = w_out.astype(BF16)
    row = lambda i: (i, 0)
    const = lambda i: (0, 0)
    per_req = lambda i: (i // (seq // tm), 0)
    return pl.pallas_call(
        _mixout_mem_kernel,
        out_shape=jax.ShapeDtypeStruct((rows, D_MODEL), F32),
        grid=(rows // tm,),
        in_specs=[pl.BlockSpec((tm, D_MODEL), row), pl.BlockSpec((tm, ML_W), row),
                  pl.BlockSpec((tm, SA_W), row), pl.BlockSpec((tm, MEM_W), row), pl.BlockSpec((tm, MEM_W), row),
                  pl.BlockSpec((n_mem, MEM_W), per_req), pl.BlockSpec((n_mem, MEM_W), per_req),
                  pl.BlockSpec((ML_W, D_MODEL), const), pl.BlockSpec((SA_W, D_MODEL), const),
                  pl.BlockSpec((MEM_W, D_MODEL), const), pl.BlockSpec((1, D_MODEL), const)],
        out_specs=pl.BlockSpec((tm, D_MODEL), row),
        compiler_params=_cparams(("arbitrary",)),
        name="mixout_mem",
    )(x2d, a, b, memq, memz, mk, mv, wb16[:ML_W], wb16[ML_W:ML_W + SA_W], wb16[ML_W + SA_W:],
      g_post.reshape(1, D_MODEL))


def _mixout(x2d, a, b, c, w_out, g_post, tm):
    rows = x2d.shape[0]
    wb16 = w_out.astype(BF16)
    row = lambda i: (i, 0)
    const = lambda i: (0, 0)
    return pl.pallas_call(
        _mixout_kernel,
        out_shape=jax.ShapeDtypeStruct((rows, D_MODEL), F32),
        grid=(rows // tm,),
        in_specs=[pl.BlockSpec((tm, D_MODEL), row), pl.BlockSpec((tm, ML_W), row),
                  pl.BlockSpec((tm, SA_W), row), pl.BlockSpec((tm, MEM_W), row),
                  pl.BlockSpec((ML_W, D_MODEL), const), pl.BlockSpec((SA_W, D_MODEL), const),
                  pl.BlockSpec((MEM_W, D_MODEL), const), pl.BlockSpec((1, D_MODEL), const)],
        out_specs=pl.BlockSpec((tm, D_MODEL), row),
        compiler_params=_cparams(("arbitrary",)),
        name="mixout",
    )(x2d, a, b, c, wb16[:ML_W], wb16[ML_W:ML_W + SA_W], wb16[ML_W + SA_W:], g_post.reshape(1, D_MODEL))


def _layer(x_p, x_s, st_c, st_n, st_m, c_k, c_v, c_kidx, c_mk, c_mv, page_table, mem_prompt,
           g_pre, w_in, b_gates, g_head, w_mem_k, w_mem_v, g_mem, w_out, g_post):
    nb, seq, _ = x_p.shape
    nreq, t_dec, _ = x_s.shape
    n_mem = mem_prompt.shape[1]
    n_past = page_table.shape[1] * PAGE_SIZE
    weights = _relayout_w_in(w_in)

    tm = min(512, seq)
    tabs = _rope_tables(jnp.arange(seq, dtype=jnp.int32))
    (qkv, oz, saq, k, v, saz, memq, memz, idxq, kidx, small, small_t) = _project(
        x_p.reshape(nb * seq, D_MODEL), g_pre, weights, tabs, tm)
    c0 = jnp.zeros((nb, ML_H, ML_DH, ML_DH), F32)
    n0 = jnp.zeros((nb, ML_H, ML_DH), F32)
    m0 = jnp.zeros((nb, ML_H), F32)
    a_p, p_c, p_n, p_m = _mlstm(qkv, oz, small, small_t, b_gates, g_head, c0, n0, m0, nb, min(ML_CHUNK, seq), 0)
    b_p = _dsa_prompt(saq, saz, idxq, small_t, k, v, kidx, nb, seq)
    mk, mv = _memkv(mem_prompt.reshape(nb * n_mem, D_MODEL), g_mem, w_mem_k, w_mem_v, n_mem)
    y_p = _mixout_mem(x_p.reshape(nb * seq, D_MODEL), a_p, b_p, memq, memz, mk, mv, w_out, g_post,
                      tm, seq).reshape(nb, seq, D_MODEL)

    rws = SAMPLE_ROWS
    n_padrow = rws - t_dec
    xs_pad = jnp.concatenate([jnp.zeros((nreq, n_padrow, D_MODEL), F32), x_s], axis=1).reshape(nreq * rws, D_MODEL)
    pos_s = jnp.tile(jnp.concatenate([jnp.zeros((n_padrow,), jnp.int32),
                                      n_past + jnp.arange(t_dec, dtype=jnp.int32)]), nreq)
    tabs_s = _rope_tables(pos_s)
    (qkv_s, oz_s, saq_s, k_s, v_s, saz_s, memq_s, memz_s, idxq_s, kidx_s, small_s, small_t_s) = _project(
        xs_pad, g_pre, weights, tabs_s, nreq * rws)
    a_s, s_c, s_n, s_m = _mlstm(qkv_s, oz_s, small_s, small_t_s, b_gates, g_head, st_c, st_n, st_m,
                                nreq, rws, n_padrow)
    n_sel = min(TOPK_MAX, (n_past + t_dec) // 4)
    sel = _dsa_select(page_table, idxq_s, small_s, kidx_s, jnp.swapaxes(c_kidx, 1, 2), n_sel, t_dec)
    b_s = _dsa_attend(page_table, saq_s, saz_s, k_s, v_s, sel,
                      c_k.reshape(c_k.shape[0], PAGE_SIZE * SA_H, SA_DH),
                      c_v.reshape(c_v.shape[0], PAGE_SIZE * SA_H, SA_DH))
    c_s = _memattn(memq_s, memz_s, c_mk.reshape(nreq * n_mem, MEM_W), c_mv.reshape(nreq * n_mem, MEM_W), nreq, rws)
    y_s = _mixout(xs_pad, a_s, b_s, c_s, w_out, g_post, nreq * rws)

    def real(a2d):
        return a2d.reshape(nreq, rws, -1)[:, n_padrow:]

    new = (p_c, p_n, p_m,
           k.reshape(nb, seq, SA_H, SA_DH), v.reshape(nb, seq, SA_H, SA_DH), kidx.reshape(nb, seq, IDX_D),
           mk.reshape(nb, n_mem, MEM_H, MEM_DH), mv.reshape(nb, n_mem, MEM_H, MEM_DH),
           s_c, s_n, s_m,
           real(k_s).reshape(nreq, t_dec, SA_H, SA_DH), real(v_s).reshape(nreq, t_dec, SA_H, SA_DH), real(kidx_s))
    return y_p, real(y_s), new


def kernel(x_prompt, x_sample, state_mlstm_C, state_mlstm_n, state_mlstm_m, cache_k, cache_v, cache_kidx,
           cache_mem_k, cache_mem_v, page_table, mem_prompt, g_pre, w_in, b_gates, g_head, w_mem_k, w_mem_v,
           g_mem, w_out, g_post):
    xp, xs = x_prompt, x_sample
    per_layer = []
    for l in range(w_in.shape[0]):
        xp, xs, new = _layer(xp, xs, state_mlstm_C[l], state_mlstm_n[l], state_mlstm_m[l],
                             cache_k[l], cache_v[l], cache_kidx[l], cache_mem_k[l], cache_mem_v[l],
                             page_table, mem_prompt, g_pre[l], w_in[l], b_gates[l], g_head[l],
                             w_mem_k[l], w_mem_v[l], g_mem[l], w_out[l], g_post[l])
        per_layer.append(new)
    stacked = [jnp.stack(a) for a in zip(*per_layer)]
    return (xp, xs, *stacked)
```

```python
import functools

import jax
import jax.numpy as jnp
from jax import lax
from jax.experimental import pallas as pl
from jax.experimental.pallas import tpu as pltpu

F32 = jnp.float32
BF16 = jnp.bfloat16

D_MODEL = 2048
ML_H = 4
ML_W = D_MODEL // 2
ML_DH = ML_W // ML_H
SA_H = 4
SA_W = D_MODEL // 4
SA_DH = SA_W // SA_H
MEM_H = 4
MEM_W = D_MODEL // 4
MEM_DH = MEM_W // MEM_H
IDX_H = 8
IDX_D = 64
IDX_SCALE = (IDX_H * IDX_D) ** -0.5
TOPK_MAX = 256
ROPE_THETA = 10000.0
LOG2E = 1.4426950408889634
EPS = 1e-6
PAGE_SIZE = 128

LANES = 128
CB = 512
SMALL_IG = 64
SMALL_LF = 68
SMALL_W = 72
SAMPLE_ROWS = 16
SEL_ROWS = 8
SEL_PAGES_PER_STEP = 32
ATT_PAGES_PER_STEP = 16
ML_CHUNK = 256
IDX_KEY_CHUNK = 256
VMEM_LIMIT = 56 * 1024 * 1024
VMEM_LIMIT_FUSED = 60 * 1024 * 1024
NEG_INF = float("-inf")
POS_INF = float("inf")

_NT = (((1,), (1,)), ((), ()))
_TN = (((0,), (0,)), ((), ()))


def _cparams(sem):
    return pltpu.CompilerParams(dimension_semantics=sem, vmem_limit_bytes=VMEM_LIMIT)


def _sigmoid(x):
    return 1.0 / (1.0 + jnp.exp(-x))


def _silu(x):
    return x * _sigmoid(x)


def _log_sigmoid(x):
    return jnp.minimum(x, 0.0) - jnp.log1p(jnp.exp(-jnp.abs(x)))


def _split3(x):
    hi = x.astype(BF16)
    r = x - hi.astype(F32)
    mid = r.astype(BF16)
    lo = (r - mid.astype(F32)).astype(BF16)
    return hi, mid, lo


def _rope128(x, cos, sin_signed):
    return x * cos + pltpu.roll(x, 64, 1) * sin_signed


def _rope64(x, cos, sin_signed):
    lane = lax.broadcasted_iota(jnp.int32, x.shape, 1)
    first_half = (lane % 64) < 32
    partner = jnp.where(first_half, pltpu.roll(x, 96, 1), pltpu.roll(x, 32, 1))
    return x * cos + partner * sin_signed


def _normed(x_ref, g_ref):
    x = x_ref[...]
    return (x * lax.rsqrt(jnp.mean(x * x, axis=-1, keepdims=True) + EPS) * g_ref[...]).astype(BF16)


def _proj_ml_kernel(x_ref, g_ref, w_ref, qkv_ref, oz_ref, u_ref):
    u_ref[...] = _normed(x_ref, g_ref)
    n_qkv = 3 * ML_W // CB
    for cb in range(5 * ML_W // CB):
        acc = lax.dot_general(u_ref[...], w_ref[cb * CB:(cb + 1) * CB, :], _NT, preferred_element_type=F32)
        if ML_W <= cb * CB < 2 * ML_W:
            acc = acc * (ML_DH ** -0.5)
        if cb < n_qkv:
            qkv_ref[:, cb * CB:(cb + 1) * CB] = acc.astype(BF16)
        else:
            oz_ref[:, (cb - n_qkv) * CB:(cb - n_qkv + 1) * CB] = acc


def _proj_rest_kernel(x_ref, g_ref, wsa_ref, wmem_ref, ws_ref, wst_ref, c128_ref, s128_ref, c64_ref, s64_ref,
                      saq_ref, k_ref, v_ref, saz_ref, memq_ref, memz_ref, idxq_ref,
                      kidx_ref, small_ref, smallt_ref, u_ref):
    u_ref[...] = _normed(x_ref, g_ref)
    sm = jnp.dot(u_ref[...], ws_ref[...], preferred_element_type=F32)
    small_ref[...] = sm
    kidx_ref[...] = _rope64(sm, c64_ref[...], s64_ref[...])[:, :IDX_D]
    smallt_ref[...] = lax.dot_general(wst_ref[...], u_ref[...], _NT, preferred_element_type=F32)

    def block(cb, w_ref=wsa_ref):
        return lax.dot_general(u_ref[...], w_ref[cb * CB:(cb + 1) * CB, :], _NT, preferred_element_type=F32)

    def rope_heads(acc, fn, cos_ref, sin_ref):
        return jnp.concatenate(
            [fn(acc[:, h * LANES:(h + 1) * LANES], cos_ref[...], sin_ref[...]) for h in range(CB // LANES)], axis=1)

    saq_ref[...] = (rope_heads(block(0), _rope128, c128_ref, s128_ref) * (SA_DH ** -0.5 * LOG2E)).astype(BF16)
    tm = x_ref.shape[0]
    k_acc = block(1)
    v_acc = block(2)
    for h in range(SA_H):
        lanes = slice(h * SA_DH, (h + 1) * SA_DH)
        k_ref[pl.ds(h, tm, stride=SA_H), :] = _rope128(k_acc[:, lanes], c128_ref[...], s128_ref[...])
        v_ref[pl.ds(h, tm, stride=SA_H), :] = v_acc[:, lanes]
    saz_ref[...] = block(3)
    idxq_ref[...] = rope_heads(block(4), _rope64, c64_ref, s64_ref).astype(BF16)
    memq_ref[...] = (block(0, wmem_ref) * (MEM_DH ** -0.5)).astype(BF16)
    memz_ref[...] = block(1, wmem_ref)


def _project(x2d, g_pre, weights, tabs, tm, with_ml=True):
    w_ml, w_sa, w_mem, w_small, w_small_t = weights
    rows = x2d.shape[0]
    c128, s128, c64, s64 = tabs
    ntab = c128.shape[0] // tm
    n_ml = 5 * ML_W
    row_only = lambda i: (i, 0)
    tab_map = lambda i: (i % ntab, 0)
    const = lambda i: (0, 0)
    resident = pl.Buffered(1)
    g2d = g_pre.reshape(1, D_MODEL)

    qkv, oz = (None, None) if not with_ml else pl.pallas_call(
        _proj_ml_kernel,
        out_shape=(jax.ShapeDtypeStruct((rows, 3 * ML_W), BF16),
                   jax.ShapeDtypeStruct((rows, 2 * ML_W), F32)),
        grid=(rows // tm,),
        in_specs=[pl.BlockSpec((tm, D_MODEL), row_only),
                  pl.BlockSpec((1, D_MODEL), const),
                  pl.BlockSpec((n_ml, D_MODEL), const, pipeline_mode=resident)],
        out_specs=(pl.BlockSpec((tm, 3 * ML_W), row_only), pl.BlockSpec((tm, 2 * ML_W), row_only)),
        scratch_shapes=[pltpu.VMEM((tm, D_MODEL), BF16)],
        compiler_params=_cparams(("arbitrary",)),
        name="proj_ml",
    )(x2d, g2d, w_ml)

    out_shape = (
        jax.ShapeDtypeStruct((rows, SA_W), BF16),
        jax.ShapeDtypeStruct((rows * SA_H, SA_DH), F32),
        jax.ShapeDtypeStruct((rows * SA_H, SA_DH), F32),
        jax.ShapeDtypeStruct((rows, SA_W), F32),
        jax.ShapeDtypeStruct((rows, MEM_W), BF16),
        jax.ShapeDtypeStruct((rows, MEM_W), F32),
        jax.ShapeDtypeStruct((rows, IDX_H * IDX_D), BF16),
        jax.ShapeDtypeStruct((rows, IDX_D), F32),
        jax.ShapeDtypeStruct((rows, LANES), F32),
        jax.ShapeDtypeStruct((LANES, rows), F32),
    )
    out_specs = (
        pl.BlockSpec((tm, CB), row_only),
        pl.BlockSpec((tm * SA_H, SA_DH), row_only),
        pl.BlockSpec((tm * SA_H, SA_DH), row_only),
        pl.BlockSpec((tm, CB), row_only),
        pl.BlockSpec((tm, CB), row_only),
        pl.BlockSpec((tm, CB), row_only),
        pl.BlockSpec((tm, CB), row_only),
        pl.BlockSpec((tm, IDX_D), row_only),
        pl.BlockSpec((tm, LANES), row_only),
        pl.BlockSpec((LANES, tm), lambda i: (0, i)),
    )
    in_specs = [
        pl.BlockSpec((tm, D_MODEL), row_only),
        pl.BlockSpec((1, D_MODEL), const),
        pl.BlockSpec(w_sa.shape, const, pipeline_mode=resident),
        pl.BlockSpec(w_mem.shape, const, pipeline_mode=resident),
        pl.BlockSpec((D_MODEL, LANES), const, pipeline_mode=resident),
        pl.BlockSpec((LANES, D_MODEL), const, pipeline_mode=resident),
        pl.BlockSpec((tm, LANES), tab_map),
        pl.BlockSpec((tm, LANES), tab_map),
        pl.BlockSpec((tm, LANES), tab_map),
        pl.BlockSpec((tm, LANES), tab_map),
    ]
    rest = pl.pallas_call(
        _proj_rest_kernel,
        out_shape=out_shape,
        grid=(rows // tm,),
        in_specs=in_specs,
        out_specs=out_specs,
        scratch_shapes=[pltpu.VMEM((tm, D_MODEL), BF16)],
        compiler_params=_cparams(("arbitrary",)),
        name="proj_rest",
    )(x2d, g2d, w_sa, w_mem, w_small, w_small_t, c128, s128, c64, s64)
    return (qkv, oz, *rest)


def _rope_tables(pos):
    def tab(half):
        inv = ROPE_THETA ** (-jnp.arange(half, dtype=F32) / half)
        ang = pos.astype(F32)[:, None] * inv[None, :]
        return jnp.cos(ang), jnp.sin(ang)

    c, s = tab(SA_DH // 2)
    c128 = jnp.concatenate([c, c], axis=1)
    s128 = jnp.concatenate([-s, s], axis=1)
    c, s = tab(IDX_D // 2)
    c64 = jnp.concatenate([c, c, c, c], axis=1)
    s64 = jnp.concatenate([-s, s, -s, s], axis=1)
    return c128, s128, c64, s64


def _relayout_w_in(w_in):
    off = {}
    o = 0
    for name, w in (('ml_q', ML_W), ('ml_k', ML_W), ('ml_v', ML_W), ('ml_o', ML_W), ('ml_z', ML_W),
                    ('ml_i', ML_H), ('ml_f', ML_H), ('sa_q', SA_W), ('sa_k', SA_W), ('sa_v', SA_W),
                    ('sa_z', SA_W), ('idx_q', IDX_H * IDX_D), ('idx_k', IDX_D), ('idx_w', IDX_H),
                    ('mem_q', MEM_W), ('mem_z', MEM_W)):
        off[name] = (o, w)
        o += w

    w_t = w_in.T

    def col(name):
        a, w = off[name]
        return w_t[a:a + w]

    def span(first, last):
        return w_t[off[first][0]:off[last][0] + off[last][1]].astype(BF16)

    w_ml = w_t.astype(BF16)
    w_sa = span('sa_q', 'idx_q')
    w_mem = span('mem_q', 'mem_z')
    small_t = jnp.concatenate([
        col('idx_k'), col('ml_i'), col('ml_f'), col('idx_w'),
        jnp.zeros((LANES - IDX_D - 2 * ML_H - IDX_H, D_MODEL), F32)], axis=0).astype(BF16)
    return (w_ml, w_sa, w_mem, small_t.T, small_t)


def _memkv_kernel(m_ref, g_ref, wk_ref, wv_ref, k_ref, v_ref):
    x = m_ref[...]
    u = (x * lax.rsqrt(jnp.mean(x * x, axis=-1, keepdims=True) + EPS) * g_ref[...]).astype(BF16)
    k_ref[...] = jnp.dot(u, wk_ref[...], preferred_element_type=F32)
    v_ref[...] = jnp.dot(u, wv_ref[...], preferred_element_type=F32)


def _memkv(mem2d, g_mem, wk, wv, n_mem):
    rows = mem2d.shape[0]
    row = lambda i: (i, 0)
    const = lambda i: (0, 0)
    return pl.pallas_call(
        _memkv_kernel,
        out_shape=(jax.ShapeDtypeStruct((rows, MEM_W), F32), jax.ShapeDtypeStruct((rows, MEM_W), F32)),
        grid=(rows // n_mem,),
        in_specs=[pl.BlockSpec((n_mem, D_MODEL), row), pl.BlockSpec((1, D_MODEL), const),
                  pl.BlockSpec((D_MODEL, MEM_W), const), pl.BlockSpec((D_MODEL, MEM_W), const)],
        out_specs=(pl.BlockSpec((n_mem, MEM_W), row), pl.BlockSpec((n_mem, MEM_W), row)),
        compiler_params=_cparams(("arbitrary",)),
        name="memkv",
    )(mem2d, g_mem.reshape(1, D_MODEL), wk.astype(BF16), wv.astype(BF16))


def _mlstm_chunk(q_of, k_of, v_of, gate_of, g_c, g_r, gh_ref, c_s, n_s, m_s, a_store, c, n_pad,
                 between_heads=None):
    ri = lax.broadcasted_iota(jnp.int32, (c, c), 0)
    cj = lax.broadcasted_iota(jnp.int32, (c, c), 1)
    causal = cj <= ri
    tri = jnp.where(causal, 1.0, 0.0).astype(BF16)
    tri_t = jnp.where(ri <= cj, 1.0, 0.0).astype(BF16)

    pad_c = lax.broadcasted_iota(jnp.int32, (c, LANES), 0) < n_pad
    ig_c = jnp.where(pad_c, NEG_INF, g_c)
    lf_c = jnp.where(pad_c, 0.0, _log_sigmoid(g_c))
    b_c = sum(jnp.dot(tri, p, preferred_element_type=F32) for p in _split3(lf_c))
    pad_r = lax.broadcasted_iota(jnp.int32, (SAMPLE_ROWS, c), 1) < n_pad
    ig_r = jnp.where(pad_r, NEG_INF, g_r)
    lf_r = jnp.where(pad_r, 0.0, _log_sigmoid(g_r))
    b_r = sum(jnp.dot(p, tri_t, preferred_element_type=F32) for p in _split3(lf_r))

    m_all = m_s[...]
    n_all = n_s[...]
    c_all = [c_s[h] for h in range(ML_H)]
    new_state = []
    for h in range(ML_H):
        hs = slice(h * ML_DH, (h + 1) * ML_DH)
        m_prev = m_all[h:h + 1, 0:1]
        b_t = b_c[:, SMALL_LF + h:SMALL_LF + h + 1]
        igc = ig_c[:, SMALL_IG + h:SMALL_IG + h + 1]
        b_s = b_r[ML_H + h:ML_H + h + 1, :]
        igr = ig_r[h:h + 1, :]
        a = jnp.where(causal, b_t - b_s + igr, NEG_INF)
        bm = b_t + m_prev
        m_t = jnp.maximum(bm, jnp.max(a, axis=1, keepdims=True))
        inter = jnp.exp(bm - m_t)
        dmat = jnp.exp(a - m_t)
        q, k, v = q_of(h), k_of(h), v_of(h)
        s = lax.dot_general(q, k, _NT, preferred_element_type=F32) * dmat
        if between_heads is not None:
            between_heads()
        c_h = c_all[h]
        n_h = n_all[h:h + 1, :]
        num = (jnp.dot(s.astype(BF16), v, preferred_element_type=F32)
               + inter * jnp.dot(q, c_h.astype(BF16), preferred_element_type=F32))
        qn = (jnp.sum(s, axis=1, keepdims=True)
              + inter * jnp.sum(q.astype(F32) * n_h, axis=1, keepdims=True))
        hh = num / jnp.maximum(jnp.abs(qn), jnp.exp(-m_t))
        hh = hh * lax.rsqrt(jnp.mean(hh * hh, axis=1, keepdims=True) + EPS)
        if between_heads is not None:
            between_heads()
        a_store(h, (hh * gh_ref[:, hs] * gate_of(h)).astype(BF16))

        m_new = m_t[c - 1:c, :]
        b_last = b_t[c - 1:c, :]
        w_end = jnp.exp(b_last - b_t + igc - m_new)
        decay = jnp.exp(b_last + m_prev - m_new)
        kw = k.astype(F32) * w_end
        new_state.append((decay * c_h + lax.dot_general(kw.astype(BF16), v, _TN, preferred_element_type=F32),
                          decay * n_h + jnp.sum(kw, axis=0, keepdims=True),
                          jnp.broadcast_to(m_new, (1, LANES))))
        if between_heads is not None:
            between_heads()

    for h, (c_new, n_new, m_new) in enumerate(new_state):
        c_s[h] = c_new
        n_s[h:h + 1, :] = n_new
        m_s[h:h + 1, :] = m_new


def _out_gate(o, z):
    return z / ((1.0 + jnp.exp(-o)) * (1.0 + jnp.exp(-z)))


def _head_cols(group, h):
    return slice(group * ML_W + h * ML_DH, group * ML_W + (h + 1) * ML_DH)


def _mlstm_kernel(qkv_ref, oz_ref, gc_ref, gt_ref, bcol_ref, brow_ref, gh_ref, c0_ref, n0_ref, m0_ref,
                  a_ref, cout_ref, nout_ref, mout_ref, c_s, n_s, m_s, *, c, n_pad):
    ci = pl.program_id(1)

    @pl.when(ci == 0)
    def _():
        c_s[...] = c0_ref[0]
        n_s[...] = n0_ref[0]
        m_s[...] = m0_ref[0]

    def a_store(h, value):
        a_ref[:, _head_cols(0, h)] = value

    _mlstm_chunk(lambda h: qkv_ref[:, _head_cols(0, h)], lambda h: qkv_ref[:, _head_cols(1, h)],
                 lambda h: qkv_ref[:, _head_cols(2, h)],
                 lambda h: _out_gate(oz_ref[:, _head_cols(0, h)], oz_ref[:, _head_cols(1, h)]),
                 gc_ref[...] + bcol_ref[...], gt_ref[0] + brow_ref[...], gh_ref, c_s, n_s, m_s, a_store, c, n_pad)

    @pl.when(ci == pl.num_programs(1) - 1)
    def _():
        cout_ref[0] = c_s[...]
        nout_ref[0] = n_s[...]
        mout_ref[0] = m_s[...]


def _ml_fused_kernel(x_ref, g_ref, w_ref, ws_ref, wst_ref, bcol_ref, brow_ref, gh_ref,
                     a_ref, cout_ref, nout_ref, mout_ref,
                     u_s, qkv_a, qkv_b, gate_a, gate_b, sm_a, sm_b, smt_a, smt_b, c_s, n_s, m_s, *, c, nrb):
    i = pl.program_id(0)
    tm = x_ref.shape[0]

    @pl.when(i == 0)
    def _():
        qkv_b[...] = jnp.zeros(qkv_b.shape, BF16)
        gate_b[...] = jnp.zeros(gate_b.shape, F32)
        sm_b[...] = jnp.zeros(sm_b.shape, F32)
        smt_b[...] = jnp.zeros(smt_b.shape, F32)

    @pl.when((i == 0) | (lax.rem(jnp.maximum(i - 1, 0), nrb) == 0))
    def _():
        c_s[...] = jnp.zeros(c_s.shape, F32)
        n_s[...] = jnp.zeros(n_s.shape, F32)
        m_s[...] = jnp.zeros(m_s.shape, F32)

    def projection_pieces(qkv_w, gate_w, sm_w, smt_w):
        pw = ML_DH

        def block(row0):
            return lax.dot_general(u_s[...], w_ref[row0:row0 + pw, :], _NT, preferred_element_type=F32)

        def narrow():
            sm_w[...] = jnp.dot(u_s[...], ws_ref[...], preferred_element_type=F32)
            smt_w[...] = lax.dot_general(wst_ref[SMALL_IG:SMALL_IG + SAMPLE_ROWS, :], u_s[...], _NT,
                                         preferred_element_type=F32)

        def qkv_block(cb):
            acc = block(cb * pw)
            if ML_W <= cb * pw < 2 * ML_W:
                acc = acc * (ML_DH ** -0.5)
            qkv_w[:, cb * pw:(cb + 1) * pw] = acc.astype(BF16)

        def gate_block(cb):
            gate_w[:, cb * pw:(cb + 1) * pw] = _out_gate(block(3 * ML_W + cb * pw), block(4 * ML_W + cb * pw))

        pieces = [narrow]
        pieces += [functools.partial(qkv_block, cb) for cb in range(3 * ML_W // pw)]
        pieces += [functools.partial(gate_block, cb) for cb in range(ML_W // pw)]
        return pieces

    def step(write, read):
        u_s[...] = _normed(x_ref, g_ref)
        pieces = projection_pieces(*write)
        n_slots = (tm // c) * ML_H * 3
        n_pieces = len(pieces)
        slot = [0]

        def emit():
            slot[0] += 1
            while n_pieces - len(pieces) < (slot[0] * n_pieces) // n_slots:
                pieces.pop(0)()

        qkv_r, gate_r, sm_r, smt_r = read
        for ck in range(tm // c):
            rows = slice(ck * c, (ck + 1) * c)

            def a_store(h, value, rows=rows):
                a_ref[rows, _head_cols(0, h)] = value

            _mlstm_chunk(lambda h, rows=rows: qkv_r[rows, _head_cols(0, h)],
                         lambda h, rows=rows: qkv_r[rows, _head_cols(1, h)],
                         lambda h, rows=rows: qkv_r[rows, _head_cols(2, h)],
                         lambda h, rows=rows: gate_r[rows, _head_cols(0, h)],
                         sm_r[rows, :] + bcol_ref[...], smt_r[:, rows] + brow_ref[...],
                         gh_ref, c_s, n_s, m_s, a_store, c, 0, between_heads=emit)
        while pieces:
            pieces.pop(0)()

    set_a = (qkv_a, gate_a, sm_a, smt_a)
    set_b = (qkv_b, gate_b, sm_b, smt_b)

    @pl.when(lax.rem(i, 2) == 0)
    def _():
        step(set_a, set_b)

    @pl.when(lax.rem(i, 2) == 1)
    def _():
        step(set_b, set_a)

    @pl.when((i >= 1) & (lax.rem(jnp.maximum(i - 1, 0), nrb) == nrb - 1))
    def _():
        cout_ref[0] = c_s[...]
        nout_ref[0] = n_s[...]
        mout_ref[0] = m_s[...]


def _ml_fused(x2d, g_pre, w_all, w_small, w_small_t, b_gates, g_head, nb, seq, tm, c):
    rows = x2d.shape[0]
    nrb = seq // tm
    n_blocks = rows // tm
    n_ml = 5 * ML_W
    bias_col = jnp.zeros((1, LANES), F32).at[0, SMALL_IG:SMALL_IG + 2 * ML_H].set(b_gates)
    bias_row = jnp.zeros((SAMPLE_ROWS, 1), F32).at[:2 * ML_H, 0].set(b_gates)
    const = lambda i: (0, 0)
    resident = pl.Buffered(1)
    lagged = lambda i: jnp.maximum(i - 1, 0)
    out_shape = (
        jax.ShapeDtypeStruct((rows, ML_W), BF16),
        jax.ShapeDtypeStruct((nb, ML_H, ML_DH, ML_DH), F32),
        jax.ShapeDtypeStruct((nb, ML_H, ML_DH), F32),
        jax.ShapeDtypeStruct((nb, 8, LANES), F32),
    )
    a, c_out, n_out, m_out = pl.pallas_call(
        functools.partial(_ml_fused_kernel, c=c, nrb=nrb),
        out_shape=out_shape,
        grid=(n_blocks + 1,),
        in_specs=[
            pl.BlockSpec((tm, D_MODEL), lambda i: (jnp.minimum(i, n_blocks - 1), 0)),
            pl.BlockSpec((1, D_MODEL), const),
            pl.BlockSpec((n_ml, D_MODEL), const, pipeline_mode=resident),
            pl.BlockSpec((D_MODEL, LANES), const, pipeline_mode=resident),
            pl.BlockSpec((LANES, D_MODEL), const, pipeline_mode=resident),
            pl.BlockSpec((1, LANES), const),
            pl.BlockSpec((SAMPLE_ROWS, 1), const),
            pl.BlockSpec((1, ML_W), const),
        ],
        out_specs=(
            pl.BlockSpec((tm, ML_W), lambda i: (lagged(i), 0)),
            pl.BlockSpec((1, ML_H, ML_DH, ML_DH), lambda i: (lagged(i) // nrb, 0, 0, 0)),
            pl.BlockSpec((1, ML_H, ML_DH), lambda i: (lagged(i) // nrb, 0, 0)),
            pl.BlockSpec((1, 8, LANES), lambda i: (lagged(i) // nrb, 0, 0)),
        ),
        scratch_shapes=[pltpu.VMEM((tm, D_MODEL), BF16),
                        pltpu.VMEM((tm, 3 * ML_W), BF16), pltpu.VMEM((tm, 3 * ML_W), BF16),
                        pltpu.VMEM((tm, ML_W), F32), pltpu.VMEM((tm, ML_W), F32),
                        pltpu.VMEM((tm, LANES), F32), pltpu.VMEM((tm, LANES), F32),
                        pltpu.VMEM((SAMPLE_ROWS, tm), F32), pltpu.VMEM((SAMPLE_ROWS, tm), F32),
                        pltpu.VMEM((ML_H, ML_DH, ML_DH), F32), pltpu.VMEM((ML_H, ML_DH), F32),
                        pltpu.VMEM((8, LANES), F32)],
        compiler_params=pltpu.CompilerParams(dimension_semantics=("arbitrary",), vmem_limit_bytes=VMEM_LIMIT_FUSED),
        name="ml_fused",
    )(x2d, g_pre.reshape(1, D_MODEL), w_all, w_small, w_small_t, bias_col, bias_row, g_head.reshape(1, ML_W))
    return a, c_out, n_out, m_out[:, :ML_H, 0]


def _mlstm(qkv, oz, small, small_t, b_gates, g_head, c0, n0, m0, nb, c, n_pad):
    rows = qkv.shape[0]
    nc = rows // (nb * c)
    bias_col = jnp.zeros((1, LANES), F32).at[0, SMALL_IG:SMALL_IG + 2 * ML_H].set(b_gates)
    bias_row = jnp.zeros((SAMPLE_ROWS, 1), F32).at[:2 * ML_H, 0].set(b_gates)
    m0b = jnp.zeros((nb, 8, LANES), F32).at[:, :ML_H, :].set(jnp.broadcast_to(m0[:, :, None], (nb, ML_H, LANES)))
    rowblk = lambda b, i: (b * nc + i, 0)
    const = lambda b, i: (0, 0)
    gates_t = small_t[SMALL_IG:SMALL_IG + SAMPLE_ROWS].reshape(SAMPLE_ROWS, rows // c, c).transpose(1, 0, 2)
    out_shape = (
        jax.ShapeDtypeStruct((rows, ML_W), BF16),
        jax.ShapeDtypeStruct((nb, ML_H, ML_DH, ML_DH), F32),
        jax.ShapeDtypeStruct((nb, ML_H, ML_DH), F32),
        jax.ShapeDtypeStruct((nb, 8, LANES), F32),
    )
    st4 = lambda b, i: (b, 0, 0, 0)
    st3 = lambda b, i: (b, 0, 0)
    a, c_out, n_out, m_out = pl.pallas_call(
        functools.partial(_mlstm_kernel, c=c, n_pad=n_pad),
        out_shape=out_shape,
        grid=(nb, nc),
        in_specs=[
            pl.BlockSpec((c, 3 * ML_W), rowblk),
            pl.BlockSpec((c, 2 * ML_W), rowblk),
            pl.BlockSpec((c, LANES), rowblk),
            pl.BlockSpec((1, SAMPLE_ROWS, c), lambda b, i: (b * nc + i, 0, 0)),
            pl.BlockSpec((1, LANES), const),
            pl.BlockSpec((SAMPLE_ROWS, 1), const),
            pl.BlockSpec((1, ML_W), const),
            pl.BlockSpec((1, ML_H, ML_DH, ML_DH), st4),
            pl.BlockSpec((1, ML_H, ML_DH), st3),
            pl.BlockSpec((1, 8, LANES), st3),
        ],
        out_specs=(
            pl.BlockSpec((c, ML_W), rowblk),
            pl.BlockSpec((1, ML_H, ML_DH, ML_DH), st4),
            pl.BlockSpec((1, ML_H, ML_DH), st3),
            pl.BlockSpec((1, 8, LANES), st3),
        ),
        scratch_shapes=[pltpu.VMEM((ML_H, ML_DH, ML_DH), F32), pltpu.VMEM((ML_H, ML_DH), F32),
                        pltpu.VMEM((8, LANES), F32)],
        compiler_params=_cparams(("arbitrary", "arbitrary")),
        name="mlstm",
    )(qkv, oz, small, gates_t, bias_col, bias_row, g_head.reshape(1, ML_W), c0, n0, m0b)
    return a, c_out, n_out, m_out[:, :ML_H, 0]


_REDUCERS = {"sum": (jnp.sum, jnp.add), "max": (jnp.max, jnp.maximum), "min": (jnp.min, jnp.minimum)}
REDUCE_CHAINS = 8


def _reduce(x, axis, op):
    fn, combine = _REDUCERS[op]
    unit = 8 if axis == 0 else LANES
    n = x.shape[axis]
    units = n // unit
    if n % unit or units < 2 * REDUCE_CHAINS:
        return fn(x, axis=axis, keepdims=True)
    base, rem = divmod(units, REDUCE_CHAINS)
    parts, start = [], 0
    for i in range(REDUCE_CHAINS):
        size = (base + (1 if i < rem else 0)) * unit
        piece = x[start:start + size] if axis == 0 else x[:, start:start + size]
        parts.append(fn(piece, axis=axis, keepdims=True))
        start += size
    while len(parts) > 1:
        parts = [combine(parts[i], parts[i + 1]) for i in range(0, len(parts), 2)]
    return parts[0]


def _count(pred, axis):
    return _reduce(jnp.where(pred, 1.0, 0.0), axis, "sum")


def _kth_largest(x_ref, k, axis, n_bisect):
    kf = float(k)
    x = x_ref[...]
    hi = _reduce(x, axis, "max")
    lo = _reduce(jnp.where(x == NEG_INF, POS_INF, x), axis, "min")

    def bisect(_, carry):
        lo, hi = carry
        mid = 0.5 * (lo + hi)
        ge = _count(x_ref[...] >= mid, axis) >= kf
        return jnp.where(ge, mid, lo), jnp.where(ge, hi, mid)

    lo, hi = lax.fori_loop(0, n_bisect, bisect, (lo, hi))

    def finished(cmin, xx):
        return jnp.where((_count(xx > cmin, axis) < kf) | (cmin == POS_INF), 1.0, 0.0)

    xx = x_ref[...]
    thr = _reduce(jnp.where(xx >= lo, xx, POS_INF), axis, "min")
    done = finished(thr, xx)

    def cond(st):
        return st[2] < 0.5

    def body(st):
        thr, done, _ = st
        xx = x_ref[...]
        cmin = _reduce(jnp.where(xx > thr, xx, POS_INF), axis, "min")
        thr = jnp.where(done < 0.5, cmin, thr)
        done = jnp.maximum(done, finished(thr, xx))
        return thr, done, jnp.min(done)

    thr, _, _ = lax.while_loop(cond, body, (thr, done, jnp.min(done)))
    return thr


def _dsa_kernel(q_ref, z_ref, idxq_ref, wt_ref, k_ref, v_ref, kidx_ref, o_ref,
                kb_s, vt_s, kib_s, x_s, sel_s, *, n_keys, qb, n_sel, n_bisect, key_step):
    j = pl.program_id(1)

    @pl.when(j == 0)
    def _():
        for h in range(SA_H):
            lanes = slice(h * SA_DH, (h + 1) * SA_DH)
            kb_s[:, lanes] = k_ref[pl.ds(h, n_keys, stride=SA_H), :].astype(BF16)
            vt_s[lanes, :] = v_ref[pl.ds(h, n_keys, stride=SA_H), :].T.astype(BF16)
        kib_s[...] = kidx_ref[...].astype(BF16)

    def attend(nk):
        xs = x_s.at[0:nk]
        ss = sel_s.at[0:nk]
        key = lax.broadcasted_iota(jnp.int32, (nk, qb), 0)
        qpos = j * qb + lax.broadcasted_iota(jnp.int32, (nk, qb), 1)
        valid = key <= qpos
        qcat = jnp.concatenate([idxq_ref[:, h * IDX_D:(h + 1) * IDX_D] for h in range(IDX_H)], axis=0)
        w_rows = [wt_ref[h:h + 1, :] * IDX_SCALE for h in range(IDX_H)]
        kc = min(IDX_KEY_CHUNK, nk)
        for c0 in range(0, nk, kc):
            d = lax.dot_general(kib_s[c0:c0 + kc, :], qcat, _NT, preferred_element_type=F32)
            sc = jnp.zeros((kc, qb), F32)
            for h in range(IDX_H):
                sc = sc + jnp.maximum(d[:, h * qb:(h + 1) * qb], 0.0) * w_rows[h]
            ok = (c0 + lax.broadcasted_iota(jnp.int32, (kc, qb), 0)) <= (
                j * qb + lax.broadcasted_iota(jnp.int32, (kc, qb), 1))
            x_s[c0:c0 + kc, :] = jnp.where(ok, sc, NEG_INF)
            sel_s[c0:c0 + kc, :] = jnp.where(ok, 1.0, 0.0)

        @pl.when((j + 1) * qb > n_sel)
        def _():
            kf = float(n_sel)
            thr = _kth_largest(xs, n_sel, 0, n_bisect)
            x = xs[...]
            need = kf - _count(x > thr, 0)
            n_tie = _count(x == thr, 0)
            qrow = j * qb + lax.broadcasted_iota(jnp.int32, (1, qb), 1)
            small = (qrow + 1) <= n_sel
            ss[...] = jnp.where(small, jnp.where(valid, 1.0, 0.0), jnp.where(x >= thr, 1.0, 0.0))
            excess = jnp.max(jnp.where((n_tie > need) & jnp.logical_not(small), 1.0, 0.0))

            @pl.when(excess > 0.5)
            def _():
                tb = min(256, nk)
                r_i = lax.broadcasted_iota(jnp.int32, (tb, tb), 0)
                c_i = lax.broadcasted_iota(jnp.int32, (tb, tb), 1)
                lower = jnp.where(c_i < r_i, 1.0, 0.0).astype(BF16)
                carry = jnp.zeros((1, qb), F32)
                for blk in range(nk // tb):
                    rows = slice(blk * tb, (blk + 1) * tb)
                    xb = x_s[rows, :]
                    tie = jnp.where(xb == thr, 1.0, 0.0)
                    rank = jnp.dot(lower, tie.astype(BF16), preferred_element_type=F32) + carry
                    keep = (xb > thr) | ((xb == thr) & (rank < need))
                    keyb = blk * tb + lax.broadcasted_iota(jnp.int32, (tb, qb), 0)
                    qposb = j * qb + lax.broadcasted_iota(jnp.int32, (tb, qb), 1)
                    smallb = jnp.where(keyb <= qposb, 1.0, 0.0)
                    sel_s[rows, :] = jnp.where(small, smallb, jnp.where(keep, 1.0, 0.0))
                    carry = carry + jnp.sum(tie, axis=0, keepdims=True)

        sel = ss[...] > 0.5
        heads = range(SA_H)
        hsl = [slice(h * SA_DH, (h + 1) * SA_DH) for h in heads]
        st = [jnp.where(sel, lax.dot_general(kb_s[0:nk, hsl[h]], q_ref[:, hsl[h]], _NT,
                                             preferred_element_type=F32), NEG_INF) for h in heads]
        mx = [_reduce(st[h], 0, "max") for h in heads]
        p = [jnp.exp2(st[h] - mx[h]) for h in heads]
        l = [_reduce(p[h], 0, "sum") for h in heads]
        ot = [jnp.dot(vt_s[hsl[h], 0:nk], p[h].astype(BF16), preferred_element_type=F32) / l[h]
              for h in heads]
        for h in heads:
            o_ref[:, hsl[h]] = (ot[h].T * _silu(z_ref[:, hsl[h]])).astype(BF16)

    n_ext = n_keys // key_step
    for e in range(n_ext):
        nk = (e + 1) * key_step
        lo_j = e * key_step // qb
        hi_j = nk // qb

        @pl.when((j >= lo_j) & (j < hi_j))
        def _(nk=nk):
            attend(nk)


def _dsa_prompt(saq, saz, idxq, small_t, k, v, kidx, nb, seq):
    rows = saq.shape[0]
    qb = min(seq, 128)
    nq = seq // qb
    n_sel = min(TOPK_MAX, seq // 4)
    qblk = lambda b, j: (b * nq + j, 0)
    per_b = lambda b, j: (b, 0)
    wt_blk = SMALL_W // 8
    return pl.pallas_call(
        functools.partial(_dsa_kernel, n_keys=seq, qb=qb, n_sel=n_sel, n_bisect=20, key_step=min(256, seq)),
        out_shape=jax.ShapeDtypeStruct((rows, SA_W), BF16),
        grid=(nb, nq),
        in_specs=[
            pl.BlockSpec((qb, SA_W), qblk),
            pl.BlockSpec((qb, SA_W), qblk),
            pl.BlockSpec((qb, IDX_H * IDX_D), qblk),
            pl.BlockSpec((8, qb), lambda b, j: (wt_blk, b * nq + j)),
            pl.BlockSpec((seq * SA_H, SA_DH), per_b),
            pl.BlockSpec((seq * SA_H, SA_DH), per_b),
            pl.BlockSpec((seq, IDX_D), per_b),
        ],
        out_specs=pl.BlockSpec((qb, SA_W), qblk),
        scratch_shapes=[pltpu.VMEM((seq, SA_W), BF16), pltpu.VMEM((SA_W, seq), BF16),
                        pltpu.VMEM((seq, IDX_D), BF16), pltpu.VMEM((seq, qb), F32),
                        pltpu.VMEM((seq, qb), F32)],
        compiler_params=_cparams(("arbitrary", "arbitrary")),
        name="dsa",
    )(saq, saz, idxq, small_t, k, v, kidx)


def _sel_kernel(pt_ref, idxq_ref, small_ref, kinew_ref, *rest, npg, n_past, n_sel, n_real, n_bisect):
    page_refs = rest[:npg]
    sel_ref, x_s = rest[npg:]
    g = pl.program_id(1)
    rws = SAMPLE_ROWS
    top = rws - SEL_ROWS
    pk = n_past + LANES
    gk = npg * PAGE_SIZE

    qs = jnp.concatenate([idxq_ref[:, h * IDX_D:(h + 1) * IDX_D] for h in range(IDX_H)], axis=0)
    small = small_ref[...]

    def scores(d):
        sc = jnp.zeros((SEL_ROWS, d.shape[1]), F32)
        for h in range(IDX_H):
            w = small[top:, SMALL_W + h:SMALL_W + h + 1] * IDX_SCALE
            sc = sc + jnp.maximum(d[h * rws + top:(h + 1) * rws, :], 0.0) * w
        return sc

    kp_t = jnp.concatenate([r[0] for r in page_refs], axis=1).astype(BF16)
    x_s[:, pl.ds(pl.multiple_of(g * gk, LANES), gk)] = scores(jnp.dot(qs, kp_t, preferred_element_type=F32))

    @pl.when(g == pl.num_programs(1) - 1)
    def _():
        knew = jnp.concatenate([kinew_ref[...], jnp.zeros((LANES - rws, IDX_D), F32)], axis=0).astype(BF16)
        row = top + lax.broadcasted_iota(jnp.int32, (SEL_ROWS, LANES), 0)
        col = lax.broadcasted_iota(jnp.int32, (SEL_ROWS, LANES), 1)
        ok = (col >= rws - n_real) & (col < rws) & (col <= row)
        d_new = lax.dot_general(qs, knew, _NT, preferred_element_type=F32)
        x_s[:, n_past:pk] = jnp.where(ok, scores(d_new), NEG_INF)

        kf = float(n_sel)
        thr = _kth_largest(x_s, n_sel, 1, n_bisect)
        x = x_s[...]
        need = kf - _count(x > thr, 1)
        n_tie = _count(x == thr, 1)
        sel_ref[0, 0:top, :] = jnp.ones((top, pk), F32)
        sel_ref[0, top:rws, :] = jnp.where(x >= thr, 1.0, 0.0)
        real = lax.broadcasted_iota(jnp.int32, (SEL_ROWS, 1), 0) >= SEL_ROWS - n_real
        excess = jnp.max(jnp.where((n_tie > need) & real, 1.0, 0.0))

        @pl.when(excess > 0.5)
        def _():
            r_i = lax.broadcasted_iota(jnp.int32, (LANES, LANES), 0)
            c_i = lax.broadcasted_iota(jnp.int32, (LANES, LANES), 1)
            upper = jnp.where(r_i < c_i, 1.0, 0.0).astype(BF16)

            def blk(i, carry):
                cols = pl.ds(pl.multiple_of(i * LANES, LANES), LANES)
                xb = x_s[:, cols]
                tie = jnp.where(xb == thr, 1.0, 0.0)
                rank = jnp.dot(tie.astype(BF16), upper, preferred_element_type=F32) + carry
                keep = (xb > thr) | ((xb == thr) & (rank < need))
                sel_ref[0, top:rws, cols] = jnp.where(keep, 1.0, 0.0)
                return carry + jnp.sum(tie, axis=1, keepdims=True)

            lax.fori_loop(0, pk // LANES, blk, jnp.zeros((SEL_ROWS, 1), F32))


def _dsa_select(page_table, idxq, small, kidx_new, cache_kidx, n_sel, n_real):
    nreq, n_pages = page_table.shape
    n_past = n_pages * PAGE_SIZE
    pk = n_past + LANES
    npg = min(SEL_PAGES_PER_STEP, n_pages)
    req = lambda b, g, pt: (b, 0)

    def page_map(i):
        return lambda b, g, pt: (pt[b, g * npg + i], 0, 0)

    grid_spec = pltpu.PrefetchScalarGridSpec(
        num_scalar_prefetch=1,
        grid=(nreq, n_pages // npg),
        in_specs=[pl.BlockSpec((SAMPLE_ROWS, IDX_H * IDX_D), req),
                  pl.BlockSpec((SAMPLE_ROWS, LANES), req),
                  pl.BlockSpec((SAMPLE_ROWS, IDX_D), req)]
                 + [pl.BlockSpec((1, IDX_D, PAGE_SIZE), page_map(i)) for i in range(npg)],
        out_specs=pl.BlockSpec((1, SAMPLE_ROWS, pk), lambda b, g, pt: (b, 0, 0)),
        scratch_shapes=[pltpu.VMEM((SEL_ROWS, pk), F32)],
    )
    assert n_real <= SEL_ROWS
    return pl.pallas_call(
        functools.partial(_sel_kernel, npg=npg, n_past=n_past, n_sel=n_sel, n_real=n_real, n_bisect=20),
        out_shape=jax.ShapeDtypeStruct((nreq, SAMPLE_ROWS, pk), F32),
        grid_spec=grid_spec,
        compiler_params=_cparams(("arbitrary", "arbitrary")),
        name="dsa_sel",
    )(page_table, idxq, small, kidx_new, *([cache_kidx] * npg))


def _att_kernel(pt_ref, q_ref, z_ref, knew_ref, vnew_ref, sel_ref, seltail_ref, *rest, npg):
    k_refs = rest[:npg]
    v_refs = rest[npg:2 * npg]
    o_ref = rest[2 * npg]
    m_s, l_s, acc_s = rest[2 * npg + 1:]
    g = pl.program_id(1)
    rws = SAMPLE_ROWS
    floor = -1e30

    @pl.when(g == 0)
    def _():
        m_s[...] = jnp.full(m_s.shape, floor, F32)
        l_s[...] = jnp.zeros(l_s.shape, F32)
        acc_s[...] = jnp.zeros(acc_s.shape, F32)

    heads = range(SA_H)
    hsl = [slice(h * SA_DH, (h + 1) * SA_DH) for h in heads]

    def update(kbs, vbs, keep):
        m_old = [m_s[h][:, 0:1] for h in heads]
        l_old = [l_s[h][:, 0:1] for h in heads]
        acc_old = [acc_s[:, hsl[h]] for h in heads]
        s = [lax.dot_general(q_ref[:, hsl[h]], kbs[h], _NT, preferred_element_type=F32) for h in heads]
        m_new = [jnp.maximum(m_old[h], jnp.max(jnp.where(keep, s[h], floor), axis=1, keepdims=True)) for h in heads]
        p = [jnp.where(keep, jnp.exp2(s[h] - m_new[h]), 0.0) for h in heads]
        pv = [jnp.dot(p[h].astype(BF16), vbs[h], preferred_element_type=F32) for h in heads]
        alpha = [jnp.exp2(m_old[h] - m_new[h]) for h in heads]
        l_new = [alpha[h] * l_old[h] + jnp.sum(p[h], axis=1, keepdims=True) for h in heads]
        for h in heads:
            acc_s[:, hsl[h]] = alpha[h] * acc_old[h] + pv[h]
            l_s[h] = jnp.broadcast_to(l_new[h], (rws, LANES))
            m_s[h] = jnp.broadcast_to(m_new[h], (rws, LANES))

    def head_rows(refs, h):
        return jnp.concatenate([r[0, pl.ds(h, PAGE_SIZE, stride=SA_H), :] for r in refs], axis=0).astype(BF16)

    update([head_rows(k_refs, h) for h in heads], [head_rows(v_refs, h) for h in heads], sel_ref[0] > 0.5)

    @pl.when(g == pl.num_programs(1) - 1)
    def _():
        update([knew_ref[pl.ds(h, rws, stride=SA_H), :].astype(BF16) for h in heads],
               [vnew_ref[pl.ds(h, rws, stride=SA_H), :].astype(BF16) for h in heads],
               seltail_ref[0][:, :rws] > 0.5)
        for h in heads:
            o_ref[:, hsl[h]] = (acc_s[:, hsl[h]] / l_s[h][:, 0:1] * _silu(z_ref[:, hsl[h]])).astype(BF16)


def _dsa_attend(page_table, saq, saz, k_new, v_new, sel, cache_k, cache_v):
    nreq, n_pages = page_table.shape
    n_past = n_pages * PAGE_SIZE
    npg = min(ATT_PAGES_PER_STEP, n_pages)
    prow = PAGE_SIZE * SA_H
    req = lambda b, g, pt: (b, 0)

    def page_map(i):
        return lambda b, g, pt: (pt[b, g * npg + i], 0, 0)

    page_specs = [pl.BlockSpec((1, prow, SA_DH), page_map(i)) for i in range(npg)]
    grid_spec = pltpu.PrefetchScalarGridSpec(
        num_scalar_prefetch=1,
        grid=(nreq, n_pages // npg),
        in_specs=[pl.BlockSpec((SAMPLE_ROWS, SA_W), req), pl.BlockSpec((SAMPLE_ROWS, SA_W), req),
                  pl.BlockSpec((SAMPLE_ROWS * SA_H, SA_DH), req), pl.BlockSpec((SAMPLE_ROWS * SA_H, SA_DH), req),
                  pl.BlockSpec((1, SAMPLE_ROWS, npg * PAGE_SIZE), lambda b, g, pt: (b, 0, g)),
                  pl.BlockSpec((1, SAMPLE_ROWS, LANES), lambda b, g, pt: (b, 0, n_past // LANES))]
                 + page_specs + page_specs,
        out_specs=pl.BlockSpec((SAMPLE_ROWS, SA_W), req),
        scratch_shapes=[pltpu.VMEM((SA_H, SAMPLE_ROWS, LANES), F32), pltpu.VMEM((SA_H, SAMPLE_ROWS, LANES), F32),
                        pltpu.VMEM((SAMPLE_ROWS, SA_W), F32)],
    )
    return pl.pallas_call(
        functools.partial(_att_kernel, npg=npg),
        out_shape=jax.ShapeDtypeStruct((nreq * SAMPLE_ROWS, SA_W), BF16),
        grid_spec=grid_spec,
        compiler_params=_cparams(("arbitrary", "arbitrary")),
        name="dsa_att",
    )(page_table, saq, saz, k_new, v_new, sel, sel, *([cache_k] * npg), *([cache_v] * npg))


def _mem_attend(q_ref, z_ref, mk_ref, mv_ref):
    mk = mk_ref[...].astype(BF16)
    mv = mv_ref[...].astype(BF16)
    outs = []
    for h in range(MEM_H):
        hs = slice(h * MEM_DH, (h + 1) * MEM_DH)
        s = lax.dot_general(q_ref[:, hs], mk[:, hs], _NT, preferred_element_type=F32)
        p = jnp.exp(s - jnp.max(s, axis=1, keepdims=True))
        l = jnp.sum(p, axis=1, keepdims=True)
        o = jnp.dot(p.astype(BF16), mv[:, hs], preferred_element_type=F32) / l
        outs.append((o * _silu(z_ref[:, hs])).astype(BF16))
    return jnp.concatenate(outs, axis=1)


def _memattn_kernel(q_ref, z_ref, mk_ref, mv_ref, o_ref):
    o_ref[...] = _mem_attend(q_ref, z_ref, mk_ref, mv_ref)


def _memattn(memq, memz, mk, mv, nb, tq):
    rows = memq.shape[0]
    n_mem = mk.shape[0] // nb
    nq = rows // (nb * tq)
    qblk = lambda b, i: (b * nq + i, 0)
    per_b = lambda b, i: (b, 0)
    return pl.pallas_call(
        _memattn_kernel,
        out_shape=jax.ShapeDtypeStruct((rows, MEM_W), BF16),
        grid=(nb, nq),
        in_specs=[pl.BlockSpec((tq, MEM_W), qblk), pl.BlockSpec((tq, MEM_W), qblk),
                  pl.BlockSpec((n_mem, MEM_W), per_b), pl.BlockSpec((n_mem, MEM_W), per_b)],
        out_specs=pl.BlockSpec((tq, MEM_W), qblk),
        compiler_params=_cparams(("arbitrary", "arbitrary")),
        name="memattn",
    )(memq, memz, mk, mv)


def _mixout_kernel(x_ref, a_ref, b_ref, c_ref, wa_ref, wb_ref, wc_ref, g_ref, y_ref):
    acc = (jnp.dot(a_ref[...], wa_ref[...], preferred_element_type=F32)
           + jnp.dot(b_ref[...], wb_ref[...], preferred_element_type=F32)
           + jnp.dot(c_ref[...], wc_ref[...], preferred_element_type=F32))
    y = acc * lax.rsqrt(jnp.mean(acc * acc, axis=-1, keepdims=True) + EPS) * g_ref[...]
    y_ref[...] = x_ref[...] + y


def _mixout_mem_kernel(x_ref, a_ref, b_ref, q_ref, z_ref, mk_ref, mv_ref, wa_ref, wb_ref, wc_ref, g_ref, y_ref):
    c = _mem_attend(q_ref, z_ref, mk_ref, mv_ref)
    acc = (jnp.dot(a_ref[...], wa_ref[...], preferred_element_type=F32)
           + jnp.dot(b_ref[...], wb_ref[...], preferred_element_type=F32)
           + jnp.dot(c, wc_ref[...], preferred_element_type=F32))
    y = acc * lax.rsqrt(jnp.mean(acc * acc, axis=-1, keepdims=True) + EPS) * g_ref[...]
    y_ref[...] = x_ref[...] + y


def _mixout_mem(x2d, a, b, memq, memz, mk, mv, w_out, g_post, tm, seq):
    rows = x2d.shape[0]
    n_mem = mk.shape[0] // (rows // seq)
    wb16 = w_out.astype(BF16)
    row = lambda i: (i, 0)
    const = lambda i: (0, 0)
    per_req = lambda i: (i // (seq // tm), 0)
    return pl.pallas_call(
        _mixout_mem_kernel,
        out_shape=jax.ShapeDtypeStruct((rows, D_MODEL), F32),
        grid=(rows // tm,),
        in_specs=[pl.BlockSpec((tm, D_MODEL), row), pl.BlockSpec((tm, ML_W), row),
                  pl.BlockSpec((tm, SA_W), row), pl.BlockSpec((tm, MEM_W), row), pl.BlockSpec((tm, MEM_W), row),
                  pl.BlockSpec((n_mem, MEM_W), per_req), pl.BlockSpec((n_mem, MEM_W), per_req),
                  pl.BlockSpec((ML_W, D_MODEL), const), pl.BlockSpec((SA_W, D_MODEL), const),
                  pl.BlockSpec((MEM_W, D_MODEL), const), pl.BlockSpec((1, D_MODEL), const)],
        out_specs=pl.BlockSpec((tm, D_MODEL), row),
        compiler_params=_cparams(("arbitrary",)),
        name="mixout_mem",
    )(x2d, a, b, memq, memz, mk, mv, wb16[:ML_W], wb16[ML_W:ML_W + SA_W], wb16[ML_W + SA_W:],
      g_post.reshape(1, D_MODEL))


def _mixout(x2d, a, b, c, w_out, g_post, tm):
    rows = x2d.shape[0]
    wb16 = w_out.astype(BF16)
    row = lambda i: (i, 0)
    const = lambda i: (0, 0)
    return pl.pallas_call(
        _mixout_kernel,
        out_shape=jax.ShapeDtypeStruct((rows, D_MODEL), F32),
        grid=(rows // tm,),
        in_specs=[pl.BlockSpec((tm, D_MODEL), row), pl.BlockSpec((tm, ML_W), row),
                  pl.BlockSpec((tm, SA_W), row), pl.BlockSpec((tm, MEM_W), row),
                  pl.BlockSpec((ML_W, D_MODEL), const), pl.BlockSpec((SA_W, D_MODEL), const),
                  pl.BlockSpec((MEM_W, D_MODEL), const), pl.BlockSpec((1, D_MODEL), const)],
        out_specs=pl.BlockSpec((tm, D_MODEL), row),
        compiler_params=_cparams(("arbitrary",)),
        name="mixout",
    )(x2d, a, b, c, wb16[:ML_W], wb16[ML_W:ML_W + SA_W], wb16[ML_W + SA_W:], g_post.reshape(1, D_MODEL))


def _layer(x_p, x_s, st_c, st_n, st_m, c_k, c_v, c_kidx, c_mk, c_mv, page_table, mem_prompt,
           g_pre, w_in, b_gates, g_head, w_mem_k, w_mem_v, g_mem, w_out, g_post):
    nb, seq, _ = x_p.shape
    nreq, t_dec, _ = x_s.shape
    n_mem = mem_prompt.shape[1]
    n_past = page_table.shape[1] * PAGE_SIZE
    weights = _relayout_w_in(w_in)

    tm = min(512, seq)
    tabs = _rope_tables(jnp.arange(seq, dtype=jnp.int32))
    x_p2d = x_p.reshape(nb * seq, D_MODEL)
    (_, _, saq, k, v, saz, memq, memz, idxq, kidx, small, small_t) = _project(
        x_p2d, g_pre, weights, tabs, tm, with_ml=False)
    a_p, p_c, p_n, p_m = _ml_fused(x_p2d, g_pre, weights[0], weights[3], weights[4], b_gates, g_head,
                                   nb, seq, tm, min(ML_CHUNK, seq))
    b_p = _dsa_prompt(saq, saz, idxq, small_t, k, v, kidx, nb, seq)
    mk, mv = _memkv(mem_prompt.reshape(nb * n_mem, D_MODEL), g_mem, w_mem_k, w_mem_v, n_mem)
    y_p = _mixout_mem(x_p.reshape(nb * seq, D_MODEL), a_p, b_p, memq, memz, mk, mv, w_out, g_post,
                      tm, seq).reshape(nb, seq, D_MODEL)

    rws = SAMPLE_ROWS
    n_padrow = rws - t_dec
    xs_pad = jnp.concatenate([jnp.zeros((nreq, n_padrow, D_MODEL), F32), x_s], axis=1).reshape(nreq * rws, D_MODEL)
    pos_s = jnp.tile(jnp.concatenate([jnp.zeros((n_padrow,), jnp.int32),
                                      n_past + jnp.arange(t_dec, dtype=jnp.int32)]), nreq)
    tabs_s = _rope_tables(pos_s)
    (qkv_s, oz_s, saq_s, k_s, v_s, saz_s, memq_s, memz_s, idxq_s, kidx_s, small_s, small_t_s) = _project(
        xs_pad, g_pre, weights, tabs_s, nreq * rws)
    a_s, s_c, s_n, s_m = _mlstm(qkv_s, oz_s, small_s, small_t_s, b_gates, g_head, st_c, st_n, st_m,
                                nreq, rws, n_padrow)
    n_sel = min(TOPK_MAX, (n_past + t_dec) // 4)
    sel = _dsa_select(page_table, idxq_s, small_s, kidx_s, jnp.swapaxes(c_kidx, 1, 2), n_sel, t_dec)
    b_s = _dsa_attend(page_table, saq_s, saz_s, k_s, v_s, sel,
                      c_k.reshape(c_k.shape[0], PAGE_SIZE * SA_H, SA_DH),
                      c_v.reshape(c_v.shape[0], PAGE_SIZE * SA_H, SA_DH))
    c_s = _memattn(memq_s, memz_s, c_mk.reshape(nreq * n_mem, MEM_W), c_mv.reshape(nreq * n_mem, MEM_W), nreq, rws)
    y_s = _mixout(xs_pad, a_s, b_s, c_s, w_out, g_post, nreq * rws)

    def real(a2d):
        return a2d.reshape(nreq, rws, -1)[:, n_padrow:]

    new = (p_c, p_n, p_m,
           k.reshape(nb, seq, SA_H, SA_DH), v.reshape(nb, seq, SA_H, SA_DH), kidx.reshape(nb, seq, IDX_D),
           mk.reshape(nb, n_mem, MEM_H, MEM_DH), mv.reshape(nb, n_mem, MEM_H, MEM_DH),
           s_c, s_n, s_m,
           real(k_s).reshape(nreq, t_dec, SA_H, SA_DH), real(v_s).reshape(nreq, t_dec, SA_H, SA_DH), real(kidx_s))
    return y_p, real(y_s), new


def kernel(x_prompt, x_sample, state_mlstm_C, state_mlstm_n, state_mlstm_m, cache_k, cache_v, cache_kidx,
           cache_mem_k, cache_mem_v, page_table, mem_prompt, g_pre, w_in, b_gates, g_head, w_mem_k, w_mem_v,
           g_mem, w_out, g_post):
    xp, xs = x_prompt, x_sample
    per_layer = []
    for l in range(w_in.shape[0]):
        xp, xs, new = _layer(xp, xs, state_mlstm_C[l], state_mlstm_n[l], state_mlstm_m[l],
                             cache_k[l], cache_v[l], cache_kidx[l], cache_mem_k[l], cache_mem_v[l],
                             page_table, mem_prompt, g_pre[l], w_in[l], b_gates[l], g_head[l],
                             w_mem_k[l], w_mem_v[l], g_mem[l], w_out[l], g_post[l])
        per_layer.append(new)
    stacked = [jnp.stack(a) for a in zip(*per_layer)]
    return (xp, xs, *stacked)
```

```python
import functools

import jax
import jax.numpy as jnp
from jax import lax
from jax.experimental import pallas as pl
from jax.experimental.pallas import tpu as pltpu

F32 = jnp.float32
BF16 = jnp.bfloat16

D_MODEL = 2048
ML_H = 4
ML_W = D_MODEL // 2
ML_DH = ML_W // ML_H
SA_H = 4
SA_W = D_MODEL // 4
SA_DH = SA_W // SA_H
MEM_H = 4
MEM_W = D_MODEL // 4
MEM_DH = MEM_W // MEM_H
IDX_H = 8
IDX_D = 64
IDX_SCALE = (IDX_H * IDX_D) ** -0.5
TOPK_MAX = 256
ROPE_THETA = 10000.0
LOG2E = 1.4426950408889634
EPS = 1e-6
PAGE_SIZE = 128

LANES = 128
CB = 512
SMALL_IG = 64
SMALL_LF = 68
SMALL_W = 72
SAMPLE_ROWS = 16
SEL_ROWS = 8
SEL_PAGES_PER_STEP = 32
ATT_PAGES_PER_STEP = 16
ML_CHUNK = 256
IDX_KEY_CHUNK = 256
ONES_ROWS = 16
VMEM_LIMIT = 56 * 1024 * 1024
VMEM_LIMIT_FUSED = 60 * 1024 * 1024
NEG_INF = float("-inf")
POS_INF = float("inf")

_NT = (((1,), (1,)), ((), ()))
_TN = (((0,), (0,)), ((), ()))


def _cparams(sem):
    return pltpu.CompilerParams(dimension_semantics=sem, vmem_limit_bytes=VMEM_LIMIT)


def _sigmoid(x):
    return 1.0 / (1.0 + jnp.exp(-x))


def _silu(x):
    return x * _sigmoid(x)


def _log_sigmoid(x):
    return jnp.minimum(x, 0.0) - jnp.log1p(jnp.exp(-jnp.abs(x)))


def _split3(x):
    hi = x.astype(BF16)
    r = x - hi.astype(F32)
    mid = r.astype(BF16)
    lo = (r - mid.astype(F32)).astype(BF16)
    return hi, mid, lo


def _rope128(x, cos, sin_signed):
    return x * cos + pltpu.roll(x, 64, 1) * sin_signed


def _rope64(x, cos, sin_signed):
    lane = lax.broadcasted_iota(jnp.int32, x.shape, 1)
    first_half = (lane % 64) < 32
    partner = jnp.where(first_half, pltpu.roll(x, 96, 1), pltpu.roll(x, 32, 1))
    return x * cos + partner * sin_signed


def _normed(x_ref, g_ref):
    x = x_ref[...]
    return (x * lax.rsqrt(jnp.mean(x * x, axis=-1, keepdims=True) + EPS) * g_ref[...]).astype(BF16)


def _proj_ml_kernel(x_ref, g_ref, w_ref, qkv_ref, oz_ref, u_ref):
    u_ref[...] = _normed(x_ref, g_ref)
    n_qkv = 3 * ML_W // CB
    for cb in range(5 * ML_W // CB):
        acc = lax.dot_general(u_ref[...], w_ref[cb * CB:(cb + 1) * CB, :], _NT, preferred_element_type=F32)
        if ML_W <= cb * CB < 2 * ML_W:
            acc = acc * (ML_DH ** -0.5)
        if cb < n_qkv:
            qkv_ref[:, cb * CB:(cb + 1) * CB] = acc.astype(BF16)
        else:
            oz_ref[:, (cb - n_qkv) * CB:(cb - n_qkv + 1) * CB] = acc


def _proj_rest_kernel(x_ref, g_ref, wsa_ref, wmem_ref, ws_ref, wst_ref, c128_ref, s128_ref, c64_ref, s64_ref,
                      saq_ref, k_ref, v_ref, saz_ref, memq_ref, memz_ref, idxq_ref,
                      kidx_ref, small_ref, smallt_ref, u_ref):
    u_ref[...] = _normed(x_ref, g_ref)
    sm = jnp.dot(u_ref[...], ws_ref[...], preferred_element_type=F32)
    small_ref[...] = sm
    kidx_ref[...] = _rope64(sm, c64_ref[...], s64_ref[...])[:, :IDX_D]
    smallt_ref[...] = lax.dot_general(wst_ref[...], u_ref[...], _NT, preferred_element_type=F32)

    def block(cb, w_ref=wsa_ref):
        return lax.dot_general(u_ref[...], w_ref[cb * CB:(cb + 1) * CB, :], _NT, preferred_element_type=F32)

    def rope_heads(acc, fn, cos_ref, sin_ref):
        return jnp.concatenate(
            [fn(acc[:, h * LANES:(h + 1) * LANES], cos_ref[...], sin_ref[...]) for h in range(CB // LANES)], axis=1)

    saq_ref[...] = (rope_heads(block(0), _rope128, c128_ref, s128_ref) * (SA_DH ** -0.5 * LOG2E)).astype(BF16)
    tm = x_ref.shape[0]
    k_acc = block(1)
    v_acc = block(2)
    for h in range(SA_H):
        lanes = slice(h * SA_DH, (h + 1) * SA_DH)
        k_ref[pl.ds(h, tm, stride=SA_H), :] = _rope128(k_acc[:, lanes], c128_ref[...], s128_ref[...])
        v_ref[pl.ds(h, tm, stride=SA_H), :] = v_acc[:, lanes]
    saz_ref[...] = block(3)
    idxq_ref[...] = rope_heads(block(4), _rope64, c64_ref, s64_ref).astype(BF16)
    memq_ref[...] = (block(0, wmem_ref) * (MEM_DH ** -0.5)).astype(BF16)
    memz_ref[...] = block(1, wmem_ref)


def _project(x2d, g_pre, weights, tabs, tm, with_ml=True):
    w_ml, w_sa, w_mem, w_small, w_small_t = weights
    rows = x2d.shape[0]
    c128, s128, c64, s64 = tabs
    ntab = c128.shape[0] // tm
    n_ml = 5 * ML_W
    row_only = lambda i: (i, 0)
    tab_map = lambda i: (i % ntab, 0)
    const = lambda i: (0, 0)
    resident = pl.Buffered(1)
    g2d = g_pre.reshape(1, D_MODEL)

    qkv, oz = (None, None) if not with_ml else pl.pallas_call(
        _proj_ml_kernel,
        out_shape=(jax.ShapeDtypeStruct((rows, 3 * ML_W), BF16),
                   jax.ShapeDtypeStruct((rows, 2 * ML_W), F32)),
        grid=(rows // tm,),
        in_specs=[pl.BlockSpec((tm, D_MODEL), row_only),
                  pl.BlockSpec((1, D_MODEL), const),
                  pl.BlockSpec((n_ml, D_MODEL), const, pipeline_mode=resident)],
        out_specs=(pl.BlockSpec((tm, 3 * ML_W), row_only), pl.BlockSpec((tm, 2 * ML_W), row_only)),
        scratch_shapes=[pltpu.VMEM((tm, D_MODEL), BF16)],
        compiler_params=_cparams(("arbitrary",)),
        name="proj_ml",
    )(x2d, g2d, w_ml)

    out_shape = (
        jax.ShapeDtypeStruct((rows, SA_W), BF16),
        jax.ShapeDtypeStruct((rows * SA_H, SA_DH), F32),
        jax.ShapeDtypeStruct((rows * SA_H, SA_DH), F32),
        jax.ShapeDtypeStruct((rows, SA_W), F32),
        jax.ShapeDtypeStruct((rows, MEM_W), BF16),
        jax.ShapeDtypeStruct((rows, MEM_W), F32),
        jax.ShapeDtypeStruct((rows, IDX_H * IDX_D), BF16),
        jax.ShapeDtypeStruct((rows, IDX_D), F32),
        jax.ShapeDtypeStruct((rows, LANES), F32),
        jax.ShapeDtypeStruct((LANES, rows), F32),
    )
    out_specs = (
        pl.BlockSpec((tm, CB), row_only),
        pl.BlockSpec((tm * SA_H, SA_DH), row_only),
        pl.BlockSpec((tm * SA_H, SA_DH), row_only),
        pl.BlockSpec((tm, CB), row_only),
        pl.BlockSpec((tm, CB), row_only),
        pl.BlockSpec((tm, CB), row_only),
        pl.BlockSpec((tm, CB), row_only),
        pl.BlockSpec((tm, IDX_D), row_only),
        pl.BlockSpec((tm, LANES), row_only),
        pl.BlockSpec((LANES, tm), lambda i: (0, i)),
    )
    in_specs = [
        pl.BlockSpec((tm, D_MODEL), row_only),
        pl.BlockSpec((1, D_MODEL), const),
        pl.BlockSpec(w_sa.shape, const, pipeline_mode=resident),
        pl.BlockSpec(w_mem.shape, const, pipeline_mode=resident),
        pl.BlockSpec((D_MODEL, LANES), const, pipeline_mode=resident),
        pl.BlockSpec((LANES, D_MODEL), const, pipeline_mode=resident),
        pl.BlockSpec((tm, LANES), tab_map),
        pl.BlockSpec((tm, LANES), tab_map),
        pl.BlockSpec((tm, LANES), tab_map),
        pl.BlockSpec((tm, LANES), tab_map),
    ]
    rest = pl.pallas_call(
        _proj_rest_kernel,
        out_shape=out_shape,
        grid=(rows // tm,),
        in_specs=in_specs,
        out_specs=out_specs,
        scratch_shapes=[pltpu.VMEM((tm, D_MODEL), BF16)],
        compiler_params=_cparams(("arbitrary",)),
        name="proj_rest",
    )(x2d, g2d, w_sa, w_mem, w_small, w_small_t, c128, s128, c64, s64)
    return (qkv, oz, *rest)


def _rope_tables(pos):
    def tab(half):
        inv = ROPE_THETA ** (-jnp.arange(half, dtype=F32) / half)
        ang = pos.astype(F32)[:, None] * inv[None, :]
        return jnp.cos(ang), jnp.sin(ang)

    c, s = tab(SA_DH // 2)
    c128 = jnp.concatenate([c, c], axis=1)
    s128 = jnp.concatenate([-s, s], axis=1)
    c, s = tab(IDX_D // 2)
    c64 = jnp.concatenate([c, c, c, c], axis=1)
    s64 = jnp.concatenate([-s, s, -s, s], axis=1)
    return c128, s128, c64, s64


def _relayout_w_in(w_in):
    off = {}
    o = 0
    for name, w in (('ml_q', ML_W), ('ml_k', ML_W), ('ml_v', ML_W), ('ml_o', ML_W), ('ml_z', ML_W),
                    ('ml_i', ML_H), ('ml_f', ML_H), ('sa_q', SA_W), ('sa_k', SA_W), ('sa_v', SA_W),
                    ('sa_z', SA_W), ('idx_q', IDX_H * IDX_D), ('idx_k', IDX_D), ('idx_w', IDX_H),
                    ('mem_q', MEM_W), ('mem_z', MEM_W)):
        off[name] = (o, w)
        o += w

    w_t = w_in.T

    def col(name):
        a, w = off[name]
        return w_t[a:a + w]

    def span(first, last):
        return w_t[off[first][0]:off[last][0] + off[last][1]].astype(BF16)

    w_ml = w_t.astype(BF16)
    w_sa = span('sa_q', 'idx_q')
    w_mem = span('mem_q', 'mem_z')
    small_t = jnp.concatenate([
        col('idx_k'), col('ml_i'), col('ml_f'), col('idx_w'),
        jnp.zeros((LANES - IDX_D - 2 * ML_H - IDX_H, D_MODEL), F32)], axis=0).astype(BF16)
    return (w_ml, w_sa, w_mem, small_t.T, small_t)


def _memkv_kernel(m_ref, g_ref, wk_ref, wv_ref, k_ref, v_ref):
    x = m_ref[...]
    u = (x * lax.rsqrt(jnp.mean(x * x, axis=-1, keepdims=True) + EPS) * g_ref[...]).astype(BF16)
    k_ref[...] = jnp.dot(u, wk_ref[...], preferred_element_type=F32)
    v_ref[...] = jnp.dot(u, wv_ref[...], preferred_element_type=F32)


def _memkv(mem2d, g_mem, wk, wv, n_mem):
    rows = mem2d.shape[0]
    row = lambda i: (i, 0)
    const = lambda i: (0, 0)
    return pl.pallas_call(
        _memkv_kernel,
        out_shape=(jax.ShapeDtypeStruct((rows, MEM_W), F32), jax.ShapeDtypeStruct((rows, MEM_W), F32)),
        grid=(rows // n_mem,),
        in_specs=[pl.BlockSpec((n_mem, D_MODEL), row), pl.BlockSpec((1, D_MODEL), const),
                  pl.BlockSpec((D_MODEL, MEM_W), const), pl.BlockSpec((D_MODEL, MEM_W), const)],
        out_specs=(pl.BlockSpec((n_mem, MEM_W), row), pl.BlockSpec((n_mem, MEM_W), row)),
        compiler_params=_cparams(("arbitrary",)),
        name="memkv",
    )(mem2d, g_mem.reshape(1, D_MODEL), wk.astype(BF16), wv.astype(BF16))


def _mlstm_chunk(q_of, k_of, v_of, gate_of, g_c, g_r, gh_ref, c_s, n_s, m_s, a_store, c, n_pad,
                 between_heads=None):
    ri = lax.broadcasted_iota(jnp.int32, (c, c), 0)
    cj = lax.broadcasted_iota(jnp.int32, (c, c), 1)
    causal = cj <= ri
    tri = jnp.where(causal, 1.0, 0.0).astype(BF16)
    tri_t = jnp.where(ri <= cj, 1.0, 0.0).astype(BF16)

    pad_c = lax.broadcasted_iota(jnp.int32, (c, LANES), 0) < n_pad
    ig_c = jnp.where(pad_c, NEG_INF, g_c)
    lf_c = jnp.where(pad_c, 0.0, _log_sigmoid(g_c))
    b_c = sum(jnp.dot(tri, p, preferred_element_type=F32) for p in _split3(lf_c))
    pad_r = lax.broadcasted_iota(jnp.int32, (SAMPLE_ROWS, c), 1) < n_pad
    ig_r = jnp.where(pad_r, NEG_INF, g_r)
    lf_r = jnp.where(pad_r, 0.0, _log_sigmoid(g_r))
    b_r = sum(jnp.dot(p, tri_t, preferred_element_type=F32) for p in _split3(lf_r))

    m_all = m_s[...]
    n_all = n_s[...]
    c_all = [c_s[h] for h in range(ML_H)]
    new_state = []
    for h in range(ML_H):
        hs = slice(h * ML_DH, (h + 1) * ML_DH)
        m_prev = m_all[h:h + 1, 0:1]
        b_t = b_c[:, SMALL_LF + h:SMALL_LF + h + 1]
        igc = ig_c[:, SMALL_IG + h:SMALL_IG + h + 1]
        b_s = b_r[ML_H + h:ML_H + h + 1, :]
        igr = ig_r[h:h + 1, :]
        a = jnp.where(causal, b_t - b_s + igr, NEG_INF)
        bm = b_t + m_prev
        m_t = jnp.maximum(bm, jnp.max(a, axis=1, keepdims=True))
        inter = jnp.exp(bm - m_t)
        dmat = jnp.exp(a - m_t)
        q, k, v = q_of(h), k_of(h), v_of(h)
        s = lax.dot_general(q, k, _NT, preferred_element_type=F32) * dmat
        if between_heads is not None:
            between_heads()
        c_h = c_all[h]
        n_h = n_all[h:h + 1, :]
        num = (jnp.dot(s.astype(BF16), v, preferred_element_type=F32)
               + inter * jnp.dot(q, c_h.astype(BF16), preferred_element_type=F32))
        qn = (jnp.sum(s, axis=1, keepdims=True)
              + inter * jnp.sum(q.astype(F32) * n_h, axis=1, keepdims=True))
        hh = num / jnp.maximum(jnp.abs(qn), jnp.exp(-m_t))
        hh = hh * lax.rsqrt(jnp.mean(hh * hh, axis=1, keepdims=True) + EPS)
        if between_heads is not None:
            between_heads()
        a_store(h, (hh * gh_ref[:, hs] * gate_of(h)).astype(BF16))

        m_new = m_t[c - 1:c, :]
        b_last = b_t[c - 1:c, :]
        w_end = jnp.exp(b_last - b_t + igc - m_new)
        decay = jnp.exp(b_last + m_prev - m_new)
        kw = k.astype(F32) * w_end
        new_state.append((decay * c_h + lax.dot_general(kw.astype(BF16), v, _TN, preferred_element_type=F32),
                          decay * n_h + jnp.sum(kw, axis=0, keepdims=True),
                          jnp.broadcast_to(m_new, (1, LANES))))
        if between_heads is not None:
            between_heads()

    for h, (c_new, n_new, m_new) in enumerate(new_state):
        c_s[h] = c_new
        n_s[h:h + 1, :] = n_new
        m_s[h:h + 1, :] = m_new


def _out_gate(o, z):
    return z / ((1.0 + jnp.exp(-o)) * (1.0 + jnp.exp(-z)))


def _head_cols(group, h):
    return slice(group * ML_W + h * ML_DH, group * ML_W + (h + 1) * ML_DH)


def _mlstm_kernel(qkv_ref, oz_ref, gc_ref, gt_ref, bcol_ref, brow_ref, gh_ref, c0_ref, n0_ref, m0_ref,
                  a_ref, cout_ref, nout_ref, mout_ref, c_s, n_s, m_s, *, c, n_pad):
    ci = pl.program_id(1)

    @pl.when(ci == 0)
    def _():
        c_s[...] = c0_ref[0]
        n_s[...] = n0_ref[0]
        m_s[...] = m0_ref[0]

    def a_store(h, value):
        a_ref[:, _head_cols(0, h)] = value

    _mlstm_chunk(lambda h: qkv_ref[:, _head_cols(0, h)], lambda h: qkv_ref[:, _head_cols(1, h)],
                 lambda h: qkv_ref[:, _head_cols(2, h)],
                 lambda h: _out_gate(oz_ref[:, _head_cols(0, h)], oz_ref[:, _head_cols(1, h)]),
                 gc_ref[...] + bcol_ref[...], gt_ref[0] + brow_ref[...], gh_ref, c_s, n_s, m_s, a_store, c, n_pad)

    @pl.when(ci == pl.num_programs(1) - 1)
    def _():
        cout_ref[0] = c_s[...]
        nout_ref[0] = n_s[...]
        mout_ref[0] = m_s[...]


def _ml_fused_kernel(x_ref, g_ref, w_ref, ws_ref, wst_ref, bcol_ref, brow_ref, gh_ref,
                     a_ref, cout_ref, nout_ref, mout_ref,
                     u_s, qkv_a, qkv_b, gate_a, gate_b, sm_a, sm_b, smt_a, smt_b, c_s, n_s, m_s, *, c, nrb):
    i = pl.program_id(0)
    tm = x_ref.shape[0]

    @pl.when(i == 0)
    def _():
        qkv_b[...] = jnp.zeros(qkv_b.shape, BF16)
        gate_b[...] = jnp.zeros(gate_b.shape, F32)
        sm_b[...] = jnp.zeros(sm_b.shape, F32)
        smt_b[...] = jnp.zeros(smt_b.shape, F32)

    @pl.when((i == 0) | (lax.rem(jnp.maximum(i - 1, 0), nrb) == 0))
    def _():
        c_s[...] = jnp.zeros(c_s.shape, F32)
        n_s[...] = jnp.zeros(n_s.shape, F32)
        m_s[...] = jnp.zeros(m_s.shape, F32)

    def projection_pieces(qkv_w, gate_w, sm_w, smt_w):
        pw = ML_DH

        def block(row0):
            return lax.dot_general(u_s[...], w_ref[row0:row0 + pw, :], _NT, preferred_element_type=F32)

        def narrow():
            sm_w[...] = jnp.dot(u_s[...], ws_ref[...], preferred_element_type=F32)
            smt_w[...] = lax.dot_general(wst_ref[SMALL_IG:SMALL_IG + SAMPLE_ROWS, :], u_s[...], _NT,
                                         preferred_element_type=F32)

        def qkv_block(cb):
            acc = block(cb * pw)
            if ML_W <= cb * pw < 2 * ML_W:
                acc = acc * (ML_DH ** -0.5)
            qkv_w[:, cb * pw:(cb + 1) * pw] = acc.astype(BF16)

        def gate_block(cb):
            gate_w[:, cb * pw:(cb + 1) * pw] = _out_gate(block(3 * ML_W + cb * pw), block(4 * ML_W + cb * pw))

        pieces = [narrow]
        pieces += [functools.partial(qkv_block, cb) for cb in range(3 * ML_W // pw)]
        pieces += [functools.partial(gate_block, cb) for cb in range(ML_W // pw)]
        return pieces

    def step(write, read):
        u_s[...] = _normed(x_ref, g_ref)
        pieces = projection_pieces(*write)
        n_slots = (tm // c) * ML_H * 3
        n_pieces = len(pieces)
        slot = [0]

        def emit():
            slot[0] += 1
            while n_pieces - len(pieces) < (slot[0] * n_pieces) // n_slots:
                pieces.pop(0)()

        qkv_r, gate_r, sm_r, smt_r = read
        for ck in range(tm // c):
            rows = slice(ck * c, (ck + 1) * c)

            def a_store(h, value, rows=rows):
                a_ref[rows, _head_cols(0, h)] = value

            _mlstm_chunk(lambda h, rows=rows: qkv_r[rows, _head_cols(0, h)],
                         lambda h, rows=rows: qkv_r[rows, _head_cols(1, h)],
                         lambda h, rows=rows: qkv_r[rows, _head_cols(2, h)],
                         lambda h, rows=rows: gate_r[rows, _head_cols(0, h)],
                         sm_r[rows, :] + bcol_ref[...], smt_r[:, rows] + brow_ref[...],
                         gh_ref, c_s, n_s, m_s, a_store, c, 0, between_heads=emit)
        while pieces:
            pieces.pop(0)()

    set_a = (qkv_a, gate_a, sm_a, smt_a)
    set_b = (qkv_b, gate_b, sm_b, smt_b)

    @pl.when(lax.rem(i, 2) == 0)
    def _():
        step(set_a, set_b)

    @pl.when(lax.rem(i, 2) == 1)
    def _():
        step(set_b, set_a)

    @pl.when((i >= 1) & (lax.rem(jnp.maximum(i - 1, 0), nrb) == nrb - 1))
    def _():
        cout_ref[0] = c_s[...]
        nout_ref[0] = n_s[...]
        mout_ref[0] = m_s[...]


def _ml_fused(x2d, g_pre, w_all, w_small, w_small_t, b_gates, g_head, nb, seq, tm, c):
    rows = x2d.shape[0]
    nrb = seq // tm
    n_blocks = rows // tm
    n_ml = 5 * ML_W
    bias_col = jnp.zeros((1, LANES), F32).at[0, SMALL_IG:SMALL_IG + 2 * ML_H].set(b_gates)
    bias_row = jnp.zeros((SAMPLE_ROWS, 1), F32).at[:2 * ML_H, 0].set(b_gates)
    const = lambda i: (0, 0)
    resident = pl.Buffered(1)
    lagged = lambda i: jnp.maximum(i - 1, 0)
    out_shape = (
        jax.ShapeDtypeStruct((rows, ML_W), BF16),
        jax.ShapeDtypeStruct((nb, ML_H, ML_DH, ML_DH), F32),
        jax.ShapeDtypeStruct((nb, ML_H, ML_DH), F32),
        jax.ShapeDtypeStruct((nb, 8, LANES), F32),
    )
    a, c_out, n_out, m_out = pl.pallas_call(
        functools.partial(_ml_fused_kernel, c=c, nrb=nrb),
        out_shape=out_shape,
        grid=(n_blocks + 1,),
        in_specs=[
            pl.BlockSpec((tm, D_MODEL), lambda i: (jnp.minimum(i, n_blocks - 1), 0)),
            pl.BlockSpec((1, D_MODEL), const),
            pl.BlockSpec((n_ml, D_MODEL), const, pipeline_mode=resident),
            pl.BlockSpec((D_MODEL, LANES), const, pipeline_mode=resident),
            pl.BlockSpec((LANES, D_MODEL), const, pipeline_mode=resident),
            pl.BlockSpec((1, LANES), const),
            pl.BlockSpec((SAMPLE_ROWS, 1), const),
            pl.BlockSpec((1, ML_W), const),
        ],
        out_specs=(
            pl.BlockSpec((tm, ML_W), lambda i: (lagged(i), 0)),
            pl.BlockSpec((1, ML_H, ML_DH, ML_DH), lambda i: (lagged(i) // nrb, 0, 0, 0)),
            pl.BlockSpec((1, ML_H, ML_DH), lambda i: (lagged(i) // nrb, 0, 0)),
            pl.BlockSpec((1, 8, LANES), lambda i: (lagged(i) // nrb, 0, 0)),
        ),
        scratch_shapes=[pltpu.VMEM((tm, D_MODEL), BF16),
                        pltpu.VMEM((tm, 3 * ML_W), BF16), pltpu.VMEM((tm, 3 * ML_W), BF16),
                        pltpu.VMEM((tm, ML_W), F32), pltpu.VMEM((tm, ML_W), F32),
                        pltpu.VMEM((tm, LANES), F32), pltpu.VMEM((tm, LANES), F32),
                        pltpu.VMEM((SAMPLE_ROWS, tm), F32), pltpu.VMEM((SAMPLE_ROWS, tm), F32),
                        pltpu.VMEM((ML_H, ML_DH, ML_DH), F32), pltpu.VMEM((ML_H, ML_DH), F32),
                        pltpu.VMEM((8, LANES), F32)],
        compiler_params=pltpu.CompilerParams(dimension_semantics=("arbitrary",), vmem_limit_bytes=VMEM_LIMIT_FUSED),
        name="ml_fused",
    )(x2d, g_pre.reshape(1, D_MODEL), w_all, w_small, w_small_t, bias_col, bias_row, g_head.reshape(1, ML_W))
    return a, c_out, n_out, m_out[:, :ML_H, 0]


def _mlstm(qkv, oz, small, small_t, b_gates, g_head, c0, n0, m0, nb, c, n_pad):
    rows = qkv.shape[0]
    nc = rows // (nb * c)
    bias_col = jnp.zeros((1, LANES), F32).at[0, SMALL_IG:SMALL_IG + 2 * ML_H].set(b_gates)
    bias_row = jnp.zeros((SAMPLE_ROWS, 1), F32).at[:2 * ML_H, 0].set(b_gates)
    m0b = jnp.zeros((nb, 8, LANES), F32).at[:, :ML_H, :].set(jnp.broadcast_to(m0[:, :, None], (nb, ML_H, LANES)))
    rowblk = lambda b, i: (b * nc + i, 0)
    const = lambda b, i: (0, 0)
    gates_t = small_t[SMALL_IG:SMALL_IG + SAMPLE_ROWS].reshape(SAMPLE_ROWS, rows // c, c).transpose(1, 0, 2)
    out_shape = (
        jax.ShapeDtypeStruct((rows, ML_W), BF16),
        jax.ShapeDtypeStruct((nb, ML_H, ML_DH, ML_DH), F32),
        jax.ShapeDtypeStruct((nb, ML_H, ML_DH), F32),
        jax.ShapeDtypeStruct((nb, 8, LANES), F32),
    )
    st4 = lambda b, i: (b, 0, 0, 0)
    st3 = lambda b, i: (b, 0, 0)
    a, c_out, n_out, m_out = pl.pallas_call(
        functools.partial(_mlstm_kernel, c=c, n_pad=n_pad),
        out_shape=out_shape,
        grid=(nb, nc),
        in_specs=[
            pl.BlockSpec((c, 3 * ML_W), rowblk),
            pl.BlockSpec((c, 2 * ML_W), rowblk),
            pl.BlockSpec((c, LANES), rowblk),
            pl.BlockSpec((1, SAMPLE_ROWS, c), lambda b, i: (b * nc + i, 0, 0)),
            pl.BlockSpec((1, LANES), const),
            pl.BlockSpec((SAMPLE_ROWS, 1), const),
            pl.BlockSpec((1, ML_W), const),
            pl.BlockSpec((1, ML_H, ML_DH, ML_DH), st4),
            pl.BlockSpec((1, ML_H, ML_DH), st3),
            pl.BlockSpec((1, 8, LANES), st3),
        ],
        out_specs=(
            pl.BlockSpec((c, ML_W), rowblk),
            pl.BlockSpec((1, ML_H, ML_DH, ML_DH), st4),
            pl.BlockSpec((1, ML_H, ML_DH), st3),
            pl.BlockSpec((1, 8, LANES), st3),
        ),
        scratch_shapes=[pltpu.VMEM((ML_H, ML_DH, ML_DH), F32), pltpu.VMEM((ML_H, ML_DH), F32),
                        pltpu.VMEM((8, LANES), F32)],
        compiler_params=_cparams(("arbitrary", "arbitrary")),
        name="mlstm",
    )(qkv, oz, small, gates_t, bias_col, bias_row, g_head.reshape(1, ML_W), c0, n0, m0b)
    return a, c_out, n_out, m_out[:, :ML_H, 0]


_REDUCERS = {"sum": (jnp.sum, jnp.add), "max": (jnp.max, jnp.maximum), "min": (jnp.min, jnp.minimum)}
REDUCE_CHAINS = 8


def _reduce(x, axis, op):
    fn, combine = _REDUCERS[op]
    unit = 8 if axis == 0 else LANES
    n = x.shape[axis]
    units = n // unit
    if n % unit or units < 2 * REDUCE_CHAINS:
        return fn(x, axis=axis, keepdims=True)
    base, rem = divmod(units, REDUCE_CHAINS)
    parts, start = [], 0
    for i in range(REDUCE_CHAINS):
        size = (base + (1 if i < rem else 0)) * unit
        piece = x[start:start + size] if axis == 0 else x[:, start:start + size]
        parts.append(fn(piece, axis=axis, keepdims=True))
        start += size
    while len(parts) > 1:
        parts = [combine(parts[i], parts[i + 1]) for i in range(0, len(parts), 2)]
    return parts[0]


def _count(pred, axis):
    return _reduce(jnp.where(pred, 1.0, 0.0), axis, "sum")


def _kth_largest(x_ref, k, axis, n_bisect, quarter_steps=False):
    kf = float(k)
    x = x_ref[...]
    hi = _reduce(x, axis, "max")
    lo = _reduce(jnp.where(x == NEG_INF, POS_INF, x), axis, "min")

    def bisect(_, carry):
        lo, hi = carry
        mid = 0.5 * (lo + hi)
        ge = _count(x_ref[...] >= mid, axis) >= kf
        return jnp.where(ge, mid, lo), jnp.where(ge, hi, mid)

    def quarter(_, carry):
        lo, hi = carry
        w = hi - lo
        m1, m2, m3 = lo + 0.25 * w, lo + 0.5 * w, lo + 0.75 * w
        xx = x_ref[...]
        g1, g2, g3 = (_count(xx >= m, axis) >= kf for m in (m1, m2, m3))
        lo = jnp.where(g3, m3, jnp.where(g2, m2, jnp.where(g1, m1, lo)))
        hi = jnp.where(g3, hi, jnp.where(g2, m3, jnp.where(g1, m2, m1)))
        return lo, hi

    if quarter_steps:
        lo, hi = lax.fori_loop(0, (n_bisect + 1) // 2, quarter, (lo, hi))
    else:
        lo, hi = lax.fori_loop(0, n_bisect, bisect, (lo, hi))

    def finished(cmin, xx):
        return jnp.where((_count(xx > cmin, axis) < kf) | (cmin == POS_INF), 1.0, 0.0)

    xx = x_ref[...]
    thr = _reduce(jnp.where(xx >= lo, xx, POS_INF), axis, "min")
    done = finished(thr, xx)

    def cond(st):
        return st[2] < 0.5

    def body(st):
        thr, done, _ = st
        xx = x_ref[...]
        cmin = _reduce(jnp.where(xx > thr, xx, POS_INF), axis, "min")
        thr = jnp.where(done < 0.5, cmin, thr)
        done = jnp.maximum(done, finished(thr, xx))
        return thr, done, jnp.min(done)

    thr, _, _ = lax.while_loop(cond, body, (thr, done, jnp.min(done)))
    return thr


def _dsa_kernel(q_ref, z_ref, idxq_ref, wt_ref, k_ref, v_ref, kidx_ref, o_ref,
                kb_s, vt_s, kib_s, x_s, sel_s, *, n_keys, qb, n_sel, n_bisect, key_step):
    j = pl.program_id(1)

    @pl.when(j == 0)
    def _():
        for h in range(SA_H):
            lanes = slice(h * SA_DH, (h + 1) * SA_DH)
            kb_s[:, lanes] = k_ref[pl.ds(h, n_keys, stride=SA_H), :].astype(BF16)
            vt_s[h, 0:SA_DH, :] = v_ref[pl.ds(h, n_keys, stride=SA_H), :].T.astype(BF16)
            vt_s[h, SA_DH:SA_DH + ONES_ROWS, :] = jnp.ones((ONES_ROWS, n_keys), BF16)
        kib_s[...] = kidx_ref[...].astype(BF16)

    def attend(nk):
        xs = x_s.at[0:nk]
        ss = sel_s.at[0:nk]
        key = lax.broadcasted_iota(jnp.int32, (nk, qb), 0)
        qpos = j * qb + lax.broadcasted_iota(jnp.int32, (nk, qb), 1)
        valid = key <= qpos
        qcat = jnp.concatenate([idxq_ref[:, h * IDX_D:(h + 1) * IDX_D] for h in range(IDX_H)], axis=0)
        w_rows = [wt_ref[h:h + 1, :] * IDX_SCALE for h in range(IDX_H)]
        kc = min(IDX_KEY_CHUNK, nk)
        for c0 in range(0, nk, kc):
            d = lax.dot_general(kib_s[c0:c0 + kc, :], qcat, _NT, preferred_element_type=F32)
            sc = jnp.zeros((kc, qb), F32)
            for h in range(IDX_H):
                sc = sc + jnp.maximum(d[:, h * qb:(h + 1) * qb], 0.0) * w_rows[h]
            ok = (c0 + lax.broadcasted_iota(jnp.int32, (kc, qb), 0)) <= (
                j * qb + lax.broadcasted_iota(jnp.int32, (kc, qb), 1))
            x_s[c0:c0 + kc, :] = jnp.where(ok, sc, NEG_INF)
            sel_s[c0:c0 + kc, :] = jnp.where(ok, 1.0, 0.0)

        @pl.when((j + 1) * qb > n_sel)
        def _():
            kf = float(n_sel)
            thr = _kth_largest(xs, n_sel, 0, n_bisect)
            x = xs[...]
            need = kf - _count(x > thr, 0)
            n_tie = _count(x == thr, 0)
            qrow = j * qb + lax.broadcasted_iota(jnp.int32, (1, qb), 1)
            small = (qrow + 1) <= n_sel
            ss[...] = jnp.where(small, jnp.where(valid, 1.0, 0.0), jnp.where(x >= thr, 1.0, 0.0))
            excess = jnp.max(jnp.where((n_tie > need) & jnp.logical_not(small), 1.0, 0.0))

            @pl.when(excess > 0.5)
            def _():
                tb = min(256, nk)
                r_i = lax.broadcasted_iota(jnp.int32, (tb, tb), 0)
                c_i = lax.broadcasted_iota(jnp.int32, (tb, tb), 1)
                lower = jnp.where(c_i < r_i, 1.0, 0.0).astype(BF16)
                carry = jnp.zeros((1, qb), F32)
                for blk in range(nk // tb):
                    rows = slice(blk * tb, (blk + 1) * tb)
                    xb = x_s[rows, :]
                    tie = jnp.where(xb == thr, 1.0, 0.0)
                    rank = jnp.dot(lower, tie.astype(BF16), preferred_element_type=F32) + carry
                    keep = (xb > thr) | ((xb == thr) & (rank < need))
                    keyb = blk * tb + lax.broadcasted_iota(jnp.int32, (tb, qb), 0)
                    qposb = j * qb + lax.broadcasted_iota(jnp.int32, (tb, qb), 1)
                    smallb = jnp.where(keyb <= qposb, 1.0, 0.0)
                    sel_s[rows, :] = jnp.where(small, smallb, jnp.where(keep, 1.0, 0.0))
                    carry = carry + jnp.sum(tie, axis=0, keepdims=True)

        sel = ss[...] > 0.5
        heads = range(SA_H)
        hsl = [slice(h * SA_DH, (h + 1) * SA_DH) for h in heads]
        st = [jnp.where(sel, lax.dot_general(kb_s[0:nk, hsl[h]], q_ref[:, hsl[h]], _NT,
                                             preferred_element_type=F32), NEG_INF) for h in heads]
        mx = [_reduce(st[h], 0, "max") for h in heads]
        p = [jnp.exp2(st[h] - mx[h]).astype(BF16) for h in heads]
        pv = [jnp.dot(vt_s[h, :, 0:nk], p[h], preferred_element_type=F32) for h in heads]
        ot = [pv[h][0:SA_DH, :] / pv[h][SA_DH:SA_DH + 1, :] for h in heads]
        for h in heads:
            o_ref[:, hsl[h]] = (ot[h].T * _silu(z_ref[:, hsl[h]])).astype(BF16)

    n_ext = n_keys // key_step
    for e in range(n_ext):
        nk = (e + 1) * key_step
        lo_j = e * key_step // qb
        hi_j = nk // qb

        @pl.when((j >= lo_j) & (j < hi_j))
        def _(nk=nk):
            attend(nk)


def _dsa_prompt(saq, saz, idxq, small_t, k, v, kidx, nb, seq):
    rows = saq.shape[0]
    qb = min(seq, 128)
    nq = seq // qb
    n_sel = min(TOPK_MAX, seq // 4)
    qblk = lambda b, j: (b * nq + j, 0)
    per_b = lambda b, j: (b, 0)
    wt_blk = SMALL_W // 8
    return pl.pallas_call(
        functools.partial(_dsa_kernel, n_keys=seq, qb=qb, n_sel=n_sel, n_bisect=20, key_step=min(256, seq)),
        out_shape=jax.ShapeDtypeStruct((rows, SA_W), BF16),
        grid=(nb, nq),
        in_specs=[
            pl.BlockSpec((qb, SA_W), qblk),
            pl.BlockSpec((qb, SA_W), qblk),
            pl.BlockSpec((qb, IDX_H * IDX_D), qblk),
            pl.BlockSpec((8, qb), lambda b, j: (wt_blk, b * nq + j)),
            pl.BlockSpec((seq * SA_H, SA_DH), per_b),
            pl.BlockSpec((seq * SA_H, SA_DH), per_b),
            pl.BlockSpec((seq, IDX_D), per_b),
        ],
        out_specs=pl.BlockSpec((qb, SA_W), qblk),
        scratch_shapes=[pltpu.VMEM((seq, SA_W), BF16), pltpu.VMEM((SA_H, SA_DH + ONES_ROWS, seq), BF16),
                        pltpu.VMEM((seq, IDX_D), BF16), pltpu.VMEM((seq, qb), F32),
                        pltpu.VMEM((seq, qb), F32)],
        compiler_params=_cparams(("arbitrary", "arbitrary")),
        name="dsa",
    )(saq, saz, idxq, small_t, k, v, kidx)


def _sel_kernel(pt_ref, idxq_ref, small_ref, kinew_ref, *rest, npg, n_past, n_sel, n_real, n_bisect):
    page_refs = rest[:npg]
    sel_ref, x_s = rest[npg:]
    g = pl.program_id(1)
    rws = SAMPLE_ROWS
    top = rws - SEL_ROWS
    pk = n_past + LANES
    gk = npg * PAGE_SIZE

    qs = jnp.concatenate([idxq_ref[:, h * IDX_D:(h + 1) * IDX_D] for h in range(IDX_H)], axis=0)
    small = small_ref[...]

    def scores(d):
        sc = jnp.zeros((SEL_ROWS, d.shape[1]), F32)
        for h in range(IDX_H):
            w = small[top:, SMALL_W + h:SMALL_W + h + 1] * IDX_SCALE
            sc = sc + jnp.maximum(d[h * rws + top:(h + 1) * rws, :], 0.0) * w
        return sc

    kp_t = jnp.concatenate([r[0] for r in page_refs], axis=1).astype(BF16)
    x_s[:, pl.ds(pl.multiple_of(g * gk, LANES), gk)] = scores(jnp.dot(qs, kp_t, preferred_element_type=F32))

    @pl.when(g == pl.num_programs(1) - 1)
    def _():
        knew = jnp.concatenate([kinew_ref[...], jnp.zeros((LANES - rws, IDX_D), F32)], axis=0).astype(BF16)
        row = top + lax.broadcasted_iota(jnp.int32, (SEL_ROWS, LANES), 0)
        col = lax.broadcasted_iota(jnp.int32, (SEL_ROWS, LANES), 1)
        ok = (col >= rws - n_real) & (col < rws) & (col <= row)
        d_new = lax.dot_general(qs, knew, _NT, preferred_element_type=F32)
        x_s[:, n_past:pk] = jnp.where(ok, scores(d_new), NEG_INF)

        kf = float(n_sel)
        thr = _kth_largest(x_s, n_sel, 1, n_bisect, quarter_steps=True)
        x = x_s[...]
        need = kf - _count(x > thr, 1)
        n_tie = _count(x == thr, 1)
        sel_ref[0, 0:top, :] = jnp.ones((top, pk), F32)
        sel_ref[0, top:rws, :] = jnp.where(x >= thr, 1.0, 0.0)
        real = lax.broadcasted_iota(jnp.int32, (SEL_ROWS, 1), 0) >= SEL_ROWS - n_real
        excess = jnp.max(jnp.where((n_tie > need) & real, 1.0, 0.0))

        @pl.when(excess > 0.5)
        def _():
            r_i = lax.broadcasted_iota(jnp.int32, (LANES, LANES), 0)
            c_i = lax.broadcasted_iota(jnp.int32, (LANES, LANES), 1)
            upper = jnp.where(r_i < c_i, 1.0, 0.0).astype(BF16)

            def blk(i, carry):
                cols = pl.ds(pl.multiple_of(i * LANES, LANES), LANES)
                xb = x_s[:, cols]
                tie = jnp.where(xb == thr, 1.0, 0.0)
                rank = jnp.dot(tie.astype(BF16), upper, preferred_element_type=F32) + carry
                keep = (xb > thr) | ((xb == thr) & (rank < need))
                sel_ref[0, top:rws, cols] = jnp.where(keep, 1.0, 0.0)
                return carry + jnp.sum(tie, axis=1, keepdims=True)

            lax.fori_loop(0, pk // LANES, blk, jnp.zeros((SEL_ROWS, 1), F32))


def _dsa_select(page_table, idxq, small, kidx_new, cache_kidx, n_sel, n_real):
    nreq, n_pages = page_table.shape
    n_past = n_pages * PAGE_SIZE
    pk = n_past + LANES
    npg = min(SEL_PAGES_PER_STEP, n_pages)
    req = lambda b, g, pt: (b, 0)

    def page_map(i):
        return lambda b, g, pt: (pt[b, g * npg + i], 0, 0)

    grid_spec = pltpu.PrefetchScalarGridSpec(
        num_scalar_prefetch=1,
        grid=(nreq, n_pages // npg),
        in_specs=[pl.BlockSpec((SAMPLE_ROWS, IDX_H * IDX_D), req),
                  pl.BlockSpec((SAMPLE_ROWS, LANES), req),
                  pl.BlockSpec((SAMPLE_ROWS, IDX_D), req)]
                 + [pl.BlockSpec((1, IDX_D, PAGE_SIZE), page_map(i)) for i in range(npg)],
        out_specs=pl.BlockSpec((1, SAMPLE_ROWS, pk), lambda b, g, pt: (b, 0, 0)),
        scratch_shapes=[pltpu.VMEM((SEL_ROWS, pk), F32)],
    )
    assert n_real <= SEL_ROWS
    return pl.pallas_call(
        functools.partial(_sel_kernel, npg=npg, n_past=n_past, n_sel=n_sel, n_real=n_real, n_bisect=20),
        out_shape=jax.ShapeDtypeStruct((nreq, SAMPLE_ROWS, pk), F32),
        grid_spec=grid_spec,
        compiler_params=_cparams(("arbitrary", "arbitrary")),
        name="dsa_sel",
    )(page_table, idxq, small, kidx_new, *([cache_kidx] * npg))


def _att_kernel(pt_ref, q_ref, z_ref, knew_ref, vnew_ref, sel_ref, seltail_ref, *rest, npg):
    k_refs = rest[:npg]
    v_refs = rest[npg:2 * npg]
    o_ref = rest[2 * npg]
    m_s, l_s, acc_s = rest[2 * npg + 1:]
    g = pl.program_id(1)
    rws = SAMPLE_ROWS
    floor = -1e30

    @pl.when(g == 0)
    def _():
        m_s[...] = jnp.full(m_s.shape, floor, F32)
        l_s[...] = jnp.zeros(l_s.shape, F32)
        acc_s[...] = jnp.zeros(acc_s.shape, F32)

    heads = range(SA_H)
    hsl = [slice(h * SA_DH, (h + 1) * SA_DH) for h in heads]

    def update(kbs, vbs, keep):
        m_old = [m_s[h][:, 0:1] for h in heads]
        l_old = [l_s[h][:, 0:1] for h in heads]
        acc_old = [acc_s[:, hsl[h]] for h in heads]
        s = [lax.dot_general(q_ref[:, hsl[h]], kbs[h], _NT, preferred_element_type=F32) for h in heads]
        m_new = [jnp.maximum(m_old[h], jnp.max(jnp.where(keep, s[h], floor), axis=1, keepdims=True)) for h in heads]
        p = [jnp.where(keep, jnp.exp2(s[h] - m_new[h]), 0.0) for h in heads]
        pv = [jnp.dot(p[h].astype(BF16), vbs[h], preferred_element_type=F32) for h in heads]
        alpha = [jnp.exp2(m_old[h] - m_new[h]) for h in heads]
        l_new = [alpha[h] * l_old[h] + jnp.sum(p[h], axis=1, keepdims=True) for h in heads]
        for h in heads:
            acc_s[:, hsl[h]] = alpha[h] * acc_old[h] + pv[h]
            l_s[h] = jnp.broadcast_to(l_new[h], (rws, LANES))
            m_s[h] = jnp.broadcast_to(m_new[h], (rws, LANES))

    def head_rows(refs, h):
        return jnp.concatenate([r[0, pl.ds(h, PAGE_SIZE, stride=SA_H), :] for r in refs], axis=0).astype(BF16)

    update([head_rows(k_refs, h) for h in heads], [head_rows(v_refs, h) for h in heads], sel_ref[0] > 0.5)

    @pl.when(g == pl.num_programs(1) - 1)
    def _():
        update([knew_ref[pl.ds(h, rws, stride=SA_H), :].astype(BF16) for h in heads],
               [vnew_ref[pl.ds(h, rws, stride=SA_H), :].astype(BF16) for h in heads],
               seltail_ref[0][:, :rws] > 0.5)
        for h in heads:
            o_ref[:, hsl[h]] = (acc_s[:, hsl[h]] / l_s[h][:, 0:1] * _silu(z_ref[:, hsl[h]])).astype(BF16)


def _dsa_attend(page_table, saq, saz, k_new, v_new, sel, cache_k, cache_v):
    nreq, n_pages = page_table.shape
    n_past = n_pages * PAGE_SIZE
    npg = min(ATT_PAGES_PER_STEP, n_pages)
    prow = PAGE_SIZE * SA_H
    req = lambda b, g, pt: (b, 0)

    def page_map(i):
        return lambda b, g, pt: (pt[b, g * npg + i], 0, 0)

    page_specs = [pl.BlockSpec((1, prow, SA_DH), page_map(i)) for i in range(npg)]
    grid_spec = pltpu.PrefetchScalarGridSpec(
        num_scalar_prefetch=1,
        grid=(nreq, n_pages // npg),
        in_specs=[pl.BlockSpec((SAMPLE_ROWS, SA_W), req), pl.BlockSpec((SAMPLE_ROWS, SA_W), req),
                  pl.BlockSpec((SAMPLE_ROWS * SA_H, SA_DH), req), pl.BlockSpec((SAMPLE_ROWS * SA_H, SA_DH), req),
                  pl.BlockSpec((1, SAMPLE_ROWS, npg * PAGE_SIZE), lambda b, g, pt: (b, 0, g)),
                  pl.BlockSpec((1, SAMPLE_ROWS, LANES), lambda b, g, pt: (b, 0, n_past // LANES))]
                 + page_specs + page_specs,
        out_specs=pl.BlockSpec((SAMPLE_ROWS, SA_W), req),
        scratch_shapes=[pltpu.VMEM((SA_H, SAMPLE_ROWS, LANES), F32), pltpu.VMEM((SA_H, SAMPLE_ROWS, LANES), F32),
                        pltpu.VMEM((SAMPLE_ROWS, SA_W), F32)],
    )
    return pl.pallas_call(
        functools.partial(_att_kernel, npg=npg),
        out_shape=jax.ShapeDtypeStruct((nreq * SAMPLE_ROWS, SA_W), BF16),
        grid_spec=grid_spec,
        compiler_params=_cparams(("arbitrary", "arbitrary")),
        name="dsa_att",
    )(page_table, saq, saz, k_new, v_new, sel, sel, *([cache_k] * npg), *([cache_v] * npg))


def _mem_attend(q_ref, z_ref, mk_ref, mv_ref):
    mk = mk_ref[...].astype(BF16)
    mv = mv_ref[...].astype(BF16)
    outs = []
    for h in range(MEM_H):
        hs = slice(h * MEM_DH, (h + 1) * MEM_DH)
        s = lax.dot_general(q_ref[:, hs], mk[:, hs], _NT, preferred_element_type=F32)
        p = jnp.exp(s - jnp.max(s, axis=1, keepdims=True))
        l = jnp.sum(p, axis=1, keepdims=True)
        o = jnp.dot(p.astype(BF16), mv[:, hs], preferred_element_type=F32) / l
        outs.append((o * _silu(z_ref[:, hs])).astype(BF16))
    return jnp.concatenate(outs, axis=1)


def _memattn_kernel(q_ref, z_ref, mk_ref, mv_ref, o_ref):
    o_ref[...] = _mem_attend(q_ref, z_ref, mk_ref, mv_ref)


def _memattn(memq, memz, mk, mv, nb, tq):
    rows = memq.shape[0]
    n_mem = mk.shape[0] // nb
    nq = rows // (nb * tq)
    qblk = lambda b, i: (b * nq + i, 0)
    per_b = lambda b, i: (b, 0)
    return pl.pallas_call(
        _memattn_kernel,
        out_shape=jax.ShapeDtypeStruct((rows, MEM_W), BF16),
        grid=(nb, nq),
        in_specs=[pl.BlockSpec((tq, MEM_W), qblk), pl.BlockSpec((tq, MEM_W), qblk),
                  pl.BlockSpec((n_mem, MEM_W), per_b), pl.BlockSpec((n_mem, MEM_W), per_b)],
        out_specs=pl.BlockSpec((tq, MEM_W), qblk),
        compiler_params=_cparams(("arbitrary", "arbitrary")),
        name="memattn",
    )(memq, memz, mk, mv)


def _mixout_kernel(x_ref, a_ref, b_ref, c_ref, wa_ref, wb_ref, wc_ref, g_ref, y_ref):
    acc = (jnp.dot(a_ref[...], wa_ref[...], preferred_element_type=F32)
           + jnp.dot(b_ref[...], wb_ref[...], preferred_element_type=F32)
           + jnp.dot(c_ref[...], wc_ref[...], preferred_element_type=F32))
    y = acc * lax.rsqrt(jnp.mean(acc * acc, axis=-1, keepdims=True) + EPS) * g_ref[...]
    y_ref[...] = x_ref[...] + y


def _mixout_mem_kernel(x_ref, a_ref, b_ref, q_ref, z_ref, mk_ref, mv_ref, wa_ref, wb_ref, wc_ref, g_ref, y_ref):
    c = _mem_attend(q_ref, z_ref, mk_ref, mv_ref)
    acc = (jnp.dot(a_ref[...], wa_ref[...], preferred_element_type=F32)
           + jnp.dot(b_ref[...], wb_ref[...], preferred_element_type=F32)
           + jnp.dot(c, wc_ref[...], preferred_element_type=F32))
    y = acc * lax.rsqrt(jnp.mean(acc * acc, axis=-1, keepdims=True) + EPS) * g_ref[...]
    y_ref[...] = x_ref[...] + y


def _mixout_mem(x2d, a, b, memq, memz, mk, mv, w_out, g_post, tm, seq):
    rows = x2d.shape[0]
    n_mem = mk.shape[0] // (rows // seq)
    wb16 = w_out.astype(BF16)
    row = lambda i: (i, 0)
    const = lambda i: (0, 0)
    per_req = lambda i: (i // (seq // tm), 0)
    return pl.pallas_call(
        _mixout_mem_kernel,
        out_shape=jax.ShapeDtypeStruct((rows, D_MODEL), F32),
        grid=(rows // tm,),
        in_specs=[pl.BlockSpec((tm, D_MODEL), row), pl.BlockSpec((tm, ML_W), row),
                  pl.BlockSpec((tm, SA_W), row), pl.BlockSpec((tm, MEM_W), row), pl.BlockSpec((tm, MEM_W), row),
                  pl.BlockSpec((n_mem, MEM_W), per_req), pl.BlockSpec((n_mem, MEM_W), per_req),
                  pl.BlockSpec((ML_W, D_MODEL), const), pl.BlockSpec((SA_W, D_MODEL), const),
                  pl.BlockSpec((MEM_W, D_MODEL), const), pl.BlockSpec((1, D_MODEL), const)],
        out_specs=pl.BlockSpec((tm, D_MODEL), row),
        compiler_params=_cparams(("arbitrary",)),
        name="mixout_mem",
    )(x2d, a, b, memq, memz, mk, mv, wb16[:ML_W], wb16[ML_W:ML_W + SA_W], wb16[ML_W + SA_W:],
      g_post.reshape(1, D_MODEL))


def _mixout(x2d, a, b, c, w_out, g_post, tm):
    rows = x2d.shape[0]
    wb16 = w_out.astype(BF16)
    row = lambda i: (i, 0)
    const = lambda i: (0, 0)
    return pl.pallas_call(
        _mixout_kernel,
        out_shape=jax.ShapeDtypeStruct((rows, D_MODEL), F32),
        grid=(rows // tm,),
        in_specs=[pl.BlockSpec((tm, D_MODEL), row), pl.BlockSpec((tm, ML_W), row),
                  pl.BlockSpec((tm, SA_W), row), pl.BlockSpec((tm, MEM_W), row),
                  pl.BlockSpec((ML_W, D_MODEL), const), pl.BlockSpec((SA_W, D_MODEL), const),
                  pl.BlockSpec((MEM_W, D_MODEL), const), pl.BlockSpec((1, D_MODEL), const)],
        out_specs=pl.BlockSpec((tm, D_MODEL), row),
        compiler_params=_cparams(("arbitrary",)),
        name="mixout",
    )(x2d, a, b, c, wb16[:ML_W], wb16[ML_W:ML_W + SA_W], wb16[ML_W + SA_W:], g_post.reshape(1, D_MODEL))


def _layer(x_p, x_s, st_c, st_n, st_m, c_k, c_v, c_kidx, c_mk, c_mv, page_table, mem_prompt,
           g_pre, w_in, b_gates, g_head, w_mem_k, w_mem_v, g_mem, w_out, g_post):
    nb, seq, _ = x_p.shape
    nreq, t_dec, _ = x_s.shape
    n_mem = mem_prompt.shape[1]
    n_past = page_table.shape[1] * PAGE_SIZE
    weights = _relayout_w_in(w_in)

    tm = min(512, seq)
    tabs = _rope_tables(jnp.arange(seq, dtype=jnp.int32))
    x_p2d = x_p.reshape(nb * seq, D_MODEL)
    (_, _, saq, k, v, saz, memq, memz, idxq, kidx, small, small_t) = _project(
        x_p2d, g_pre, weights, tabs, tm, with_ml=False)
    a_p, p_c, p_n, p_m = _ml_fused(x_p2d, g_pre, weights[0], weights[3], weights[4], b_gates, g_head,
                                   nb, seq, tm, min(ML_CHUNK, seq))
    b_p = _dsa_prompt(saq, saz, idxq, small_t, k, v, kidx, nb, seq)
    mk, mv = _memkv(mem_prompt.reshape(nb * n_mem, D_MODEL), g_mem, w_mem_k, w_mem_v, n_mem)
    y_p = _mixout_mem(x_p.reshape(nb * seq, D_MODEL), a_p, b_p, memq, memz, mk, mv, w_out, g_post,
                      tm, seq).reshape(nb, seq, D_MODEL)

    rws = SAMPLE_ROWS
    n_padrow = rws - t_dec
    xs_pad = jnp.concatenate([jnp.zeros((nreq, n_padrow, D_MODEL), F32), x_s], axis=1).reshape(nreq * rws, D_MODEL)
    pos_s = jnp.tile(jnp.concatenate([jnp.zeros((n_padrow,), jnp.int32),
                                      n_past + jnp.arange(t_dec, dtype=jnp.int32)]), nreq)
    tabs_s = _rope_tables(pos_s)
    (qkv_s, oz_s, saq_s, k_s, v_s, saz_s, memq_s, memz_s, idxq_s, kidx_s, small_s, small_t_s) = _project(
        xs_pad, g_pre, weights, tabs_s, nreq * rws)
    a_s, s_c, s_n, s_m = _mlstm(qkv_s, oz_s, small_s, small_t_s, b_gates, g_head, st_c, st_n, st_m,
                                nreq, rws, n_padrow)
    n_sel = min(TOPK_MAX, (n_past + t_dec) // 4)
    sel = _dsa_select(page_table, idxq_s, small_s, kidx_s, jnp.swapaxes(c_kidx, 1, 2), n_sel, t_dec)
    b_s = _dsa_attend(page_table, saq_s, saz_s, k_s, v_s, sel,
                      c_k.reshape(c_k.shape[0], PAGE_SIZE * SA_H, SA_DH),
                      c_v.reshape(c_v.shape[0], PAGE_SIZE * SA_H, SA_DH))
    c_s = _memattn(memq_s, memz_s, c_mk.reshape(nreq * n_mem, MEM_W), c_mv.reshape(nreq * n_mem, MEM_W), nreq, rws)
    y_s = _mixout(xs_pad, a_s, b_s, c_s, w_out, g_post, nreq * rws)

    def real(a2d):
        return a2d.reshape(nreq, rws, -1)[:, n_padrow:]

    new = (p_c, p_n, p_m,
           k.reshape(nb, seq, SA_H, SA_DH), v.reshape(nb, seq, SA_H, SA_DH), kidx.reshape(nb, seq, IDX_D),
           mk.reshape(nb, n_mem, MEM_H, MEM_DH), mv.reshape(nb, n_mem, MEM_H, MEM_DH),
           s_c, s_n, s_m,
           real(k_s).reshape(nreq, t_dec, SA_H, SA_DH), real(v_s).reshape(nreq, t_dec, SA_H, SA_DH), real(kidx_s))
    return y_p, real(y_s), new


def kernel(x_prompt, x_sample, state_mlstm_C, state_mlstm_n, state_mlstm_m, cache_k, cache_v, cache_kidx,
           cache_mem_k, cache_mem_v, page_table, mem_prompt, g_pre, w_in, b_gates, g_head, w_mem_k, w_mem_v,
           g_mem, w_out, g_post):
    xp, xs = x_prompt, x_sample
    per_layer = []
    for l in range(w_in.shape[0]):
        xp, xs, new = _layer(xp, xs, state_mlstm_C[l], state_mlstm_n[l], state_mlstm_m[l],
                             cache_k[l], cache_v[l], cache_kidx[l], cache_mem_k[l], cache_mem_v[l],
                             page_table, mem_prompt, g_pre[l], w_in[l], b_gates[l], g_head[l],
                             w_mem_k[l], w_mem_v[l], g_mem[l], w_out[l], g_post[l])
        per_layer.append(new)
    stacked = [jnp.stack(a) for a in zip(*per_layer)]
    return (xp, xs, *stacked)
```

```python
import functools
import math

import jax
import jax.numpy as jnp
from jax import lax
from jax.experimental import pallas as pl
from jax.experimental.pallas import tpu as pltpu

F32 = jnp.float32
BF16 = jnp.bfloat16

D_MODEL = 2048
ML_H = 4
ML_W = D_MODEL // 2
ML_DH = ML_W // ML_H
SA_H = 4
SA_W = D_MODEL // 4
SA_DH = SA_W // SA_H
MEM_H = 4
MEM_W = D_MODEL // 4
MEM_DH = MEM_W // MEM_H
IDX_H = 8
IDX_D = 64
IDX_SCALE = (IDX_H * IDX_D) ** -0.5
TOPK_MAX = 256
ROPE_THETA = 10000.0
LOG2E = 1.4426950408889634
EPS = 1e-6
PAGE_SIZE = 128

LANES = 128
CB = 512
SMALL_IG = 64
SMALL_LF = 68
SMALL_W = 72
SAMPLE_ROWS = 16
SEL_ROWS = 8
SEL_PAGES_PER_STEP = 32
ATT_PAGES_PER_STEP = 16
ML_CHUNK = 256
IDX_KEY_CHUNK = 256
DSA_KEY_STEP = 128
TIE_BLOCK = 256
ONES_ROWS = 16
VMEM_LIMIT = 56 * 1024 * 1024
VMEM_LIMIT_FUSED = 60 * 1024 * 1024
NEG_INF = float("-inf")
POS_INF = float("inf")

_NT = (((1,), (1,)), ((), ()))
_TN = (((0,), (0,)), ((), ()))


def _cparams(sem):
    return pltpu.CompilerParams(dimension_semantics=sem, vmem_limit_bytes=VMEM_LIMIT)


def _sigmoid(x):
    return 1.0 / (1.0 + jnp.exp(-x))


def _silu(x):
    return x * _sigmoid(x)


def _log_sigmoid(x):
    return jnp.minimum(x, 0.0) - jnp.log1p(jnp.exp(-jnp.abs(x)))


def _split3(x):
    hi = x.astype(BF16)
    r = x - hi.astype(F32)
    mid = r.astype(BF16)
    lo = (r - mid.astype(F32)).astype(BF16)
    return hi, mid, lo


def _rope128(x, cos, sin_signed):
    return x * cos + pltpu.roll(x, 64, 1) * sin_signed


def _rope64(x, cos, sin_signed):
    lane = lax.broadcasted_iota(jnp.int32, x.shape, 1)
    first_half = (lane % 64) < 32
    partner = jnp.where(first_half, pltpu.roll(x, 96, 1), pltpu.roll(x, 32, 1))
    return x * cos + partner * sin_signed


def _normed(x_ref, g_ref):
    x = x_ref[...]
    return (x * lax.rsqrt(jnp.mean(x * x, axis=-1, keepdims=True) + EPS) * g_ref[...]).astype(BF16)


def _proj_ml_kernel(x_ref, g_ref, w_ref, qkv_ref, oz_ref, u_ref):
    u_ref[...] = _normed(x_ref, g_ref)
    n_qkv = 3 * ML_W // CB
    for cb in range(5 * ML_W // CB):
        acc = lax.dot_general(u_ref[...], w_ref[cb * CB:(cb + 1) * CB, :], _NT, preferred_element_type=F32)
        if ML_W <= cb * CB < 2 * ML_W:
            acc = acc * (ML_DH ** -0.5)
        if cb < n_qkv:
            qkv_ref[:, cb * CB:(cb + 1) * CB] = acc.astype(BF16)
        else:
            oz_ref[:, (cb - n_qkv) * CB:(cb - n_qkv + 1) * CB] = acc


def _proj_rest_kernel(x_ref, g_ref, wsa_ref, wmem_ref, ws_ref, wst_ref, c128_ref, s128_ref, c64_ref, s64_ref,
                      saq_ref, k_ref, v_ref, saz_ref, memq_ref, memz_ref, idxq_ref,
                      kidx_ref, small_ref, smallt_ref, u_ref):
    u_ref[...] = _normed(x_ref, g_ref)
    sm = jnp.dot(u_ref[...], ws_ref[...], preferred_element_type=F32)
    small_ref[...] = sm
    kidx_ref[...] = _rope64(sm, c64_ref[...], s64_ref[...])[:, :IDX_D]
    smallt_ref[...] = lax.dot_general(wst_ref[...], u_ref[...], _NT, preferred_element_type=F32)

    def block(cb, w_ref=wsa_ref):
        return lax.dot_general(u_ref[...], w_ref[cb * CB:(cb + 1) * CB, :], _NT, preferred_element_type=F32)

    def rope_heads(acc, fn, cos_ref, sin_ref):
        return jnp.concatenate(
            [fn(acc[:, h * LANES:(h + 1) * LANES], cos_ref[...], sin_ref[...]) for h in range(CB // LANES)], axis=1)

    saq_ref[...] = (rope_heads(block(0), _rope128, c128_ref, s128_ref) * (SA_DH ** -0.5 * LOG2E)).astype(BF16)
    tm = x_ref.shape[0]
    k_acc = block(1)
    v_acc = block(2)
    for h in range(SA_H):
        lanes = slice(h * SA_DH, (h + 1) * SA_DH)
        k_ref[pl.ds(h, tm, stride=SA_H), :] = _rope128(k_acc[:, lanes], c128_ref[...], s128_ref[...])
        v_ref[pl.ds(h, tm, stride=SA_H), :] = v_acc[:, lanes]
    saz_ref[...] = block(3)
    idxq_ref[...] = rope_heads(block(4), _rope64, c64_ref, s64_ref).astype(BF16)
    memq_ref[...] = (block(0, wmem_ref) * (MEM_DH ** -0.5)).astype(BF16)
    memz_ref[...] = block(1, wmem_ref)


def _project(x2d, g_pre, weights, tabs, tm, with_ml=True):
    w_ml, w_sa, w_mem, w_small, w_small_t = weights
    rows = x2d.shape[0]
    c128, s128, c64, s64 = tabs
    ntab = c128.shape[0] // tm
    n_ml = 5 * ML_W
    row_only = lambda i: (i, 0)
    tab_map = lambda i: (i % ntab, 0)
    const = lambda i: (0, 0)
    resident = pl.Buffered(1)
    g2d = g_pre.reshape(1, D_MODEL)

    qkv, oz = (None, None) if not with_ml else pl.pallas_call(
        _proj_ml_kernel,
        out_shape=(jax.ShapeDtypeStruct((rows, 3 * ML_W), BF16),
                   jax.ShapeDtypeStruct((rows, 2 * ML_W), F32)),
        grid=(rows // tm,),
        in_specs=[pl.BlockSpec((tm, D_MODEL), row_only),
                  pl.BlockSpec((1, D_MODEL), const),
                  pl.BlockSpec((n_ml, D_MODEL), const, pipeline_mode=resident)],
        out_specs=(pl.BlockSpec((tm, 3 * ML_W), row_only), pl.BlockSpec((tm, 2 * ML_W), row_only)),
        scratch_shapes=[pltpu.VMEM((tm, D_MODEL), BF16)],
        compiler_params=_cparams(("arbitrary",)),
        name="proj_ml",
    )(x2d, g2d, w_ml)

    out_shape = (
        jax.ShapeDtypeStruct((rows, SA_W), BF16),
        jax.ShapeDtypeStruct((rows * SA_H, SA_DH), F32),
        jax.ShapeDtypeStruct((rows * SA_H, SA_DH), F32),
        jax.ShapeDtypeStruct((rows, SA_W), F32),
        jax.ShapeDtypeStruct((rows, MEM_W), BF16),
        jax.ShapeDtypeStruct((rows, MEM_W), F32),
        jax.ShapeDtypeStruct((rows, IDX_H * IDX_D), BF16),
        jax.ShapeDtypeStruct((rows, IDX_D), F32),
        jax.ShapeDtypeStruct((rows, LANES), F32),
        jax.ShapeDtypeStruct((LANES, rows), F32),
    )
    out_specs = (
        pl.BlockSpec((tm, CB), row_only),
        pl.BlockSpec((tm * SA_H, SA_DH), row_only),
        pl.BlockSpec((tm * SA_H, SA_DH), row_only),
        pl.BlockSpec((tm, CB), row_only),
        pl.BlockSpec((tm, CB), row_only),
        pl.BlockSpec((tm, CB), row_only),
        pl.BlockSpec((tm, CB), row_only),
        pl.BlockSpec((tm, IDX_D), row_only),
        pl.BlockSpec((tm, LANES), row_only),
        pl.BlockSpec((LANES, tm), lambda i: (0, i)),
    )
    in_specs = [
        pl.BlockSpec((tm, D_MODEL), row_only),
        pl.BlockSpec((1, D_MODEL), const),
        pl.BlockSpec(w_sa.shape, const, pipeline_mode=resident),
        pl.BlockSpec(w_mem.shape, const, pipeline_mode=resident),
        pl.BlockSpec((D_MODEL, LANES), const, pipeline_mode=resident),
        pl.BlockSpec((LANES, D_MODEL), const, pipeline_mode=resident),
        pl.BlockSpec((tm, LANES), tab_map),
        pl.BlockSpec((tm, LANES), tab_map),
        pl.BlockSpec((tm, LANES), tab_map),
        pl.BlockSpec((tm, LANES), tab_map),
    ]
    rest = pl.pallas_call(
        _proj_rest_kernel,
        out_shape=out_shape,
        grid=(rows // tm,),
        in_specs=in_specs,
        out_specs=out_specs,
        scratch_shapes=[pltpu.VMEM((tm, D_MODEL), BF16)],
        compiler_params=_cparams(("arbitrary",)),
        name="proj_rest",
    )(x2d, g2d, w_sa, w_mem, w_small, w_small_t, c128, s128, c64, s64)
    return (qkv, oz, *rest)


def _rope_tables(pos):
    def tab(half):
        inv = ROPE_THETA ** (-jnp.arange(half, dtype=F32) / half)
        ang = pos.astype(F32)[:, None] * inv[None, :]
        return jnp.cos(ang), jnp.sin(ang)

    c, s = tab(SA_DH // 2)
    c128 = jnp.concatenate([c, c], axis=1)
    s128 = jnp.concatenate([-s, s], axis=1)
    c, s = tab(IDX_D // 2)
    c64 = jnp.concatenate([c, c, c, c], axis=1)
    s64 = jnp.concatenate([-s, s, -s, s], axis=1)
    return c128, s128, c64, s64


def _relayout_w_in(w_in):
    off = {}
    o = 0
    for name, w in (('ml_q', ML_W), ('ml_k', ML_W), ('ml_v', ML_W), ('ml_o', ML_W), ('ml_z', ML_W),
                    ('ml_i', ML_H), ('ml_f', ML_H), ('sa_q', SA_W), ('sa_k', SA_W), ('sa_v', SA_W),
                    ('sa_z', SA_W), ('idx_q', IDX_H * IDX_D), ('idx_k', IDX_D), ('idx_w', IDX_H),
                    ('mem_q', MEM_W), ('mem_z', MEM_W)):
        off[name] = (o, w)
        o += w

    w_t = w_in.T

    def col(name):
        a, w = off[name]
        return w_t[a:a + w]

    def span(first, last):
        return w_t[off[first][0]:off[last][0] + off[last][1]].astype(BF16)

    w_ml = w_t.astype(BF16)
    w_sa = span('sa_q', 'idx_q')
    w_mem = span('mem_q', 'mem_z')
    small_t = jnp.concatenate([
        col('idx_k'), col('ml_i'), col('ml_f'), col('idx_w'),
        jnp.zeros((LANES - IDX_D - 2 * ML_H - IDX_H, D_MODEL), F32)], axis=0).astype(BF16)
    return (w_ml, w_sa, w_mem, small_t.T, small_t)


def _memkv_kernel(m_ref, g_ref, wk_ref, wv_ref, k_ref, v_ref):
    x = m_ref[...]
    u = (x * lax.rsqrt(jnp.mean(x * x, axis=-1, keepdims=True) + EPS) * g_ref[...]).astype(BF16)
    k_ref[...] = jnp.dot(u, wk_ref[...], preferred_element_type=F32)
    v_ref[...] = jnp.dot(u, wv_ref[...], preferred_element_type=F32)


def _memkv(mem2d, g_mem, wk, wv, n_mem):
    rows = mem2d.shape[0]
    row = lambda i: (i, 0)
    const = lambda i: (0, 0)
    return pl.pallas_call(
        _memkv_kernel,
        out_shape=(jax.ShapeDtypeStruct((rows, MEM_W), F32), jax.ShapeDtypeStruct((rows, MEM_W), F32)),
        grid=(rows // n_mem,),
        in_specs=[pl.BlockSpec((n_mem, D_MODEL), row), pl.BlockSpec((1, D_MODEL), const),
                  pl.BlockSpec((D_MODEL, MEM_W), const), pl.BlockSpec((D_MODEL, MEM_W), const)],
        out_specs=(pl.BlockSpec((n_mem, MEM_W), row), pl.BlockSpec((n_mem, MEM_W), row)),
        compiler_params=_cparams(("arbitrary",)),
        name="memkv",
    )(mem2d, g_mem.reshape(1, D_MODEL), wk.astype(BF16), wv.astype(BF16))


def _mlstm_chunk(q_of, k_of, v_of, gate_of, g_c, g_r, gh_ref, c_s, n_s, m_s, a_store, c, n_pad,
                 between_heads=None):
    ri = lax.broadcasted_iota(jnp.int32, (c, c), 0)
    cj = lax.broadcasted_iota(jnp.int32, (c, c), 1)
    causal = cj <= ri
    tri = jnp.where(causal, 1.0, 0.0).astype(BF16)
    tri_t = jnp.where(ri <= cj, 1.0, 0.0).astype(BF16)

    pad_c = lax.broadcasted_iota(jnp.int32, (c, LANES), 0) < n_pad
    ig_c = jnp.where(pad_c, NEG_INF, g_c)
    lf_c = jnp.where(pad_c, 0.0, _log_sigmoid(g_c))
    b_c = sum(jnp.dot(tri, p, preferred_element_type=F32) for p in _split3(lf_c))
    pad_r = lax.broadcasted_iota(jnp.int32, (SAMPLE_ROWS, c), 1) < n_pad
    ig_r = jnp.where(pad_r, NEG_INF, g_r)
    lf_r = jnp.where(pad_r, 0.0, _log_sigmoid(g_r))
    b_r = sum(jnp.dot(p, tri_t, preferred_element_type=F32) for p in _split3(lf_r))

    m_all = m_s[...]
    n_all = n_s[...]
    c_all = [c_s[h] for h in range(ML_H)]
    new_state = []
    for h in range(ML_H):
        hs = slice(h * ML_DH, (h + 1) * ML_DH)
        m_prev = m_all[h:h + 1, 0:1]
        b_t = b_c[:, SMALL_LF + h:SMALL_LF + h + 1]
        igc = ig_c[:, SMALL_IG + h:SMALL_IG + h + 1]
        b_s = b_r[ML_H + h:ML_H + h + 1, :]
        igr = ig_r[h:h + 1, :]
        a = jnp.where(causal, b_t - b_s + igr, NEG_INF)
        bm = b_t + m_prev
        m_t = jnp.maximum(bm, jnp.max(a, axis=1, keepdims=True))
        inter = jnp.exp(bm - m_t)
        dmat = jnp.exp(a - m_t)
        q, k, v = q_of(h), k_of(h), v_of(h)
        s = lax.dot_general(q, k, _NT, preferred_element_type=F32) * dmat
        if between_heads is not None:
            between_heads()
        c_h = c_all[h]
        n_h = n_all[h:h + 1, :]
        num = (jnp.dot(s.astype(BF16), v, preferred_element_type=F32)
               + inter * jnp.dot(q, c_h.astype(BF16), preferred_element_type=F32))
        qn = (jnp.sum(s, axis=1, keepdims=True)
              + inter * jnp.sum(q.astype(F32) * n_h, axis=1, keepdims=True))
        hh = num / jnp.maximum(jnp.abs(qn), jnp.exp(-m_t))
        hh = hh * lax.rsqrt(jnp.mean(hh * hh, axis=1, keepdims=True) + EPS)
        if between_heads is not None:
            between_heads()
        a_store(h, (hh * gh_ref[:, hs] * gate_of(h)).astype(BF16))

        m_new = m_t[c - 1:c, :]
        b_last = b_t[c - 1:c, :]
        w_end = jnp.exp(b_last - b_t + igc - m_new)
        decay = jnp.exp(b_last + m_prev - m_new)
        kw = k.astype(F32) * w_end
        new_state.append((decay * c_h + lax.dot_general(kw.astype(BF16), v, _TN, preferred_element_type=F32),
                          decay * n_h + jnp.sum(kw, axis=0, keepdims=True),
                          jnp.broadcast_to(m_new, (1, LANES))))
        if between_heads is not None:
            between_heads()

    for h, (c_new, n_new, m_new) in enumerate(new_state):
        c_s[h] = c_new
        n_s[h:h + 1, :] = n_new
        m_s[h:h + 1, :] = m_new


def _out_gate(o, z):
    return z / ((1.0 + jnp.exp(-o)) * (1.0 + jnp.exp(-z)))


def _head_cols(group, h):
    return slice(group * ML_W + h * ML_DH, group * ML_W + (h + 1) * ML_DH)


def _mlstm_kernel(qkv_ref, oz_ref, gc_ref, gt_ref, bcol_ref, brow_ref, gh_ref, c0_ref, n0_ref, m0_ref,
                  a_ref, cout_ref, nout_ref, mout_ref, c_s, n_s, m_s, *, c, n_pad):
    ci = pl.program_id(1)

    @pl.when(ci == 0)
    def _():
        c_s[...] = c0_ref[0]
        n_s[...] = n0_ref[0]
        m_s[...] = m0_ref[0]

    def a_store(h, value):
        a_ref[:, _head_cols(0, h)] = value

    _mlstm_chunk(lambda h: qkv_ref[:, _head_cols(0, h)], lambda h: qkv_ref[:, _head_cols(1, h)],
                 lambda h: qkv_ref[:, _head_cols(2, h)],
                 lambda h: _out_gate(oz_ref[:, _head_cols(0, h)], oz_ref[:, _head_cols(1, h)]),
                 gc_ref[...] + bcol_ref[...], gt_ref[0] + brow_ref[...], gh_ref, c_s, n_s, m_s, a_store, c, n_pad)

    @pl.when(ci == pl.num_programs(1) - 1)
    def _():
        cout_ref[0] = c_s[...]
        nout_ref[0] = n_s[...]
        mout_ref[0] = m_s[...]


def _ml_fused_kernel(x_ref, g_ref, w_ref, ws_ref, wst_ref, bcol_ref, brow_ref, gh_ref,
                     a_ref, cout_ref, nout_ref, mout_ref,
                     u_s, qkv_a, qkv_b, gate_a, gate_b, sm_a, sm_b, smt_a, smt_b, c_s, n_s, m_s, *, c, nrb, n_blocks):
    i = pl.program_id(0)
    tm = x_ref.shape[0]

    @pl.when((i >= 1) & (lax.rem(jnp.maximum(i - 1, 0), nrb) == 0))
    def _():
        c_s[...] = jnp.zeros(c_s.shape, F32)
        n_s[...] = jnp.zeros(n_s.shape, F32)
        m_s[...] = jnp.zeros(m_s.shape, F32)

    def projection_pieces(qkv_w, gate_w, sm_w, smt_w):
        pw = ML_DH

        def block(row0):
            return lax.dot_general(u_s[...], w_ref[row0:row0 + pw, :], _NT, preferred_element_type=F32)

        def narrow():
            sm_w[...] = jnp.dot(u_s[...], ws_ref[...], preferred_element_type=F32)
            smt_w[...] = lax.dot_general(wst_ref[SMALL_IG:SMALL_IG + SAMPLE_ROWS, :], u_s[...], _NT,
                                         preferred_element_type=F32)

        def qkv_block(cb):
            acc = block(cb * pw)
            if ML_W <= cb * pw < 2 * ML_W:
                acc = acc * (ML_DH ** -0.5)
            qkv_w[:, cb * pw:(cb + 1) * pw] = acc.astype(BF16)

        def gate_block(cb):
            gate_w[:, cb * pw:(cb + 1) * pw] = _out_gate(block(3 * ML_W + cb * pw), block(4 * ML_W + cb * pw))

        pieces = [narrow]
        pieces += [functools.partial(qkv_block, cb) for cb in range(3 * ML_W // pw)]
        pieces += [functools.partial(gate_block, cb) for cb in range(ML_W // pw)]
        return pieces

    def step(write, read):
        pieces = []
        if write is not None:
            u_s[...] = _normed(x_ref, g_ref)
            pieces = projection_pieces(*write)
        n_slots = (tm // c) * ML_H * 3
        n_pieces = len(pieces)
        slot = [0]

        def emit():
            slot[0] += 1
            while n_pieces - len(pieces) < (slot[0] * n_pieces) // n_slots:
                pieces.pop(0)()

        qkv_r, gate_r, sm_r, smt_r = read if read is not None else (None,) * 4
        for ck in range(tm // c if read is not None else 0):
            rows = slice(ck * c, (ck + 1) * c)

            def a_store(h, value, rows=rows):
                a_ref[rows, _head_cols(0, h)] = value

            _mlstm_chunk(lambda h, rows=rows: qkv_r[rows, _head_cols(0, h)],
                         lambda h, rows=rows: qkv_r[rows, _head_cols(1, h)],
                         lambda h, rows=rows: qkv_r[rows, _head_cols(2, h)],
                         lambda h, rows=rows: gate_r[rows, _head_cols(0, h)],
                         sm_r[rows, :] + bcol_ref[...], smt_r[:, rows] + brow_ref[...],
                         gh_ref, c_s, n_s, m_s, a_store, c, 0, between_heads=emit)
        while pieces:
            pieces.pop(0)()

    set_a = (qkv_a, gate_a, sm_a, smt_a)
    set_b = (qkv_b, gate_b, sm_b, smt_b)

    middle = (i > 0) & (i < n_blocks)

    @pl.when(i == 0)
    def _():
        step(set_a, None)

    @pl.when(middle & (lax.rem(i, 2) == 0))
    def _():
        step(set_a, set_b)

    @pl.when(middle & (lax.rem(i, 2) == 1))
    def _():
        step(set_b, set_a)

    @pl.when(i == n_blocks)
    def _():
        step(None, set_b if n_blocks % 2 == 0 else set_a)

    @pl.when((i >= 1) & (lax.rem(jnp.maximum(i - 1, 0), nrb) == nrb - 1))
    def _():
        cout_ref[0] = c_s[...]
        nout_ref[0] = n_s[...]
        mout_ref[0] = m_s[...]


def _ml_fused(x2d, g_pre, w_all, w_small, w_small_t, b_gates, g_head, nb, seq, tm, c):
    rows = x2d.shape[0]
    nrb = seq // tm
    n_blocks = rows // tm
    n_ml = 5 * ML_W
    bias_col = jnp.zeros((1, LANES), F32).at[0, SMALL_IG:SMALL_IG + 2 * ML_H].set(b_gates)
    bias_row = jnp.zeros((SAMPLE_ROWS, 1), F32).at[:2 * ML_H, 0].set(b_gates)
    const = lambda i: (0, 0)
    resident = pl.Buffered(1)
    lagged = lambda i: jnp.maximum(i - 1, 0)
    out_shape = (
        jax.ShapeDtypeStruct((rows, ML_W), BF16),
        jax.ShapeDtypeStruct((nb, ML_H, ML_DH, ML_DH), F32),
        jax.ShapeDtypeStruct((nb, ML_H, ML_DH), F32),
        jax.ShapeDtypeStruct((nb, 8, LANES), F32),
    )
    a, c_out, n_out, m_out = pl.pallas_call(
        functools.partial(_ml_fused_kernel, c=c, nrb=nrb, n_blocks=n_blocks),
        out_shape=out_shape,
        grid=(n_blocks + 1,),
        in_specs=[
            pl.BlockSpec((tm, D_MODEL), lambda i: (jnp.minimum(i, n_blocks - 1), 0)),
            pl.BlockSpec((1, D_MODEL), const),
            pl.BlockSpec((n_ml, D_MODEL), const, pipeline_mode=resident),
            pl.BlockSpec((D_MODEL, LANES), const, pipeline_mode=resident),
            pl.BlockSpec((LANES, D_MODEL), const, pipeline_mode=resident),
            pl.BlockSpec((1, LANES), const),
            pl.BlockSpec((SAMPLE_ROWS, 1), const),
            pl.BlockSpec((1, ML_W), const),
        ],
        out_specs=(
            pl.BlockSpec((tm, ML_W), lambda i: (lagged(i), 0)),
            pl.BlockSpec((1, ML_H, ML_DH, ML_DH), lambda i: (lagged(i) // nrb, 0, 0, 0)),
            pl.BlockSpec((1, ML_H, ML_DH), lambda i: (lagged(i) // nrb, 0, 0)),
            pl.BlockSpec((1, 8, LANES), lambda i: (lagged(i) // nrb, 0, 0)),
        ),
        scratch_shapes=[pltpu.VMEM((tm, D_MODEL), BF16),
                        pltpu.VMEM((tm, 3 * ML_W), BF16), pltpu.VMEM((tm, 3 * ML_W), BF16),
                        pltpu.VMEM((tm, ML_W), F32), pltpu.VMEM((tm, ML_W), F32),
                        pltpu.VMEM((tm, LANES), F32), pltpu.VMEM((tm, LANES), F32),
                        pltpu.VMEM((SAMPLE_ROWS, tm), F32), pltpu.VMEM((SAMPLE_ROWS, tm), F32),
                        pltpu.VMEM((ML_H, ML_DH, ML_DH), F32), pltpu.VMEM((ML_H, ML_DH), F32),
                        pltpu.VMEM((8, LANES), F32)],
        compiler_params=pltpu.CompilerParams(dimension_semantics=("arbitrary",), vmem_limit_bytes=VMEM_LIMIT_FUSED),
        name="ml_fused",
    )(x2d, g_pre.reshape(1, D_MODEL), w_all, w_small, w_small_t, bias_col, bias_row, g_head.reshape(1, ML_W))
    return a, c_out, n_out, m_out[:, :ML_H, 0]


def _mlstm(qkv, oz, small, small_t, b_gates, g_head, c0, n0, m0, nb, c, n_pad):
    rows = qkv.shape[0]
    nc = rows // (nb * c)
    bias_col = jnp.zeros((1, LANES), F32).at[0, SMALL_IG:SMALL_IG + 2 * ML_H].set(b_gates)
    bias_row = jnp.zeros((SAMPLE_ROWS, 1), F32).at[:2 * ML_H, 0].set(b_gates)
    m0b = jnp.zeros((nb, 8, LANES), F32).at[:, :ML_H, :].set(jnp.broadcast_to(m0[:, :, None], (nb, ML_H, LANES)))
    rowblk = lambda b, i: (b * nc + i, 0)
    const = lambda b, i: (0, 0)
    gates_t = small_t[SMALL_IG:SMALL_IG + SAMPLE_ROWS].reshape(SAMPLE_ROWS, rows // c, c).transpose(1, 0, 2)
    out_shape = (
        jax.ShapeDtypeStruct((rows, ML_W), BF16),
        jax.ShapeDtypeStruct((nb, ML_H, ML_DH, ML_DH), F32),
        jax.ShapeDtypeStruct((nb, ML_H, ML_DH), F32),
        jax.ShapeDtypeStruct((nb, 8, LANES), F32),
    )
    st4 = lambda b, i: (b, 0, 0, 0)
    st3 = lambda b, i: (b, 0, 0)
    a, c_out, n_out, m_out = pl.pallas_call(
        functools.partial(_mlstm_kernel, c=c, n_pad=n_pad),
        out_shape=out_shape,
        grid=(nb, nc),
        in_specs=[
            pl.BlockSpec((c, 3 * ML_W), rowblk),
            pl.BlockSpec((c, 2 * ML_W), rowblk),
            pl.BlockSpec((c, LANES), rowblk),
            pl.BlockSpec((1, SAMPLE_ROWS, c), lambda b, i: (b * nc + i, 0, 0)),
            pl.BlockSpec((1, LANES), const),
            pl.BlockSpec((SAMPLE_ROWS, 1), const),
            pl.BlockSpec((1, ML_W), const),
            pl.BlockSpec((1, ML_H, ML_DH, ML_DH), st4),
            pl.BlockSpec((1, ML_H, ML_DH), st3),
            pl.BlockSpec((1, 8, LANES), st3),
        ],
        out_specs=(
            pl.BlockSpec((c, ML_W), rowblk),
            pl.BlockSpec((1, ML_H, ML_DH, ML_DH), st4),
            pl.BlockSpec((1, ML_H, ML_DH), st3),
            pl.BlockSpec((1, 8, LANES), st3),
        ),
        scratch_shapes=[pltpu.VMEM((ML_H, ML_DH, ML_DH), F32), pltpu.VMEM((ML_H, ML_DH), F32),
                        pltpu.VMEM((8, LANES), F32)],
        compiler_params=_cparams(("arbitrary", "arbitrary")),
        name="mlstm",
    )(qkv, oz, small, gates_t, bias_col, bias_row, g_head.reshape(1, ML_W), c0, n0, m0b)
    return a, c_out, n_out, m_out[:, :ML_H, 0]


_REDUCERS = {"sum": (jnp.sum, jnp.add), "max": (jnp.max, jnp.maximum), "min": (jnp.min, jnp.minimum)}
REDUCE_CHAINS = 8


def _reduce(x, axis, op):
    fn, combine = _REDUCERS[op]
    unit = 8 if axis == 0 else LANES
    n = x.shape[axis]
    units = n // unit
    if n % unit or units < 2 * REDUCE_CHAINS:
        return fn(x, axis=axis, keepdims=True)
    base, rem = divmod(units, REDUCE_CHAINS)
    parts, start = [], 0
    for i in range(REDUCE_CHAINS):
        size = (base + (1 if i < rem else 0)) * unit
        piece = x[start:start + size] if axis == 0 else x[:, start:start + size]
        parts.append(fn(piece, axis=axis, keepdims=True))
        start += size
    while len(parts) > 1:
        parts = [combine(parts[i], parts[i + 1]) for i in range(0, len(parts), 2)]
    return parts[0]


def _count(pred, axis):
    return _reduce(jnp.where(pred, 1.0, 0.0), axis, "sum")


def _kth_largest(x_ref, k, axis, n_bisect, quarter_steps=False):
    kf = float(k)
    x = x_ref[...]
    hi = _reduce(x, axis, "max")
    lo = _reduce(jnp.where(x == NEG_INF, POS_INF, x), axis, "min")

    def bisect(_, carry):
        lo, hi = carry
        mid = 0.5 * (lo + hi)
        ge = _count(x_ref[...] >= mid, axis) >= kf
        return jnp.where(ge, mid, lo), jnp.where(ge, hi, mid)

    def quarter(_, carry):
        lo, hi = carry
        w = hi - lo
        m1, m2, m3 = lo + 0.25 * w, lo + 0.5 * w, lo + 0.75 * w
        xx = x_ref[...]
        g1, g2, g3 = (_count(xx >= m, axis) >= kf for m in (m1, m2, m3))
        lo = jnp.where(g3, m3, jnp.where(g2, m2, jnp.where(g1, m1, lo)))
        hi = jnp.where(g3, hi, jnp.where(g2, m3, jnp.where(g1, m2, m1)))
        return lo, hi

    if quarter_steps:
        lo, hi = lax.fori_loop(0, (n_bisect + 1) // 2, quarter, (lo, hi))
    else:
        lo, hi = lax.fori_loop(0, n_bisect, bisect, (lo, hi))

    def finished(cmin, xx):
        return jnp.where((_count(xx > cmin, axis) < kf) | (cmin == POS_INF), 1.0, 0.0)

    xx = x_ref[...]
    thr = _reduce(jnp.where(xx >= lo, xx, POS_INF), axis, "min")
    done = finished(thr, xx)

    def cond(st):
        return st[2] < 0.5

    def body(st):
        thr, done, _ = st
        xx = x_ref[...]
        cmin = _reduce(jnp.where(xx > thr, xx, POS_INF), axis, "min")
        thr = jnp.where(done < 0.5, cmin, thr)
        done = jnp.maximum(done, finished(thr, xx))
        return thr, done, jnp.min(done)

    thr, _, _ = lax.while_loop(cond, body, (thr, done, jnp.min(done)))
    return thr


def _dsa_kernel(q_ref, z_ref, idxq_ref, wt_ref, k_ref, v_ref, kidx_ref, o_ref,
                kb_s, vt_s, kib_s, x_s, sel_s, *, n_keys, qb, n_sel, n_bisect, key_step):
    j = pl.program_id(1)

    @pl.when(j == 0)
    def _():
        for h in range(SA_H):
            lanes = slice(h * SA_DH, (h + 1) * SA_DH)
            kb_s[:, lanes] = k_ref[pl.ds(h, n_keys, stride=SA_H), :].astype(BF16)
            vt_s[h, 0:SA_DH, :] = v_ref[pl.ds(h, n_keys, stride=SA_H), :].T.astype(BF16)
            vt_s[h, SA_DH:SA_DH + ONES_ROWS, :] = jnp.ones((ONES_ROWS, n_keys), BF16)
        kib_s[...] = kidx_ref[...].astype(BF16)

    def attend(nk):
        xs = x_s.at[0:nk]
        ss = sel_s.at[0:nk]
        key = lax.broadcasted_iota(jnp.int32, (nk, qb), 0)
        qpos = j * qb + lax.broadcasted_iota(jnp.int32, (nk, qb), 1)
        valid = key <= qpos
        qcat = jnp.concatenate([idxq_ref[:, h * IDX_D:(h + 1) * IDX_D] for h in range(IDX_H)], axis=0)
        w_rows = [wt_ref[h:h + 1, :] * IDX_SCALE for h in range(IDX_H)]
        kc = math.gcd(IDX_KEY_CHUNK, nk)
        for c0 in range(0, nk, kc):
            d = lax.dot_general(kib_s[c0:c0 + kc, :], qcat, _NT, preferred_element_type=F32)
            sc = jnp.zeros((kc, qb), F32)
            for h in range(IDX_H):
                sc = sc + jnp.maximum(d[:, h * qb:(h + 1) * qb], 0.0) * w_rows[h]
            ok = (c0 + lax.broadcasted_iota(jnp.int32, (kc, qb), 0)) <= (
                j * qb + lax.broadcasted_iota(jnp.int32, (kc, qb), 1))
            x_s[c0:c0 + kc, :] = jnp.where(ok, sc, NEG_INF)
            sel_s[c0:c0 + kc, :] = jnp.where(ok, 1.0, 0.0)

        @pl.when((j + 1) * qb > n_sel)
        def _():
            kf = float(n_sel)
            thr = _kth_largest(xs, n_sel, 0, n_bisect)
            x = xs[...]
            need = kf - _count(x > thr, 0)
            n_tie = _count(x == thr, 0)
            qrow = j * qb + lax.broadcasted_iota(jnp.int32, (1, qb), 1)
            small = (qrow + 1) <= n_sel
            ss[...] = jnp.where(small, jnp.where(valid, 1.0, 0.0), jnp.where(x >= thr, 1.0, 0.0))
            excess = jnp.max(jnp.where((n_tie > need) & jnp.logical_not(small), 1.0, 0.0))

            @pl.when(excess > 0.5)
            def _():
                tb = math.gcd(TIE_BLOCK, nk)
                r_i = lax.broadcasted_iota(jnp.int32, (tb, tb), 0)
                c_i = lax.broadcasted_iota(jnp.int32, (tb, tb), 1)
                lower = jnp.where(c_i < r_i, 1.0, 0.0).astype(BF16)
                carry = jnp.zeros((1, qb), F32)
                for blk in range(nk // tb):
                    rows = slice(blk * tb, (blk + 1) * tb)
                    xb = x_s[rows, :]
                    tie = jnp.where(xb == thr, 1.0, 0.0)
                    rank = jnp.dot(lower, tie.astype(BF16), preferred_element_type=F32) + carry
                    keep = (xb > thr) | ((xb == thr) & (rank < need))
                    keyb = blk * tb + lax.broadcasted_iota(jnp.int32, (tb, qb), 0)
                    qposb = j * qb + lax.broadcasted_iota(jnp.int32, (tb, qb), 1)
                    smallb = jnp.where(keyb <= qposb, 1.0, 0.0)
                    sel_s[rows, :] = jnp.where(small, smallb, jnp.where(keep, 1.0, 0.0))
                    carry = carry + jnp.sum(tie, axis=0, keepdims=True)

        sel = ss[...] > 0.5
        heads = range(SA_H)
        hsl = [slice(h * SA_DH, (h + 1) * SA_DH) for h in heads]
        st = [jnp.where(sel, lax.dot_general(kb_s[0:nk, hsl[h]], q_ref[:, hsl[h]], _NT,
                                             preferred_element_type=F32), NEG_INF) for h in heads]
        mx = [_reduce(st[h], 0, "max") for h in heads]
        p = [jnp.exp2(st[h] - mx[h]).astype(BF16) for h in heads]
        pv = [jnp.dot(vt_s[h, :, 0:nk], p[h], preferred_element_type=F32) for h in heads]
        ot = [pv[h][0:SA_DH, :] / pv[h][SA_DH:SA_DH + 1, :] for h in heads]
        for h in heads:
            o_ref[:, hsl[h]] = (ot[h].T * _silu(z_ref[:, hsl[h]])).astype(BF16)

    n_ext = n_keys // key_step
    for e in range(n_ext):
        nk = (e + 1) * key_step
        lo_j = e * key_step // qb
        hi_j = nk // qb

        @pl.when((j >= lo_j) & (j < hi_j))
        def _(nk=nk):
            attend(nk)


def _dsa_prompt(saq, saz, idxq, small_t, k, v, kidx, nb, seq):
    rows = saq.shape[0]
    qb = min(seq, 128)
    nq = seq // qb
    n_sel = min(TOPK_MAX, seq // 4)
    qblk = lambda b, j: (b * nq + j, 0)
    per_b = lambda b, j: (b, 0)
    wt_blk = SMALL_W // 8
    return pl.pallas_call(
        functools.partial(_dsa_kernel, n_keys=seq, qb=qb, n_sel=n_sel, n_bisect=20, key_step=min(DSA_KEY_STEP, seq)),
        out_shape=jax.ShapeDtypeStruct((rows, SA_W), BF16),
        grid=(nb, nq),
        in_specs=[
            pl.BlockSpec((qb, SA_W), qblk),
            pl.BlockSpec((qb, SA_W), qblk),
            pl.BlockSpec((qb, IDX_H * IDX_D), qblk),
            pl.BlockSpec((8, qb), lambda b, j: (wt_blk, b * nq + j)),
            pl.BlockSpec((seq * SA_H, SA_DH), per_b),
            pl.BlockSpec((seq * SA_H, SA_DH), per_b),
            pl.BlockSpec((seq, IDX_D), per_b),
        ],
        out_specs=pl.BlockSpec((qb, SA_W), qblk),
        scratch_shapes=[pltpu.VMEM((seq, SA_W), BF16), pltpu.VMEM((SA_H, SA_DH + ONES_ROWS, seq), BF16),
                        pltpu.VMEM((seq, IDX_D), BF16), pltpu.VMEM((seq, qb), F32),
                        pltpu.VMEM((seq, qb), F32)],
        compiler_params=_cparams(("arbitrary", "arbitrary")),
        name="dsa",
    )(saq, saz, idxq, small_t, k, v, kidx)


def _sel_kernel(pt_ref, idxq_ref, small_ref, kinew_ref, *rest, npg, n_past, n_sel, n_real, n_bisect):
    page_refs = rest[:npg]
    sel_ref, x_s = rest[npg:]
    g = pl.program_id(1)
    rws = SAMPLE_ROWS
    top = rws - SEL_ROWS
    pk = n_past + LANES
    gk = npg * PAGE_SIZE

    qs = jnp.concatenate([idxq_ref[:, h * IDX_D:(h + 1) * IDX_D] for h in range(IDX_H)], axis=0)
    small = small_ref[...]

    def scores(d):
        sc = jnp.zeros((SEL_ROWS, d.shape[1]), F32)
        for h in range(IDX_H):
            w = small[top:, SMALL_W + h:SMALL_W + h + 1] * IDX_SCALE
            sc = sc + jnp.maximum(d[h * rws + top:(h + 1) * rws, :], 0.0) * w
        return sc

    kp_t = jnp.concatenate([r[0] for r in page_refs], axis=1).astype(BF16)
    x_s[:, pl.ds(pl.multiple_of(g * gk, LANES), gk)] = scores(jnp.dot(qs, kp_t, preferred_element_type=F32))

    @pl.when(g == pl.num_programs(1) - 1)
    def _():
        knew = jnp.concatenate([kinew_ref[...], jnp.zeros((LANES - rws, IDX_D), F32)], axis=0).astype(BF16)
        row = top + lax.broadcasted_iota(jnp.int32, (SEL_ROWS, LANES), 0)
        col = lax.broadcasted_iota(jnp.int32, (SEL_ROWS, LANES), 1)
        ok = (col >= rws - n_real) & (col < rws) & (col <= row)
        d_new = lax.dot_general(qs, knew, _NT, preferred_element_type=F32)
        x_s[:, n_past:pk] = jnp.where(ok, scores(d_new), NEG_INF)

        kf = float(n_sel)
        thr = _kth_largest(x_s, n_sel, 1, n_bisect, quarter_steps=True)
        x = x_s[...]
        need = kf - _count(x > thr, 1)
        n_tie = _count(x == thr, 1)
        sel_ref[0, 0:top, :] = jnp.ones((top, pk), F32)
        sel_ref[0, top:rws, :] = jnp.where(x >= thr, 1.0, 0.0)
        real = lax.broadcasted_iota(jnp.int32, (SEL_ROWS, 1), 0) >= SEL_ROWS - n_real
        excess = jnp.max(jnp.where((n_tie > need) & real, 1.0, 0.0))

        @pl.when(excess > 0.5)
        def _():
            r_i = lax.broadcasted_iota(jnp.int32, (LANES, LANES), 0)
            c_i = lax.broadcasted_iota(jnp.int32, (LANES, LANES), 1)
            upper = jnp.where(r_i < c_i, 1.0, 0.0).astype(BF16)

            def blk(i, carry):
                cols = pl.ds(pl.multiple_of(i * LANES, LANES), LANES)
                xb = x_s[:, cols]
                tie = jnp.where(xb == thr, 1.0, 0.0)
                rank = jnp.dot(tie.astype(BF16), upper, preferred_element_type=F32) + carry
                keep = (xb > thr) | ((xb == thr) & (rank < need))
                sel_ref[0, top:rws, cols] = jnp.where(keep, 1.0, 0.0)
                return carry + jnp.sum(tie, axis=1, keepdims=True)

            lax.fori_loop(0, pk // LANES, blk, jnp.zeros((SEL_ROWS, 1), F32))


def _dsa_select(page_table, idxq, small, kidx_new, cache_kidx, n_sel, n_real):
    nreq, n_pages = page_table.shape
    n_past = n_pages * PAGE_SIZE
    pk = n_past + LANES
    npg = min(SEL_PAGES_PER_STEP, n_pages)
    req = lambda b, g, pt: (b, 0)

    def page_map(i):
        return lambda b, g, pt: (pt[b, g * npg + i], 0, 0)

    grid_spec = pltpu.PrefetchScalarGridSpec(
        num_scalar_prefetch=1,
        grid=(nreq, n_pages // npg),
        in_specs=[pl.BlockSpec((SAMPLE_ROWS, IDX_H * IDX_D), req),
                  pl.BlockSpec((SAMPLE_ROWS, LANES), req),
                  pl.BlockSpec((SAMPLE_ROWS, IDX_D), req)]
                 + [pl.BlockSpec((1, IDX_D, PAGE_SIZE), page_map(i)) for i in range(npg)],
        out_specs=pl.BlockSpec((1, SAMPLE_ROWS, pk), lambda b, g, pt: (b, 0, 0)),
        scratch_shapes=[pltpu.VMEM((SEL_ROWS, pk), F32)],
    )
    assert n_real <= SEL_ROWS
    return pl.pallas_call(
        functools.partial(_sel_kernel, npg=npg, n_past=n_past, n_sel=n_sel, n_real=n_real, n_bisect=20),
        out_shape=jax.ShapeDtypeStruct((nreq, SAMPLE_ROWS, pk), F32),
        grid_spec=grid_spec,
        compiler_params=_cparams(("arbitrary", "arbitrary")),
        name="dsa_sel",
    )(page_table, idxq, small, kidx_new, *([cache_kidx] * npg))


def _att_kernel(pt_ref, q_ref, z_ref, knew_ref, vnew_ref, sel_ref, seltail_ref, *rest, npg):
    k_refs = rest[:npg]
    v_refs = rest[npg:2 * npg]
    o_ref = rest[2 * npg]
    m_s, l_s, acc_s = rest[2 * npg + 1:]
    g = pl.program_id(1)
    rws = SAMPLE_ROWS
    floor = -1e30

    @pl.when(g == 0)
    def _():
        m_s[...] = jnp.full(m_s.shape, floor, F32)
        l_s[...] = jnp.zeros(l_s.shape, F32)
        acc_s[...] = jnp.zeros(acc_s.shape, F32)

    heads = range(SA_H)
    hsl = [slice(h * SA_DH, (h + 1) * SA_DH) for h in heads]

    def update(kbs, vbs, keep):
        m_old = [m_s[h][:, 0:1] for h in heads]
        l_old = [l_s[h][:, 0:1] for h in heads]
        acc_old = [acc_s[:, hsl[h]] for h in heads]
        s = [lax.dot_general(q_ref[:, hsl[h]], kbs[h], _NT, preferred_element_type=F32) for h in heads]
        m_new = [jnp.maximum(m_old[h], jnp.max(jnp.where(keep, s[h], floor), axis=1, keepdims=True)) for h in heads]
        p = [jnp.where(keep, jnp.exp2(s[h] - m_new[h]), 0.0) for h in heads]
        pv = [jnp.dot(p[h].astype(BF16), vbs[h], preferred_element_type=F32) for h in heads]
        alpha = [jnp.exp2(m_old[h] - m_new[h]) for h in heads]
        l_new = [alpha[h] * l_old[h] + jnp.sum(p[h], axis=1, keepdims=True) for h in heads]
        for h in heads:
            acc_s[:, hsl[h]] = alpha[h] * acc_old[h] + pv[h]
            l_s[h] = jnp.broadcast_to(l_new[h], (rws, LANES))
            m_s[h] = jnp.broadcast_to(m_new[h], (rws, LANES))

    def head_rows(refs, h):
        return jnp.concatenate([r[0, pl.ds(h, PAGE_SIZE, stride=SA_H), :] for r in refs], axis=0).astype(BF16)

    update([head_rows(k_refs, h) for h in heads], [head_rows(v_refs, h) for h in heads], sel_ref[0] > 0.5)

    @pl.when(g == pl.num_programs(1) - 1)
    def _():
        update([knew_ref[pl.ds(h, rws, stride=SA_H), :].astype(BF16) for h in heads],
               [vnew_ref[pl.ds(h, rws, stride=SA_H), :].astype(BF16) for h in heads],
               seltail_ref[0][:, :rws] > 0.5)
        for h in heads:
            o_ref[:, hsl[h]] = (acc_s[:, hsl[h]] / l_s[h][:, 0:1] * _silu(z_ref[:, hsl[h]])).astype(BF16)


def _dsa_attend(page_table, saq, saz, k_new, v_new, sel, cache_k, cache_v):
    nreq, n_pages = page_table.shape
    n_past = n_pages * PAGE_SIZE
    npg = min(ATT_PAGES_PER_STEP, n_pages)
    prow = PAGE_SIZE * SA_H
    req = lambda b, g, pt: (b, 0)

    def page_map(i):
        return lambda b, g, pt: (pt[b, g * npg + i], 0, 0)

    page_specs = [pl.BlockSpec((1, prow, SA_DH), page_map(i)) for i in range(npg)]
    grid_spec = pltpu.PrefetchScalarGridSpec(
        num_scalar_prefetch=1,
        grid=(nreq, n_pages // npg),
        in_specs=[pl.BlockSpec((SAMPLE_ROWS, SA_W), req), pl.BlockSpec((SAMPLE_ROWS, SA_W), req),
                  pl.BlockSpec((SAMPLE_ROWS * SA_H, SA_DH), req), pl.BlockSpec((SAMPLE_ROWS * SA_H, SA_DH), req),
                  pl.BlockSpec((1, SAMPLE_ROWS, npg * PAGE_SIZE), lambda b, g, pt: (b, 0, g)),
                  pl.BlockSpec((1, SAMPLE_ROWS, LANES), lambda b, g, pt: (b, 0, n_past // LANES))]
                 + page_specs + page_specs,
        out_specs=pl.BlockSpec((SAMPLE_ROWS, SA_W), req),
        scratch_shapes=[pltpu.VMEM((SA_H, SAMPLE_ROWS, LANES), F32), pltpu.VMEM((SA_H, SAMPLE_ROWS, LANES), F32),
                        pltpu.VMEM((SAMPLE_ROWS, SA_W), F32)],
    )
    return pl.pallas_call(
        functools.partial(_att_kernel, npg=npg),
        out_shape=jax.ShapeDtypeStruct((nreq * SAMPLE_ROWS, SA_W), BF16),
        grid_spec=grid_spec,
        compiler_params=_cparams(("arbitrary", "arbitrary")),
        name="dsa_att",
    )(page_table, saq, saz, k_new, v_new, sel, sel, *([cache_k] * npg), *([cache_v] * npg))


def _mem_attend(q_ref, z_ref, mk_ref, mv_ref):
    mk = mk_ref[...].astype(BF16)
    mv = mv_ref[...].astype(BF16)
    outs = []
    for h in range(MEM_H):
        hs = slice(h * MEM_DH, (h + 1) * MEM_DH)
        s = lax.dot_general(q_ref[:, hs], mk[:, hs], _NT, preferred_element_type=F32)
        p = jnp.exp(s - jnp.max(s, axis=1, keepdims=True))
        l = jnp.sum(p, axis=1, keepdims=True)
        o = jnp.dot(p.astype(BF16), mv[:, hs], preferred_element_type=F32) / l
        outs.append((o * _silu(z_ref[:, hs])).astype(BF16))
    return jnp.concatenate(outs, axis=1)


def _memattn_kernel(q_ref, z_ref, mk_ref, mv_ref, o_ref):
    o_ref[...] = _mem_attend(q_ref, z_ref, mk_ref, mv_ref)


def _memattn(memq, memz, mk, mv, nb, tq):
    rows = memq.shape[0]
    n_mem = mk.shape[0] // nb
    nq = rows // (nb * tq)
    qblk = lambda b, i: (b * nq + i, 0)
    per_b = lambda b, i: (b, 0)
    return pl.pallas_call(
        _memattn_kernel,
        out_shape=jax.ShapeDtypeStruct((rows, MEM_W), BF16),
        grid=(nb, nq),
        in_specs=[pl.BlockSpec((tq, MEM_W), qblk), pl.BlockSpec((tq, MEM_W), qblk),
                  pl.BlockSpec((n_mem, MEM_W), per_b), pl.BlockSpec((n_mem, MEM_W), per_b)],
        out_specs=pl.BlockSpec((tq, MEM_W), qblk),
        compiler_params=_cparams(("arbitrary", "arbitrary")),
        name="memattn",
    )(memq, memz, mk, mv)


def _mixout_kernel(x_ref, a_ref, b_ref, c_ref, wa_ref, wb_ref, wc_ref, g_ref, y_ref):
    acc = (jnp.dot(a_ref[...], wa_ref[...], preferred_element_type=F32)
           + jnp.dot(b_ref[...], wb_ref[...], preferred_element_type=F32)
           + jnp.dot(c_ref[...], wc_ref[...], preferred_element_type=F32))
    y = acc * lax.rsqrt(jnp.mean(acc * acc, axis=-1, keepdims=True) + EPS) * g_ref[...]
    y_ref[...] = x_ref[...] + y


def _mixout_mem_kernel(x_ref, a_ref, b_ref, q_ref, z_ref, mk_ref, mv_ref, wa_ref, wb_ref, wc_ref, g_ref, y_ref):
    c = _mem_attend(q_ref, z_ref, mk_ref, mv_ref)
    acc = (jnp.dot(a_ref[...], wa_ref[...], preferred_element_type=F32)
           + jnp.dot(b_ref[...], wb_ref[...], preferred_element_type=F32)
           + jnp.dot(c, wc_ref[...], preferred_element_type=F32))
    y = acc * lax.rsqrt(jnp.mean(acc * acc, axis=-1, keepdims=True) + EPS) * g_ref[...]
    y_ref[...] = x_ref[...] + y


def _mixout_mem(x2d, a, b, memq, memz, mk, mv, w_out, g_post, tm, seq):
    rows = x2d.shape[0]
    n_mem = mk.shape[0] // (rows // seq)
    wb16 = w_out.astype(BF16)
    row = lambda i: (i, 0)
    const = lambda i: (0, 0)
    per_req = lambda i: (i // (seq // tm), 0)
    return pl.pallas_call(
        _mixout_mem_kernel,
        out_shape=jax.ShapeDtypeStruct((rows, D_MODEL), F32),
        grid=(rows // tm,),
        in_specs=[pl.BlockSpec((tm, D_MODEL), row), pl.BlockSpec((tm, ML_W), row),
                  pl.BlockSpec((tm, SA_W), row), pl.BlockSpec((tm, MEM_W), row), pl.BlockSpec((tm, MEM_W), row),
                  pl.BlockSpec((n_mem, MEM_W), per_req), pl.BlockSpec((n_mem, MEM_W), per_req),
                  pl.BlockSpec((ML_W, D_MODEL), const), pl.BlockSpec((SA_W, D_MODEL), const),
                  pl.BlockSpec((MEM_W, D_MODEL), const), pl.BlockSpec((1, D_MODEL), const)],
        out_specs=pl.BlockSpec((tm, D_MODEL), row),
        compiler_params=_cparams(("arbitrary",)),
        name="mixout_mem",
    )(x2d, a, b, memq, memz, mk, mv, wb16[:ML_W], wb16[ML_W:ML_W + SA_W], wb16[ML_W + SA_W:],
      g_post.reshape(1, D_MODEL))


def _mixout(x2d, a, b, c, w_out, g_post, tm):
    rows = x2d.shape[0]
    wb16 = w_out.astype(BF16)
    row = lambda i: (i, 0)
    const = lambda i: (0, 0)
    return pl.pallas_call(
        _mixout_kernel,
        out_shape=jax.ShapeDtypeStruct((rows, D_MODEL), F32),
        grid=(rows // tm,),
        in_specs=[pl.BlockSpec((tm, D_MODEL), row), pl.BlockSpec((tm, ML_W), row),
                  pl.BlockSpec((tm, SA_W), row), pl.BlockSpec((tm, MEM_W), row),
                  pl.BlockSpec((ML_W, D_MODEL), const), pl.BlockSpec((SA_W, D_MODEL), const),
                  pl.BlockSpec((MEM_W, D_MODEL), const), pl.BlockSpec((1, D_MODEL), const)],
        out_specs=pl.BlockSpec((tm, D_MODEL), row),
        compiler_params=_cparams(("arbitrary",)),
        name="mixout",
    )(x2d, a, b, c, wb16[:ML_W], wb16[ML_W:ML_W + SA_W], wb16[ML_W + SA_W:], g_post.reshape(1, D_MODEL))


def _layer(x_p, x_s, st_c, st_n, st_m, c_k, c_v, c_kidx, c_mk, c_mv, page_table, mem_prompt,
           g_pre, w_in, b_gates, g_head, w_mem_k, w_mem_v, g_mem, w_out, g_post):
    nb, seq, _ = x_p.shape
    nreq, t_dec, _ = x_s.shape
    n_mem = mem_prompt.shape[1]
    n_past = page_table.shape[1] * PAGE_SIZE
    weights = _relayout_w_in(w_in)

    tm = min(512, seq)
    tabs = _rope_tables(jnp.arange(seq, dtype=jnp.int32))
    x_p2d = x_p.reshape(nb * seq, D_MODEL)
    (_, _, saq, k, v, saz, memq, memz, idxq, kidx, small, small_t) = _project(
        x_p2d, g_pre, weights, tabs, tm, with_ml=False)
    a_p, p_c, p_n, p_m = _ml_fused(x_p2d, g_pre, weights[0], weights[3], weights[4], b_gates, g_head,
                                   nb, seq, tm, min(ML_CHUNK, seq))
    b_p = _dsa_prompt(saq, saz, idxq, small_t, k, v, kidx, nb, seq)
    mk, mv = _memkv(mem_prompt.reshape(nb * n_mem, D_MODEL), g_mem, w_mem_k, w_mem_v, n_mem)
    y_p = _mixout_mem(x_p.reshape(nb * seq, D_MODEL), a_p, b_p, memq, memz, mk, mv, w_out, g_post,
                      tm, seq).reshape(nb, seq, D_MODEL)

    rws = SAMPLE_ROWS
    n_padrow = rws - t_dec
    xs_pad = jnp.concatenate([jnp.zeros((nreq, n_padrow, D_MODEL), F32), x_s], axis=1).reshape(nreq * rws, D_MODEL)
    pos_s = jnp.tile(jnp.concatenate([jnp.zeros((n_padrow,), jnp.int32),
                                      n_past + jnp.arange(t_dec, dtype=jnp.int32)]), nreq)
    tabs_s = _rope_tables(pos_s)
    (qkv_s, oz_s, saq_s, k_s, v_s, saz_s, memq_s, memz_s, idxq_s, kidx_s, small_s, small_t_s) = _project(
        xs_pad, g_pre, weights, tabs_s, nreq * rws)
    a_s, s_c, s_n, s_m = _mlstm(qkv_s, oz_s, small_s, small_t_s, b_gates, g_head, st_c, st_n, st_m,
                                nreq, rws, n_padrow)
    n_sel = min(TOPK_MAX, (n_past + t_dec) // 4)
    sel = _dsa_select(page_table, idxq_s, small_s, kidx_s, jnp.swapaxes(c_kidx, 1, 2), n_sel, t_dec)
    b_s = _dsa_attend(page_table, saq_s, saz_s, k_s, v_s, sel,
                      c_k.reshape(c_k.shape[0], PAGE_SIZE * SA_H, SA_DH),
                      c_v.reshape(c_v.shape[0], PAGE_SIZE * SA_H, SA_DH))
    c_s = _memattn(memq_s, memz_s, c_mk.reshape(nreq * n_mem, MEM_W), c_mv.reshape(nreq * n_mem, MEM_W), nreq, rws)
    y_s = _mixout(xs_pad, a_s, b_s, c_s, w_out, g_post, nreq * rws)

    def real(a2d):
        return a2d.reshape(nreq, rws, -1)[:, n_padrow:]

    new = (p_c, p_n, p_m,
           k.reshape(nb, seq, SA_H, SA_DH), v.reshape(nb, seq, SA_H, SA_DH), kidx.reshape(nb, seq, IDX_D),
           mk.reshape(nb, n_mem, MEM_H, MEM_DH), mv.reshape(nb, n_mem, MEM_H, MEM_DH),
           s_c, s_n, s_m,
           real(k_s).reshape(nreq, t_dec, SA_H, SA_DH), real(v_s).reshape(nreq, t_dec, SA_H, SA_DH), real(kidx_s))
    return y_p, real(y_s), new


def kernel(x_prompt, x_sample, state_mlstm_C, state_mlstm_n, state_mlstm_m, cache_k, cache_v, cache_kidx,
           cache_mem_k, cache_mem_v, page_table, mem_prompt, g_pre, w_in, b_gates, g_head, w_mem_k, w_mem_v,
           g_mem, w_out, g_post):
    xp, xs = x_prompt, x_sample
    per_layer = []
    for l in range(w_in.shape[0]):
        xp, xs, new = _layer(xp, xs, state_mlstm_C[l], state_mlstm_n[l], state_mlstm_m[l],
                             cache_k[l], cache_v[l], cache_kidx[l], cache_mem_k[l], cache_mem_v[l],
                             page_table, mem_prompt, g_pre[l], w_in[l], b_gates[l], g_head[l],
                             w_mem_k[l], w_mem_v[l], g_mem[l], w_out[l], g_post[l])
        per_layer.append(new)
    stacked = [jnp.stack(a) for a in zip(*per_layer)]
    return (xp, xs, *stacked)
```

```python
import functools
import math

import jax
import jax.numpy as jnp
from jax import lax
from jax.experimental import pallas as pl
from jax.experimental.pallas import tpu as pltpu

F32 = jnp.float32
BF16 = jnp.bfloat16

D_MODEL = 2048
ML_H = 4
ML_W = D_MODEL // 2
ML_DH = ML_W // ML_H
SA_H = 4
SA_W = D_MODEL // 4
SA_DH = SA_W // SA_H
MEM_H = 4
MEM_W = D_MODEL // 4
MEM_DH = MEM_W // MEM_H
IDX_H = 8
IDX_D = 64
IDX_SCALE = (IDX_H * IDX_D) ** -0.5
TOPK_MAX = 256
ROPE_THETA = 10000.0
LOG2E = 1.4426950408889634
EPS = 1e-6
PAGE_SIZE = 128

LANES = 128
CB = 512
SMALL_IG = 64
SMALL_LF = 68
SMALL_W = 72
SAMPLE_ROWS = 16
SEL_ROWS = 8
SEL_PAGES_PER_STEP = 32
ATT_PAGES_PER_STEP = 16
ML_CHUNK = 256
IDX_KEY_CHUNK = 256
DSA_KEY_STEP = 512
ML_FUSED_ROWS = 256
TIE_BLOCK = 256
ONES_ROWS = 16
VMEM_LIMIT = 56 * 1024 * 1024
VMEM_LIMIT_FUSED = 60 * 1024 * 1024
NEG_INF = float("-inf")
POS_INF = float("inf")

_NT = (((1,), (1,)), ((), ()))
_TN = (((0,), (0,)), ((), ()))


def _cparams(sem):
    return pltpu.CompilerParams(dimension_semantics=sem, vmem_limit_bytes=VMEM_LIMIT)


def _sigmoid(x):
    return 1.0 / (1.0 + jnp.exp(-x))


def _silu(x):
    return x * _sigmoid(x)


def _log_sigmoid(x):
    return jnp.minimum(x, 0.0) - jnp.log1p(jnp.exp(-jnp.abs(x)))


def _split3(x):
    hi = x.astype(BF16)
    r = x - hi.astype(F32)
    mid = r.astype(BF16)
    lo = (r - mid.astype(F32)).astype(BF16)
    return hi, mid, lo


def _rope128(x, cos, sin_signed):
    return x * cos + pltpu.roll(x, 64, 1) * sin_signed


def _rope64(x, cos, sin_signed):
    lane = lax.broadcasted_iota(jnp.int32, x.shape, 1)
    first_half = (lane % 64) < 32
    partner = jnp.where(first_half, pltpu.roll(x, 96, 1), pltpu.roll(x, 32, 1))
    return x * cos + partner * sin_signed


def _normed(x_ref, g_ref):
    x = x_ref[...]
    return (x * lax.rsqrt(jnp.mean(x * x, axis=-1, keepdims=True) + EPS) * g_ref[...]).astype(BF16)


def _proj_ml_kernel(x_ref, g_ref, w_ref, qkv_ref, oz_ref, u_ref):
    u_ref[...] = _normed(x_ref, g_ref)
    n_qkv = 3 * ML_W // CB
    for cb in range(5 * ML_W // CB):
        acc = lax.dot_general(u_ref[...], w_ref[cb * CB:(cb + 1) * CB, :], _NT, preferred_element_type=F32)
        if ML_W <= cb * CB < 2 * ML_W:
            acc = acc * (ML_DH ** -0.5)
        if cb < n_qkv:
            qkv_ref[:, cb * CB:(cb + 1) * CB] = acc.astype(BF16)
        else:
            oz_ref[:, (cb - n_qkv) * CB:(cb - n_qkv + 1) * CB] = acc


def _proj_rest_kernel(x_ref, g_ref, wsa_ref, wmem_ref, ws_ref, wst_ref, c128_ref, s128_ref, c64_ref, s64_ref,
                      saq_ref, k_ref, v_ref, saz_ref, memq_ref, memz_ref, idxq_ref,
                      kidx_ref, small_ref, smallt_ref, u_ref):
    u_ref[...] = _normed(x_ref, g_ref)
    sm = jnp.dot(u_ref[...], ws_ref[...], preferred_element_type=F32)
    small_ref[...] = sm
    kidx_ref[...] = _rope64(sm, c64_ref[...], s64_ref[...])[:, :IDX_D]
    smallt_ref[...] = lax.dot_general(wst_ref[...], u_ref[...], _NT, preferred_element_type=F32)

    def block(cb, w_ref=wsa_ref):
        return lax.dot_general(u_ref[...], w_ref[cb * CB:(cb + 1) * CB, :], _NT, preferred_element_type=F32)

    def rope_heads(acc, fn, cos_ref, sin_ref):
        return jnp.concatenate(
            [fn(acc[:, h * LANES:(h + 1) * LANES], cos_ref[...], sin_ref[...]) for h in range(CB // LANES)], axis=1)

    saq_ref[...] = (rope_heads(block(0), _rope128, c128_ref, s128_ref) * (SA_DH ** -0.5 * LOG2E)).astype(BF16)
    tm = x_ref.shape[0]
    k_acc = block(1)
    v_acc = block(2)
    for h in range(SA_H):
        lanes = slice(h * SA_DH, (h + 1) * SA_DH)
        k_ref[pl.ds(h, tm, stride=SA_H), :] = _rope128(k_acc[:, lanes], c128_ref[...], s128_ref[...])
        v_ref[pl.ds(h, tm, stride=SA_H), :] = v_acc[:, lanes]
    saz_ref[...] = block(3)
    idxq_ref[...] = rope_heads(block(4), _rope64, c64_ref, s64_ref).astype(BF16)
    memq_ref[...] = (block(0, wmem_ref) * (MEM_DH ** -0.5)).astype(BF16)
    memz_ref[...] = block(1, wmem_ref)


def _project(x2d, g_pre, weights, tabs, tm, with_ml=True):
    w_ml, w_sa, w_mem, w_small, w_small_t = weights
    rows = x2d.shape[0]
    c128, s128, c64, s64 = tabs
    ntab = c128.shape[0] // tm
    n_ml = 5 * ML_W
    row_only = lambda i: (i, 0)
    tab_map = lambda i: (i % ntab, 0)
    const = lambda i: (0, 0)
    resident = pl.Buffered(1)
    g2d = g_pre.reshape(1, D_MODEL)

    qkv, oz = (None, None) if not with_ml else pl.pallas_call(
        _proj_ml_kernel,
        out_shape=(jax.ShapeDtypeStruct((rows, 3 * ML_W), BF16),
                   jax.ShapeDtypeStruct((rows, 2 * ML_W), F32)),
        grid=(rows // tm,),
        in_specs=[pl.BlockSpec((tm, D_MODEL), row_only),
                  pl.BlockSpec((1, D_MODEL), const),
                  pl.BlockSpec((n_ml, D_MODEL), const, pipeline_mode=resident)],
        out_specs=(pl.BlockSpec((tm, 3 * ML_W), row_only), pl.BlockSpec((tm, 2 * ML_W), row_only)),
        scratch_shapes=[pltpu.VMEM((tm, D_MODEL), BF16)],
        compiler_params=_cparams(("arbitrary",)),
        name="proj_ml",
    )(x2d, g2d, w_ml)

    out_shape = (
        jax.ShapeDtypeStruct((rows, SA_W), BF16),
        jax.ShapeDtypeStruct((rows * SA_H, SA_DH), F32),
        jax.ShapeDtypeStruct((rows * SA_H, SA_DH), F32),
        jax.ShapeDtypeStruct((rows, SA_W), F32),
        jax.ShapeDtypeStruct((rows, MEM_W), BF16),
        jax.ShapeDtypeStruct((rows, MEM_W), F32),
        jax.ShapeDtypeStruct((rows, IDX_H * IDX_D), BF16),
        jax.ShapeDtypeStruct((rows, IDX_D), F32),
        jax.ShapeDtypeStruct((rows, LANES), F32),
        jax.ShapeDtypeStruct((LANES, rows), F32),
    )
    out_specs = (
        pl.BlockSpec((tm, CB), row_only),
        pl.BlockSpec((tm * SA_H, SA_DH), row_only),
        pl.BlockSpec((tm * SA_H, SA_DH), row_only),
        pl.BlockSpec((tm, CB), row_only),
        pl.BlockSpec((tm, CB), row_only),
        pl.BlockSpec((tm, CB), row_only),
        pl.BlockSpec((tm, CB), row_only),
        pl.BlockSpec((tm, IDX_D), row_only),
        pl.BlockSpec((tm, LANES), row_only),
        pl.BlockSpec((LANES, tm), lambda i: (0, i)),
    )
    in_specs = [
        pl.BlockSpec((tm, D_MODEL), row_only),
        pl.BlockSpec((1, D_MODEL), const),
        pl.BlockSpec(w_sa.shape, const, pipeline_mode=resident),
        pl.BlockSpec(w_mem.shape, const, pipeline_mode=resident),
        pl.BlockSpec((D_MODEL, LANES), const, pipeline_mode=resident),
        pl.BlockSpec((LANES, D_MODEL), const, pipeline_mode=resident),
        pl.BlockSpec((tm, LANES), tab_map),
        pl.BlockSpec((tm, LANES), tab_map),
        pl.BlockSpec((tm, LANES), tab_map),
        pl.BlockSpec((tm, LANES), tab_map),
    ]
    rest = pl.pallas_call(
        _proj_rest_kernel,
        out_shape=out_shape,
        grid=(rows // tm,),
        in_specs=in_specs,
        out_specs=out_specs,
        scratch_shapes=[pltpu.VMEM((tm, D_MODEL), BF16)],
        compiler_params=_cparams(("arbitrary",)),
        name="proj_rest",
    )(x2d, g2d, w_sa, w_mem, w_small, w_small_t, c128, s128, c64, s64)
    return (qkv, oz, *rest)


def _rope_tables(pos):
    def tab(half):
        inv = ROPE_THETA ** (-jnp.arange(half, dtype=F32) / half)
        ang = pos.astype(F32)[:, None] * inv[None, :]
        return jnp.cos(ang), jnp.sin(ang)

    c, s = tab(SA_DH // 2)
    c128 = jnp.concatenate([c, c], axis=1)
    s128 = jnp.concatenate([-s, s], axis=1)
    c, s = tab(IDX_D // 2)
    c64 = jnp.concatenate([c, c, c, c], axis=1)
    s64 = jnp.concatenate([-s, s, -s, s], axis=1)
    return c128, s128, c64, s64


def _relayout_w_in(w_in):
    off = {}
    o = 0
    for name, w in (('ml_q', ML_W), ('ml_k', ML_W), ('ml_v', ML_W), ('ml_o', ML_W), ('ml_z', ML_W),
                    ('ml_i', ML_H), ('ml_f', ML_H), ('sa_q', SA_W), ('sa_k', SA_W), ('sa_v', SA_W),
                    ('sa_z', SA_W), ('idx_q', IDX_H * IDX_D), ('idx_k', IDX_D), ('idx_w', IDX_H),
                    ('mem_q', MEM_W), ('mem_z', MEM_W)):
        off[name] = (o, w)
        o += w

    w_t = w_in.T

    def col(name):
        a, w = off[name]
        return w_t[a:a + w]

    def span(first, last):
        return w_t[off[first][0]:off[last][0] + off[last][1]].astype(BF16)

    w_ml = w_t.astype(BF16)
    w_sa = span('sa_q', 'idx_q')
    w_mem = span('mem_q', 'mem_z')
    small_t = jnp.concatenate([
        col('idx_k'), col('ml_i'), col('ml_f'), col('idx_w'),
        jnp.zeros((LANES - IDX_D - 2 * ML_H - IDX_H, D_MODEL), F32)], axis=0).astype(BF16)
    return (w_ml, w_sa, w_mem, small_t.T, small_t)


def _memkv_kernel(m_ref, g_ref, wk_ref, wv_ref, k_ref, v_ref):
    x = m_ref[...]
    u = (x * lax.rsqrt(jnp.mean(x * x, axis=-1, keepdims=True) + EPS) * g_ref[...]).astype(BF16)
    k_ref[...] = jnp.dot(u, wk_ref[...], preferred_element_type=F32)
    v_ref[...] = jnp.dot(u, wv_ref[...], preferred_element_type=F32)


def _memkv(mem2d, g_mem, wk, wv, n_mem):
    rows = mem2d.shape[0]
    row = lambda i: (i, 0)
    const = lambda i: (0, 0)
    return pl.pallas_call(
        _memkv_kernel,
        out_shape=(jax.ShapeDtypeStruct((rows, MEM_W), F32), jax.ShapeDtypeStruct((rows, MEM_W), F32)),
        grid=(rows // n_mem,),
        in_specs=[pl.BlockSpec((n_mem, D_MODEL), row), pl.BlockSpec((1, D_MODEL), const),
                  pl.BlockSpec((D_MODEL, MEM_W), const), pl.BlockSpec((D_MODEL, MEM_W), const)],
        out_specs=(pl.BlockSpec((n_mem, MEM_W), row), pl.BlockSpec((n_mem, MEM_W), row)),
        compiler_params=_cparams(("arbitrary",)),
        name="memkv",
    )(mem2d, g_mem.reshape(1, D_MODEL), wk.astype(BF16), wv.astype(BF16))


def _mlstm_chunk(q_of, k_of, v_of, gate_of, g_c, g_r, gh_ref, c_s, n_s, m_s, a_store, c, n_pad,
                 between_heads=None):
    ri = lax.broadcasted_iota(jnp.int32, (c, c), 0)
    cj = lax.broadcasted_iota(jnp.int32, (c, c), 1)
    causal = cj <= ri
    tri = jnp.where(causal, 1.0, 0.0).astype(BF16)
    tri_t = jnp.where(ri <= cj, 1.0, 0.0).astype(BF16)

    pad_c = lax.broadcasted_iota(jnp.int32, (c, LANES), 0) < n_pad
    ig_c = jnp.where(pad_c, NEG_INF, g_c)
    lf_c = jnp.where(pad_c, 0.0, _log_sigmoid(g_c))
    b_c = sum(jnp.dot(tri, p, preferred_element_type=F32) for p in _split3(lf_c))
    pad_r = lax.broadcasted_iota(jnp.int32, (SAMPLE_ROWS, c), 1) < n_pad
    ig_r = jnp.where(pad_r, NEG_INF, g_r)
    lf_r = jnp.where(pad_r, 0.0, _log_sigmoid(g_r))
    b_r = sum(jnp.dot(p, tri_t, preferred_element_type=F32) for p in _split3(lf_r))

    m_all = m_s[...]
    n_all = n_s[...]
    c_all = [c_s[h] for h in range(ML_H)]
    new_state = []
    for h in range(ML_H):
        hs = slice(h * ML_DH, (h + 1) * ML_DH)
        m_prev = m_all[h:h + 1, 0:1]
        b_t = b_c[:, SMALL_LF + h:SMALL_LF + h + 1]
        igc = ig_c[:, SMALL_IG + h:SMALL_IG + h + 1]
        b_s = b_r[ML_H + h:ML_H + h + 1, :]
        igr = ig_r[h:h + 1, :]
        a = jnp.where(causal, b_t - b_s + igr, NEG_INF)
        bm = b_t + m_prev
        m_t = jnp.maximum(bm, jnp.max(a, axis=1, keepdims=True))
        inter = jnp.exp(bm - m_t)
        dmat = jnp.exp(a - m_t)
        q, k, v = q_of(h), k_of(h), v_of(h)
        s = lax.dot_general(q, k, _NT, preferred_element_type=F32) * dmat
        if between_heads is not None:
            between_heads()
        c_h = c_all[h]
        n_h = n_all[h:h + 1, :]
        num = (jnp.dot(s.astype(BF16), v, preferred_element_type=F32)
               + inter * jnp.dot(q, c_h.astype(BF16), preferred_element_type=F32))
        qn = (jnp.sum(s, axis=1, keepdims=True)
              + inter * jnp.sum(q.astype(F32) * n_h, axis=1, keepdims=True))
        hh = num / jnp.maximum(jnp.abs(qn), jnp.exp(-m_t))
        hh = hh * lax.rsqrt(jnp.mean(hh * hh, axis=1, keepdims=True) + EPS)
        if between_heads is not None:
            between_heads()
        a_store(h, (hh * gh_ref[:, hs] * gate_of(h)).astype(BF16))

        m_new = m_t[c - 1:c, :]
        b_last = b_t[c - 1:c, :]
        w_end = jnp.exp(b_last - b_t + igc - m_new)
        decay = jnp.exp(b_last + m_prev - m_new)
        kw = k.astype(F32) * w_end
        new_state.append((decay * c_h + lax.dot_general(kw.astype(BF16), v, _TN, preferred_element_type=F32),
                          decay * n_h + jnp.sum(kw, axis=0, keepdims=True),
                          jnp.broadcast_to(m_new, (1, LANES))))
        if between_heads is not None:
            between_heads()

    for h, (c_new, n_new, m_new) in enumerate(new_state):
        c_s[h] = c_new
        n_s[h:h + 1, :] = n_new
        m_s[h:h + 1, :] = m_new


def _out_gate(o, z):
    return z / ((1.0 + jnp.exp(-o)) * (1.0 + jnp.exp(-z)))


def _head_cols(group, h):
    return slice(group * ML_W + h * ML_DH, group * ML_W + (h + 1) * ML_DH)


def _mlstm_kernel(qkv_ref, oz_ref, gc_ref, gt_ref, bcol_ref, brow_ref, gh_ref, c0_ref, n0_ref, m0_ref,
                  a_ref, cout_ref, nout_ref, mout_ref, c_s, n_s, m_s, *, c, n_pad):
    ci = pl.program_id(1)

    @pl.when(ci == 0)
    def _():
        c_s[...] = c0_ref[0]
        n_s[...] = n0_ref[0]
        m_s[...] = m0_ref[0]

    def a_store(h, value):
        a_ref[:, _head_cols(0, h)] = value

    _mlstm_chunk(lambda h: qkv_ref[:, _head_cols(0, h)], lambda h: qkv_ref[:, _head_cols(1, h)],
                 lambda h: qkv_ref[:, _head_cols(2, h)],
                 lambda h: _out_gate(oz_ref[:, _head_cols(0, h)], oz_ref[:, _head_cols(1, h)]),
                 gc_ref[...] + bcol_ref[...], gt_ref[0] + brow_ref[...], gh_ref, c_s, n_s, m_s, a_store, c, n_pad)

    @pl.when(ci == pl.num_programs(1) - 1)
    def _():
        cout_ref[0] = c_s[...]
        nout_ref[0] = n_s[...]
        mout_ref[0] = m_s[...]


def _ml_fused_kernel(x_ref, g_ref, w_ref, ws_ref, wst_ref, bcol_ref, brow_ref, gh_ref,
                     a_ref, cout_ref, nout_ref, mout_ref,
                     u_s, qkv_a, qkv_b, gate_a, gate_b, sm_a, sm_b, smt_a, smt_b, c_s, n_s, m_s, *, c, nrb, n_blocks):
    i = pl.program_id(0)
    tm = x_ref.shape[0]

    @pl.when(i == 0)
    def _():
        qkv_b[...] = jnp.zeros(qkv_b.shape, BF16)
        gate_b[...] = jnp.zeros(gate_b.shape, F32)
        sm_b[...] = jnp.zeros(sm_b.shape, F32)
        smt_b[...] = jnp.zeros(smt_b.shape, F32)

    @pl.when((i == 0) | (lax.rem(jnp.maximum(i - 1, 0), nrb) == 0))
    def _():
        c_s[...] = jnp.zeros(c_s.shape, F32)
        n_s[...] = jnp.zeros(n_s.shape, F32)
        m_s[...] = jnp.zeros(m_s.shape, F32)

    def projection_pieces(qkv_w, gate_w, sm_w, smt_w):
        pw = ML_DH

        def block(row0):
            return lax.dot_general(u_s[...], w_ref[row0:row0 + pw, :], _NT, preferred_element_type=F32)

        def narrow():
            sm_w[...] = jnp.dot(u_s[...], ws_ref[...], preferred_element_type=F32)
            smt_w[...] = lax.dot_general(wst_ref[SMALL_IG:SMALL_IG + SAMPLE_ROWS, :], u_s[...], _NT,
                                         preferred_element_type=F32)

        def qkv_block(cb):
            acc = block(cb * pw)
            if ML_W <= cb * pw < 2 * ML_W:
                acc = acc * (ML_DH ** -0.5)
            qkv_w[:, cb * pw:(cb + 1) * pw] = acc.astype(BF16)

        def gate_block(cb):
            gate_w[:, cb * pw:(cb + 1) * pw] = _out_gate(block(3 * ML_W + cb * pw), block(4 * ML_W + cb * pw))

        pieces = [narrow]
        pieces += [functools.partial(qkv_block, cb) for cb in range(3 * ML_W // pw)]
        pieces += [functools.partial(gate_block, cb) for cb in range(ML_W // pw)]
        return pieces

    def step(write, read):
        pieces = []
        if write is not None:
            u_s[...] = _normed(x_ref, g_ref)
            pieces = projection_pieces(*write)
        n_slots = (tm // c) * ML_H * 3
        n_pieces = len(pieces)
        slot = [0]

        def emit():
            slot[0] += 1
            while n_pieces - len(pieces) < (slot[0] * n_pieces) // n_slots:
                pieces.pop(0)()

        qkv_r, gate_r, sm_r, smt_r = read if read is not None else (None,) * 4
        for ck in range(tm // c if read is not None else 0):
            rows = slice(ck * c, (ck + 1) * c)

            def a_store(h, value, rows=rows):
                a_ref[rows, _head_cols(0, h)] = value

            _mlstm_chunk(lambda h, rows=rows: qkv_r[rows, _head_cols(0, h)],
                         lambda h, rows=rows: qkv_r[rows, _head_cols(1, h)],
                         lambda h, rows=rows: qkv_r[rows, _head_cols(2, h)],
                         lambda h, rows=rows: gate_r[rows, _head_cols(0, h)],
                         sm_r[rows, :] + bcol_ref[...], smt_r[:, rows] + brow_ref[...],
                         gh_ref, c_s, n_s, m_s, a_store, c, 0, between_heads=emit)
        while pieces:
            pieces.pop(0)()

    set_a = (qkv_a, gate_a, sm_a, smt_a)
    set_b = (qkv_b, gate_b, sm_b, smt_b)

    @pl.when(lax.rem(i, 2) == 0)
    def _():
        step(set_a, set_b)

    @pl.when(lax.rem(i, 2) == 1)
    def _():
        step(set_b, set_a)

    @pl.when((i >= 1) & (lax.rem(jnp.maximum(i - 1, 0), nrb) == nrb - 1))
    def _():
        cout_ref[0] = c_s[...]
        nout_ref[0] = n_s[...]
        mout_ref[0] = m_s[...]


def _ml_fused(x2d, g_pre, w_all, w_small, w_small_t, b_gates, g_head, nb, seq, tm, c):
    rows = x2d.shape[0]
    nrb = seq // tm
    n_blocks = rows // tm
    n_ml = 5 * ML_W
    bias_col = jnp.zeros((1, LANES), F32).at[0, SMALL_IG:SMALL_IG + 2 * ML_H].set(b_gates)
    bias_row = jnp.zeros((SAMPLE_ROWS, 1), F32).at[:2 * ML_H, 0].set(b_gates)
    const = lambda i: (0, 0)
    resident = pl.Buffered(1)
    lagged = lambda i: jnp.maximum(i - 1, 0)
    out_shape = (
        jax.ShapeDtypeStruct((rows, ML_W), BF16),
        jax.ShapeDtypeStruct((nb, ML_H, ML_DH, ML_DH), F32),
        jax.ShapeDtypeStruct((nb, ML_H, ML_DH), F32),
        jax.ShapeDtypeStruct((nb, 8, LANES), F32),
    )
    a, c_out, n_out, m_out = pl.pallas_call(
        functools.partial(_ml_fused_kernel, c=c, nrb=nrb, n_blocks=n_blocks),
        out_shape=out_shape,
        grid=(n_blocks + 1,),
        in_specs=[
            pl.BlockSpec((tm, D_MODEL), lambda i: (jnp.minimum(i, n_blocks - 1), 0)),
            pl.BlockSpec((1, D_MODEL), const),
            pl.BlockSpec((n_ml, D_MODEL), const, pipeline_mode=resident),
            pl.BlockSpec((D_MODEL, LANES), const, pipeline_mode=resident),
            pl.BlockSpec((LANES, D_MODEL), const, pipeline_mode=resident),
            pl.BlockSpec((1, LANES), const),
            pl.BlockSpec((SAMPLE_ROWS, 1), const),
            pl.BlockSpec((1, ML_W), const),
        ],
        out_specs=(
            pl.BlockSpec((tm, ML_W), lambda i: (lagged(i), 0)),
            pl.BlockSpec((1, ML_H, ML_DH, ML_DH), lambda i: (lagged(i) // nrb, 0, 0, 0)),
            pl.BlockSpec((1, ML_H, ML_DH), lambda i: (lagged(i) // nrb, 0, 0)),
            pl.BlockSpec((1, 8, LANES), lambda i: (lagged(i) // nrb, 0, 0)),
        ),
        scratch_shapes=[pltpu.VMEM((tm, D_MODEL), BF16),
                        pltpu.VMEM((tm, 3 * ML_W), BF16), pltpu.VMEM((tm, 3 * ML_W), BF16),
                        pltpu.VMEM((tm, ML_W), F32), pltpu.VMEM((tm, ML_W), F32),
                        pltpu.VMEM((tm, LANES), F32), pltpu.VMEM((tm, LANES), F32),
                        pltpu.VMEM((SAMPLE_ROWS, tm), F32), pltpu.VMEM((SAMPLE_ROWS, tm), F32),
                        pltpu.VMEM((ML_H, ML_DH, ML_DH), F32), pltpu.VMEM((ML_H, ML_DH), F32),
                        pltpu.VMEM((8, LANES), F32)],
        compiler_params=pltpu.CompilerParams(dimension_semantics=("arbitrary",), vmem_limit_bytes=VMEM_LIMIT_FUSED),
        name="ml_fused",
    )(x2d, g_pre.reshape(1, D_MODEL), w_all, w_small, w_small_t, bias_col, bias_row, g_head.reshape(1, ML_W))
    return a, c_out, n_out, m_out[:, :ML_H, 0]


def _mlstm(qkv, oz, small, small_t, b_gates, g_head, c0, n0, m0, nb, c, n_pad):
    rows = qkv.shape[0]
    nc = rows // (nb * c)
    bias_col = jnp.zeros((1, LANES), F32).at[0, SMALL_IG:SMALL_IG + 2 * ML_H].set(b_gates)
    bias_row = jnp.zeros((SAMPLE_ROWS, 1), F32).at[:2 * ML_H, 0].set(b_gates)
    m0b = jnp.zeros((nb, 8, LANES), F32).at[:, :ML_H, :].set(jnp.broadcast_to(m0[:, :, None], (nb, ML_H, LANES)))
    rowblk = lambda b, i: (b * nc + i, 0)
    const = lambda b, i: (0, 0)
    gates_t = small_t[SMALL_IG:SMALL_IG + SAMPLE_ROWS].reshape(SAMPLE_ROWS, rows // c, c).transpose(1, 0, 2)
    out_shape = (
        jax.ShapeDtypeStruct((rows, ML_W), BF16),
        jax.ShapeDtypeStruct((nb, ML_H, ML_DH, ML_DH), F32),
        jax.ShapeDtypeStruct((nb, ML_H, ML_DH), F32),
        jax.ShapeDtypeStruct((nb, 8, LANES), F32),
    )
    st4 = lambda b, i: (b, 0, 0, 0)
    st3 = lambda b, i: (b, 0, 0)
    a, c_out, n_out, m_out = pl.pallas_call(
        functools.partial(_mlstm_kernel, c=c, n_pad=n_pad),
        out_shape=out_shape,
        grid=(nb, nc),
        in_specs=[
            pl.BlockSpec((c, 3 * ML_W), rowblk),
            pl.BlockSpec((c, 2 * ML_W), rowblk),
            pl.BlockSpec((c, LANES), rowblk),
            pl.BlockSpec((1, SAMPLE_ROWS, c), lambda b, i: (b * nc + i, 0, 0)),
            pl.BlockSpec((1, LANES), const),
            pl.BlockSpec((SAMPLE_ROWS, 1), const),
            pl.BlockSpec((1, ML_W), const),
            pl.BlockSpec((1, ML_H, ML_DH, ML_DH), st4),
            pl.BlockSpec((1, ML_H, ML_DH), st3),
            pl.BlockSpec((1, 8, LANES), st3),
        ],
        out_specs=(
            pl.BlockSpec((c, ML_W), rowblk),
            pl.BlockSpec((1, ML_H, ML_DH, ML_DH), st4),
            pl.BlockSpec((1, ML_H, ML_DH), st3),
            pl.BlockSpec((1, 8, LANES), st3),
        ),
        scratch_shapes=[pltpu.VMEM((ML_H, ML_DH, ML_DH), F32), pltpu.VMEM((ML_H, ML_DH), F32),
                        pltpu.VMEM((8, LANES), F32)],
        compiler_params=_cparams(("arbitrary", "arbitrary")),
        name="mlstm",
    )(qkv, oz, small, gates_t, bias_col, bias_row, g_head.reshape(1, ML_W), c0, n0, m0b)
    return a, c_out, n_out, m_out[:, :ML_H, 0]


_REDUCERS = {"sum": (jnp.sum, jnp.add), "max": (jnp.max, jnp.maximum), "min": (jnp.min, jnp.minimum)}
REDUCE_CHAINS = 8


def _reduce(x, axis, op):
    fn, combine = _REDUCERS[op]
    unit = 8 if axis == 0 else LANES
    n = x.shape[axis]
    units = n // unit
    if n % unit or units < 2 * REDUCE_CHAINS:
        return fn(x, axis=axis, keepdims=True)
    base, rem = divmod(units, REDUCE_CHAINS)
    parts, start = [], 0
    for i in range(REDUCE_CHAINS):
        size = (base + (1 if i < rem else 0)) * unit
        piece = x[start:start + size] if axis == 0 else x[:, start:start + size]
        parts.append(fn(piece, axis=axis, keepdims=True))
        start += size
    while len(parts) > 1:
        parts = [combine(parts[i], parts[i + 1]) for i in range(0, len(parts), 2)]
    return parts[0]


def _count(pred, axis):
    return _reduce(jnp.where(pred, 1.0, 0.0), axis, "sum")


def _kth_largest(x_ref, k, axis, n_bisect, quarter_steps=False):
    kf = float(k)
    x = x_ref[...]
    hi = _reduce(x, axis, "max")
    lo = _reduce(jnp.where(x == NEG_INF, POS_INF, x), axis, "min")

    def bisect(_, carry):
        lo, hi = carry
        mid = 0.5 * (lo + hi)
        ge = _count(x_ref[...] >= mid, axis) >= kf
        return jnp.where(ge, mid, lo), jnp.where(ge, hi, mid)

    def quarter(_, carry):
        lo, hi = carry
        w = hi - lo
        m1, m2, m3 = lo + 0.25 * w, lo + 0.5 * w, lo + 0.75 * w
        xx = x_ref[...]
        g1, g2, g3 = (_count(xx >= m, axis) >= kf for m in (m1, m2, m3))
        lo = jnp.where(g3, m3, jnp.where(g2, m2, jnp.where(g1, m1, lo)))
        hi = jnp.where(g3, hi, jnp.where(g2, m3, jnp.where(g1, m2, m1)))
        return lo, hi

    if quarter_steps:
        lo, hi = lax.fori_loop(0, (n_bisect + 1) // 2, quarter, (lo, hi))
    else:
        lo, hi = lax.fori_loop(0, n_bisect, bisect, (lo, hi))

    def finished(cmin, xx):
        return jnp.where((_count(xx > cmin, axis) < kf) | (cmin == POS_INF), 1.0, 0.0)

    xx = x_ref[...]
    thr = _reduce(jnp.where(xx >= lo, xx, POS_INF), axis, "min")
    done = finished(thr, xx)

    def cond(st):
        return st[2] < 0.5

    def body(st):
        thr, done, _ = st
        xx = x_ref[...]
        cmin = _reduce(jnp.where(xx > thr, xx, POS_INF), axis, "min")
        thr = jnp.where(done < 0.5, cmin, thr)
        done = jnp.maximum(done, finished(thr, xx))
        return thr, done, jnp.min(done)

    thr, _, _ = lax.while_loop(cond, body, (thr, done, jnp.min(done)))
    return thr


def _dsa_kernel(q_ref, z_ref, idxq_ref, wt_ref, k_ref, v_ref, kidx_ref, o_ref,
                kb_s, vt_s, kib_s, x_s, sel_s, *, n_keys, qb, n_sel, n_bisect, key_step):
    j = pl.program_id(1)

    @pl.when(j == 0)
    def _():
        for h in range(SA_H):
            lanes = slice(h * SA_DH, (h + 1) * SA_DH)
            kb_s[:, lanes] = k_ref[pl.ds(h, n_keys, stride=SA_H), :].astype(BF16)
            vt_s[h, 0:SA_DH, :] = v_ref[pl.ds(h, n_keys, stride=SA_H), :].T.astype(BF16)
            vt_s[h, SA_DH:SA_DH + ONES_ROWS, :] = jnp.ones((ONES_ROWS, n_keys), BF16)
        kib_s[...] = kidx_ref[...].astype(BF16)

    def attend(nk):
        xs = x_s.at[0:nk]
        ss = sel_s.at[0:nk]
        key = lax.broadcasted_iota(jnp.int32, (nk, qb), 0)
        qpos = j * qb + lax.broadcasted_iota(jnp.int32, (nk, qb), 1)
        valid = key <= qpos
        qcat = jnp.concatenate([idxq_ref[:, h * IDX_D:(h + 1) * IDX_D] for h in range(IDX_H)], axis=0)
        w_rows = [wt_ref[h:h + 1, :] * IDX_SCALE for h in range(IDX_H)]
        kc = math.gcd(IDX_KEY_CHUNK, nk)
        for c0 in range(0, nk, kc):
            d = lax.dot_general(kib_s[c0:c0 + kc, :], qcat, _NT, preferred_element_type=F32)
            sc = jnp.zeros((kc, qb), F32)
            for h in range(IDX_H):
                sc = sc + jnp.maximum(d[:, h * qb:(h + 1) * qb], 0.0) * w_rows[h]
            ok = (c0 + lax.broadcasted_iota(jnp.int32, (kc, qb), 0)) <= (
                j * qb + lax.broadcasted_iota(jnp.int32, (kc, qb), 1))
            x_s[c0:c0 + kc, :] = jnp.where(ok, sc, NEG_INF)
            sel_s[c0:c0 + kc, :] = jnp.where(ok, 1.0, 0.0)

        @pl.when((j + 1) * qb > n_sel)
        def _():
            kf = float(n_sel)
            thr = _kth_largest(xs, n_sel, 0, n_bisect)
            x = xs[...]
            need = kf - _count(x > thr, 0)
            n_tie = _count(x == thr, 0)
            qrow = j * qb + lax.broadcasted_iota(jnp.int32, (1, qb), 1)
            small = (qrow + 1) <= n_sel
            ss[...] = jnp.where(small, jnp.where(valid, 1.0, 0.0), jnp.where(x >= thr, 1.0, 0.0))
            excess = jnp.max(jnp.where((n_tie > need) & jnp.logical_not(small), 1.0, 0.0))

            @pl.when(excess > 0.5)
            def _():
                tb = math.gcd(TIE_BLOCK, nk)
                r_i = lax.broadcasted_iota(jnp.int32, (tb, tb), 0)
                c_i = lax.broadcasted_iota(jnp.int32, (tb, tb), 1)
                lower = jnp.where(c_i < r_i, 1.0, 0.0).astype(BF16)
                carry = jnp.zeros((1, qb), F32)
                for blk in range(nk // tb):
                    rows = slice(blk * tb, (blk + 1) * tb)
                    xb = x_s[rows, :]
                    tie = jnp.where(xb == thr, 1.0, 0.0)
                    rank = jnp.dot(lower, tie.astype(BF16), preferred_element_type=F32) + carry
                    keep = (xb > thr) | ((xb == thr) & (rank < need))
                    keyb = blk * tb + lax.broadcasted_iota(jnp.int32, (tb, qb), 0)
                    qposb = j * qb + lax.broadcasted_iota(jnp.int32, (tb, qb), 1)
                    smallb = jnp.where(keyb <= qposb, 1.0, 0.0)
                    sel_s[rows, :] = jnp.where(small, smallb, jnp.where(keep, 1.0, 0.0))
                    carry = carry + jnp.sum(tie, axis=0, keepdims=True)

        sel = ss[...] > 0.5
        heads = range(SA_H)
        hsl = [slice(h * SA_DH, (h + 1) * SA_DH) for h in heads]
        st = [jnp.where(sel, lax.dot_general(kb_s[0:nk, hsl[h]], q_ref[:, hsl[h]], _NT,
                                             preferred_element_type=F32), NEG_INF) for h in heads]
        mx = [_reduce(st[h], 0, "max") for h in heads]
        p = [jnp.exp2(st[h] - mx[h]).astype(BF16) for h in heads]
        pv = [jnp.dot(vt_s[h, :, 0:nk], p[h], preferred_element_type=F32) for h in heads]
        ot = [pv[h][0:SA_DH, :] / pv[h][SA_DH:SA_DH + 1, :] for h in heads]
        for h in heads:
            o_ref[:, hsl[h]] = (ot[h].T * _silu(z_ref[:, hsl[h]])).astype(BF16)

    n_ext = n_keys // key_step
    for e in range(n_ext):
        nk = (e + 1) * key_step
        lo_j = e * key_step // qb
        hi_j = nk // qb

        @pl.when((j >= lo_j) & (j < hi_j))
        def _(nk=nk):
            attend(nk)


def _dsa_prompt(saq, saz, idxq, small_t, k, v, kidx, nb, seq):
    rows = saq.shape[0]
    qb = min(seq, 128)
    nq = seq // qb
    n_sel = min(TOPK_MAX, seq // 4)
    qblk = lambda b, j: (b * nq + j, 0)
    per_b = lambda b, j: (b, 0)
    wt_blk = SMALL_W // 8
    return pl.pallas_call(
        functools.partial(_dsa_kernel, n_keys=seq, qb=qb, n_sel=n_sel, n_bisect=20, key_step=min(DSA_KEY_STEP, seq)),
        out_shape=jax.ShapeDtypeStruct((rows, SA_W), BF16),
        grid=(nb, nq),
        in_specs=[
            pl.BlockSpec((qb, SA_W), qblk),
            pl.BlockSpec((qb, SA_W), qblk),
            pl.BlockSpec((qb, IDX_H * IDX_D), qblk),
            pl.BlockSpec((8, qb), lambda b, j: (wt_blk, b * nq + j)),
            pl.BlockSpec((seq * SA_H, SA_DH), per_b),
            pl.BlockSpec((seq * SA_H, SA_DH), per_b),
            pl.BlockSpec((seq, IDX_D), per_b),
        ],
        out_specs=pl.BlockSpec((qb, SA_W), qblk),
        scratch_shapes=[pltpu.VMEM((seq, SA_W), BF16), pltpu.VMEM((SA_H, SA_DH + ONES_ROWS, seq), BF16),
                        pltpu.VMEM((seq, IDX_D), BF16), pltpu.VMEM((seq, qb), F32),
                        pltpu.VMEM((seq, qb), F32)],
        compiler_params=_cparams(("arbitrary", "arbitrary")),
        name="dsa",
    )(saq, saz, idxq, small_t, k, v, kidx)


def _sel_kernel(pt_ref, idxq_ref, small_ref, kinew_ref, *rest, npg, n_past, n_sel, n_real, n_bisect):
    page_refs = rest[:npg]
    sel_ref, x_s = rest[npg:]
    g = pl.program_id(1)
    rws = SAMPLE_ROWS
    top = rws - SEL_ROWS
    pk = n_past + LANES
    gk = npg * PAGE_SIZE

    qs = jnp.concatenate([idxq_ref[:, h * IDX_D:(h + 1) * IDX_D] for h in range(IDX_H)], axis=0)
    small = small_ref[...]

    def scores(d):
        sc = jnp.zeros((SEL_ROWS, d.shape[1]), F32)
        for h in range(IDX_H):
            w = small[top:, SMALL_W + h:SMALL_W + h + 1] * IDX_SCALE
            sc = sc + jnp.maximum(d[h * rws + top:(h + 1) * rws, :], 0.0) * w
        return sc

    kp_t = jnp.concatenate([r[0] for r in page_refs], axis=1).astype(BF16)
    x_s[:, pl.ds(pl.multiple_of(g * gk, LANES), gk)] = scores(jnp.dot(qs, kp_t, preferred_element_type=F32))

    @pl.when(g == pl.num_programs(1) - 1)
    def _():
        knew = jnp.concatenate([kinew_ref[...], jnp.zeros((LANES - rws, IDX_D), F32)], axis=0).astype(BF16)
        row = top + lax.broadcasted_iota(jnp.int32, (SEL_ROWS, LANES), 0)
        col = lax.broadcasted_iota(jnp.int32, (SEL_ROWS, LANES), 1)
        ok = (col >= rws - n_real) & (col < rws) & (col <= row)
        d_new = lax.dot_general(qs, knew, _NT, preferred_element_type=F32)
        x_s[:, n_past:pk] = jnp.where(ok, scores(d_new), NEG_INF)

        kf = float(n_sel)
        thr = _kth_largest(x_s, n_sel, 1, n_bisect, quarter_steps=True)
        x = x_s[...]
        need = kf - _count(x > thr, 1)
        n_tie = _count(x == thr, 1)
        sel_ref[0, 0:top, :] = jnp.ones((top, pk), F32)
        sel_ref[0, top:rws, :] = jnp.where(x >= thr, 1.0, 0.0)
        real = lax.broadcasted_iota(jnp.int32, (SEL_ROWS, 1), 0) >= SEL_ROWS - n_real
        excess = jnp.max(jnp.where((n_tie > need) & real, 1.0, 0.0))

        @pl.when(excess > 0.5)
        def _():
            r_i = lax.broadcasted_iota(jnp.int32, (LANES, LANES), 0)
            c_i = lax.broadcasted_iota(jnp.int32, (LANES, LANES), 1)
            upper = jnp.where(r_i < c_i, 1.0, 0.0).astype(BF16)

            def blk(i, carry):
                cols = pl.ds(pl.multiple_of(i * LANES, LANES), LANES)
                xb = x_s[:, cols]
                tie = jnp.where(xb == thr, 1.0, 0.0)
                rank = jnp.dot(tie.astype(BF16), upper, preferred_element_type=F32) + carry
                keep = (xb > thr) | ((xb == thr) & (rank < need))
                sel_ref[0, top:rws, cols] = jnp.where(keep, 1.0, 0.0)
                return carry + jnp.sum(tie, axis=1, keepdims=True)

            lax.fori_loop(0, pk // LANES, blk, jnp.zeros((SEL_ROWS, 1), F32))


def _dsa_select(page_table, idxq, small, kidx_new, cache_kidx, n_sel, n_real):
    nreq, n_pages = page_table.shape
    n_past = n_pages * PAGE_SIZE
    pk = n_past + LANES
    npg = min(SEL_PAGES_PER_STEP, n_pages)
    req = lambda b, g, pt: (b, 0)

    def page_map(i):
        return lambda b, g, pt: (pt[b, g * npg + i], 0, 0)

    grid_spec = pltpu.PrefetchScalarGridSpec(
        num_scalar_prefetch=1,
        grid=(nreq, n_pages // npg),
        in_specs=[pl.BlockSpec((SAMPLE_ROWS, IDX_H * IDX_D), req),
                  pl.BlockSpec((SAMPLE_ROWS, LANES), req),
                  pl.BlockSpec((SAMPLE_ROWS, IDX_D), req)]
                 + [pl.BlockSpec((1, IDX_D, PAGE_SIZE), page_map(i)) for i in range(npg)],
        out_specs=pl.BlockSpec((1, SAMPLE_ROWS, pk), lambda b, g, pt: (b, 0, 0)),
        scratch_shapes=[pltpu.VMEM((SEL_ROWS, pk), F32)],
    )
    assert n_real <= SEL_ROWS
    return pl.pallas_call(
        functools.partial(_sel_kernel, npg=npg, n_past=n_past, n_sel=n_sel, n_real=n_real, n_bisect=20),
        out_shape=jax.ShapeDtypeStruct((nreq, SAMPLE_ROWS, pk), F32),
        grid_spec=grid_spec,
        compiler_params=_cparams(("arbitrary", "arbitrary")),
        name="dsa_sel",
    )(page_table, idxq, small, kidx_new, *([cache_kidx] * npg))


def _att_kernel(pt_ref, q_ref, z_ref, knew_ref, vnew_ref, sel_ref, seltail_ref, *rest, npg):
    k_refs = rest[:npg]
    v_refs = rest[npg:2 * npg]
    o_ref = rest[2 * npg]
    m_s, l_s, acc_s = rest[2 * npg + 1:]
    g = pl.program_id(1)
    rws = SAMPLE_ROWS
    floor = -1e30

    @pl.when(g == 0)
    def _():
        m_s[...] = jnp.full(m_s.shape, floor, F32)
        l_s[...] = jnp.zeros(l_s.shape, F32)
        acc_s[...] = jnp.zeros(acc_s.shape, F32)

    heads = range(SA_H)
    hsl = [slice(h * SA_DH, (h + 1) * SA_DH) for h in heads]

    def update(kbs, vbs, keep):
        m_old = [m_s[h][:, 0:1] for h in heads]
        l_old = [l_s[h][:, 0:1] for h in heads]
        acc_old = [acc_s[:, hsl[h]] for h in heads]
        s = [lax.dot_general(q_ref[:, hsl[h]], kbs[h], _NT, preferred_element_type=F32) for h in heads]
        m_new = [jnp.maximum(m_old[h], jnp.max(jnp.where(keep, s[h], floor), axis=1, keepdims=True)) for h in heads]
        p = [jnp.where(keep, jnp.exp2(s[h] - m_new[h]), 0.0) for h in heads]
        pv = [jnp.dot(p[h].astype(BF16), vbs[h], preferred_element_type=F32) for h in heads]
        alpha = [jnp.exp2(m_old[h] - m_new[h]) for h in heads]
        l_new = [alpha[h] * l_old[h] + jnp.sum(p[h], axis=1, keepdims=True) for h in heads]
        for h in heads:
            acc_s[:, hsl[h]] = alpha[h] * acc_old[h] + pv[h]
            l_s[h] = jnp.broadcast_to(l_new[h], (rws, LANES))
            m_s[h] = jnp.broadcast_to(m_new[h], (rws, LANES))

    def head_rows(refs, h):
        return jnp.concatenate([r[0, pl.ds(h, PAGE_SIZE, stride=SA_H), :] for r in refs], axis=0).astype(BF16)

    update([head_rows(k_refs, h) for h in heads], [head_rows(v_refs, h) for h in heads], sel_ref[0] > 0.5)

    @pl.when(g == pl.num_programs(1) - 1)
    def _():
        update([knew_ref[pl.ds(h, rws, stride=SA_H), :].astype(BF16) for h in heads],
               [vnew_ref[pl.ds(h, rws, stride=SA_H), :].astype(BF16) for h in heads],
               seltail_ref[0][:, :rws] > 0.5)
        for h in heads:
            o_ref[:, hsl[h]] = (acc_s[:, hsl[h]] / l_s[h][:, 0:1] * _silu(z_ref[:, hsl[h]])).astype(BF16)


def _dsa_attend(page_table, saq, saz, k_new, v_new, sel, cache_k, cache_v):
    nreq, n_pages = page_table.shape
    n_past = n_pages * PAGE_SIZE
    npg = min(ATT_PAGES_PER_STEP, n_pages)
    prow = PAGE_SIZE * SA_H
    req = lambda b, g, pt: (b, 0)

    def page_map(i):
        return lambda b, g, pt: (pt[b, g * npg + i], 0, 0)

    page_specs = [pl.BlockSpec((1, prow, SA_DH), page_map(i)) for i in range(npg)]
    grid_spec = pltpu.PrefetchScalarGridSpec(
        num_scalar_prefetch=1,
        grid=(nreq, n_pages // npg),
        in_specs=[pl.BlockSpec((SAMPLE_ROWS, SA_W), req), pl.BlockSpec((SAMPLE_ROWS, SA_W), req),
                  pl.BlockSpec((SAMPLE_ROWS * SA_H, SA_DH), req), pl.BlockSpec((SAMPLE_ROWS * SA_H, SA_DH), req),
                  pl.BlockSpec((1, SAMPLE_ROWS, npg * PAGE_SIZE), lambda b, g, pt: (b, 0, g)),
                  pl.BlockSpec((1, SAMPLE_ROWS, LANES), lambda b, g, pt: (b, 0, n_past // LANES))]
                 + page_specs + page_specs,
        out_specs=pl.BlockSpec((SAMPLE_ROWS, SA_W), req),
        scratch_shapes=[pltpu.VMEM((SA_H, SAMPLE_ROWS, LANES), F32), pltpu.VMEM((SA_H, SAMPLE_ROWS, LANES), F32),
                        pltpu.VMEM((SAMPLE_ROWS, SA_W), F32)],
    )
    return pl.pallas_call(
        functools.partial(_att_kernel, npg=npg),
        out_shape=jax.ShapeDtypeStruct((nreq * SAMPLE_ROWS, SA_W), BF16),
        grid_spec=grid_spec,
        compiler_params=_cparams(("arbitrary", "arbitrary")),
        name="dsa_att",
    )(page_table, saq, saz, k_new, v_new, sel, sel, *([cache_k] * npg), *([cache_v] * npg))


def _mem_attend(q_ref, z_ref, mk_ref, mv_ref):
    mk = mk_ref[...].astype(BF16)
    mv = mv_ref[...].astype(BF16)
    outs = []
    for h in range(MEM_H):
        hs = slice(h * MEM_DH, (h + 1) * MEM_DH)
        s = lax.dot_general(q_ref[:, hs], mk[:, hs], _NT, preferred_element_type=F32)
        p = jnp.exp(s - jnp.max(s, axis=1, keepdims=True))
        l = jnp.sum(p, axis=1, keepdims=True)
        o = jnp.dot(p.astype(BF16), mv[:, hs], preferred_element_type=F32) / l
        outs.append((o * _silu(z_ref[:, hs])).astype(BF16))
    return jnp.concatenate(outs, axis=1)


def _memattn_kernel(q_ref, z_ref, mk_ref, mv_ref, o_ref):
    o_ref[...] = _mem_attend(q_ref, z_ref, mk_ref, mv_ref)


def _memattn(memq, memz, mk, mv, nb, tq):
    rows = memq.shape[0]
    n_mem = mk.shape[0] // nb
    nq = rows // (nb * tq)
    qblk = lambda b, i: (b * nq + i, 0)
    per_b = lambda b, i: (b, 0)
    return pl.pallas_call(
        _memattn_kernel,
        out_shape=jax.ShapeDtypeStruct((rows, MEM_W), BF16),
        grid=(nb, nq),
        in_specs=[pl.BlockSpec((tq, MEM_W), qblk), pl.BlockSpec((tq, MEM_W), qblk),
                  pl.BlockSpec((n_mem, MEM_W), per_b), pl.BlockSpec((n_mem, MEM_W), per_b)],
        out_specs=pl.BlockSpec((tq, MEM_W), qblk),
        compiler_params=_cparams(("arbitrary", "arbitrary")),
        name="memattn",
    )(memq, memz, mk, mv)


def _mixout_kernel(x_ref, a_ref, b_ref, c_ref, wa_ref, wb_ref, wc_ref, g_ref, y_ref):
    acc = (jnp.dot(a_ref[...], wa_ref[...], preferred_element_type=F32)
           + jnp.dot(b_ref[...], wb_ref[...], preferred_element_type=F32)
           + jnp.dot(c_ref[...], wc_ref[...], preferred_element_type=F32))
    y = acc * lax.rsqrt(jnp.mean(acc * acc, axis=-1, keepdims=True) + EPS) * g_ref[...]
    y_ref[...] = x_ref[...] + y


def _mixout_mem_kernel(x_ref, a_ref, b_ref, q_ref, z_ref, mk_ref, mv_ref, wa_ref, wb_ref, wc_ref, g_ref, y_ref):
    c = _mem_attend(q_ref, z_ref, mk_ref, mv_ref)
    acc = (jnp.dot(a_ref[...], wa_ref[...], preferred_element_type=F32)
           + jnp.dot(b_ref[...], wb_ref[...], preferred_element_type=F32)
           + jnp.dot(c, wc_ref[...], preferred_element_type=F32))
    y = acc * lax.rsqrt(jnp.mean(acc * acc, axis=-1, keepdims=True) + EPS) * g_ref[...]
    y_ref[...] = x_ref[...] + y


def _mixout_mem(x2d, a, b, memq, memz, mk, mv, w_out, g_post, tm, seq):
    rows = x2d.shape[0]
    n_mem = mk.shape[0] // (rows // seq)
    wb16 = w_out.astype(BF16)
    row = lambda i: (i, 0)
    const = lambda i: (0, 0)
    per_req = lambda i: (i // (seq // tm), 0)
    return pl.pallas_call(
        _mixout_mem_kernel,
        out_shape=jax.ShapeDtypeStruct((rows, D_MODEL), F32),
        grid=(rows // tm,),
        in_specs=[pl.BlockSpec((tm, D_MODEL), row), pl.BlockSpec((tm, ML_W), row),
                  pl.BlockSpec((tm, SA_W), row), pl.BlockSpec((tm, MEM_W), row), pl.BlockSpec((tm, MEM_W), row),
                  pl.BlockSpec((n_mem, MEM_W), per_req), pl.BlockSpec((n_mem, MEM_W), per_req),
                  pl.BlockSpec((ML_W, D_MODEL), const), pl.BlockSpec((SA_W, D_MODEL), const),
                  pl.BlockSpec((MEM_W, D_MODEL), const), pl.BlockSpec((1, D_MODEL), const)],
        out_specs=pl.BlockSpec((tm, D_MODEL), row),
        compiler_params=_cparams(("arbitrary",)),
        name="mixout_mem",
    )(x2d, a, b, memq, memz, mk, mv, wb16[:ML_W], wb16[ML_W:ML_W + SA_W], wb16[ML_W + SA_W:],
      g_post.reshape(1, D_MODEL))


def _mixout(x2d, a, b, c, w_out, g_post, tm):
    rows = x2d.shape[0]
    wb16 = w_out.astype(BF16)
    row = lambda i: (i, 0)
    const = lambda i: (0, 0)
    return pl.pallas_call(
        _mixout_kernel,
        out_shape=jax.ShapeDtypeStruct((rows, D_MODEL), F32),
        grid=(rows // tm,),
        in_specs=[pl.BlockSpec((tm, D_MODEL), row), pl.BlockSpec((tm, ML_W), row),
                  pl.BlockSpec((tm, SA_W), row), pl.BlockSpec((tm, MEM_W), row),
                  pl.BlockSpec((ML_W, D_MODEL), const), pl.BlockSpec((SA_W, D_MODEL), const),
                  pl.BlockSpec((MEM_W, D_MODEL), const), pl.BlockSpec((1, D_MODEL), const)],
        out_specs=pl.BlockSpec((tm, D_MODEL), row),
        compiler_params=_cparams(("arbitrary",)),
        name="mixout",
    )(x2d, a, b, c, wb16[:ML_W], wb16[ML_W:ML_W + SA_W], wb16[ML_W + SA_W:], g_post.reshape(1, D_MODEL))


def _layer(x_p, x_s, st_c, st_n, st_m, c_k, c_v, c_kidx, c_mk, c_mv, page_table, mem_prompt,
           g_pre, w_in, b_gates, g_head, w_mem_k, w_mem_v, g_mem, w_out, g_post):
    nb, seq, _ = x_p.shape
    nreq, t_dec, _ = x_s.shape
    n_mem = mem_prompt.shape[1]
    n_past = page_table.shape[1] * PAGE_SIZE
    weights = _relayout_w_in(w_in)

    tm = min(512, seq)
    tabs = _rope_tables(jnp.arange(seq, dtype=jnp.int32))
    x_p2d = x_p.reshape(nb * seq, D_MODEL)
    (_, _, saq, k, v, saz, memq, memz, idxq, kidx, small, small_t) = _project(
        x_p2d, g_pre, weights, tabs, tm, with_ml=False)
    a_p, p_c, p_n, p_m = _ml_fused(x_p2d, g_pre, weights[0], weights[3], weights[4], b_gates, g_head,
                                   nb, seq, min(ML_FUSED_ROWS, seq), min(ML_CHUNK, seq))
    b_p = _dsa_prompt(saq, saz, idxq, small_t, k, v, kidx, nb, seq)
    mk, mv = _memkv(mem_prompt.reshape(nb * n_mem, D_MODEL), g_mem, w_mem_k, w_mem_v, n_mem)
    y_p = _mixout_mem(x_p.reshape(nb * seq, D_MODEL), a_p, b_p, memq, memz, mk, mv, w_out, g_post,
                      tm, seq).reshape(nb, seq, D_MODEL)

    rws = SAMPLE_ROWS
    n_padrow = rws - t_dec
    xs_pad = jnp.concatenate([jnp.zeros((nreq, n_padrow, D_MODEL), F32), x_s], axis=1).reshape(nreq * rws, D_MODEL)
    pos_s = jnp.tile(jnp.concatenate([jnp.zeros((n_padrow,), jnp.int32),
                                      n_past + jnp.arange(t_dec, dtype=jnp.int32)]), nreq)
    tabs_s = _rope_tables(pos_s)
    (qkv_s, oz_s, saq_s, k_s, v_s, saz_s, memq_s, memz_s, idxq_s, kidx_s, small_s, small_t_s) = _project(
        xs_pad, g_pre, weights, tabs_s, nreq * rws)
    a_s, s_c, s_n, s_m = _mlstm(qkv_s, oz_s, small_s, small_t_s, b_gates, g_head, st_c, st_n, st_m,
                                nreq, rws, n_padrow)
    n_sel = min(TOPK_MAX, (n_past + t_dec) // 4)
    sel = _dsa_select(page_table, idxq_s, small_s, kidx_s, jnp.swapaxes(c_kidx, 1, 2), n_sel, t_dec)
    b_s = _dsa_attend(page_table, saq_s, saz_s, k_s, v_s, sel,
                      c_k.reshape(c_k.shape[0], PAGE_SIZE * SA_H, SA_DH),
                      c_v.reshape(c_v.shape[0], PAGE_SIZE * SA_H, SA_DH))
    c_s = _memattn(memq_s, memz_s, c_mk.reshape(nreq * n_mem, MEM_W), c_mv.reshape(nreq * n_mem, MEM_W), nreq, rws)
    y_s = _mixout(xs_pad, a_s, b_s, c_s, w_out, g_post, nreq * rws)

    def real(a2d):
        return a2d.reshape(nreq, rws, -1)[:, n_padrow:]

    new = (p_c, p_n, p_m,
           k.reshape(nb, seq, SA_H, SA_DH), v.reshape(nb, seq, SA_H, SA_DH), kidx.reshape(nb, seq, IDX_D),
           mk.reshape(nb, n_mem, MEM_H, MEM_DH), mv.reshape(nb, n_mem, MEM_H, MEM_DH),
           s_c, s_n, s_m,
           real(k_s).reshape(nreq, t_dec, SA_H, SA_DH), real(v_s).reshape(nreq, t_dec, SA_H, SA_DH), real(kidx_s))
    return y_p, real(y_s), new


def kernel(x_prompt, x_sample, state_mlstm_C, state_mlstm_n, state_mlstm_m, cache_k, cache_v, cache_kidx,
           cache_mem_k, cache_mem_v, page_table, mem_prompt, g_pre, w_in, b_gates, g_head, w_mem_k, w_mem_v,
           g_mem, w_out, g_post):
    xp, xs = x_prompt, x_sample
    per_layer = []
    for l in range(w_in.shape[0]):
        xp, xs, new = _layer(xp, xs, state_mlstm_C[l], state_mlstm_n[l], state_mlstm_m[l],
                             cache_k[l], cache_v[l], cache_kidx[l], cache_mem_k[l], cache_mem_v[l],
                             page_table, mem_prompt, g_pre[l], w_in[l], b_gates[l], g_head[l],
                             w_mem_k[l], w_mem_v[l], g_mem[l], w_out[l], g_post[l])
        per_layer.append(new)
    stacked = [jnp.stack(a) for a in zip(*per_layer)]
    return (xp, xs, *stacked)
```

```python
import functools
import math

import jax
import jax.numpy as jnp
from jax import lax
from jax.experimental import pallas as pl
from jax.experimental.pallas import tpu as pltpu

F32 = jnp.float32
BF16 = jnp.bfloat16

D_MODEL = 2048
ML_H = 4
ML_W = D_MODEL // 2
ML_DH = ML_W // ML_H
SA_H = 4
SA_W = D_MODEL // 4
SA_DH = SA_W // SA_H
MEM_H = 4
MEM_W = D_MODEL // 4
MEM_DH = MEM_W // MEM_H
IDX_H = 8
IDX_D = 64
IDX_SCALE = (IDX_H * IDX_D) ** -0.5
TOPK_MAX = 256
ROPE_THETA = 10000.0
LOG2E = 1.4426950408889634
EPS = 1e-6
PAGE_SIZE = 128

LANES = 128
CB = 512
SMALL_IG = 64
SMALL_LF = 68
SMALL_W = 72
SAMPLE_ROWS = 16
SEL_ROWS = 8
SEL_PAGES_PER_STEP = 32
ATT_PAGES_PER_STEP = 16
ML_CHUNK = 256
IDX_KEY_CHUNK = 256
DSA_KEY_STEP = 256
ML_FUSED_ROWS = 256
TIE_BLOCK = 256
ONES_ROWS = 16
VMEM_LIMIT = 56 * 1024 * 1024
VMEM_LIMIT_FUSED = 60 * 1024 * 1024
NEG_INF = float("-inf")
POS_INF = float("inf")

_NT = (((1,), (1,)), ((), ()))
_TN = (((0,), (0,)), ((), ()))


def _cparams(sem):
    return pltpu.CompilerParams(dimension_semantics=sem, vmem_limit_bytes=VMEM_LIMIT)


def _sigmoid(x):
    return 1.0 / (1.0 + jnp.exp(-x))


def _silu(x):
    return x * _sigmoid(x)


def _log_sigmoid(x):
    return jnp.minimum(x, 0.0) - jnp.log1p(jnp.exp(-jnp.abs(x)))


def _split3(x):
    hi = x.astype(BF16)
    r = x - hi.astype(F32)
    mid = r.astype(BF16)
    lo = (r - mid.astype(F32)).astype(BF16)
    return hi, mid, lo


def _rope128(x, cos, sin_signed):
    return x * cos + pltpu.roll(x, 64, 1) * sin_signed


def _rope64(x, cos, sin_signed):
    lane = lax.broadcasted_iota(jnp.int32, x.shape, 1)
    first_half = (lane % 64) < 32
    partner = jnp.where(first_half, pltpu.roll(x, 96, 1), pltpu.roll(x, 32, 1))
    return x * cos + partner * sin_signed


def _normed(x_ref, g_ref):
    x = x_ref[...]
    return (x * lax.rsqrt(jnp.mean(x * x, axis=-1, keepdims=True) + EPS) * g_ref[...]).astype(BF16)


def _proj_ml_kernel(x_ref, g_ref, w_ref, qkv_ref, oz_ref, u_ref):
    u_ref[...] = _normed(x_ref, g_ref)
    n_qkv = 3 * ML_W // CB
    for cb in range(5 * ML_W // CB):
        acc = lax.dot_general(u_ref[...], w_ref[cb * CB:(cb + 1) * CB, :], _NT, preferred_element_type=F32)
        if ML_W <= cb * CB < 2 * ML_W:
            acc = acc * (ML_DH ** -0.5)
        if cb < n_qkv:
            qkv_ref[:, cb * CB:(cb + 1) * CB] = acc.astype(BF16)
        else:
            oz_ref[:, (cb - n_qkv) * CB:(cb - n_qkv + 1) * CB] = acc


def _proj_rest_kernel(x_ref, g_ref, wsa_ref, wmem_ref, ws_ref, wst_ref, c128_ref, s128_ref, c64_ref, s64_ref,
                      saq_ref, k_ref, v_ref, saz_ref, memq_ref, memz_ref, idxq_ref,
                      kidx_ref, small_ref, smallt_ref, u_ref):
    u_ref[...] = _normed(x_ref, g_ref)
    sm = jnp.dot(u_ref[...], ws_ref[...], preferred_element_type=F32)
    small_ref[...] = sm
    kidx_ref[...] = _rope64(sm, c64_ref[...], s64_ref[...])[:, :IDX_D]
    smallt_ref[...] = lax.dot_general(wst_ref[...], u_ref[...], _NT, preferred_element_type=F32)

    def block(cb, w_ref=wsa_ref):
        return lax.dot_general(u_ref[...], w_ref[cb * CB:(cb + 1) * CB, :], _NT, preferred_element_type=F32)

    def rope_heads(acc, fn, cos_ref, sin_ref):
        return jnp.concatenate(
            [fn(acc[:, h * LANES:(h + 1) * LANES], cos_ref[...], sin_ref[...]) for h in range(CB // LANES)], axis=1)

    saq_ref[...] = (rope_heads(block(0), _rope128, c128_ref, s128_ref) * (SA_DH ** -0.5 * LOG2E)).astype(BF16)
    tm = x_ref.shape[0]
    k_acc = block(1)
    v_acc = block(2)
    for h in range(SA_H):
        lanes = slice(h * SA_DH, (h + 1) * SA_DH)
        k_ref[pl.ds(h, tm, stride=SA_H), :] = _rope128(k_acc[:, lanes], c128_ref[...], s128_ref[...])
        v_ref[pl.ds(h, tm, stride=SA_H), :] = v_acc[:, lanes]
    saz_ref[...] = block(3)
    idxq_ref[...] = rope_heads(block(4), _rope64, c64_ref, s64_ref).astype(BF16)
    memq_ref[...] = (block(0, wmem_ref) * (MEM_DH ** -0.5)).astype(BF16)
    memz_ref[...] = block(1, wmem_ref)


def _project(x2d, g_pre, weights, tabs, tm, with_ml=True):
    w_ml, w_sa, w_mem, w_small, w_small_t = weights
    rows = x2d.shape[0]
    c128, s128, c64, s64 = tabs
    ntab = c128.shape[0] // tm
    n_ml = 5 * ML_W
    row_only = lambda i: (i, 0)
    tab_map = lambda i: (i % ntab, 0)
    const = lambda i: (0, 0)
    resident = pl.Buffered(1)
    g2d = g_pre.reshape(1, D_MODEL)

    qkv, oz = (None, None) if not with_ml else pl.pallas_call(
        _proj_ml_kernel,
        out_shape=(jax.ShapeDtypeStruct((rows, 3 * ML_W), BF16),
                   jax.ShapeDtypeStruct((rows, 2 * ML_W), F32)),
        grid=(rows // tm,),
        in_specs=[pl.BlockSpec((tm, D_MODEL), row_only),
                  pl.BlockSpec((1, D_MODEL), const),
                  pl.BlockSpec((n_ml, D_MODEL), const, pipeline_mode=resident)],
        out_specs=(pl.BlockSpec((tm, 3 * ML_W), row_only), pl.BlockSpec((tm, 2 * ML_W), row_only)),
        scratch_shapes=[pltpu.VMEM((tm, D_MODEL), BF16)],
        compiler_params=_cparams(("arbitrary",)),
        name="proj_ml",
    )(x2d, g2d, w_ml)

    out_shape = (
        jax.ShapeDtypeStruct((rows, SA_W), BF16),
        jax.ShapeDtypeStruct((rows * SA_H, SA_DH), F32),
        jax.ShapeDtypeStruct((rows * SA_H, SA_DH), F32),
        jax.ShapeDtypeStruct((rows, SA_W), F32),
        jax.ShapeDtypeStruct((rows, MEM_W), BF16),
        jax.ShapeDtypeStruct((rows, MEM_W), F32),
        jax.ShapeDtypeStruct((rows, IDX_H * IDX_D), BF16),
        jax.ShapeDtypeStruct((rows, IDX_D), F32),
        jax.ShapeDtypeStruct((rows, LANES), F32),
        jax.ShapeDtypeStruct((LANES, rows), F32),
    )
    out_specs = (
        pl.BlockSpec((tm, CB), row_only),
        pl.BlockSpec((tm * SA_H, SA_DH), row_only),
        pl.BlockSpec((tm * SA_H, SA_DH), row_only),
        pl.BlockSpec((tm, CB), row_only),
        pl.BlockSpec((tm, CB), row_only),
        pl.BlockSpec((tm, CB), row_only),
        pl.BlockSpec((tm, CB), row_only),
        pl.BlockSpec((tm, IDX_D), row_only),
        pl.BlockSpec((tm, LANES), row_only),
        pl.BlockSpec((LANES, tm), lambda i: (0, i)),
    )
    in_specs = [
        pl.BlockSpec((tm, D_MODEL), row_only),
        pl.BlockSpec((1, D_MODEL), const),
        pl.BlockSpec(w_sa.shape, const, pipeline_mode=resident),
        pl.BlockSpec(w_mem.shape, const, pipeline_mode=resident),
        pl.BlockSpec((D_MODEL, LANES), const, pipeline_mode=resident),
        pl.BlockSpec((LANES, D_MODEL), const, pipeline_mode=resident),
        pl.BlockSpec((tm, LANES), tab_map),
        pl.BlockSpec((tm, LANES), tab_map),
        pl.BlockSpec((tm, LANES), tab_map),
        pl.BlockSpec((tm, LANES), tab_map),
    ]
    rest = pl.pallas_call(
        _proj_rest_kernel,
        out_shape=out_shape,
        grid=(rows // tm,),
        in_specs=in_specs,
        out_specs=out_specs,
        scratch_shapes=[pltpu.VMEM((tm, D_MODEL), BF16)],
        compiler_params=_cparams(("arbitrary",)),
        name="proj_rest",
    )(x2d, g2d, w_sa, w_mem, w_small, w_small_t, c128, s128, c64, s64)
    return (qkv, oz, *rest)


def _rope_tables(pos):
    def tab(half):
        inv = ROPE_THETA ** (-jnp.arange(half, dtype=F32) / half)
        ang = pos.astype(F32)[:, None] * inv[None, :]
        return jnp.cos(ang), jnp.sin(ang)

    c, s = tab(SA_DH // 2)
    c128 = jnp.concatenate([c, c], axis=1)
    s128 = jnp.concatenate([-s, s], axis=1)
    c, s = tab(IDX_D // 2)
    c64 = jnp.concatenate([c, c, c, c], axis=1)
    s64 = jnp.concatenate([-s, s, -s, s], axis=1)
    return c128, s128, c64, s64


def _relayout_w_in(w_in):
    off = {}
    o = 0
    for name, w in (('ml_q', ML_W), ('ml_k', ML_W), ('ml_v', ML_W), ('ml_o', ML_W), ('ml_z', ML_W),
                    ('ml_i', ML_H), ('ml_f', ML_H), ('sa_q', SA_W), ('sa_k', SA_W), ('sa_v', SA_W),
                    ('sa_z', SA_W), ('idx_q', IDX_H * IDX_D), ('idx_k', IDX_D), ('idx_w', IDX_H),
                    ('mem_q', MEM_W), ('mem_z', MEM_W)):
        off[name] = (o, w)
        o += w

    w_t = w_in.T

    def col(name):
        a, w = off[name]
        return w_t[a:a + w]

    def span(first, last):
        return w_t[off[first][0]:off[last][0] + off[last][1]].astype(BF16)

    w_ml = w_t.astype(BF16)
    w_sa = span('sa_q', 'idx_q')
    w_mem = span('mem_q', 'mem_z')
    small_t = jnp.concatenate([
        col('idx_k'), col('ml_i'), col('ml_f'), col('idx_w'),
        jnp.zeros((LANES - IDX_D - 2 * ML_H - IDX_H, D_MODEL), F32)], axis=0).astype(BF16)
    return (w_ml, w_sa, w_mem, small_t.T, small_t)


def _memkv_kernel(m_ref, g_ref, wk_ref, wv_ref, k_ref, v_ref):
    x = m_ref[...]
    u = (x * lax.rsqrt(jnp.mean(x * x, axis=-1, keepdims=True) + EPS) * g_ref[...]).astype(BF16)
    k_ref[...] = jnp.dot(u, wk_ref[...], preferred_element_type=F32)
    v_ref[...] = jnp.dot(u, wv_ref[...], preferred_element_type=F32)


def _memkv(mem2d, g_mem, wk, wv, n_mem):
    rows = mem2d.shape[0]
    row = lambda i: (i, 0)
    const = lambda i: (0, 0)
    return pl.pallas_call(
        _memkv_kernel,
        out_shape=(jax.ShapeDtypeStruct((rows, MEM_W), F32), jax.ShapeDtypeStruct((rows, MEM_W), F32)),
        grid=(rows // n_mem,),
        in_specs=[pl.BlockSpec((n_mem, D_MODEL), row), pl.BlockSpec((1, D_MODEL), const),
                  pl.BlockSpec((D_MODEL, MEM_W), const), pl.BlockSpec((D_MODEL, MEM_W), const)],
        out_specs=(pl.BlockSpec((n_mem, MEM_W), row), pl.BlockSpec((n_mem, MEM_W), row)),
        compiler_params=_cparams(("arbitrary",)),
        name="memkv",
    )(mem2d, g_mem.reshape(1, D_MODEL), wk.astype(BF16), wv.astype(BF16))


def _mlstm_chunk(q_of, k_of, v_of, gate_of, g_c, g_r, gh_ref, c_s, n_s, m_s, a_store, c, n_pad,
                 between_heads=None):
    ri = lax.broadcasted_iota(jnp.int32, (c, c), 0)
    cj = lax.broadcasted_iota(jnp.int32, (c, c), 1)
    causal = cj <= ri
    tri = jnp.where(causal, 1.0, 0.0).astype(BF16)
    tri_t = jnp.where(ri <= cj, 1.0, 0.0).astype(BF16)

    pad_c = lax.broadcasted_iota(jnp.int32, (c, LANES), 0) < n_pad
    ig_c = jnp.where(pad_c, NEG_INF, g_c)
    lf_c = jnp.where(pad_c, 0.0, _log_sigmoid(g_c))
    b_c = sum(jnp.dot(tri, p, preferred_element_type=F32) for p in _split3(lf_c))
    pad_r = lax.broadcasted_iota(jnp.int32, (SAMPLE_ROWS, c), 1) < n_pad
    ig_r = jnp.where(pad_r, NEG_INF, g_r)
    lf_r = jnp.where(pad_r, 0.0, _log_sigmoid(g_r))
    b_r = sum(jnp.dot(p, tri_t, preferred_element_type=F32) for p in _split3(lf_r))

    m_all = m_s[...]
    n_all = n_s[...]
    c_all = [c_s[h] for h in range(ML_H)]
    new_state = []
    for h in range(ML_H):
        hs = slice(h * ML_DH, (h + 1) * ML_DH)
        m_prev = m_all[h:h + 1, 0:1]
        b_t = b_c[:, SMALL_LF + h:SMALL_LF + h + 1]
        igc = ig_c[:, SMALL_IG + h:SMALL_IG + h + 1]
        b_s = b_r[ML_H + h:ML_H + h + 1, :]
        igr = ig_r[h:h + 1, :]
        a = jnp.where(causal, b_t - b_s + igr, NEG_INF)
        bm = b_t + m_prev
        m_t = jnp.maximum(bm, jnp.max(a, axis=1, keepdims=True))
        inter = jnp.exp(bm - m_t)
        dmat = jnp.exp(a - m_t)
        q, k, v = q_of(h), k_of(h), v_of(h)
        s = lax.dot_general(q, k, _NT, preferred_element_type=F32) * dmat
        if between_heads is not None:
            between_heads()
        c_h = c_all[h]
        n_h = n_all[h:h + 1, :]
        num = (jnp.dot(s.astype(BF16), v, preferred_element_type=F32)
               + inter * jnp.dot(q, c_h.astype(BF16), preferred_element_type=F32))
        qn = (jnp.sum(s, axis=1, keepdims=True)
              + inter * jnp.sum(q.astype(F32) * n_h, axis=1, keepdims=True))
        hh = num / jnp.maximum(jnp.abs(qn), jnp.exp(-m_t))
        hh = hh * lax.rsqrt(jnp.mean(hh * hh, axis=1, keepdims=True) + EPS)
        if between_heads is not None:
            between_heads()
        a_store(h, (hh * gh_ref[:, hs] * gate_of(h)).astype(BF16))

        m_new = m_t[c - 1:c, :]
        b_last = b_t[c - 1:c, :]
        w_end = jnp.exp(b_last - b_t + igc - m_new)
        decay = jnp.exp(b_last + m_prev - m_new)
        kw = k.astype(F32) * w_end
        new_state.append((decay * c_h + lax.dot_general(kw.astype(BF16), v, _TN, preferred_element_type=F32),
                          decay * n_h + jnp.sum(kw, axis=0, keepdims=True),
                          jnp.broadcast_to(m_new, (1, LANES))))
        if between_heads is not None:
            between_heads()

    for h, (c_new, n_new, m_new) in enumerate(new_state):
        c_s[h] = c_new
        n_s[h:h + 1, :] = n_new
        m_s[h:h + 1, :] = m_new


def _out_gate(o, z):
    return z / ((1.0 + jnp.exp(-o)) * (1.0 + jnp.exp(-z)))


def _head_cols(group, h):
    return slice(group * ML_W + h * ML_DH, group * ML_W + (h + 1) * ML_DH)


def _mlstm_kernel(qkv_ref, oz_ref, gc_ref, gt_ref, bcol_ref, brow_ref, gh_ref, c0_ref, n0_ref, m0_ref,
                  a_ref, cout_ref, nout_ref, mout_ref, c_s, n_s, m_s, *, c, n_pad):
    ci = pl.program_id(1)

    @pl.when(ci == 0)
    def _():
        c_s[...] = c0_ref[0]
        n_s[...] = n0_ref[0]
        m_s[...] = m0_ref[0]

    def a_store(h, value):
        a_ref[:, _head_cols(0, h)] = value

    _mlstm_chunk(lambda h: qkv_ref[:, _head_cols(0, h)], lambda h: qkv_ref[:, _head_cols(1, h)],
                 lambda h: qkv_ref[:, _head_cols(2, h)],
                 lambda h: _out_gate(oz_ref[:, _head_cols(0, h)], oz_ref[:, _head_cols(1, h)]),
                 gc_ref[...] + bcol_ref[...], gt_ref[0] + brow_ref[...], gh_ref, c_s, n_s, m_s, a_store, c, n_pad)

    @pl.when(ci == pl.num_programs(1) - 1)
    def _():
        cout_ref[0] = c_s[...]
        nout_ref[0] = n_s[...]
        mout_ref[0] = m_s[...]


def _ml_fused_kernel(x_ref, g_ref, w_ref, ws_ref, wst_ref, bcol_ref, brow_ref, gh_ref,
                     a_ref, cout_ref, nout_ref, mout_ref,
                     u_s, qkv_a, qkv_b, gate_a, gate_b, sm_a, sm_b, smt_a, smt_b, c_s, n_s, m_s, *, c, nrb, n_blocks):
    i = pl.program_id(0)
    tm = x_ref.shape[0]

    @pl.when(i == 0)
    def _():
        qkv_b[...] = jnp.zeros(qkv_b.shape, BF16)
        gate_b[...] = jnp.zeros(gate_b.shape, F32)
        sm_b[...] = jnp.zeros(sm_b.shape, F32)
        smt_b[...] = jnp.zeros(smt_b.shape, F32)

    @pl.when((i == 0) | (lax.rem(jnp.maximum(i - 1, 0), nrb) == 0))
    def _():
        c_s[...] = jnp.zeros(c_s.shape, F32)
        n_s[...] = jnp.zeros(n_s.shape, F32)
        m_s[...] = jnp.zeros(m_s.shape, F32)

    def projection_pieces(qkv_w, gate_w, sm_w, smt_w):
        pw = ML_DH

        def block(row0):
            return lax.dot_general(u_s[...], w_ref[row0:row0 + pw, :], _NT, preferred_element_type=F32)

        def narrow():
            sm_w[...] = jnp.dot(u_s[...], ws_ref[...], preferred_element_type=F32)
            smt_w[...] = lax.dot_general(wst_ref[SMALL_IG:SMALL_IG + SAMPLE_ROWS, :], u_s[...], _NT,
                                         preferred_element_type=F32)

        def qkv_block(cb):
            acc = block(cb * pw)
            if ML_W <= cb * pw < 2 * ML_W:
                acc = acc * (ML_DH ** -0.5)
            qkv_w[:, cb * pw:(cb + 1) * pw] = acc.astype(BF16)

        def gate_block(cb):
            gate_w[:, cb * pw:(cb + 1) * pw] = _out_gate(block(3 * ML_W + cb * pw), block(4 * ML_W + cb * pw))

        pieces = [narrow]
        pieces += [functools.partial(qkv_block, cb) for cb in range(3 * ML_W // pw)]
        pieces += [functools.partial(gate_block, cb) for cb in range(ML_W // pw)]
        return pieces

    def step(write, read):
        pieces = []
        if write is not None:
            u_s[...] = _normed(x_ref, g_ref)
            pieces = projection_pieces(*write)
        n_slots = (tm // c) * ML_H * 3
        n_pieces = len(pieces)
        slot = [0]

        def emit():
            slot[0] += 1
            while n_pieces - len(pieces) < (slot[0] * n_pieces) // n_slots:
                pieces.pop(0)()

        qkv_r, gate_r, sm_r, smt_r = read if read is not None else (None,) * 4
        for ck in range(tm // c if read is not None else 0):
            rows = slice(ck * c, (ck + 1) * c)

            def a_store(h, value, rows=rows):
                a_ref[rows, _head_cols(0, h)] = value

            _mlstm_chunk(lambda h, rows=rows: qkv_r[rows, _head_cols(0, h)],
                         lambda h, rows=rows: qkv_r[rows, _head_cols(1, h)],
                         lambda h, rows=rows: qkv_r[rows, _head_cols(2, h)],
                         lambda h, rows=rows: gate_r[rows, _head_cols(0, h)],
                         sm_r[rows, :] + bcol_ref[...], smt_r[:, rows] + brow_ref[...],
                         gh_ref, c_s, n_s, m_s, a_store, c, 0, between_heads=emit)
        while pieces:
            pieces.pop(0)()

    set_a = (qkv_a, gate_a, sm_a, smt_a)
    set_b = (qkv_b, gate_b, sm_b, smt_b)

    @pl.when(lax.rem(i, 2) == 0)
    def _():
        step(set_a, set_b)

    @pl.when(lax.rem(i, 2) == 1)
    def _():
        step(set_b, set_a)

    @pl.when((i >= 1) & (lax.rem(jnp.maximum(i - 1, 0), nrb) == nrb - 1))
    def _():
        cout_ref[0] = c_s[...]
        nout_ref[0] = n_s[...]
        mout_ref[0] = m_s[...]


def _ml_fused(x2d, g_pre, w_all, w_small, w_small_t, b_gates, g_head, nb, seq, tm, c):
    rows = x2d.shape[0]
    nrb = seq // tm
    n_blocks = rows // tm
    n_ml = 5 * ML_W
    bias_col = jnp.zeros((1, LANES), F32).at[0, SMALL_IG:SMALL_IG + 2 * ML_H].set(b_gates)
    bias_row = jnp.zeros((SAMPLE_ROWS, 1), F32).at[:2 * ML_H, 0].set(b_gates)
    const = lambda i: (0, 0)
    resident = pl.Buffered(1)
    lagged = lambda i: jnp.maximum(i - 1, 0)
    out_shape = (
        jax.ShapeDtypeStruct((rows, ML_W), BF16),
        jax.ShapeDtypeStruct((nb, ML_H, ML_DH, ML_DH), F32),
        jax.ShapeDtypeStruct((nb, ML_H, ML_DH), F32),
        jax.ShapeDtypeStruct((nb, 8, LANES), F32),
    )
    a, c_out, n_out, m_out = pl.pallas_call(
        functools.partial(_ml_fused_kernel, c=c, nrb=nrb, n_blocks=n_blocks),
        out_shape=out_shape,
        grid=(n_blocks + 1,),
        in_specs=[
            pl.BlockSpec((tm, D_MODEL), lambda i: (jnp.minimum(i, n_blocks - 1), 0)),
            pl.BlockSpec((1, D_MODEL), const),
            pl.BlockSpec((n_ml, D_MODEL), const, pipeline_mode=resident),
            pl.BlockSpec((D_MODEL, LANES), const, pipeline_mode=resident),
            pl.BlockSpec((LANES, D_MODEL), const, pipeline_mode=resident),
            pl.BlockSpec((1, LANES), const),
            pl.BlockSpec((SAMPLE_ROWS, 1), const),
            pl.BlockSpec((1, ML_W), const),
        ],
        out_specs=(
            pl.BlockSpec((tm, ML_W), lambda i: (lagged(i), 0)),
            pl.BlockSpec((1, ML_H, ML_DH, ML_DH), lambda i: (lagged(i) // nrb, 0, 0, 0)),
            pl.BlockSpec((1, ML_H, ML_DH), lambda i: (lagged(i) // nrb, 0, 0)),
            pl.BlockSpec((1, 8, LANES), lambda i: (lagged(i) // nrb, 0, 0)),
        ),
        scratch_shapes=[pltpu.VMEM((tm, D_MODEL), BF16),
                        pltpu.VMEM((tm, 3 * ML_W), BF16), pltpu.VMEM((tm, 3 * ML_W), BF16),
                        pltpu.VMEM((tm, ML_W), F32), pltpu.VMEM((tm, ML_W), F32),
                        pltpu.VMEM((tm, LANES), F32), pltpu.VMEM((tm, LANES), F32),
                        pltpu.VMEM((SAMPLE_ROWS, tm), F32), pltpu.VMEM((SAMPLE_ROWS, tm), F32),
                        pltpu.VMEM((ML_H, ML_DH, ML_DH), F32), pltpu.VMEM((ML_H, ML_DH), F32),
                        pltpu.VMEM((8, LANES), F32)],
        compiler_params=pltpu.CompilerParams(dimension_semantics=("arbitrary",), vmem_limit_bytes=VMEM_LIMIT_FUSED),
        name="ml_fused",
    )(x2d, g_pre.reshape(1, D_MODEL), w_all, w_small, w_small_t, bias_col, bias_row, g_head.reshape(1, ML_W))
    return a, c_out, n_out, m_out[:, :ML_H, 0]


def _mlstm(qkv, oz, small, small_t, b_gates, g_head, c0, n0, m0, nb, c, n_pad):
    rows = qkv.shape[0]
    nc = rows // (nb * c)
    bias_col = jnp.zeros((1, LANES), F32).at[0, SMALL_IG:SMALL_IG + 2 * ML_H].set(b_gates)
    bias_row = jnp.zeros((SAMPLE_ROWS, 1), F32).at[:2 * ML_H, 0].set(b_gates)
    m0b = jnp.zeros((nb, 8, LANES), F32).at[:, :ML_H, :].set(jnp.broadcast_to(m0[:, :, None], (nb, ML_H, LANES)))
    rowblk = lambda b, i: (b * nc + i, 0)
    const = lambda b, i: (0, 0)
    gates_t = small_t[SMALL_IG:SMALL_IG + SAMPLE_ROWS].reshape(SAMPLE_ROWS, rows // c, c).transpose(1, 0, 2)
    out_shape = (
        jax.ShapeDtypeStruct((rows, ML_W), BF16),
        jax.ShapeDtypeStruct((nb, ML_H, ML_DH, ML_DH), F32),
        jax.ShapeDtypeStruct((nb, ML_H, ML_DH), F32),
        jax.ShapeDtypeStruct((nb, 8, LANES), F32),
    )
    st4 = lambda b, i: (b, 0, 0, 0)
    st3 = lambda b, i: (b, 0, 0)
    a, c_out, n_out, m_out = pl.pallas_call(
        functools.partial(_mlstm_kernel, c=c, n_pad=n_pad),
        out_shape=out_shape,
        grid=(nb, nc),
        in_specs=[
            pl.BlockSpec((c, 3 * ML_W), rowblk),
            pl.BlockSpec((c, 2 * ML_W), rowblk),
            pl.BlockSpec((c, LANES), rowblk),
            pl.BlockSpec((1, SAMPLE_ROWS, c), lambda b, i: (b * nc + i, 0, 0)),
            pl.BlockSpec((1, LANES), const),
            pl.BlockSpec((SAMPLE_ROWS, 1), const),
            pl.BlockSpec((1, ML_W), const),
            pl.BlockSpec((1, ML_H, ML_DH, ML_DH), st4),
            pl.BlockSpec((1, ML_H, ML_DH), st3),
            pl.BlockSpec((1, 8, LANES), st3),
        ],
        out_specs=(
            pl.BlockSpec((c, ML_W), rowblk),
            pl.BlockSpec((1, ML_H, ML_DH, ML_DH), st4),
            pl.BlockSpec((1, ML_H, ML_DH), st3),
            pl.BlockSpec((1, 8, LANES), st3),
        ),
        scratch_shapes=[pltpu.VMEM((ML_H, ML_DH, ML_DH), F32), pltpu.VMEM((ML_H, ML_DH), F32),
                        pltpu.VMEM((8, LANES), F32)],
        compiler_params=_cparams(("arbitrary", "arbitrary")),
        name="mlstm",
    )(qkv, oz, small, gates_t, bias_col, bias_row, g_head.reshape(1, ML_W), c0, n0, m0b)
    return a, c_out, n_out, m_out[:, :ML_H, 0]


_REDUCERS = {"sum": (jnp.sum, jnp.add), "max": (jnp.max, jnp.maximum), "min": (jnp.min, jnp.minimum)}
REDUCE_CHAINS = 8


def _reduce(x, axis, op):
    fn, combine = _REDUCERS[op]
    unit = 8 if axis == 0 else LANES
    n = x.shape[axis]
    units = n // unit
    if n % unit or units < 2 * REDUCE_CHAINS:
        return fn(x, axis=axis, keepdims=True)
    base, rem = divmod(units, REDUCE_CHAINS)
    parts, start = [], 0
    for i in range(REDUCE_CHAINS):
        size = (base + (1 if i < rem else 0)) * unit
        piece = x[start:start + size] if axis == 0 else x[:, start:start + size]
        parts.append(fn(piece, axis=axis, keepdims=True))
        start += size
    while len(parts) > 1:
        parts = [combine(parts[i], parts[i + 1]) for i in range(0, len(parts), 2)]
    return parts[0]


def _count(pred, axis):
    return _reduce(jnp.where(pred, 1.0, 0.0), axis, "sum")


def _kth_largest(x_ref, k, axis, n_bisect, quarter_steps=False):
    kf = float(k)
    x = x_ref[...]
    hi = _reduce(x, axis, "max")
    lo = _reduce(jnp.where(x == NEG_INF, POS_INF, x), axis, "min")

    def bisect(_, carry):
        lo, hi = carry
        mid = 0.5 * (lo + hi)
        ge = _count(x_ref[...] >= mid, axis) >= kf
        return jnp.where(ge, mid, lo), jnp.where(ge, hi, mid)

    def quarter(_, carry):
        lo, hi = carry
        w = hi - lo
        m1, m2, m3 = lo + 0.25 * w, lo + 0.5 * w, lo + 0.75 * w
        xx = x_ref[...]
        g1, g2, g3 = (_count(xx >= m, axis) >= kf for m in (m1, m2, m3))
        lo = jnp.where(g3, m3, jnp.where(g2, m2, jnp.where(g1, m1, lo)))
        hi = jnp.where(g3, hi, jnp.where(g2, m3, jnp.where(g1, m2, m1)))
        return lo, hi

    if quarter_steps:
        lo, hi = lax.fori_loop(0, (n_bisect + 1) // 2, quarter, (lo, hi))
    else:
        lo, hi = lax.fori_loop(0, n_bisect, bisect, (lo, hi))

    def finished(cmin, xx):
        return jnp.where((_count(xx > cmin, axis) < kf) | (cmin == POS_INF), 1.0, 0.0)

    xx = x_ref[...]
    thr = _reduce(jnp.where(xx >= lo, xx, POS_INF), axis, "min")
    done = finished(thr, xx)

    def cond(st):
        return st[2] < 0.5

    def body(st):
        thr, done, _ = st
        xx = x_ref[...]
        cmin = _reduce(jnp.where(xx > thr, xx, POS_INF), axis, "min")
        thr = jnp.where(done < 0.5, cmin, thr)
        done = jnp.maximum(done, finished(thr, xx))
        return thr, done, jnp.min(done)

    thr, _, _ = lax.while_loop(cond, body, (thr, done, jnp.min(done)))
    return thr


def _dsa_kernel(q_ref, z_ref, idxq_ref, wt_ref, k_ref, v_ref, kidx_ref, o_ref,
                kb_s, vt_s, kib_s, x_s, sel_s, *, n_keys, qb, n_sel, n_bisect, key_step):
    j = pl.program_id(1)

    @pl.when(j == 0)
    def _():
        for h in range(SA_H):
            lanes = slice(h * SA_DH, (h + 1) * SA_DH)
            kb_s[:, lanes] = k_ref[pl.ds(h, n_keys, stride=SA_H), :].astype(BF16)
            vt_s[h, 0:SA_DH, :] = v_ref[pl.ds(h, n_keys, stride=SA_H), :].T.astype(BF16)
            vt_s[h, SA_DH:SA_DH + ONES_ROWS, :] = jnp.ones((ONES_ROWS, n_keys), BF16)
        kib_s[...] = kidx_ref[...].astype(BF16)

    def attend(nk):
        xs = x_s.at[0:nk]
        ss = sel_s.at[0:nk]
        key = lax.broadcasted_iota(jnp.int32, (nk, qb), 0)
        qpos = j * qb + lax.broadcasted_iota(jnp.int32, (nk, qb), 1)
        valid = key <= qpos
        qcat = jnp.concatenate([idxq_ref[:, h * IDX_D:(h + 1) * IDX_D] for h in range(IDX_H)], axis=0)
        w_rows = [wt_ref[h:h + 1, :] * IDX_SCALE for h in range(IDX_H)]
        kc = math.gcd(IDX_KEY_CHUNK, nk)
        for c0 in range(0, nk, kc):
            d = lax.dot_general(kib_s[c0:c0 + kc, :], qcat, _NT, preferred_element_type=F32)
            sc = jnp.zeros((kc, qb), F32)
            for h in range(IDX_H):
                sc = sc + jnp.maximum(d[:, h * qb:(h + 1) * qb], 0.0) * w_rows[h]
            ok = (c0 + lax.broadcasted_iota(jnp.int32, (kc, qb), 0)) <= (
                j * qb + lax.broadcasted_iota(jnp.int32, (kc, qb), 1))
            x_s[c0:c0 + kc, :] = jnp.where(ok, sc, NEG_INF)
            sel_s[c0:c0 + kc, :] = jnp.where(ok, 1.0, 0.0)

        @pl.when((j + 1) * qb > n_sel)
        def _():
            kf = float(n_sel)
            thr = _kth_largest(xs, n_sel, 0, n_bisect)
            x = xs[...]
            need = kf - _count(x > thr, 0)
            n_tie = _count(x == thr, 0)
            qrow = j * qb + lax.broadcasted_iota(jnp.int32, (1, qb), 1)
            small = (qrow + 1) <= n_sel
            ss[...] = jnp.where(small, jnp.where(valid, 1.0, 0.0), jnp.where(x >= thr, 1.0, 0.0))
            excess = jnp.max(jnp.where((n_tie > need) & jnp.logical_not(small), 1.0, 0.0))

            @pl.when(excess > 0.5)
            def _():
                tb = math.gcd(TIE_BLOCK, nk)
                r_i = lax.broadcasted_iota(jnp.int32, (tb, tb), 0)
                c_i = lax.broadcasted_iota(jnp.int32, (tb, tb), 1)
                lower = jnp.where(c_i < r_i, 1.0, 0.0).astype(BF16)
                carry = jnp.zeros((1, qb), F32)
                for blk in range(nk // tb):
                    rows = slice(blk * tb, (blk + 1) * tb)
                    xb = x_s[rows, :]
                    tie = jnp.where(xb == thr, 1.0, 0.0)
                    rank = jnp.dot(lower, tie.astype(BF16), preferred_element_type=F32) + carry
                    keep = (xb > thr) | ((xb == thr) & (rank < need))
                    keyb = blk * tb + lax.broadcasted_iota(jnp.int32, (tb, qb), 0)
                    qposb = j * qb + lax.broadcasted_iota(jnp.int32, (tb, qb), 1)
                    smallb = jnp.where(keyb <= qposb, 1.0, 0.0)
                    sel_s[rows, :] = jnp.where(small, smallb, jnp.where(keep, 1.0, 0.0))
                    carry = carry + jnp.sum(tie, axis=0, keepdims=True)

        sel = ss[...] > 0.5
        heads = range(SA_H)
        hsl = [slice(h * SA_DH, (h + 1) * SA_DH) for h in heads]
        st = [jnp.where(sel, lax.dot_general(kb_s[0:nk, hsl[h]], q_ref[:, hsl[h]], _NT,
                                             preferred_element_type=F32), NEG_INF) for h in heads]
        mx = [_reduce(st[h], 0, "max") for h in heads]
        p = [jnp.exp2(st[h] - mx[h]).astype(BF16) for h in heads]
        pv = [jnp.dot(vt_s[h, :, 0:nk], p[h], preferred_element_type=F32) for h in heads]
        ot = [pv[h][0:SA_DH, :] / pv[h][SA_DH:SA_DH + 1, :] for h in heads]
        for h in heads:
            o_ref[:, hsl[h]] = (ot[h].T * _silu(z_ref[:, hsl[h]])).astype(BF16)

    n_ext = n_keys // key_step
    for e in range(n_ext):
        nk = (e + 1) * key_step
        lo_j = e * key_step // qb
        hi_j = nk // qb

        @pl.when((j >= lo_j) & (j < hi_j))
        def _(nk=nk):
            attend(nk)


def _dsa_prompt(saq, saz, idxq, small_t, k, v, kidx, nb, seq):
    rows = saq.shape[0]
    qb = min(seq, 128)
    nq = seq // qb
    n_sel = min(TOPK_MAX, seq // 4)
    qblk = lambda b, j: (b * nq + j, 0)
    per_b = lambda b, j: (b, 0)
    wt_blk = SMALL_W // 8
    return pl.pallas_call(
        functools.partial(_dsa_kernel, n_keys=seq, qb=qb, n_sel=n_sel, n_bisect=20, key_step=min(DSA_KEY_STEP, seq)),
        out_shape=jax.ShapeDtypeStruct((rows, SA_W), BF16),
        grid=(nb, nq),
        in_specs=[
            pl.BlockSpec((qb, SA_W), qblk),
            pl.BlockSpec((qb, SA_W), qblk),
            pl.BlockSpec((qb, IDX_H * IDX_D), qblk),
            pl.BlockSpec((8, qb), lambda b, j: (wt_blk, b * nq + j)),
            pl.BlockSpec((seq * SA_H, SA_DH), per_b),
            pl.BlockSpec((seq * SA_H, SA_DH), per_b),
            pl.BlockSpec((seq, IDX_D), per_b),
        ],
        out_specs=pl.BlockSpec((qb, SA_W), qblk),
        scratch_shapes=[pltpu.VMEM((seq, SA_W), BF16), pltpu.VMEM((SA_H, SA_DH + ONES_ROWS, seq), BF16),
                        pltpu.VMEM((seq, IDX_D), BF16), pltpu.VMEM((seq, qb), F32),
                        pltpu.VMEM((seq, qb), F32)],
        compiler_params=_cparams(("arbitrary", "arbitrary")),
        name="dsa",
    )(saq, saz, idxq, small_t, k, v, kidx)


def _sel_kernel(pt_ref, idxq_ref, small_ref, kinew_ref, *rest, npg, n_past, n_sel, n_real, n_bisect):
    page_refs = rest[:npg]
    sel_ref, x_s = rest[npg:]
    g = pl.program_id(1)
    rws = SAMPLE_ROWS
    top = rws - SEL_ROWS
    pk = n_past + LANES
    gk = npg * PAGE_SIZE

    qs = jnp.concatenate([idxq_ref[:, h * IDX_D:(h + 1) * IDX_D] for h in range(IDX_H)], axis=0)
    small = small_ref[...]

    def scores(d):
        sc = jnp.zeros((SEL_ROWS, d.shape[1]), F32)
        for h in range(IDX_H):
            w = small[top:, SMALL_W + h:SMALL_W + h + 1] * IDX_SCALE
            sc = sc + jnp.maximum(d[h * rws + top:(h + 1) * rws, :], 0.0) * w
        return sc

    kp_t = jnp.concatenate([r[0] for r in page_refs], axis=1).astype(BF16)
    x_s[:, pl.ds(pl.multiple_of(g * gk, LANES), gk)] = scores(jnp.dot(qs, kp_t, preferred_element_type=F32))

    @pl.when(g == pl.num_programs(1) - 1)
    def _():
        knew = jnp.concatenate([kinew_ref[...], jnp.zeros((LANES - rws, IDX_D), F32)], axis=0).astype(BF16)
        row = top + lax.broadcasted_iota(jnp.int32, (SEL_ROWS, LANES), 0)
        col = lax.broadcasted_iota(jnp.int32, (SEL_ROWS, LANES), 1)
        ok = (col >= rws - n_real) & (col < rws) & (col <= row)
        d_new = lax.dot_general(qs, knew, _NT, preferred_element_type=F32)
        x_s[:, n_past:pk] = jnp.where(ok, scores(d_new), NEG_INF)

        kf = float(n_sel)
        thr = _kth_largest(x_s, n_sel, 1, n_bisect, quarter_steps=True)
        x = x_s[...]
        need = kf - _count(x > thr, 1)
        n_tie = _count(x == thr, 1)
        sel_ref[0, 0:top, :] = jnp.ones((top, pk), F32)
        sel_ref[0, top:rws, :] = jnp.where(x >= thr, 1.0, 0.0)
        real = lax.broadcasted_iota(jnp.int32, (SEL_ROWS, 1), 0) >= SEL_ROWS - n_real
        excess = jnp.max(jnp.where((n_tie > need) & real, 1.0, 0.0))

        @pl.when(excess > 0.5)
        def _():
            r_i = lax.broadcasted_iota(jnp.int32, (LANES, LANES), 0)
            c_i = lax.broadcasted_iota(jnp.int32, (LANES, LANES), 1)
            upper = jnp.where(r_i < c_i, 1.0, 0.0).astype(BF16)

            def blk(i, carry):
                cols = pl.ds(pl.multiple_of(i * LANES, LANES), LANES)
                xb = x_s[:, cols]
                tie = jnp.where(xb == thr, 1.0, 0.0)
                rank = jnp.dot(tie.astype(BF16), upper, preferred_element_type=F32) + carry
                keep = (xb > thr) | ((xb == thr) & (rank < need))
                sel_ref[0, top:rws, cols] = jnp.where(keep, 1.0, 0.0)
                return carry + jnp.sum(tie, axis=1, keepdims=True)

            lax.fori_loop(0, pk // LANES, blk, jnp.zeros((SEL_ROWS, 1), F32))


def _dsa_select(page_table, idxq, small, kidx_new, cache_kidx, n_sel, n_real):
    nreq, n_pages = page_table.shape
    n_past = n_pages * PAGE_SIZE
    pk = n_past + LANES
    npg = min(SEL_PAGES_PER_STEP, n_pages)
    req = lambda b, g, pt: (b, 0)

    def page_map(i):
        return lambda b, g, pt: (pt[b, g * npg + i], 0, 0)

    grid_spec = pltpu.PrefetchScalarGridSpec(
        num_scalar_prefetch=1,
        grid=(nreq, n_pages // npg),
        in_specs=[pl.BlockSpec((SAMPLE_ROWS, IDX_H * IDX_D), req),
                  pl.BlockSpec((SAMPLE_ROWS, LANES), req),
                  pl.BlockSpec((SAMPLE_ROWS, IDX_D), req)]
                 + [pl.BlockSpec((1, IDX_D, PAGE_SIZE), page_map(i)) for i in range(npg)],
        out_specs=pl.BlockSpec((1, SAMPLE_ROWS, pk), lambda b, g, pt: (b, 0, 0)),
        scratch_shapes=[pltpu.VMEM((SEL_ROWS, pk), F32)],
    )
    assert n_real <= SEL_ROWS
    return pl.pallas_call(
        functools.partial(_sel_kernel, npg=npg, n_past=n_past, n_sel=n_sel, n_real=n_real, n_bisect=20),
        out_shape=jax.ShapeDtypeStruct((nreq, SAMPLE_ROWS, pk), F32),
        grid_spec=grid_spec,
        compiler_params=_cparams(("arbitrary", "arbitrary")),
        name="dsa_sel",
    )(page_table, idxq, small, kidx_new, *([cache_kidx] * npg))


def _att_kernel(pt_ref, q_ref, z_ref, knew_ref, vnew_ref, sel_ref, seltail_ref, *rest, npg):
    k_refs = rest[:npg]
    v_refs = rest[npg:2 * npg]
    o_ref = rest[2 * npg]
    m_s, l_s, acc_s = rest[2 * npg + 1:]
    g = pl.program_id(1)
    rws = SAMPLE_ROWS
    floor = -1e30

    @pl.when(g == 0)
    def _():
        m_s[...] = jnp.full(m_s.shape, floor, F32)
        l_s[...] = jnp.zeros(l_s.shape, F32)
        acc_s[...] = jnp.zeros(acc_s.shape, F32)

    heads = range(SA_H)
    hsl = [slice(h * SA_DH, (h + 1) * SA_DH) for h in heads]

    def update(kbs, vbs, keep):
        m_old = [m_s[h][:, 0:1] for h in heads]
        l_old = [l_s[h][:, 0:1] for h in heads]
        acc_old = [acc_s[:, hsl[h]] for h in heads]
        s = [lax.dot_general(q_ref[:, hsl[h]], kbs[h], _NT, preferred_element_type=F32) for h in heads]
        m_new = [jnp.maximum(m_old[h], jnp.max(jnp.where(keep, s[h], floor), axis=1, keepdims=True)) for h in heads]
        p = [jnp.where(keep, jnp.exp2(s[h] - m_new[h]), 0.0) for h in heads]
        pv = [jnp.dot(p[h].astype(BF16), vbs[h], preferred_element_type=F32) for h in heads]
        alpha = [jnp.exp2(m_old[h] - m_new[h]) for h in heads]
        l_new = [alpha[h] * l_old[h] + jnp.sum(p[h], axis=1, keepdims=True) for h in heads]
        for h in heads:
            acc_s[:, hsl[h]] = alpha[h] * acc_old[h] + pv[h]
            l_s[h] = jnp.broadcast_to(l_new[h], (rws, LANES))
            m_s[h] = jnp.broadcast_to(m_new[h], (rws, LANES))

    def head_rows(refs, h):
        return jnp.concatenate([r[0, pl.ds(h, PAGE_SIZE, stride=SA_H), :] for r in refs], axis=0).astype(BF16)

    update([head_rows(k_refs, h) for h in heads], [head_rows(v_refs, h) for h in heads], sel_ref[0] > 0.5)

    @pl.when(g == pl.num_programs(1) - 1)
    def _():
        update([knew_ref[pl.ds(h, rws, stride=SA_H), :].astype(BF16) for h in heads],
               [vnew_ref[pl.ds(h, rws, stride=SA_H), :].astype(BF16) for h in heads],
               seltail_ref[0][:, :rws] > 0.5)
        for h in heads:
            o_ref[:, hsl[h]] = (acc_s[:, hsl[h]] / l_s[h][:, 0:1] * _silu(z_ref[:, hsl[h]])).astype(BF16)


def _dsa_attend(page_table, saq, saz, k_new, v_new, sel, cache_k, cache_v):
    nreq, n_pages = page_table.shape
    n_past = n_pages * PAGE_SIZE
    npg = min(ATT_PAGES_PER_STEP, n_pages)
    prow = PAGE_SIZE * SA_H
    req = lambda b, g, pt: (b, 0)

    def page_map(i):
        return lambda b, g, pt: (pt[b, g * npg + i], 0, 0)

    page_specs = [pl.BlockSpec((1, prow, SA_DH), page_map(i)) for i in range(npg)]
    grid_spec = pltpu.PrefetchScalarGridSpec(
        num_scalar_prefetch=1,
        grid=(nreq, n_pages // npg),
        in_specs=[pl.BlockSpec((SAMPLE_ROWS, SA_W), req), pl.BlockSpec((SAMPLE_ROWS, SA_W), req),
                  pl.BlockSpec((SAMPLE_ROWS * SA_H, SA_DH), req), pl.BlockSpec((SAMPLE_ROWS * SA_H, SA_DH), req),
                  pl.BlockSpec((1, SAMPLE_ROWS, npg * PAGE_SIZE), lambda b, g, pt: (b, 0, g)),
                  pl.BlockSpec((1, SAMPLE_ROWS, LANES), lambda b, g, pt: (b, 0, n_past // LANES))]
                 + page_specs + page_specs,
        out_specs=pl.BlockSpec((SAMPLE_ROWS, SA_W), req),
        scratch_shapes=[pltpu.VMEM((SA_H, SAMPLE_ROWS, LANES), F32), pltpu.VMEM((SA_H, SAMPLE_ROWS, LANES), F32),
                        pltpu.VMEM((SAMPLE_ROWS, SA_W), F32)],
    )
    return pl.pallas_call(
        functools.partial(_att_kernel, npg=npg),
        out_shape=jax.ShapeDtypeStruct((nreq * SAMPLE_ROWS, SA_W), BF16),
        grid_spec=grid_spec,
        compiler_params=_cparams(("arbitrary", "arbitrary")),
        name="dsa_att",
    )(page_table, saq, saz, k_new, v_new, sel, sel, *([cache_k] * npg), *([cache_v] * npg))


def _mem_attend(q_ref, z_ref, mk_ref, mv_ref):
    mk = mk_ref[...].astype(BF16)
    mv = mv_ref[...].astype(BF16)
    outs = []
    for h in range(MEM_H):
        hs = slice(h * MEM_DH, (h + 1) * MEM_DH)
        s = lax.dot_general(q_ref[:, hs], mk[:, hs], _NT, preferred_element_type=F32)
        p = jnp.exp(s - jnp.max(s, axis=1, keepdims=True))
        l = jnp.sum(p, axis=1, keepdims=True)
        o = jnp.dot(p.astype(BF16), mv[:, hs], preferred_element_type=F32) / l
        outs.append((o * _silu(z_ref[:, hs])).astype(BF16))
    return jnp.concatenate(outs, axis=1)


def _memattn_kernel(q_ref, z_ref, mk_ref, mv_ref, o_ref):
    o_ref[...] = _mem_attend(q_ref, z_ref, mk_ref, mv_ref)


def _memattn(memq, memz, mk, mv, nb, tq):
    rows = memq.shape[0]
    n_mem = mk.shape[0] // nb
    nq = rows // (nb * tq)
    qblk = lambda b, i: (b * nq + i, 0)
    per_b = lambda b, i: (b, 0)
    return pl.pallas_call(
        _memattn_kernel,
        out_shape=jax.ShapeDtypeStruct((rows, MEM_W), BF16),
        grid=(nb, nq),
        in_specs=[pl.BlockSpec((tq, MEM_W), qblk), pl.BlockSpec((tq, MEM_W), qblk),
                  pl.BlockSpec((n_mem, MEM_W), per_b), pl.BlockSpec((n_mem, MEM_W), per_b)],
        out_specs=pl.BlockSpec((tq, MEM_W), qblk),
        compiler_params=_cparams(("arbitrary", "arbitrary")),
        name="memattn",
    )(memq, memz, mk, mv)


def _mixout_kernel(x_ref, a_ref, b_ref, c_ref, wa_ref, wb_ref, wc_ref, g_ref, y_ref):
    acc = (jnp.dot(a_ref[...], wa_ref[...], preferred_element_type=F32)
           + jnp.dot(b_ref[...], wb_ref[...], preferred_element_type=F32)
           + jnp.dot(c_ref[...], wc_ref[...], preferred_element_type=F32))
    y = acc * lax.rsqrt(jnp.mean(acc * acc, axis=-1, keepdims=True) + EPS) * g_ref[...]
    y_ref[...] = x_ref[...] + y


def _mixout_mem_kernel(x_ref, a_ref, b_ref, q_ref, z_ref, mk_ref, mv_ref, wa_ref, wb_ref, wc_ref, g_ref, y_ref):
    c = _mem_attend(q_ref, z_ref, mk_ref, mv_ref)
    acc = (jnp.dot(a_ref[...], wa_ref[...], preferred_element_type=F32)
           + jnp.dot(b_ref[...], wb_ref[...], preferred_element_type=F32)
           + jnp.dot(c, wc_ref[...], preferred_element_type=F32))
    y = acc * lax.rsqrt(jnp.mean(acc * acc, axis=-1, keepdims=True) + EPS) * g_ref[...]
    y_ref[...] = x_ref[...] + y


def _mixout_mem(x2d, a, b, memq, memz, mk, mv, w_out, g_post, tm, seq):
    rows = x2d.shape[0]
    n_mem = mk.shape[0] // (rows // seq)
    wb16 = w_out.astype(BF16)
    row = lambda i: (i, 0)
    const = lambda i: (0, 0)
    per_req = lambda i: (i // (seq // tm), 0)
    return pl.pallas_call(
        _mixout_mem_kernel,
        out_shape=jax.ShapeDtypeStruct((rows, D_MODEL), F32),
        grid=(rows // tm,),
        in_specs=[pl.BlockSpec((tm, D_MODEL), row), pl.BlockSpec((tm, ML_W), row),
                  pl.BlockSpec((tm, SA_W), row), pl.BlockSpec((tm, MEM_W), row), pl.BlockSpec((tm, MEM_W), row),
                  pl.BlockSpec((n_mem, MEM_W), per_req), pl.BlockSpec((n_mem, MEM_W), per_req),
                  pl.BlockSpec((ML_W, D_MODEL), const), pl.BlockSpec((SA_W, D_MODEL), const),
                  pl.BlockSpec((MEM_W, D_MODEL), const), pl.BlockSpec((1, D_MODEL), const)],
        out_specs=pl.BlockSpec((tm, D_MODEL), row),
        compiler_params=_cparams(("arbitrary",)),
        name="mixout_mem",
    )(x2d, a, b, memq, memz, mk, mv, wb16[:ML_W], wb16[ML_W:ML_W + SA_W], wb16[ML_W + SA_W:],
      g_post.reshape(1, D_MODEL))


def _mixout(x2d, a, b, c, w_out, g_post, tm):
    rows = x2d.shape[0]
    wb16 = w_out.astype(BF16)
    row = lambda i: (i, 0)
    const = lambda i: (0, 0)
    return pl.pallas_call(
        _mixout_kernel,
        out_shape=jax.ShapeDtypeStruct((rows, D_MODEL), F32),
        grid=(rows // tm,),
        in_specs=[pl.BlockSpec((tm, D_MODEL), row), pl.BlockSpec((tm, ML_W), row),
                  pl.BlockSpec((tm, SA_W), row), pl.BlockSpec((tm, MEM_W), row),
                  pl.BlockSpec((ML_W, D_MODEL), const), pl.BlockSpec((SA_W, D_MODEL), const),
                  pl.BlockSpec((MEM_W, D_MODEL), const), pl.BlockSpec((1, D_MODEL), const)],
        out_specs=pl.BlockSpec((tm, D_MODEL), row),
        compiler_params=_cparams(("arbitrary",)),
        name="mixout",
    )(x2d, a, b, c, wb16[:ML_W], wb16[ML_W:ML_W + SA_W], wb16[ML_W + SA_W:], g_post.reshape(1, D_MODEL))


def _layer(x_p, x_s, st_c, st_n, st_m, c_k, c_v, c_kidx, c_mk, c_mv, page_table, mem_prompt,
           g_pre, w_in, b_gates, g_head, w_mem_k, w_mem_v, g_mem, w_out, g_post):
    nb, seq, _ = x_p.shape
    nreq, t_dec, _ = x_s.shape
    n_mem = mem_prompt.shape[1]
    n_past = page_table.shape[1] * PAGE_SIZE
    weights = _relayout_w_in(w_in)

    tm = min(512, seq)
    tabs = _rope_tables(jnp.arange(seq, dtype=jnp.int32))
    x_p2d = x_p.reshape(nb * seq, D_MODEL)
    (_, _, saq, k, v, saz, memq, memz, idxq, kidx, small, small_t) = _project(
        x_p2d, g_pre, weights, tabs, tm, with_ml=False)
    a_p, p_c, p_n, p_m = _ml_fused(x_p2d, g_pre, weights[0], weights[3], weights[4], b_gates, g_head,
                                   nb, seq, min(ML_FUSED_ROWS, seq), min(ML_CHUNK, seq))
    b_p = _dsa_prompt(saq, saz, idxq, small_t, k, v, kidx, nb, seq)
    mk, mv = _memkv(mem_prompt.reshape(nb * n_mem, D_MODEL), g_mem, w_mem_k, w_mem_v, n_mem)
    y_p = _mixout_mem(x_p.reshape(nb * seq, D_MODEL), a_p, b_p, memq, memz, mk, mv, w_out, g_post,
                      tm, seq).reshape(nb, seq, D_MODEL)

    rws = SAMPLE_ROWS
    n_padrow = rws - t_dec
    xs_pad = jnp.concatenate([jnp.zeros((nreq, n_padrow, D_MODEL), F32), x_s], axis=1).reshape(nreq * rws, D_MODEL)
    pos_s = jnp.tile(jnp.concatenate([jnp.zeros((n_padrow,), jnp.int32),
                                      n_past + jnp.arange(t_dec, dtype=jnp.int32)]), nreq)
    tabs_s = _rope_tables(pos_s)
    (qkv_s, oz_s, saq_s, k_s, v_s, saz_s, memq_s, memz_s, idxq_s, kidx_s, small_s, small_t_s) = _project(
        xs_pad, g_pre, weights, tabs_s, nreq * rws)
    a_s, s_c, s_n, s_m = _mlstm(qkv_s, oz_s, small_s, small_t_s, b_gates, g_head, st_c, st_n, st_m,
                                nreq, rws, n_padrow)
    n_sel = min(TOPK_MAX, (n_past + t_dec) // 4)
    sel = _dsa_select(page_table, idxq_s, small_s, kidx_s, jnp.swapaxes(c_kidx, 1, 2), n_sel, t_dec)
    b_s = _dsa_attend(page_table, saq_s, saz_s, k_s, v_s, sel,
                      c_k.reshape(c_k.shape[0], PAGE_SIZE * SA_H, SA_DH),
                      c_v.reshape(c_v.shape[0], PAGE_SIZE * SA_H, SA_DH))
    c_s = _memattn(memq_s, memz_s, c_mk.reshape(nreq * n_mem, MEM_W), c_mv.reshape(nreq * n_mem, MEM_W), nreq, rws)
    y_s = _mixout(xs_pad, a_s, b_s, c_s, w_out, g_post, nreq * rws)

    def real(a2d):
        return a2d.reshape(nreq, rws, -1)[:, n_padrow:]

    new = (p_c, p_n, p_m,
           k.reshape(nb, seq, SA_H, SA_DH), v.reshape(nb, seq, SA_H, SA_DH), kidx.reshape(nb, seq, IDX_D),
           mk.reshape(nb, n_mem, MEM_H, MEM_DH), mv.reshape(nb, n_mem, MEM_H, MEM_DH),
           s_c, s_n, s_m,
           real(k_s).reshape(nreq, t_dec, SA_H, SA_DH), real(v_s).reshape(nreq, t_dec, SA_H, SA_DH), real(kidx_s))
    return y_p, real(y_s), new


def kernel(x_prompt, x_sample, state_mlstm_C, state_mlstm_n, state_mlstm_m, cache_k, cache_v, cache_kidx,
           cache_mem_k, cache_mem_v, page_table, mem_prompt, g_pre, w_in, b_gates, g_head, w_mem_k, w_mem_v,
           g_mem, w_out, g_post):
    xp, xs = x_prompt, x_sample
    per_layer = []
    for l in range(w_in.shape[0]):
        xp, xs, new = _layer(xp, xs, state_mlstm_C[l], state_mlstm_n[l], state_mlstm_m[l],
                             cache_k[l], cache_v[l], cache_kidx[l], cache_mem_k[l], cache_mem_v[l],
                             page_table, mem_prompt, g_pre[l], w_in[l], b_gates[l], g_head[l],
                             w_mem_k[l], w_mem_v[l], g_mem[l], w_out[l], g_post[l])
        per_layer.append(new)
    stacked = [jnp.stack(a) for a in zip(*per_layer)]
    return (xp, xs, *stacked)
```

```python
import functools
import math

import jax
import jax.numpy as jnp
from jax import lax
from jax.experimental import pallas as pl
from jax.experimental.pallas import tpu as pltpu

F32 = jnp.float32
BF16 = jnp.bfloat16

D_MODEL = 2048
ML_H = 4
ML_W = D_MODEL // 2
ML_DH = ML_W // ML_H
SA_H = 4
SA_W = D_MODEL // 4
SA_DH = SA_W // SA_H
MEM_H = 4
MEM_W = D_MODEL // 4
MEM_DH = MEM_W // MEM_H
IDX_H = 8
IDX_D = 64
IDX_SCALE = (IDX_H * IDX_D) ** -0.5
TOPK_MAX = 256
ROPE_THETA = 10000.0
LOG2E = 1.4426950408889634
EPS = 1e-6
PAGE_SIZE = 128

LANES = 128
CB = 512
SMALL_IG = 64
SMALL_LF = 68
SMALL_W = 72
SAMPLE_ROWS = 16
SEL_ROWS = 8
SEL_PAGES_PER_STEP = 64
ATT_PAGES_PER_STEP = 32
ML_CHUNK = 256
IDX_KEY_CHUNK = 256
DSA_KEY_STEP = 256
ML_FUSED_ROWS = 256
TIE_BLOCK = 256
ONES_ROWS = 16
VMEM_LIMIT = 56 * 1024 * 1024
VMEM_LIMIT_FUSED = 60 * 1024 * 1024
NEG_INF = float("-inf")
POS_INF = float("inf")

_NT = (((1,), (1,)), ((), ()))
_TN = (((0,), (0,)), ((), ()))


def _cparams(sem):
    return pltpu.CompilerParams(dimension_semantics=sem, vmem_limit_bytes=VMEM_LIMIT)


def _sigmoid(x):
    return 1.0 / (1.0 + jnp.exp(-x))


def _silu(x):
    return x * _sigmoid(x)


def _log_sigmoid(x):
    return jnp.minimum(x, 0.0) - jnp.log1p(jnp.exp(-jnp.abs(x)))


def _split3(x):
    hi = x.astype(BF16)
    r = x - hi.astype(F32)
    mid = r.astype(BF16)
    lo = (r - mid.astype(F32)).astype(BF16)
    return hi, mid, lo


def _rope128(x, cos, sin_signed):
    return x * cos + pltpu.roll(x, 64, 1) * sin_signed


def _rope64(x, cos, sin_signed):
    lane = lax.broadcasted_iota(jnp.int32, x.shape, 1)
    first_half = (lane % 64) < 32
    partner = jnp.where(first_half, pltpu.roll(x, 96, 1), pltpu.roll(x, 32, 1))
    return x * cos + partner * sin_signed


def _normed(x_ref, g_ref):
    x = x_ref[...]
    return (x * lax.rsqrt(jnp.mean(x * x, axis=-1, keepdims=True) + EPS) * g_ref[...]).astype(BF16)


def _proj_ml_kernel(x_ref, g_ref, w_ref, qkv_ref, oz_ref, u_ref):
    u_ref[...] = _normed(x_ref, g_ref)
    n_qkv = 3 * ML_W // CB
    for cb in range(5 * ML_W // CB):
        acc = lax.dot_general(u_ref[...], w_ref[cb * CB:(cb + 1) * CB, :], _NT, preferred_element_type=F32)
        if ML_W <= cb * CB < 2 * ML_W:
            acc = acc * (ML_DH ** -0.5)
        if cb < n_qkv:
            qkv_ref[:, cb * CB:(cb + 1) * CB] = acc.astype(BF16)
        else:
            oz_ref[:, (cb - n_qkv) * CB:(cb - n_qkv + 1) * CB] = acc


def _proj_rest_kernel(x_ref, g_ref, wsa_ref, wmem_ref, ws_ref, wst_ref, c128_ref, s128_ref, c64_ref, s64_ref,
                      saq_ref, k_ref, v_ref, saz_ref, memq_ref, memz_ref, idxq_ref,
                      kidx_ref, small_ref, smallt_ref, u_ref):
    u_ref[...] = _normed(x_ref, g_ref)
    sm = jnp.dot(u_ref[...], ws_ref[...], preferred_element_type=F32)
    small_ref[...] = sm
    kidx_ref[...] = _rope64(sm, c64_ref[...], s64_ref[...])[:, :IDX_D]
    smallt_ref[...] = lax.dot_general(wst_ref[...], u_ref[...], _NT, preferred_element_type=F32)

    def block(cb, w_ref=wsa_ref):
        return lax.dot_general(u_ref[...], w_ref[cb * CB:(cb + 1) * CB, :], _NT, preferred_element_type=F32)

    def rope_heads(acc, fn, cos_ref, sin_ref):
        return jnp.concatenate(
            [fn(acc[:, h * LANES:(h + 1) * LANES], cos_ref[...], sin_ref[...]) for h in range(CB // LANES)], axis=1)

    saq_ref[...] = (rope_heads(block(0), _rope128, c128_ref, s128_ref) * (SA_DH ** -0.5 * LOG2E)).astype(BF16)
    tm = x_ref.shape[0]
    k_acc = block(1)
    v_acc = block(2)
    for h in range(SA_H):
        lanes = slice(h * SA_DH, (h + 1) * SA_DH)
        k_ref[pl.ds(h, tm, stride=SA_H), :] = _rope128(k_acc[:, lanes], c128_ref[...], s128_ref[...])
        v_ref[pl.ds(h, tm, stride=SA_H), :] = v_acc[:, lanes]
    saz_ref[...] = block(3)
    idxq_ref[...] = rope_heads(block(4), _rope64, c64_ref, s64_ref).astype(BF16)
    memq_ref[...] = (block(0, wmem_ref) * (MEM_DH ** -0.5)).astype(BF16)
    memz_ref[...] = block(1, wmem_ref)


def _project(x2d, g_pre, weights, tabs, tm, with_ml=True):
    w_ml, w_sa, w_mem, w_small, w_small_t = weights
    rows = x2d.shape[0]
    c128, s128, c64, s64 = tabs
    ntab = c128.shape[0] // tm
    n_ml = 5 * ML_W
    row_only = lambda i: (i, 0)
    tab_map = lambda i: (i % ntab, 0)
    const = lambda i: (0, 0)
    resident = pl.Buffered(1)
    g2d = g_pre.reshape(1, D_MODEL)

    qkv, oz = (None, None) if not with_ml else pl.pallas_call(
        _proj_ml_kernel,
        out_shape=(jax.ShapeDtypeStruct((rows, 3 * ML_W), BF16),
                   jax.ShapeDtypeStruct((rows, 2 * ML_W), F32)),
        grid=(rows // tm,),
        in_specs=[pl.BlockSpec((tm, D_MODEL), row_only),
                  pl.BlockSpec((1, D_MODEL), const),
                  pl.BlockSpec((n_ml, D_MODEL), const, pipeline_mode=resident)],
        out_specs=(pl.BlockSpec((tm, 3 * ML_W), row_only), pl.BlockSpec((tm, 2 * ML_W), row_only)),
        scratch_shapes=[pltpu.VMEM((tm, D_MODEL), BF16)],
        compiler_params=_cparams(("arbitrary",)),
        name="proj_ml",
    )(x2d, g2d, w_ml)

    out_shape = (
        jax.ShapeDtypeStruct((rows, SA_W), BF16),
        jax.ShapeDtypeStruct((rows * SA_H, SA_DH), F32),
        jax.ShapeDtypeStruct((rows * SA_H, SA_DH), F32),
        jax.ShapeDtypeStruct((rows, SA_W), F32),
        jax.ShapeDtypeStruct((rows, MEM_W), BF16),
        jax.ShapeDtypeStruct((rows, MEM_W), F32),
        jax.ShapeDtypeStruct((rows, IDX_H * IDX_D), BF16),
        jax.ShapeDtypeStruct((rows, IDX_D), F32),
        jax.ShapeDtypeStruct((rows, LANES), F32),
        jax.ShapeDtypeStruct((LANES, rows), F32),
    )
    out_specs = (
        pl.BlockSpec((tm, CB), row_only),
        pl.BlockSpec((tm * SA_H, SA_DH), row_only),
        pl.BlockSpec((tm * SA_H, SA_DH), row_only),
        pl.BlockSpec((tm, CB), row_only),
        pl.BlockSpec((tm, CB), row_only),
        pl.BlockSpec((tm, CB), row_only),
        pl.BlockSpec((tm, CB), row_only),
        pl.BlockSpec((tm, IDX_D), row_only),
        pl.BlockSpec((tm, LANES), row_only),
        pl.BlockSpec((LANES, tm), lambda i: (0, i)),
    )
    in_specs = [
        pl.BlockSpec((tm, D_MODEL), row_only),
        pl.BlockSpec((1, D_MODEL), const),
        pl.BlockSpec(w_sa.shape, const, pipeline_mode=resident),
        pl.BlockSpec(w_mem.shape, const, pipeline_mode=resident),
        pl.BlockSpec((D_MODEL, LANES), const, pipeline_mode=resident),
        pl.BlockSpec((LANES, D_MODEL), const, pipeline_mode=resident),
        pl.BlockSpec((tm, LANES), tab_map),
        pl.BlockSpec((tm, LANES), tab_map),
        pl.BlockSpec((tm, LANES), tab_map),
        pl.BlockSpec((tm, LANES), tab_map),
    ]
    rest = pl.pallas_call(
        _proj_rest_kernel,
        out_shape=out_shape,
        grid=(rows // tm,),
        in_specs=in_specs,
        out_specs=out_specs,
        scratch_shapes=[pltpu.VMEM((tm, D_MODEL), BF16)],
        compiler_params=_cparams(("arbitrary",)),
        name="proj_rest",
    )(x2d, g2d, w_sa, w_mem, w_small, w_small_t, c128, s128, c64, s64)
    return (qkv, oz, *rest)


def _rope_tables(pos):
    def tab(half):
        inv = ROPE_THETA ** (-jnp.arange(half, dtype=F32) / half)
        ang = pos.astype(F32)[:, None] * inv[None, :]
        return jnp.cos(ang), jnp.sin(ang)

    c, s = tab(SA_DH // 2)
    c128 = jnp.concatenate([c, c], axis=1)
    s128 = jnp.concatenate([-s, s], axis=1)
    c, s = tab(IDX_D // 2)
    c64 = jnp.concatenate([c, c, c, c], axis=1)
    s64 = jnp.concatenate([-s, s, -s, s], axis=1)
    return c128, s128, c64, s64


def _relayout_w_in(w_in):
    off = {}
    o = 0
    for name, w in (('ml_q', ML_W), ('ml_k', ML_W), ('ml_v', ML_W), ('ml_o', ML_W), ('ml_z', ML_W),
                    ('ml_i', ML_H), ('ml_f', ML_H), ('sa_q', SA_W), ('sa_k', SA_W), ('sa_v', SA_W),
                    ('sa_z', SA_W), ('idx_q', IDX_H * IDX_D), ('idx_k', IDX_D), ('idx_w', IDX_H),
                    ('mem_q', MEM_W), ('mem_z', MEM_W)):
        off[name] = (o, w)
        o += w

    w_t = w_in.T

    def col(name):
        a, w = off[name]
        return w_t[a:a + w]

    def span(first, last):
        return w_t[off[first][0]:off[last][0] + off[last][1]].astype(BF16)

    w_ml = w_t.astype(BF16)
    w_sa = span('sa_q', 'idx_q')
    w_mem = span('mem_q', 'mem_z')
    small_t = jnp.concatenate([
        col('idx_k'), col('ml_i'), col('ml_f'), col('idx_w'),
        jnp.zeros((LANES - IDX_D - 2 * ML_H - IDX_H, D_MODEL), F32)], axis=0).astype(BF16)
    return (w_ml, w_sa, w_mem, small_t.T, small_t)


def _memkv_kernel(m_ref, g_ref, wk_ref, wv_ref, k_ref, v_ref):
    x = m_ref[...]
    u = (x * lax.rsqrt(jnp.mean(x * x, axis=-1, keepdims=True) + EPS) * g_ref[...]).astype(BF16)
    k_ref[...] = jnp.dot(u, wk_ref[...], preferred_element_type=F32)
    v_ref[...] = jnp.dot(u, wv_ref[...], preferred_element_type=F32)


def _memkv(mem2d, g_mem, wk, wv, n_mem):
    rows = mem2d.shape[0]
    row = lambda i: (i, 0)
    const = lambda i: (0, 0)
    return pl.pallas_call(
        _memkv_kernel,
        out_shape=(jax.ShapeDtypeStruct((rows, MEM_W), F32), jax.ShapeDtypeStruct((rows, MEM_W), F32)),
        grid=(rows // n_mem,),
        in_specs=[pl.BlockSpec((n_mem, D_MODEL), row), pl.BlockSpec((1, D_MODEL), const),
                  pl.BlockSpec((D_MODEL, MEM_W), const), pl.BlockSpec((D_MODEL, MEM_W), const)],
        out_specs=(pl.BlockSpec((n_mem, MEM_W), row), pl.BlockSpec((n_mem, MEM_W), row)),
        compiler_params=_cparams(("arbitrary",)),
        name="memkv",
    )(mem2d, g_mem.reshape(1, D_MODEL), wk.astype(BF16), wv.astype(BF16))


def _mlstm_chunk(q_of, k_of, v_of, gate_of, g_c, g_r, gh_ref, c_s, n_s, m_s, a_store, c, n_pad,
                 between_heads=None):
    ri = lax.broadcasted_iota(jnp.int32, (c, c), 0)
    cj = lax.broadcasted_iota(jnp.int32, (c, c), 1)
    causal = cj <= ri
    tri = jnp.where(causal, 1.0, 0.0).astype(BF16)
    tri_t = jnp.where(ri <= cj, 1.0, 0.0).astype(BF16)

    pad_c = lax.broadcasted_iota(jnp.int32, (c, LANES), 0) < n_pad
    ig_c = jnp.where(pad_c, NEG_INF, g_c)
    lf_c = jnp.where(pad_c, 0.0, _log_sigmoid(g_c))
    b_c = sum(jnp.dot(tri, p, preferred_element_type=F32) for p in _split3(lf_c))
    pad_r = lax.broadcasted_iota(jnp.int32, (SAMPLE_ROWS, c), 1) < n_pad
    ig_r = jnp.where(pad_r, NEG_INF, g_r)
    lf_r = jnp.where(pad_r, 0.0, _log_sigmoid(g_r))
    b_r = sum(jnp.dot(p, tri_t, preferred_element_type=F32) for p in _split3(lf_r))

    m_all = m_s[...]
    n_all = n_s[...]
    c_all = [c_s[h] for h in range(ML_H)]
    new_state = []
    for h in range(ML_H):
        hs = slice(h * ML_DH, (h + 1) * ML_DH)
        m_prev = m_all[h:h + 1, 0:1]
        b_t = b_c[:, SMALL_LF + h:SMALL_LF + h + 1]
        igc = ig_c[:, SMALL_IG + h:SMALL_IG + h + 1]
        b_s = b_r[ML_H + h:ML_H + h + 1, :]
        igr = ig_r[h:h + 1, :]
        a = jnp.where(causal, b_t - b_s + igr, NEG_INF)
        bm = b_t + m_prev
        m_t = jnp.maximum(bm, jnp.max(a, axis=1, keepdims=True))
        inter = jnp.exp(bm - m_t)
        dmat = jnp.exp(a - m_t)
        q, k, v = q_of(h), k_of(h), v_of(h)
        s = lax.dot_general(q, k, _NT, preferred_element_type=F32) * dmat
        if between_heads is not None:
            between_heads()
        c_h = c_all[h]
        n_h = n_all[h:h + 1, :]
        num = (jnp.dot(s.astype(BF16), v, preferred_element_type=F32)
               + inter * jnp.dot(q, c_h.astype(BF16), preferred_element_type=F32))
        qn = (jnp.sum(s, axis=1, keepdims=True)
              + inter * jnp.sum(q.astype(F32) * n_h, axis=1, keepdims=True))
        hh = num / jnp.maximum(jnp.abs(qn), jnp.exp(-m_t))
        hh = hh * lax.rsqrt(jnp.mean(hh * hh, axis=1, keepdims=True) + EPS)
        if between_heads is not None:
            between_heads()
        a_store(h, (hh * gh_ref[:, hs] * gate_of(h)).astype(BF16))

        m_new = m_t[c - 1:c, :]
        b_last = b_t[c - 1:c, :]
        w_end = jnp.exp(b_last - b_t + igc - m_new)
        decay = jnp.exp(b_last + m_prev - m_new)
        kw = k.astype(F32) * w_end
        new_state.append((decay * c_h + lax.dot_general(kw.astype(BF16), v, _TN, preferred_element_type=F32),
                          decay * n_h + jnp.sum(kw, axis=0, keepdims=True),
                          jnp.broadcast_to(m_new, (1, LANES))))
        if between_heads is not None:
            between_heads()

    for h, (c_new, n_new, m_new) in enumerate(new_state):
        c_s[h] = c_new
        n_s[h:h + 1, :] = n_new
        m_s[h:h + 1, :] = m_new


def _out_gate(o, z):
    return z / ((1.0 + jnp.exp(-o)) * (1.0 + jnp.exp(-z)))


def _head_cols(group, h):
    return slice(group * ML_W + h * ML_DH, group * ML_W + (h + 1) * ML_DH)


def _mlstm_kernel(qkv_ref, oz_ref, gc_ref, gt_ref, bcol_ref, brow_ref, gh_ref, c0_ref, n0_ref, m0_ref,
                  a_ref, cout_ref, nout_ref, mout_ref, c_s, n_s, m_s, *, c, n_pad):
    ci = pl.program_id(1)

    @pl.when(ci == 0)
    def _():
        c_s[...] = c0_ref[0]
        n_s[...] = n0_ref[0]
        m_s[...] = m0_ref[0]

    def a_store(h, value):
        a_ref[:, _head_cols(0, h)] = value

    _mlstm_chunk(lambda h: qkv_ref[:, _head_cols(0, h)], lambda h: qkv_ref[:, _head_cols(1, h)],
                 lambda h: qkv_ref[:, _head_cols(2, h)],
                 lambda h: _out_gate(oz_ref[:, _head_cols(0, h)], oz_ref[:, _head_cols(1, h)]),
                 gc_ref[...] + bcol_ref[...], gt_ref[0] + brow_ref[...], gh_ref, c_s, n_s, m_s, a_store, c, n_pad)

    @pl.when(ci == pl.num_programs(1) - 1)
    def _():
        cout_ref[0] = c_s[...]
        nout_ref[0] = n_s[...]
        mout_ref[0] = m_s[...]


def _ml_fused_kernel(x_ref, g_ref, w_ref, ws_ref, wst_ref, bcol_ref, brow_ref, gh_ref,
                     a_ref, cout_ref, nout_ref, mout_ref,
                     u_s, qkv_a, qkv_b, gate_a, gate_b, sm_a, sm_b, smt_a, smt_b, c_s, n_s, m_s, *, c, nrb, n_blocks):
    i = pl.program_id(0)
    tm = x_ref.shape[0]

    @pl.when(i == 0)
    def _():
        qkv_b[...] = jnp.zeros(qkv_b.shape, BF16)
        gate_b[...] = jnp.zeros(gate_b.shape, F32)
        sm_b[...] = jnp.zeros(sm_b.shape, F32)
        smt_b[...] = jnp.zeros(smt_b.shape, F32)

    @pl.when((i == 0) | (lax.rem(jnp.maximum(i - 1, 0), nrb) == 0))
    def _():
        c_s[...] = jnp.zeros(c_s.shape, F32)
        n_s[...] = jnp.zeros(n_s.shape, F32)
        m_s[...] = jnp.zeros(m_s.shape, F32)

    def projection_pieces(qkv_w, gate_w, sm_w, smt_w):
        pw = ML_DH

        def block(row0):
            return lax.dot_general(u_s[...], w_ref[row0:row0 + pw, :], _NT, preferred_element_type=F32)

        def narrow():
            sm_w[...] = jnp.dot(u_s[...], ws_ref[...], preferred_element_type=F32)
            smt_w[...] = lax.dot_general(wst_ref[SMALL_IG:SMALL_IG + SAMPLE_ROWS, :], u_s[...], _NT,
                                         preferred_element_type=F32)

        def qkv_block(cb):
            acc = block(cb * pw)
            if ML_W <= cb * pw < 2 * ML_W:
                acc = acc * (ML_DH ** -0.5)
            qkv_w[:, cb * pw:(cb + 1) * pw] = acc.astype(BF16)

        def gate_block(cb):
            gate_w[:, cb * pw:(cb + 1) * pw] = _out_gate(block(3 * ML_W + cb * pw), block(4 * ML_W + cb * pw))

        pieces = [narrow]
        pieces += [functools.partial(qkv_block, cb) for cb in range(3 * ML_W // pw)]
        pieces += [functools.partial(gate_block, cb) for cb in range(ML_W // pw)]
        return pieces

    def step(write, read):
        pieces = []
        if write is not None:
            u_s[...] = _normed(x_ref, g_ref)
            pieces = projection_pieces(*write)
        n_slots = (tm // c) * ML_H * 3
        n_pieces = len(pieces)
        slot = [0]

        def emit():
            slot[0] += 1
            while n_pieces - len(pieces) < (slot[0] * n_pieces) // n_slots:
                pieces.pop(0)()

        qkv_r, gate_r, sm_r, smt_r = read if read is not None else (None,) * 4
        for ck in range(tm // c if read is not None else 0):
            rows = slice(ck * c, (ck + 1) * c)

            def a_store(h, value, rows=rows):
                a_ref[rows, _head_cols(0, h)] = value

            _mlstm_chunk(lambda h, rows=rows: qkv_r[rows, _head_cols(0, h)],
                         lambda h, rows=rows: qkv_r[rows, _head_cols(1, h)],
                         lambda h, rows=rows: qkv_r[rows, _head_cols(2, h)],
                         lambda h, rows=rows: gate_r[rows, _head_cols(0, h)],
                         sm_r[rows, :] + bcol_ref[...], smt_r[:, rows] + brow_ref[...],
                         gh_ref, c_s, n_s, m_s, a_store, c, 0, between_heads=emit)
        while pieces:
            pieces.pop(0)()

    set_a = (qkv_a, gate_a, sm_a, smt_a)
    set_b = (qkv_b, gate_b, sm_b, smt_b)

    @pl.when(lax.rem(i, 2) == 0)
    def _():
        step(set_a, set_b)

    @pl.when(lax.rem(i, 2) == 1)
    def _():
        step(set_b, set_a)

    @pl.when((i >= 1) & (lax.rem(jnp.maximum(i - 1, 0), nrb) == nrb - 1))
    def _():
        cout_ref[0] = c_s[...]
        nout_ref[0] = n_s[...]
        mout_ref[0] = m_s[...]


def _ml_fused(x2d, g_pre, w_all, w_small, w_small_t, b_gates, g_head, nb, seq, tm, c):
    rows = x2d.shape[0]
    nrb = seq // tm
    n_blocks = rows // tm
    n_ml = 5 * ML_W
    bias_col = jnp.zeros((1, LANES), F32).at[0, SMALL_IG:SMALL_IG + 2 * ML_H].set(b_gates)
    bias_row = jnp.zeros((SAMPLE_ROWS, 1), F32).at[:2 * ML_H, 0].set(b_gates)
    const = lambda i: (0, 0)
    resident = pl.Buffered(1)
    lagged = lambda i: jnp.maximum(i - 1, 0)
    out_shape = (
        jax.ShapeDtypeStruct((rows, ML_W), BF16),
        jax.ShapeDtypeStruct((nb, ML_H, ML_DH, ML_DH), F32),
        jax.ShapeDtypeStruct((nb, ML_H, ML_DH), F32),
        jax.ShapeDtypeStruct((nb, 8, LANES), F32),
    )
    a, c_out, n_out, m_out = pl.pallas_call(
        functools.partial(_ml_fused_kernel, c=c, nrb=nrb, n_blocks=n_blocks),
        out_shape=out_shape,
        grid=(n_blocks + 1,),
        in_specs=[
            pl.BlockSpec((tm, D_MODEL), lambda i: (jnp.minimum(i, n_blocks - 1), 0)),
            pl.BlockSpec((1, D_MODEL), const),
            pl.BlockSpec((n_ml, D_MODEL), const, pipeline_mode=resident),
            pl.BlockSpec((D_MODEL, LANES), const, pipeline_mode=resident),
            pl.BlockSpec((LANES, D_MODEL), const, pipeline_mode=resident),
            pl.BlockSpec((1, LANES), const),
            pl.BlockSpec((SAMPLE_ROWS, 1), const),
            pl.BlockSpec((1, ML_W), const),
        ],
        out_specs=(
            pl.BlockSpec((tm, ML_W), lambda i: (lagged(i), 0)),
            pl.BlockSpec((1, ML_H, ML_DH, ML_DH), lambda i: (lagged(i) // nrb, 0, 0, 0)),
            pl.BlockSpec((1, ML_H, ML_DH), lambda i: (lagged(i) // nrb, 0, 0)),
            pl.BlockSpec((1, 8, LANES), lambda i: (lagged(i) // nrb, 0, 0)),
        ),
        scratch_shapes=[pltpu.VMEM((tm, D_MODEL), BF16),
                        pltpu.VMEM((tm, 3 * ML_W), BF16), pltpu.VMEM((tm, 3 * ML_W), BF16),
                        pltpu.VMEM((tm, ML_W), F32), pltpu.VMEM((tm, ML_W), F32),
                        pltpu.VMEM((tm, LANES), F32), pltpu.VMEM((tm, LANES), F32),
                        pltpu.VMEM((SAMPLE_ROWS, tm), F32), pltpu.VMEM((SAMPLE_ROWS, tm), F32),
                        pltpu.VMEM((ML_H, ML_DH, ML_DH), F32), pltpu.VMEM((ML_H, ML_DH), F32),
                        pltpu.VMEM((8, LANES), F32)],
        compiler_params=pltpu.CompilerParams(dimension_semantics=("arbitrary",), vmem_limit_bytes=VMEM_LIMIT_FUSED),
        name="ml_fused",
    )(x2d, g_pre.reshape(1, D_MODEL), w_all, w_small, w_small_t, bias_col, bias_row, g_head.reshape(1, ML_W))
    return a, c_out, n_out, m_out[:, :ML_H, 0]


def _mlstm(qkv, oz, small, small_t, b_gates, g_head, c0, n0, m0, nb, c, n_pad):
    rows = qkv.shape[0]
    nc = rows // (nb * c)
    bias_col = jnp.zeros((1, LANES), F32).at[0, SMALL_IG:SMALL_IG + 2 * ML_H].set(b_gates)
    bias_row = jnp.zeros((SAMPLE_ROWS, 1), F32).at[:2 * ML_H, 0].set(b_gates)
    m0b = jnp.zeros((nb, 8, LANES), F32).at[:, :ML_H, :].set(jnp.broadcast_to(m0[:, :, None], (nb, ML_H, LANES)))
    rowblk = lambda b, i: (b * nc + i, 0)
    const = lambda b, i: (0, 0)
    gates_t = small_t[SMALL_IG:SMALL_IG + SAMPLE_ROWS].reshape(SAMPLE_ROWS, rows // c, c).transpose(1, 0, 2)
    out_shape = (
        jax.ShapeDtypeStruct((rows, ML_W), BF16),
        jax.ShapeDtypeStruct((nb, ML_H, ML_DH, ML_DH), F32),
        jax.ShapeDtypeStruct((nb, ML_H, ML_DH), F32),
        jax.ShapeDtypeStruct((nb, 8, LANES), F32),
    )
    st4 = lambda b, i: (b, 0, 0, 0)
    st3 = lambda b, i: (b, 0, 0)
    a, c_out, n_out, m_out = pl.pallas_call(
        functools.partial(_mlstm_kernel, c=c, n_pad=n_pad),
        out_shape=out_shape,
        grid=(nb, nc),
        in_specs=[
            pl.BlockSpec((c, 3 * ML_W), rowblk),
            pl.BlockSpec((c, 2 * ML_W), rowblk),
            pl.BlockSpec((c, LANES), rowblk),
            pl.BlockSpec((1, SAMPLE_ROWS, c), lambda b, i: (b * nc + i, 0, 0)),
            pl.BlockSpec((1, LANES), const),
            pl.BlockSpec((SAMPLE_ROWS, 1), const),
            pl.BlockSpec((1, ML_W), const),
            pl.BlockSpec((1, ML_H, ML_DH, ML_DH), st4),
            pl.BlockSpec((1, ML_H, ML_DH), st3),
            pl.BlockSpec((1, 8, LANES), st3),
        ],
        out_specs=(
            pl.BlockSpec((c, ML_W), rowblk),
            pl.BlockSpec((1, ML_H, ML_DH, ML_DH), st4),
            pl.BlockSpec((1, ML_H, ML_DH), st3),
            pl.BlockSpec((1, 8, LANES), st3),
        ),
        scratch_shapes=[pltpu.VMEM((ML_H, ML_DH, ML_DH), F32), pltpu.VMEM((ML_H, ML_DH), F32),
                        pltpu.VMEM((8, LANES), F32)],
        compiler_params=_cparams(("arbitrary", "arbitrary")),
        name="mlstm",
    )(qkv, oz, small, gates_t, bias_col, bias_row, g_head.reshape(1, ML_W), c0, n0, m0b)
    return a, c_out, n_out, m_out[:, :ML_H, 0]


_REDUCERS = {"sum": (jnp.sum, jnp.add), "max": (jnp.max, jnp.maximum), "min": (jnp.min, jnp.minimum)}
REDUCE_CHAINS = 8


def _reduce(x, axis, op):
    fn, combine = _REDUCERS[op]
    unit = 8 if axis == 0 else LANES
    n = x.shape[axis]
    units = n // unit
    if n % unit or units < 2 * REDUCE_CHAINS:
        return fn(x, axis=axis, keepdims=True)
    base, rem = divmod(units, REDUCE_CHAINS)
    parts, start = [], 0
    for i in range(REDUCE_CHAINS):
        size = (base + (1 if i < rem else 0)) * unit
        piece = x[start:start + size] if axis == 0 else x[:, start:start + size]
        parts.append(fn(piece, axis=axis, keepdims=True))
        start += size
    while len(parts) > 1:
        parts = [combine(parts[i], parts[i + 1]) for i in range(0, len(parts), 2)]
    return parts[0]


def _count(pred, axis):
    return _reduce(jnp.where(pred, 1.0, 0.0), axis, "sum")


def _kth_largest(x_ref, k, axis, n_bisect, quarter_steps=False):
    kf = float(k)
    x = x_ref[...]
    hi = _reduce(x, axis, "max")
    lo = _reduce(jnp.where(x == NEG_INF, POS_INF, x), axis, "min")

    def bisect(_, carry):
        lo, hi = carry
        mid = 0.5 * (lo + hi)
        ge = _count(x_ref[...] >= mid, axis) >= kf
        return jnp.where(ge, mid, lo), jnp.where(ge, hi, mid)

    def quarter(_, carry):
        lo, hi = carry
        w = hi - lo
        m1, m2, m3 = lo + 0.25 * w, lo + 0.5 * w, lo + 0.75 * w
        xx = x_ref[...]
        g1, g2, g3 = (_count(xx >= m, axis) >= kf for m in (m1, m2, m3))
        lo = jnp.where(g3, m3, jnp.where(g2, m2, jnp.where(g1, m1, lo)))
        hi = jnp.where(g3, hi, jnp.where(g2, m3, jnp.where(g1, m2, m1)))
        return lo, hi

    if quarter_steps:
        lo, hi = lax.fori_loop(0, (n_bisect + 1) // 2, quarter, (lo, hi))
    else:
        lo, hi = lax.fori_loop(0, n_bisect, bisect, (lo, hi))

    def finished(cmin, xx):
        return jnp.where((_count(xx > cmin, axis) < kf) | (cmin == POS_INF), 1.0, 0.0)

    xx = x_ref[...]
    thr = _reduce(jnp.where(xx >= lo, xx, POS_INF), axis, "min")
    done = finished(thr, xx)

    def cond(st):
        return st[2] < 0.5

    def body(st):
        thr, done, _ = st
        xx = x_ref[...]
        cmin = _reduce(jnp.where(xx > thr, xx, POS_INF), axis, "min")
        thr = jnp.where(done < 0.5, cmin, thr)
        done = jnp.maximum(done, finished(thr, xx))
        return thr, done, jnp.min(done)

    thr, _, _ = lax.while_loop(cond, body, (thr, done, jnp.min(done)))
    return thr


def _dsa_kernel(q_ref, z_ref, idxq_ref, wt_ref, k_ref, v_ref, kidx_ref, o_ref,
                kb_s, vt_s, kib_s, x_s, sel_s, *, n_keys, qb, n_sel, n_bisect, key_step):
    j = pl.program_id(1)

    @pl.when(j == 0)
    def _():
        for h in range(SA_H):
            lanes = slice(h * SA_DH, (h + 1) * SA_DH)
            kb_s[:, lanes] = k_ref[pl.ds(h, n_keys, stride=SA_H), :].astype(BF16)
            vt_s[h, 0:SA_DH, :] = v_ref[pl.ds(h, n_keys, stride=SA_H), :].T.astype(BF16)
            vt_s[h, SA_DH:SA_DH + ONES_ROWS, :] = jnp.ones((ONES_ROWS, n_keys), BF16)
        kib_s[...] = kidx_ref[...].astype(BF16)

    def attend(nk):
        xs = x_s.at[0:nk]
        ss = sel_s.at[0:nk]
        key = lax.broadcasted_iota(jnp.int32, (nk, qb), 0)
        qpos = j * qb + lax.broadcasted_iota(jnp.int32, (nk, qb), 1)
        valid = key <= qpos
        qcat = jnp.concatenate([idxq_ref[:, h * IDX_D:(h + 1) * IDX_D] for h in range(IDX_H)], axis=0)
        w_rows = [wt_ref[h:h + 1, :] * IDX_SCALE for h in range(IDX_H)]
        kc = math.gcd(IDX_KEY_CHUNK, nk)
        for c0 in range(0, nk, kc):
            d = lax.dot_general(kib_s[c0:c0 + kc, :], qcat, _NT, preferred_element_type=F32)
            sc = jnp.zeros((kc, qb), F32)
            for h in range(IDX_H):
                sc = sc + jnp.maximum(d[:, h * qb:(h + 1) * qb], 0.0) * w_rows[h]
            ok = (c0 + lax.broadcasted_iota(jnp.int32, (kc, qb), 0)) <= (
                j * qb + lax.broadcasted_iota(jnp.int32, (kc, qb), 1))
            x_s[c0:c0 + kc, :] = jnp.where(ok, sc, NEG_INF)
            sel_s[c0:c0 + kc, :] = jnp.where(ok, 1.0, 0.0)

        @pl.when((j + 1) * qb > n_sel)
        def _():
            kf = float(n_sel)
            thr = _kth_largest(xs, n_sel, 0, n_bisect)
            x = xs[...]
            need = kf - _count(x > thr, 0)
            n_tie = _count(x == thr, 0)
            qrow = j * qb + lax.broadcasted_iota(jnp.int32, (1, qb), 1)
            small = (qrow + 1) <= n_sel
            ss[...] = jnp.where(small, jnp.where(valid, 1.0, 0.0), jnp.where(x >= thr, 1.0, 0.0))
            excess = jnp.max(jnp.where((n_tie > need) & jnp.logical_not(small), 1.0, 0.0))

            @pl.when(excess > 0.5)
            def _():
                tb = math.gcd(TIE_BLOCK, nk)
                r_i = lax.broadcasted_iota(jnp.int32, (tb, tb), 0)
                c_i = lax.broadcasted_iota(jnp.int32, (tb, tb), 1)
                lower = jnp.where(c_i < r_i, 1.0, 0.0).astype(BF16)
                carry = jnp.zeros((1, qb), F32)
                for blk in range(nk // tb):
                    rows = slice(blk * tb, (blk + 1) * tb)
                    xb = x_s[rows, :]
                    tie = jnp.where(xb == thr, 1.0, 0.0)
                    rank = jnp.dot(lower, tie.astype(BF16), preferred_element_type=F32) + carry
                    keep = (xb > thr) | ((xb == thr) & (rank < need))
                    keyb = blk * tb + lax.broadcasted_iota(jnp.int32, (tb, qb), 0)
                    qposb = j * qb + lax.broadcasted_iota(jnp.int32, (tb, qb), 1)
                    smallb = jnp.where(keyb <= qposb, 1.0, 0.0)
                    sel_s[rows, :] = jnp.where(small, smallb, jnp.where(keep, 1.0, 0.0))
                    carry = carry + jnp.sum(tie, axis=0, keepdims=True)

        sel = ss[...] > 0.5
        heads = range(SA_H)
        hsl = [slice(h * SA_DH, (h + 1) * SA_DH) for h in heads]
        st = [jnp.where(sel, lax.dot_general(kb_s[0:nk, hsl[h]], q_ref[:, hsl[h]], _NT,
                                             preferred_element_type=F32), NEG_INF) for h in heads]
        mx = [_reduce(st[h], 0, "max") for h in heads]
        p = [jnp.exp2(st[h] - mx[h]).astype(BF16) for h in heads]
        pv = [jnp.dot(vt_s[h, :, 0:nk], p[h], preferred_element_type=F32) for h in heads]
        ot = [pv[h][0:SA_DH, :] / pv[h][SA_DH:SA_DH + 1, :] for h in heads]
        for h in heads:
            o_ref[:, hsl[h]] = (ot[h].T * _silu(z_ref[:, hsl[h]])).astype(BF16)

    n_ext = n_keys // key_step
    for e in range(n_ext):
        nk = (e + 1) * key_step
        lo_j = e * key_step // qb
        hi_j = nk // qb

        @pl.when((j >= lo_j) & (j < hi_j))
        def _(nk=nk):
            attend(nk)


def _dsa_prompt(saq, saz, idxq, small_t, k, v, kidx, nb, seq):
    rows = saq.shape[0]
    qb = min(seq, 128)
    nq = seq // qb
    n_sel = min(TOPK_MAX, seq // 4)
    qblk = lambda b, j: (b * nq + j, 0)
    per_b = lambda b, j: (b, 0)
    wt_blk = SMALL_W // 8
    return pl.pallas_call(
        functools.partial(_dsa_kernel, n_keys=seq, qb=qb, n_sel=n_sel, n_bisect=20, key_step=min(DSA_KEY_STEP, seq)),
        out_shape=jax.ShapeDtypeStruct((rows, SA_W), BF16),
        grid=(nb, nq),
        in_specs=[
            pl.BlockSpec((qb, SA_W), qblk),
            pl.BlockSpec((qb, SA_W), qblk),
            pl.BlockSpec((qb, IDX_H * IDX_D), qblk),
            pl.BlockSpec((8, qb), lambda b, j: (wt_blk, b * nq + j)),
            pl.BlockSpec((seq * SA_H, SA_DH), per_b),
            pl.BlockSpec((seq * SA_H, SA_DH), per_b),
            pl.BlockSpec((seq, IDX_D), per_b),
        ],
        out_specs=pl.BlockSpec((qb, SA_W), qblk),
        scratch_shapes=[pltpu.VMEM((seq, SA_W), BF16), pltpu.VMEM((SA_H, SA_DH + ONES_ROWS, seq), BF16),
                        pltpu.VMEM((seq, IDX_D), BF16), pltpu.VMEM((seq, qb), F32),
                        pltpu.VMEM((seq, qb), F32)],
        compiler_params=_cparams(("arbitrary", "arbitrary")),
        name="dsa",
    )(saq, saz, idxq, small_t, k, v, kidx)


def _sel_kernel(pt_ref, idxq_ref, small_ref, kinew_ref, *rest, npg, n_past, n_sel, n_real, n_bisect):
    page_refs = rest[:npg]
    sel_ref, x_s = rest[npg:]
    g = pl.program_id(1)
    rws = SAMPLE_ROWS
    top = rws - SEL_ROWS
    pk = n_past + LANES
    gk = npg * PAGE_SIZE

    qs = jnp.concatenate([idxq_ref[:, h * IDX_D:(h + 1) * IDX_D] for h in range(IDX_H)], axis=0)
    small = small_ref[...]

    def scores(d):
        sc = jnp.zeros((SEL_ROWS, d.shape[1]), F32)
        for h in range(IDX_H):
            w = small[top:, SMALL_W + h:SMALL_W + h + 1] * IDX_SCALE
            sc = sc + jnp.maximum(d[h * rws + top:(h + 1) * rws, :], 0.0) * w
        return sc

    kp_t = jnp.concatenate([r[0] for r in page_refs], axis=1).astype(BF16)
    x_s[:, pl.ds(pl.multiple_of(g * gk, LANES), gk)] = scores(jnp.dot(qs, kp_t, preferred_element_type=F32))

    @pl.when(g == pl.num_programs(1) - 1)
    def _():
        knew = jnp.concatenate([kinew_ref[...], jnp.zeros((LANES - rws, IDX_D), F32)], axis=0).astype(BF16)
        row = top + lax.broadcasted_iota(jnp.int32, (SEL_ROWS, LANES), 0)
        col = lax.broadcasted_iota(jnp.int32, (SEL_ROWS, LANES), 1)
        ok = (col >= rws - n_real) & (col < rws) & (col <= row)
        d_new = lax.dot_general(qs, knew, _NT, preferred_element_type=F32)
        x_s[:, n_past:pk] = jnp.where(ok, scores(d_new), NEG_INF)

        kf = float(n_sel)
        thr = _kth_largest(x_s, n_sel, 1, n_bisect, quarter_steps=True)
        x = x_s[...]
        need = kf - _count(x > thr, 1)
        n_tie = _count(x == thr, 1)
        sel_ref[0, 0:top, :] = jnp.ones((top, pk), F32)
        sel_ref[0, top:rws, :] = jnp.where(x >= thr, 1.0, 0.0)
        real = lax.broadcasted_iota(jnp.int32, (SEL_ROWS, 1), 0) >= SEL_ROWS - n_real
        excess = jnp.max(jnp.where((n_tie > need) & real, 1.0, 0.0))

        @pl.when(excess > 0.5)
        def _():
            r_i = lax.broadcasted_iota(jnp.int32, (LANES, LANES), 0)
            c_i = lax.broadcasted_iota(jnp.int32, (LANES, LANES), 1)
            upper = jnp.where(r_i < c_i, 1.0, 0.0).astype(BF16)

            def blk(i, carry):
                cols = pl.ds(pl.multiple_of(i * LANES, LANES), LANES)
                xb = x_s[:, cols]
                tie = jnp.where(xb == thr, 1.0, 0.0)
                rank = jnp.dot(tie.astype(BF16), upper, preferred_element_type=F32) + carry
                keep = (xb > thr) | ((xb == thr) & (rank < need))
                sel_ref[0, top:rws, cols] = jnp.where(keep, 1.0, 0.0)
                return carry + jnp.sum(tie, axis=1, keepdims=True)

            lax.fori_loop(0, pk // LANES, blk, jnp.zeros((SEL_ROWS, 1), F32))


def _dsa_select(page_table, idxq, small, kidx_new, cache_kidx, n_sel, n_real):
    nreq, n_pages = page_table.shape
    n_past = n_pages * PAGE_SIZE
    pk = n_past + LANES
    npg = min(SEL_PAGES_PER_STEP, n_pages)
    req = lambda b, g, pt: (b, 0)

    def page_map(i):
        return lambda b, g, pt: (pt[b, g * npg + i], 0, 0)

    grid_spec = pltpu.PrefetchScalarGridSpec(
        num_scalar_prefetch=1,
        grid=(nreq, n_pages // npg),
        in_specs=[pl.BlockSpec((SAMPLE_ROWS, IDX_H * IDX_D), req),
                  pl.BlockSpec((SAMPLE_ROWS, LANES), req),
                  pl.BlockSpec((SAMPLE_ROWS, IDX_D), req)]
                 + [pl.BlockSpec((1, IDX_D, PAGE_SIZE), page_map(i)) for i in range(npg)],
        out_specs=pl.BlockSpec((1, SAMPLE_ROWS, pk), lambda b, g, pt: (b, 0, 0)),
        scratch_shapes=[pltpu.VMEM((SEL_ROWS, pk), F32)],
    )
    assert n_real <= SEL_ROWS
    return pl.pallas_call(
        functools.partial(_sel_kernel, npg=npg, n_past=n_past, n_sel=n_sel, n_real=n_real, n_bisect=20),
        out_shape=jax.ShapeDtypeStruct((nreq, SAMPLE_ROWS, pk), F32),
        grid_spec=grid_spec,
        compiler_params=_cparams(("arbitrary", "arbitrary")),
        name="dsa_sel",
    )(page_table, idxq, small, kidx_new, *([cache_kidx] * npg))


def _att_kernel(pt_ref, q_ref, z_ref, knew_ref, vnew_ref, sel_ref, seltail_ref, *rest, npg):
    k_refs = rest[:npg]
    v_refs = rest[npg:2 * npg]
    o_ref = rest[2 * npg]
    m_s, l_s, acc_s = rest[2 * npg + 1:]
    g = pl.program_id(1)
    rws = SAMPLE_ROWS
    floor = -1e30

    @pl.when(g == 0)
    def _():
        m_s[...] = jnp.full(m_s.shape, floor, F32)
        l_s[...] = jnp.zeros(l_s.shape, F32)
        acc_s[...] = jnp.zeros(acc_s.shape, F32)

    heads = range(SA_H)
    hsl = [slice(h * SA_DH, (h + 1) * SA_DH) for h in heads]

    def update(kbs, vbs, keep):
        m_old = [m_s[h][:, 0:1] for h in heads]
        l_old = [l_s[h][:, 0:1] for h in heads]
        acc_old = [acc_s[:, hsl[h]] for h in heads]
        s = [lax.dot_general(q_ref[:, hsl[h]], kbs[h], _NT, preferred_element_type=F32) for h in heads]
        m_new = [jnp.maximum(m_old[h], jnp.max(jnp.where(keep, s[h], floor), axis=1, keepdims=True)) for h in heads]
        p = [jnp.where(keep, jnp.exp2(s[h] - m_new[h]), 0.0) for h in heads]
        pv = [jnp.dot(p[h].astype(BF16), vbs[h], preferred_element_type=F32) for h in heads]
        alpha = [jnp.exp2(m_old[h] - m_new[h]) for h in heads]
        l_new = [alpha[h] * l_old[h] + jnp.sum(p[h], axis=1, keepdims=True) for h in heads]
        for h in heads:
            acc_s[:, hsl[h]] = alpha[h] * acc_old[h] + pv[h]
            l_s[h] = jnp.broadcast_to(l_new[h], (rws, LANES))
            m_s[h] = jnp.broadcast_to(m_new[h], (rws, LANES))

    def head_rows(refs, h):
        return jnp.concatenate([r[0, pl.ds(h, PAGE_SIZE, stride=SA_H), :] for r in refs], axis=0).astype(BF16)

    update([head_rows(k_refs, h) for h in heads], [head_rows(v_refs, h) for h in heads], sel_ref[0] > 0.5)

    @pl.when(g == pl.num_programs(1) - 1)
    def _():
        update([knew_ref[pl.ds(h, rws, stride=SA_H), :].astype(BF16) for h in heads],
               [vnew_ref[pl.ds(h, rws, stride=SA_H), :].astype(BF16) for h in heads],
               seltail_ref[0][:, :rws] > 0.5)
        for h in heads:
            o_ref[:, hsl[h]] = (acc_s[:, hsl[h]] / l_s[h][:, 0:1] * _silu(z_ref[:, hsl[h]])).astype(BF16)


def _dsa_attend(page_table, saq, saz, k_new, v_new, sel, cache_k, cache_v):
    nreq, n_pages = page_table.shape
    n_past = n_pages * PAGE_SIZE
    npg = min(ATT_PAGES_PER_STEP, n_pages)
    prow = PAGE_SIZE * SA_H
    req = lambda b, g, pt: (b, 0)

    def page_map(i):
        return lambda b, g, pt: (pt[b, g * npg + i], 0, 0)

    page_specs = [pl.BlockSpec((1, prow, SA_DH), page_map(i)) for i in range(npg)]
    grid_spec = pltpu.PrefetchScalarGridSpec(
        num_scalar_prefetch=1,
        grid=(nreq, n_pages // npg),
        in_specs=[pl.BlockSpec((SAMPLE_ROWS, SA_W), req), pl.BlockSpec((SAMPLE_ROWS, SA_W), req),
                  pl.BlockSpec((SAMPLE_ROWS * SA_H, SA_DH), req), pl.BlockSpec((SAMPLE_ROWS * SA_H, SA_DH), req),
                  pl.BlockSpec((1, SAMPLE_ROWS, npg * PAGE_SIZE), lambda b, g, pt: (b, 0, g)),
                  pl.BlockSpec((1, SAMPLE_ROWS, LANES), lambda b, g, pt: (b, 0, n_past // LANES))]
                 + page_specs + page_specs,
        out_specs=pl.BlockSpec((SAMPLE_ROWS, SA_W), req),
        scratch_shapes=[pltpu.VMEM((SA_H, SAMPLE_ROWS, LANES), F32), pltpu.VMEM((SA_H, SAMPLE_ROWS, LANES), F32),
                        pltpu.VMEM((SAMPLE_ROWS, SA_W), F32)],
    )
    return pl.pallas_call(
        functools.partial(_att_kernel, npg=npg),
        out_shape=jax.ShapeDtypeStruct((nreq * SAMPLE_ROWS, SA_W), BF16),
        grid_spec=grid_spec,
        compiler_params=_cparams(("arbitrary", "arbitrary")),
        name="dsa_att",
    )(page_table, saq, saz, k_new, v_new, sel, sel, *([cache_k] * npg), *([cache_v] * npg))


def _mem_attend(q_ref, z_ref, mk_ref, mv_ref):
    mk = mk_ref[...].astype(BF16)
    mv = mv_ref[...].astype(BF16)
    outs = []
    for h in range(MEM_H):
        hs = slice(h * MEM_DH, (h + 1) * MEM_DH)
        s = lax.dot_general(q_ref[:, hs], mk[:, hs], _NT, preferred_element_type=F32)
        p = jnp.exp(s - jnp.max(s, axis=1, keepdims=True))
        l = jnp.sum(p, axis=1, keepdims=True)
        o = jnp.dot(p.astype(BF16), mv[:, hs], preferred_element_type=F32) / l
        outs.append((o * _silu(z_ref[:, hs])).astype(BF16))
    return jnp.concatenate(outs, axis=1)


def _memattn_kernel(q_ref, z_ref, mk_ref, mv_ref, o_ref):
    o_ref[...] = _mem_attend(q_ref, z_ref, mk_ref, mv_ref)


def _memattn(memq, memz, mk, mv, nb, tq):
    rows = memq.shape[0]
    n_mem = mk.shape[0] // nb
    nq = rows // (nb * tq)
    qblk = lambda b, i: (b * nq + i, 0)
    per_b = lambda b, i: (b, 0)
    return pl.pallas_call(
        _memattn_kernel,
        out_shape=jax.ShapeDtypeStruct((rows, MEM_W), BF16),
        grid=(nb, nq),
        in_specs=[pl.BlockSpec((tq, MEM_W), qblk), pl.BlockSpec((tq, MEM_W), qblk),
                  pl.BlockSpec((n_mem, MEM_W), per_b), pl.BlockSpec((n_mem, MEM_W), per_b)],
        out_specs=pl.BlockSpec((tq, MEM_W), qblk),
        compiler_params=_cparams(("arbitrary", "arbitrary")),
        name="memattn",
    )(memq, memz, mk, mv)


def _mixout_kernel(x_ref, a_ref, b_ref, c_ref, wa_ref, wb_ref, wc_ref, g_ref, y_ref):
    acc = (jnp.dot(a_ref[...], wa_ref[...], preferred_element_type=F32)
           + jnp.dot(b_ref[...], wb_ref[...], preferred_element_type=F32)
           + jnp.dot(c_ref[...], wc_ref[...], preferred_element_type=F32))
    y = acc * lax.rsqrt(jnp.mean(acc * acc, axis=-1, keepdims=True) + EPS) * g_ref[...]
    y_ref[...] = x_ref[...] + y


def _mixout_mem_kernel(x_ref, a_ref, b_ref, q_ref, z_ref, mk_ref, mv_ref, wa_ref, wb_ref, wc_ref, g_ref, y_ref):
    c = _mem_attend(q_ref, z_ref, mk_ref, mv_ref)
    acc = (jnp.dot(a_ref[...], wa_ref[...], preferred_element_type=F32)
           + jnp.dot(b_ref[...], wb_ref[...], preferred_element_type=F32)
           + jnp.dot(c, wc_ref[...], preferred_element_type=F32))
    y = acc * lax.rsqrt(jnp.mean(acc * acc, axis=-1, keepdims=True) + EPS) * g_ref[...]
    y_ref[...] = x_ref[...] + y


def _mixout_mem(x2d, a, b, memq, memz, mk, mv, w_out, g_post, tm, seq):
    rows = x2d.shape[0]
    n_mem = mk.shape[0] // (rows // seq)
    wb16 = w_out.astype(BF16)
    row = lambda i: (i, 0)
    const = lambda i: (0, 0)
    per_req = lambda i: (i // (seq // tm), 0)
    return pl.pallas_call(
        _mixout_mem_kernel,
        out_shape=jax.ShapeDtypeStruct((rows, D_MODEL), F32),
        grid=(rows // tm,),
        in_specs=[pl.BlockSpec((tm, D_MODEL), row), pl.BlockSpec((tm, ML_W), row),
                  pl.BlockSpec((tm, SA_W), row), pl.BlockSpec((tm, MEM_W), row), pl.BlockSpec((tm, MEM_W), row),
                  pl.BlockSpec((n_mem, MEM_W), per_req), pl.BlockSpec((n_mem, MEM_W), per_req),
                  pl.BlockSpec((ML_W, D_MODEL), const), pl.BlockSpec((SA_W, D_MODEL), const),
                  pl.BlockSpec((MEM_W, D_MODEL), const), pl.BlockSpec((1, D_MODEL), const)],
        out_specs=pl.BlockSpec((tm, D_MODEL), row),
        compiler_params=_cparams(("arbitrary",)),
        name="mixout_mem",
    )(x2d, a, b, memq, memz, mk, mv, wb16[:ML_W], wb16[ML_W:ML_W + SA_W], wb16[ML_W + SA_W:],
      g_post.reshape(1, D_MODEL))


def _mixout(x2d, a, b, c, w_out, g_post, tm):
    rows = x2d.shape[0]
    wb16 = w_out.astype(BF16)
    row = lambda i: (i, 0)
    const = lambda i: (0, 0)
    return pl.pallas_call(
        _mixout_kernel,
        out_shape=jax.ShapeDtypeStruct((rows, D_MODEL), F32),
        grid=(rows // tm,),
        in_specs=[pl.BlockSpec((tm, D_MODEL), row), pl.BlockSpec((tm, ML_W), row),
                  pl.BlockSpec((tm, SA_W), row), pl.BlockSpec((tm, MEM_W), row),
                  pl.BlockSpec((ML_W, D_MODEL), const), pl.BlockSpec((SA_W, D_MODEL), const),
                  pl.BlockSpec((MEM_W, D_MODEL), const), pl.BlockSpec((1, D_MODEL), const)],
        out_specs=pl.BlockSpec((tm, D_MODEL), row),
        compiler_params=_cparams(("arbitrary",)),
        name="mixout",
    )(x2d, a, b, c, wb16[:ML_W], wb16[ML_W:ML_W + SA_W], wb16[ML_W + SA_W:], g_post.reshape(1, D_MODEL))


def _layer(x_p, x_s, st_c, st_n, st_m, c_k, c_v, c_kidx, c_mk, c_mv, page_table, mem_prompt,
           g_pre, w_in, b_gates, g_head, w_mem_k, w_mem_v, g_mem, w_out, g_post):
    nb, seq, _ = x_p.shape
    nreq, t_dec, _ = x_s.shape
    n_mem = mem_prompt.shape[1]
    n_past = page_table.shape[1] * PAGE_SIZE
    weights = _relayout_w_in(w_in)

    tm = min(512, seq)
    tabs = _rope_tables(jnp.arange(seq, dtype=jnp.int32))
    x_p2d = x_p.reshape(nb * seq, D_MODEL)
    (_, _, saq, k, v, saz, memq, memz, idxq, kidx, small, small_t) = _project(
        x_p2d, g_pre, weights, tabs, tm, with_ml=False)
    a_p, p_c, p_n, p_m = _ml_fused(x_p2d, g_pre, weights[0], weights[3], weights[4], b_gates, g_head,
                                   nb, seq, min(ML_FUSED_ROWS, seq), min(ML_CHUNK, seq))
    b_p = _dsa_prompt(saq, saz, idxq, small_t, k, v, kidx, nb, seq)
    mk, mv = _memkv(mem_prompt.reshape(nb * n_mem, D_MODEL), g_mem, w_mem_k, w_mem_v, n_mem)
    y_p = _mixout_mem(x_p.reshape(nb * seq, D_MODEL), a_p, b_p, memq, memz, mk, mv, w_out, g_post,
                      tm, seq).reshape(nb, seq, D_MODEL)

    rws = SAMPLE_ROWS
    n_padrow = rws - t_dec
    xs_pad = jnp.concatenate([jnp.zeros((nreq, n_padrow, D_MODEL), F32), x_s], axis=1).reshape(nreq * rws, D_MODEL)
    pos_s = jnp.tile(jnp.concatenate([jnp.zeros((n_padrow,), jnp.int32),
                                      n_past + jnp.arange(t_dec, dtype=jnp.int32)]), nreq)
    tabs_s = _rope_tables(pos_s)
    (qkv_s, oz_s, saq_s, k_s, v_s, saz_s, memq_s, memz_s, idxq_s, kidx_s, small_s, small_t_s) = _project(
        xs_pad, g_pre, weights, tabs_s, nreq * rws)
    a_s, s_c, s_n, s_m = _mlstm(qkv_s, oz_s, small_s, small_t_s, b_gates, g_head, st_c, st_n, st_m,
                                nreq, rws, n_padrow)
    n_sel = min(TOPK_MAX, (n_past + t_dec) // 4)
    sel = _dsa_select(page_table, idxq_s, small_s, kidx_s, jnp.swapaxes(c_kidx, 1, 2), n_sel, t_dec)
    b_s = _dsa_attend(page_table, saq_s, saz_s, k_s, v_s, sel,
                      c_k.reshape(c_k.shape[0], PAGE_SIZE * SA_H, SA_DH),
                      c_v.reshape(c_v.shape[0], PAGE_SIZE * SA_H, SA_DH))
    c_s = _memattn(memq_s, memz_s, c_mk.reshape(nreq * n_mem, MEM_W), c_mv.reshape(nreq * n_mem, MEM_W), nreq, rws)
    y_s = _mixout(xs_pad, a_s, b_s, c_s, w_out, g_post, nreq * rws)

    def real(a2d):
        return a2d.reshape(nreq, rws, -1)[:, n_padrow:]

    new = (p_c, p_n, p_m,
           k.reshape(nb, seq, SA_H, SA_DH), v.reshape(nb, seq, SA_H, SA_DH), kidx.reshape(nb, seq, IDX_D),
           mk.reshape(nb, n_mem, MEM_H, MEM_DH), mv.reshape(nb, n_mem, MEM_H, MEM_DH),
           s_c, s_n, s_m,
           real(k_s).reshape(nreq, t_dec, SA_H, SA_DH), real(v_s).reshape(nreq, t_dec, SA_H, SA_DH), real(kidx_s))
    return y_p, real(y_s), new


def kernel(x_prompt, x_sample, state_mlstm_C, state_mlstm_n, state_mlstm_m, cache_k, cache_v, cache_kidx,
           cache_mem_k, cache_mem_v, page_table, mem_prompt, g_pre, w_in, b_gates, g_head, w_mem_k, w_mem_v,
           g_mem, w_out, g_post):
    xp, xs = x_prompt, x_sample
    per_layer = []
    for l in range(w_in.shape[0]):
        xp, xs, new = _layer(xp, xs, state_mlstm_C[l], state_mlstm_n[l], state_mlstm_m[l],
                             cache_k[l], cache_v[l], cache_kidx[l], cache_mem_k[l], cache_mem_v[l],
                             page_table, mem_prompt, g_pre[l], w_in[l], b_gates[l], g_head[l],
                             w_mem_k[l], w_mem_v[l], g_mem[l], w_out[l], g_post[l])
        per_layer.append(new)
    stacked = [jnp.stack(a) for a in zip(*per_layer)]
    return (xp, xs, *stacked)
```

```python
import functools
import math

import jax
import jax.numpy as jnp
from jax import lax
from jax.experimental import pallas as pl
from jax.experimental.pallas import tpu as pltpu

F32 = jnp.float32
BF16 = jnp.bfloat16

D_MODEL = 2048
ML_H = 4
ML_W = D_MODEL // 2
ML_DH = ML_W // ML_H
SA_H = 4
SA_W = D_MODEL // 4
SA_DH = SA_W // SA_H
MEM_H = 4
MEM_W = D_MODEL // 4
MEM_DH = MEM_W // MEM_H
IDX_H = 8
IDX_D = 64
IDX_SCALE = (IDX_H * IDX_D) ** -0.5
TOPK_MAX = 256
ROPE_THETA = 10000.0
LOG2E = 1.4426950408889634
EPS = 1e-6
PAGE_SIZE = 128

LANES = 128
CB = 512
SMALL_IG = 64
SMALL_LF = 68
SMALL_W = 72
SAMPLE_ROWS = 16
SEL_ROWS = 8
SEL_PAGES_PER_STEP = 64
ATT_PAGES_PER_STEP = 32
ML_CHUNK = 256
IDX_KEY_CHUNK = 256
DSA_KEY_STEP = 256
ML_FUSED_ROWS = 256
TIE_BLOCK = 256
ONES_ROWS = 16
VMEM_LIMIT = 56 * 1024 * 1024
VMEM_LIMIT_FUSED = 60 * 1024 * 1024
NEG_INF = float("-inf")
POS_INF = float("inf")

_NT = (((1,), (1,)), ((), ()))
_TN = (((0,), (0,)), ((), ()))


def _cparams(sem):
    return pltpu.CompilerParams(dimension_semantics=sem, vmem_limit_bytes=VMEM_LIMIT)


def _sigmoid(x):
    return 1.0 / (1.0 + jnp.exp(-x))


def _silu(x):
    return x * _sigmoid(x)


def _log_sigmoid(x):
    return jnp.minimum(x, 0.0) - jnp.log1p(jnp.exp(-jnp.abs(x)))


def _split3(x):
    hi = x.astype(BF16)
    r = x - hi.astype(F32)
    mid = r.astype(BF16)
    lo = (r - mid.astype(F32)).astype(BF16)
    return hi, mid, lo


def _rope128(x, cos, sin_signed):
    return x * cos + pltpu.roll(x, 64, 1) * sin_signed


def _rope64(x, cos, sin_signed):
    lane = lax.broadcasted_iota(jnp.int32, x.shape, 1)
    first_half = (lane % 64) < 32
    partner = jnp.where(first_half, pltpu.roll(x, 96, 1), pltpu.roll(x, 32, 1))
    return x * cos + partner * sin_signed


def _normed(x_ref, g_ref):
    x = x_ref[...]
    return (x * lax.rsqrt(jnp.mean(x * x, axis=-1, keepdims=True) + EPS) * g_ref[...]).astype(BF16)


def _proj_ml_kernel(x_ref, g_ref, w_ref, qkv_ref, oz_ref, u_ref):
    u_ref[...] = _normed(x_ref, g_ref)
    n_qkv = 3 * ML_W // CB
    for cb in range(5 * ML_W // CB):
        acc = lax.dot_general(u_ref[...], w_ref[cb * CB:(cb + 1) * CB, :], _NT, preferred_element_type=F32)
        if ML_W <= cb * CB < 2 * ML_W:
            acc = acc * (ML_DH ** -0.5)
        if cb < n_qkv:
            qkv_ref[:, cb * CB:(cb + 1) * CB] = acc.astype(BF16)
        else:
            oz_ref[:, (cb - n_qkv) * CB:(cb - n_qkv + 1) * CB] = acc


def _proj_rest_kernel(x_ref, g_ref, wsa_ref, wmem_ref, ws_ref, wst_ref, c128_ref, s128_ref, c64_ref, s64_ref,
                      saq_ref, k_ref, v_ref, saz_ref, memq_ref, memz_ref, idxq_ref,
                      kidx_ref, small_ref, smallt_ref, u_ref):
    u_ref[...] = _normed(x_ref, g_ref)
    sm = jnp.dot(u_ref[...], ws_ref[...], preferred_element_type=F32)
    small_ref[...] = sm
    kidx_ref[...] = _rope64(sm, c64_ref[...], s64_ref[...])[:, :IDX_D]
    smallt_ref[...] = lax.dot_general(wst_ref[...], u_ref[...], _NT, preferred_element_type=F32)

    def block(cb, w_ref=wsa_ref):
        return lax.dot_general(u_ref[...], w_ref[cb * CB:(cb + 1) * CB, :], _NT, preferred_element_type=F32)

    def rope_heads(acc, fn, cos_ref, sin_ref):
        return jnp.concatenate(
            [fn(acc[:, h * LANES:(h + 1) * LANES], cos_ref[...], sin_ref[...]) for h in range(CB // LANES)], axis=1)

    saq_ref[...] = (rope_heads(block(0), _rope128, c128_ref, s128_ref) * (SA_DH ** -0.5 * LOG2E)).astype(BF16)
    tm = x_ref.shape[0]
    k_acc = block(1)
    v_acc = block(2)
    for h in range(SA_H):
        lanes = slice(h * SA_DH, (h + 1) * SA_DH)
        k_ref[pl.ds(h, tm, stride=SA_H), :] = _rope128(k_acc[:, lanes], c128_ref[...], s128_ref[...])
        v_ref[pl.ds(h, tm, stride=SA_H), :] = v_acc[:, lanes]
    saz_ref[...] = block(3)
    idxq_ref[...] = rope_heads(block(4), _rope64, c64_ref, s64_ref).astype(BF16)
    memq_ref[...] = (block(0, wmem_ref) * (MEM_DH ** -0.5)).astype(BF16)
    memz_ref[...] = block(1, wmem_ref)


def _project(x2d, g_pre, weights, tabs, tm, with_ml=True):
    w_ml, w_sa, w_mem, w_small, w_small_t = weights
    rows = x2d.shape[0]
    c128, s128, c64, s64 = tabs
    ntab = c128.shape[0] // tm
    n_ml = 5 * ML_W
    row_only = lambda i: (i, 0)
    tab_map = lambda i: (i % ntab, 0)
    const = lambda i: (0, 0)
    resident = pl.Buffered(1)
    g2d = g_pre.reshape(1, D_MODEL)

    qkv, oz = (None, None) if not with_ml else pl.pallas_call(
        _proj_ml_kernel,
        out_shape=(jax.ShapeDtypeStruct((rows, 3 * ML_W), BF16),
                   jax.ShapeDtypeStruct((rows, 2 * ML_W), F32)),
        grid=(rows // tm,),
        in_specs=[pl.BlockSpec((tm, D_MODEL), row_only),
                  pl.BlockSpec((1, D_MODEL), const),
                  pl.BlockSpec((n_ml, D_MODEL), const, pipeline_mode=resident)],
        out_specs=(pl.BlockSpec((tm, 3 * ML_W), row_only), pl.BlockSpec((tm, 2 * ML_W), row_only)),
        scratch_shapes=[pltpu.VMEM((tm, D_MODEL), BF16)],
        compiler_params=_cparams(("arbitrary",)),
        name="proj_ml",
    )(x2d, g2d, w_ml)

    out_shape = (
        jax.ShapeDtypeStruct((rows, SA_W), BF16),
        jax.ShapeDtypeStruct((rows * SA_H, SA_DH), F32),
        jax.ShapeDtypeStruct((rows * SA_H, SA_DH), F32),
        jax.ShapeDtypeStruct((rows, SA_W), F32),
        jax.ShapeDtypeStruct((rows, MEM_W), BF16),
        jax.ShapeDtypeStruct((rows, MEM_W), F32),
        jax.ShapeDtypeStruct((rows, IDX_H * IDX_D), BF16),
        jax.ShapeDtypeStruct((rows, IDX_D), F32),
        jax.ShapeDtypeStruct((rows, LANES), F32),
        jax.ShapeDtypeStruct((LANES, rows), F32),
    )
    out_specs = (
        pl.BlockSpec((tm, CB), row_only),
        pl.BlockSpec((tm * SA_H, SA_DH), row_only),
        pl.BlockSpec((tm * SA_H, SA_DH), row_only),
        pl.BlockSpec((tm, CB), row_only),
        pl.BlockSpec((tm, CB), row_only),
        pl.BlockSpec((tm, CB), row_only),
        pl.BlockSpec((tm, CB), row_only),
        pl.BlockSpec((tm, IDX_D), row_only),
        pl.BlockSpec((tm, LANES), row_only),
        pl.BlockSpec((LANES, tm), lambda i: (0, i)),
    )
    in_specs = [
        pl.BlockSpec((tm, D_MODEL), row_only),
        pl.BlockSpec((1, D_MODEL), const),
        pl.BlockSpec(w_sa.shape, const, pipeline_mode=resident),
        pl.BlockSpec(w_mem.shape, const, pipeline_mode=resident),
        pl.BlockSpec((D_MODEL, LANES), const, pipeline_mode=resident),
        pl.BlockSpec((LANES, D_MODEL), const, pipeline_mode=resident),
        pl.BlockSpec((tm, LANES), tab_map),
        pl.BlockSpec((tm, LANES), tab_map),
        pl.BlockSpec((tm, LANES), tab_map),
        pl.BlockSpec((tm, LANES), tab_map),
    ]
    rest = pl.pallas_call(
        _proj_rest_kernel,
        out_shape=out_shape,
        grid=(rows // tm,),
        in_specs=in_specs,
        out_specs=out_specs,
        scratch_shapes=[pltpu.VMEM((tm, D_MODEL), BF16)],
        compiler_params=_cparams(("arbitrary",)),
        name="proj_rest",
    )(x2d, g2d, w_sa, w_mem, w_small, w_small_t, c128, s128, c64, s64)
    return (qkv, oz, *rest)


def _rope_tables(pos):
    def tab(half):
        inv = ROPE_THETA ** (-jnp.arange(half, dtype=F32) / half)
        ang = pos.astype(F32)[:, None] * inv[None, :]
        return jnp.cos(ang), jnp.sin(ang)

    c, s = tab(SA_DH // 2)
    c128 = jnp.concatenate([c, c], axis=1)
    s128 = jnp.concatenate([-s, s], axis=1)
    c, s = tab(IDX_D // 2)
    c64 = jnp.concatenate([c, c, c, c], axis=1)
    s64 = jnp.concatenate([-s, s, -s, s], axis=1)
    return c128, s128, c64, s64


def _relayout_w_in(w_in):
    off = {}
    o = 0
    for name, w in (('ml_q', ML_W), ('ml_k', ML_W), ('ml_v', ML_W), ('ml_o', ML_W), ('ml_z', ML_W),
                    ('ml_i', ML_H), ('ml_f', ML_H), ('sa_q', SA_W), ('sa_k', SA_W), ('sa_v', SA_W),
                    ('sa_z', SA_W), ('idx_q', IDX_H * IDX_D), ('idx_k', IDX_D), ('idx_w', IDX_H),
                    ('mem_q', MEM_W), ('mem_z', MEM_W)):
        off[name] = (o, w)
        o += w

    w_t = w_in.T

    def col(name):
        a, w = off[name]
        return w_t[a:a + w]

    def span(first, last):
        return w_t[off[first][0]:off[last][0] + off[last][1]].astype(BF16)

    w_ml = w_t.astype(BF16)
    w_sa = span('sa_q', 'idx_q')
    w_mem = span('mem_q', 'mem_z')
    small_t = jnp.concatenate([
        col('idx_k'), col('ml_i'), col('ml_f'), col('idx_w'),
        jnp.zeros((LANES - IDX_D - 2 * ML_H - IDX_H, D_MODEL), F32)], axis=0).astype(BF16)
    return (w_ml, w_sa, w_mem, small_t.T, small_t)


def _memkv_kernel(m_ref, g_ref, wk_ref, wv_ref, k_ref, v_ref):
    x = m_ref[...]
    u = (x * lax.rsqrt(jnp.mean(x * x, axis=-1, keepdims=True) + EPS) * g_ref[...]).astype(BF16)
    n_mem = m_ref.shape[0]
    k_acc = jnp.dot(u, wk_ref[...], preferred_element_type=F32)
    v_acc = jnp.dot(u, wv_ref[...], preferred_element_type=F32)
    for h in range(MEM_H):
        lanes = slice(h * MEM_DH, (h + 1) * MEM_DH)
        k_ref[pl.ds(h, n_mem, stride=MEM_H), :] = k_acc[:, lanes]
        v_ref[pl.ds(h, n_mem, stride=MEM_H), :] = v_acc[:, lanes]


def _memkv(mem2d, g_mem, wk, wv, n_mem):
    rows = mem2d.shape[0]
    row = lambda i: (i, 0)
    const = lambda i: (0, 0)
    return pl.pallas_call(
        _memkv_kernel,
        out_shape=(jax.ShapeDtypeStruct((rows * MEM_H, MEM_DH), F32),
                   jax.ShapeDtypeStruct((rows * MEM_H, MEM_DH), F32)),
        grid=(rows // n_mem,),
        in_specs=[pl.BlockSpec((n_mem, D_MODEL), row), pl.BlockSpec((1, D_MODEL), const),
                  pl.BlockSpec((D_MODEL, MEM_W), const), pl.BlockSpec((D_MODEL, MEM_W), const)],
        out_specs=(pl.BlockSpec((n_mem * MEM_H, MEM_DH), row), pl.BlockSpec((n_mem * MEM_H, MEM_DH), row)),
        compiler_params=_cparams(("arbitrary",)),
        name="memkv",
    )(mem2d, g_mem.reshape(1, D_MODEL), wk.astype(BF16), wv.astype(BF16))


def _mlstm_chunk(q_of, k_of, v_of, gate_of, g_c, g_r, gh_ref, c_s, n_s, m_s, a_store, c, n_pad,
                 between_heads=None):
    ri = lax.broadcasted_iota(jnp.int32, (c, c), 0)
    cj = lax.broadcasted_iota(jnp.int32, (c, c), 1)
    causal = cj <= ri
    tri = jnp.where(causal, 1.0, 0.0).astype(BF16)
    tri_t = jnp.where(ri <= cj, 1.0, 0.0).astype(BF16)

    pad_c = lax.broadcasted_iota(jnp.int32, (c, LANES), 0) < n_pad
    ig_c = jnp.where(pad_c, NEG_INF, g_c)
    lf_c = jnp.where(pad_c, 0.0, _log_sigmoid(g_c))
    b_c = sum(jnp.dot(tri, p, preferred_element_type=F32) for p in _split3(lf_c))
    pad_r = lax.broadcasted_iota(jnp.int32, (SAMPLE_ROWS, c), 1) < n_pad
    ig_r = jnp.where(pad_r, NEG_INF, g_r)
    lf_r = jnp.where(pad_r, 0.0, _log_sigmoid(g_r))
    b_r = sum(jnp.dot(p, tri_t, preferred_element_type=F32) for p in _split3(lf_r))

    m_all = m_s[...]
    n_all = n_s[...]
    c_all = [c_s[h] for h in range(ML_H)]
    new_state = []
    for h in range(ML_H):
        hs = slice(h * ML_DH, (h + 1) * ML_DH)
        m_prev = m_all[h:h + 1, 0:1]
        b_t = b_c[:, SMALL_LF + h:SMALL_LF + h + 1]
        igc = ig_c[:, SMALL_IG + h:SMALL_IG + h + 1]
        b_s = b_r[ML_H + h:ML_H + h + 1, :]
        igr = ig_r[h:h + 1, :]
        a = jnp.where(causal, b_t - b_s + igr, NEG_INF)
        bm = b_t + m_prev
        m_t = jnp.maximum(bm, jnp.max(a, axis=1, keepdims=True))
        inter = jnp.exp(bm - m_t)
        dmat = jnp.exp(a - m_t)
        q, k, v = q_of(h), k_of(h), v_of(h)
        s = lax.dot_general(q, k, _NT, preferred_element_type=F32) * dmat
        if between_heads is not None:
            between_heads()
        c_h = c_all[h]
        n_h = n_all[h:h + 1, :]
        num = (jnp.dot(s.astype(BF16), v, preferred_element_type=F32)
               + inter * jnp.dot(q, c_h.astype(BF16), preferred_element_type=F32))
        qn = (jnp.sum(s, axis=1, keepdims=True)
              + inter * jnp.sum(q.astype(F32) * n_h, axis=1, keepdims=True))
        hh = num / jnp.maximum(jnp.abs(qn), jnp.exp(-m_t))
        hh = hh * lax.rsqrt(jnp.mean(hh * hh, axis=1, keepdims=True) + EPS)
        if between_heads is not None:
            between_heads()
        a_store(h, (hh * gh_ref[:, hs] * gate_of(h)).astype(BF16))

        m_new = m_t[c - 1:c, :]
        b_last = b_t[c - 1:c, :]
        w_end = jnp.exp(b_last - b_t + igc - m_new)
        decay = jnp.exp(b_last + m_prev - m_new)
        kw = k.astype(F32) * w_end
        new_state.append((decay * c_h + lax.dot_general(kw.astype(BF16), v, _TN, preferred_element_type=F32),
                          decay * n_h + jnp.sum(kw, axis=0, keepdims=True),
                          jnp.broadcast_to(m_new, (1, LANES))))
        if between_heads is not None:
            between_heads()

    for h, (c_new, n_new, m_new) in enumerate(new_state):
        c_s[h] = c_new
        n_s[h:h + 1, :] = n_new
        m_s[h:h + 1, :] = m_new


def _out_gate(o, z):
    return z / ((1.0 + jnp.exp(-o)) * (1.0 + jnp.exp(-z)))


def _head_cols(group, h):
    return slice(group * ML_W + h * ML_DH, group * ML_W + (h + 1) * ML_DH)


def _mlstm_kernel(qkv_ref, oz_ref, gc_ref, gt_ref, bcol_ref, brow_ref, gh_ref, c0_ref, n0_ref, m0_ref,
                  a_ref, cout_ref, nout_ref, mout_ref, c_s, n_s, m_s, *, c, n_pad):
    ci = pl.program_id(1)

    @pl.when(ci == 0)
    def _():
        c_s[...] = c0_ref[0]
        n_s[...] = n0_ref[0]
        m_s[...] = m0_ref[0]

    def a_store(h, value):
        a_ref[:, _head_cols(0, h)] = value

    _mlstm_chunk(lambda h: qkv_ref[:, _head_cols(0, h)], lambda h: qkv_ref[:, _head_cols(1, h)],
                 lambda h: qkv_ref[:, _head_cols(2, h)],
                 lambda h: _out_gate(oz_ref[:, _head_cols(0, h)], oz_ref[:, _head_cols(1, h)]),
                 gc_ref[...] + bcol_ref[...], gt_ref[0] + brow_ref[...], gh_ref, c_s, n_s, m_s, a_store, c, n_pad)

    @pl.when(ci == pl.num_programs(1) - 1)
    def _():
        cout_ref[0] = c_s[...]
        nout_ref[0] = n_s[...]
        mout_ref[0] = m_s[...]


def _ml_fused_kernel(x_ref, g_ref, w_ref, ws_ref, wst_ref, bcol_ref, brow_ref, gh_ref,
                     a_ref, cout_ref, nout_ref, mout_ref,
                     u_s, qkv_a, qkv_b, gate_a, gate_b, sm_a, sm_b, smt_a, smt_b, c_s, n_s, m_s, *, c, nrb, n_blocks):
    i = pl.program_id(0)
    tm = x_ref.shape[0]

    @pl.when(i == 0)
    def _():
        qkv_b[...] = jnp.zeros(qkv_b.shape, BF16)
        gate_b[...] = jnp.zeros(gate_b.shape, F32)
        sm_b[...] = jnp.zeros(sm_b.shape, F32)
        smt_b[...] = jnp.zeros(smt_b.shape, F32)

    @pl.when((i == 0) | (lax.rem(jnp.maximum(i - 1, 0), nrb) == 0))
    def _():
        c_s[...] = jnp.zeros(c_s.shape, F32)
        n_s[...] = jnp.zeros(n_s.shape, F32)
        m_s[...] = jnp.zeros(m_s.shape, F32)

    def projection_pieces(qkv_w, gate_w, sm_w, smt_w):
        pw = ML_DH

        def block(row0):
            return lax.dot_general(u_s[...], w_ref[row0:row0 + pw, :], _NT, preferred_element_type=F32)

        def narrow():
            sm_w[...] = jnp.dot(u_s[...], ws_ref[...], preferred_element_type=F32)
            smt_w[...] = lax.dot_general(wst_ref[SMALL_IG:SMALL_IG + SAMPLE_ROWS, :], u_s[...], _NT,
                                         preferred_element_type=F32)

        def qkv_block(cb):
            acc = block(cb * pw)
            if ML_W <= cb * pw < 2 * ML_W:
                acc = acc * (ML_DH ** -0.5)
            qkv_w[:, cb * pw:(cb + 1) * pw] = acc.astype(BF16)

        def gate_block(cb):
            gate_w[:, cb * pw:(cb + 1) * pw] = _out_gate(block(3 * ML_W + cb * pw), block(4 * ML_W + cb * pw))

        pieces = [narrow]
        pieces += [functools.partial(qkv_block, cb) for cb in range(3 * ML_W // pw)]
        pieces += [functools.partial(gate_block, cb) for cb in range(ML_W // pw)]
        return pieces

    def step(write, read):
        pieces = []
        if write is not None:
            u_s[...] = _normed(x_ref, g_ref)
            pieces = projection_pieces(*write)
        n_slots = (tm // c) * ML_H * 3
        n_pieces = len(pieces)
        slot = [0]

        def emit():
            slot[0] += 1
            while n_pieces - len(pieces) < (slot[0] * n_pieces) // n_slots:
                pieces.pop(0)()

        qkv_r, gate_r, sm_r, smt_r = read if read is not None else (None,) * 4
        for ck in range(tm // c if read is not None else 0):
            rows = slice(ck * c, (ck + 1) * c)

            def a_store(h, value, rows=rows):
                a_ref[rows, _head_cols(0, h)] = value

            _mlstm_chunk(lambda h, rows=rows: qkv_r[rows, _head_cols(0, h)],
                         lambda h, rows=rows: qkv_r[rows, _head_cols(1, h)],
                         lambda h, rows=rows: qkv_r[rows, _head_cols(2, h)],
                         lambda h, rows=rows: gate_r[rows, _head_cols(0, h)],
                         sm_r[rows, :] + bcol_ref[...], smt_r[:, rows] + brow_ref[...],
                         gh_ref, c_s, n_s, m_s, a_store, c, 0, between_heads=emit)
        while pieces:
            pieces.pop(0)()

    set_a = (qkv_a, gate_a, sm_a, smt_a)
    set_b = (qkv_b, gate_b, sm_b, smt_b)

    @pl.when(lax.rem(i, 2) == 0)
    def _():
        step(set_a, set_b)

    @pl.when(lax.rem(i, 2) == 1)
    def _():
        step(set_b, set_a)

    @pl.when((i >= 1) & (lax.rem(jnp.maximum(i - 1, 0), nrb) == nrb - 1))
    def _():
        cout_ref[0] = c_s[...]
        nout_ref[0] = n_s[...]
        mout_ref[0] = m_s[...]


def _ml_fused(x2d, g_pre, w_all, w_small, w_small_t, b_gates, g_head, nb, seq, tm, c):
    rows = x2d.shape[0]
    nrb = seq // tm
    n_blocks = rows // tm
    n_ml = 5 * ML_W
    bias_col = jnp.zeros((1, LANES), F32).at[0, SMALL_IG:SMALL_IG + 2 * ML_H].set(b_gates)
    bias_row = jnp.zeros((SAMPLE_ROWS, 1), F32).at[:2 * ML_H, 0].set(b_gates)
    const = lambda i: (0, 0)
    resident = pl.Buffered(1)
    lagged = lambda i: jnp.maximum(i - 1, 0)
    out_shape = (
        jax.ShapeDtypeStruct((rows, ML_W), BF16),
        jax.ShapeDtypeStruct((nb, ML_H, ML_DH, ML_DH), F32),
        jax.ShapeDtypeStruct((nb, ML_H, ML_DH), F32),
        jax.ShapeDtypeStruct((nb, 8, LANES), F32),
    )
    a, c_out, n_out, m_out = pl.pallas_call(
        functools.partial(_ml_fused_kernel, c=c, nrb=nrb, n_blocks=n_blocks),
        out_shape=out_shape,
        grid=(n_blocks + 1,),
        in_specs=[
            pl.BlockSpec((tm, D_MODEL), lambda i: (jnp.minimum(i, n_blocks - 1), 0)),
            pl.BlockSpec((1, D_MODEL), const),
            pl.BlockSpec((n_ml, D_MODEL), const, pipeline_mode=resident),
            pl.BlockSpec((D_MODEL, LANES), const, pipeline_mode=resident),
            pl.BlockSpec((LANES, D_MODEL), const, pipeline_mode=resident),
            pl.BlockSpec((1, LANES), const),
            pl.BlockSpec((SAMPLE_ROWS, 1), const),
            pl.BlockSpec((1, ML_W), const),
        ],
        out_specs=(
            pl.BlockSpec((tm, ML_W), lambda i: (lagged(i), 0)),
            pl.BlockSpec((1, ML_H, ML_DH, ML_DH), lambda i: (lagged(i) // nrb, 0, 0, 0)),
            pl.BlockSpec((1, ML_H, ML_DH), lambda i: (lagged(i) // nrb, 0, 0)),
            pl.BlockSpec((1, 8, LANES), lambda i: (lagged(i) // nrb, 0, 0)),
        ),
        scratch_shapes=[pltpu.VMEM((tm, D_MODEL), BF16),
                        pltpu.VMEM((tm, 3 * ML_W), BF16), pltpu.VMEM((tm, 3 * ML_W), BF16),
                        pltpu.VMEM((tm, ML_W), F32), pltpu.VMEM((tm, ML_W), F32),
                        pltpu.VMEM((tm, LANES), F32), pltpu.VMEM((tm, LANES), F32),
                        pltpu.VMEM((SAMPLE_ROWS, tm), F32), pltpu.VMEM((SAMPLE_ROWS, tm), F32),
                        pltpu.VMEM((ML_H, ML_DH, ML_DH), F32), pltpu.VMEM((ML_H, ML_DH), F32),
                        pltpu.VMEM((8, LANES), F32)],
        compiler_params=pltpu.CompilerParams(dimension_semantics=("arbitrary",), vmem_limit_bytes=VMEM_LIMIT_FUSED),
        name="ml_fused",
    )(x2d, g_pre.reshape(1, D_MODEL), w_all, w_small, w_small_t, bias_col, bias_row, g_head.reshape(1, ML_W))
    return a, c_out, n_out, m_out[:, :ML_H, 0]


def _mlstm(qkv, oz, small, small_t, b_gates, g_head, c0, n0, m0, nb, c, n_pad):
    rows = qkv.shape[0]
    nc = rows // (nb * c)
    bias_col = jnp.zeros((1, LANES), F32).at[0, SMALL_IG:SMALL_IG + 2 * ML_H].set(b_gates)
    bias_row = jnp.zeros((SAMPLE_ROWS, 1), F32).at[:2 * ML_H, 0].set(b_gates)
    m0b = jnp.zeros((nb, 8, LANES), F32).at[:, :ML_H, :].set(jnp.broadcast_to(m0[:, :, None], (nb, ML_H, LANES)))
    rowblk = lambda b, i: (b * nc + i, 0)
    const = lambda b, i: (0, 0)
    gates_t = small_t[SMALL_IG:SMALL_IG + SAMPLE_ROWS].reshape(SAMPLE_ROWS, rows // c, c).transpose(1, 0, 2)
    out_shape = (
        jax.ShapeDtypeStruct((rows, ML_W), BF16),
        jax.ShapeDtypeStruct((nb, ML_H, ML_DH, ML_DH), F32),
        jax.ShapeDtypeStruct((nb, ML_H, ML_DH), F32),
        jax.ShapeDtypeStruct((nb, 8, LANES), F32),
    )
    st4 = lambda b, i: (b, 0, 0, 0)
    st3 = lambda b, i: (b, 0, 0)
    a, c_out, n_out, m_out = pl.pallas_call(
        functools.partial(_mlstm_kernel, c=c, n_pad=n_pad),
        out_shape=out_shape,
        grid=(nb, nc),
        in_specs=[
            pl.BlockSpec((c, 3 * ML_W), rowblk),
            pl.BlockSpec((c, 2 * ML_W), rowblk),
            pl.BlockSpec((c, LANES), rowblk),
            pl.BlockSpec((1, SAMPLE_ROWS, c), lambda b, i: (b * nc + i, 0, 0)),
            pl.BlockSpec((1, LANES), const),
            pl.BlockSpec((SAMPLE_ROWS, 1), const),
            pl.BlockSpec((1, ML_W), const),
            pl.BlockSpec((1, ML_H, ML_DH, ML_DH), st4),
            pl.BlockSpec((1, ML_H, ML_DH), st3),
            pl.BlockSpec((1, 8, LANES), st3),
        ],
        out_specs=(
            pl.BlockSpec((c, ML_W), rowblk),
            pl.BlockSpec((1, ML_H, ML_DH, ML_DH), st4),
            pl.BlockSpec((1, ML_H, ML_DH), st3),
            pl.BlockSpec((1, 8, LANES), st3),
        ),
        scratch_shapes=[pltpu.VMEM((ML_H, ML_DH, ML_DH), F32), pltpu.VMEM((ML_H, ML_DH), F32),
                        pltpu.VMEM((8, LANES), F32)],
        compiler_params=_cparams(("arbitrary", "arbitrary")),
        name="mlstm",
    )(qkv, oz, small, gates_t, bias_col, bias_row, g_head.reshape(1, ML_W), c0, n0, m0b)
    return a, c_out, n_out, m_out[:, :ML_H, 0]


_REDUCERS = {"sum": (jnp.sum, jnp.add), "max": (jnp.max, jnp.maximum), "min": (jnp.min, jnp.minimum)}
REDUCE_CHAINS = 8


def _reduce(x, axis, op):
    fn, combine = _REDUCERS[op]
    unit = 8 if axis == 0 else LANES
    n = x.shape[axis]
    units = n // unit
    if n % unit or units < 2 * REDUCE_CHAINS:
        return fn(x, axis=axis, keepdims=True)
    base, rem = divmod(units, REDUCE_CHAINS)
    parts, start = [], 0
    for i in range(REDUCE_CHAINS):
        size = (base + (1 if i < rem else 0)) * unit
        piece = x[start:start + size] if axis == 0 else x[:, start:start + size]
        parts.append(fn(piece, axis=axis, keepdims=True))
        start += size
    while len(parts) > 1:
        parts = [combine(parts[i], parts[i + 1]) for i in range(0, len(parts), 2)]
    return parts[0]


def _count(pred, axis):
    return _reduce(jnp.where(pred, 1.0, 0.0), axis, "sum")


def _kth_largest(x_ref, k, axis, n_bisect, quarter_steps=False):
    kf = float(k)
    x = x_ref[...]
    hi = _reduce(x, axis, "max")
    lo = _reduce(jnp.where(x == NEG_INF, POS_INF, x), axis, "min")

    def bisect(_, carry):
        lo, hi = carry
        mid = 0.5 * (lo + hi)
        ge = _count(x_ref[...] >= mid, axis) >= kf
        return jnp.where(ge, mid, lo), jnp.where(ge, hi, mid)

    def quarter(_, carry):
        lo, hi = carry
        w = hi - lo
        m1, m2, m3 = lo + 0.25 * w, lo + 0.5 * w, lo + 0.75 * w
        xx = x_ref[...]
        g1, g2, g3 = (_count(xx >= m, axis) >= kf for m in (m1, m2, m3))
        lo = jnp.where(g3, m3, jnp.where(g2, m2, jnp.where(g1, m1, lo)))
        hi = jnp.where(g3, hi, jnp.where(g2, m3, jnp.where(g1, m2, m1)))
        return lo, hi

    if quarter_steps:
        lo, hi = lax.fori_loop(0, (n_bisect + 1) // 2, quarter, (lo, hi))
    else:
        lo, hi = lax.fori_loop(0, n_bisect, bisect, (lo, hi))

    def finished(cmin, xx):
        return jnp.where((_count(xx > cmin, axis) < kf) | (cmin == POS_INF), 1.0, 0.0)

    xx = x_ref[...]
    thr = _reduce(jnp.where(xx >= lo, xx, POS_INF), axis, "min")
    done = finished(thr, xx)

    def cond(st):
        return st[2] < 0.5

    def body(st):
        thr, done, _ = st
        xx = x_ref[...]
        cmin = _reduce(jnp.where(xx > thr, xx, POS_INF), axis, "min")
        thr = jnp.where(done < 0.5, cmin, thr)
        done = jnp.maximum(done, finished(thr, xx))
        return thr, done, jnp.min(done)

    thr, _, _ = lax.while_loop(cond, body, (thr, done, jnp.min(done)))
    return thr


def _dsa_kernel(q_ref, z_ref, idxq_ref, wt_ref, k_ref, v_ref, kidx_ref, o_ref,
                kb_s, vt_s, kib_s, x_s, sel_s, *, n_keys, qb, n_sel, n_bisect, key_step):
    j = pl.program_id(1)

    @pl.when(j == 0)
    def _():
        for h in range(SA_H):
            lanes = slice(h * SA_DH, (h + 1) * SA_DH)
            kb_s[:, lanes] = k_ref[pl.ds(h, n_keys, stride=SA_H), :].astype(BF16)
            vt_s[h, 0:SA_DH, :] = v_ref[pl.ds(h, n_keys, stride=SA_H), :].T.astype(BF16)
            vt_s[h, SA_DH:SA_DH + ONES_ROWS, :] = jnp.ones((ONES_ROWS, n_keys), BF16)
        kib_s[...] = kidx_ref[...].astype(BF16)

    def attend(nk):
        xs = x_s.at[0:nk]
        ss = sel_s.at[0:nk]
        key = lax.broadcasted_iota(jnp.int32, (nk, qb), 0)
        qpos = j * qb + lax.broadcasted_iota(jnp.int32, (nk, qb), 1)
        valid = key <= qpos
        qcat = jnp.concatenate([idxq_ref[:, h * IDX_D:(h + 1) * IDX_D] for h in range(IDX_H)], axis=0)
        w_rows = [wt_ref[h:h + 1, :] * IDX_SCALE for h in range(IDX_H)]
        kc = math.gcd(IDX_KEY_CHUNK, nk)
        for c0 in range(0, nk, kc):
            d = lax.dot_general(kib_s[c0:c0 + kc, :], qcat, _NT, preferred_element_type=F32)
            sc = jnp.zeros((kc, qb), F32)
            for h in range(IDX_H):
                sc = sc + jnp.maximum(d[:, h * qb:(h + 1) * qb], 0.0) * w_rows[h]
            ok = (c0 + lax.broadcasted_iota(jnp.int32, (kc, qb), 0)) <= (
                j * qb + lax.broadcasted_iota(jnp.int32, (kc, qb), 1))
            x_s[c0:c0 + kc, :] = jnp.where(ok, sc, NEG_INF)
            sel_s[c0:c0 + kc, :] = jnp.where(ok, 1.0, 0.0)

        @pl.when((j + 1) * qb > n_sel)
        def _():
            kf = float(n_sel)
            thr = _kth_largest(xs, n_sel, 0, n_bisect)
            x = xs[...]
            need = kf - _count(x > thr, 0)
            n_tie = _count(x == thr, 0)
            qrow = j * qb + lax.broadcasted_iota(jnp.int32, (1, qb), 1)
            small = (qrow + 1) <= n_sel
            ss[...] = jnp.where(small, jnp.where(valid, 1.0, 0.0), jnp.where(x >= thr, 1.0, 0.0))
            excess = jnp.max(jnp.where((n_tie > need) & jnp.logical_not(small), 1.0, 0.0))

            @pl.when(excess > 0.5)
            def _():
                tb = math.gcd(TIE_BLOCK, nk)
                r_i = lax.broadcasted_iota(jnp.int32, (tb, tb), 0)
                c_i = lax.broadcasted_iota(jnp.int32, (tb, tb), 1)
                lower = jnp.where(c_i < r_i, 1.0, 0.0).astype(BF16)
                carry = jnp.zeros((1, qb), F32)
                for blk in range(nk // tb):
                    rows = slice(blk * tb, (blk + 1) * tb)
                    xb = x_s[rows, :]
                    tie = jnp.where(xb == thr, 1.0, 0.0)
                    rank = jnp.dot(lower, tie.astype(BF16), preferred_element_type=F32) + carry
                    keep = (xb > thr) | ((xb == thr) & (rank < need))
                    keyb = blk * tb + lax.broadcasted_iota(jnp.int32, (tb, qb), 0)
                    qposb = j * qb + lax.broadcasted_iota(jnp.int32, (tb, qb), 1)
                    smallb = jnp.where(keyb <= qposb, 1.0, 0.0)
                    sel_s[rows, :] = jnp.where(small, smallb, jnp.where(keep, 1.0, 0.0))
                    carry = carry + jnp.sum(tie, axis=0, keepdims=True)

        sel = ss[...] > 0.5
        heads = range(SA_H)
        hsl = [slice(h * SA_DH, (h + 1) * SA_DH) for h in heads]
        st = [jnp.where(sel, lax.dot_general(kb_s[0:nk, hsl[h]], q_ref[:, hsl[h]], _NT,
                                             preferred_element_type=F32), NEG_INF) for h in heads]
        mx = [_reduce(st[h], 0, "max") for h in heads]
        p = [jnp.exp2(st[h] - mx[h]).astype(BF16) for h in heads]
        pv = [jnp.dot(vt_s[h, :, 0:nk], p[h], preferred_element_type=F32) for h in heads]
        ot = [pv[h][0:SA_DH, :] / pv[h][SA_DH:SA_DH + 1, :] for h in heads]
        for h in heads:
            o_ref[:, hsl[h]] = (ot[h].T * _silu(z_ref[:, hsl[h]])).astype(BF16)

    n_ext = n_keys // key_step
    for e in range(n_ext):
        nk = (e + 1) * key_step
        lo_j = e * key_step // qb
        hi_j = nk // qb

        @pl.when((j >= lo_j) & (j < hi_j))
        def _(nk=nk):
            attend(nk)


def _dsa_prompt(saq, saz, idxq, small_t, k, v, kidx, nb, seq):
    rows = saq.shape[0]
    qb = min(seq, 128)
    nq = seq // qb
    n_sel = min(TOPK_MAX, seq // 4)
    qblk = lambda b, j: (b * nq + j, 0)
    per_b = lambda b, j: (b, 0)
    wt_blk = SMALL_W // 8
    return pl.pallas_call(
        functools.partial(_dsa_kernel, n_keys=seq, qb=qb, n_sel=n_sel, n_bisect=20, key_step=min(DSA_KEY_STEP, seq)),
        out_shape=jax.ShapeDtypeStruct((rows, SA_W), BF16),
        grid=(nb, nq),
        in_specs=[
            pl.BlockSpec((qb, SA_W), qblk),
            pl.BlockSpec((qb, SA_W), qblk),
            pl.BlockSpec((qb, IDX_H * IDX_D), qblk),
            pl.BlockSpec((8, qb), lambda b, j: (wt_blk, b * nq + j)),
            pl.BlockSpec((seq * SA_H, SA_DH), per_b),
            pl.BlockSpec((seq * SA_H, SA_DH), per_b),
            pl.BlockSpec((seq, IDX_D), per_b),
        ],
        out_specs=pl.BlockSpec((qb, SA_W), qblk),
        scratch_shapes=[pltpu.VMEM((seq, SA_W), BF16), pltpu.VMEM((SA_H, SA_DH + ONES_ROWS, seq), BF16),
                        pltpu.VMEM((seq, IDX_D), BF16), pltpu.VMEM((seq, qb), F32),
                        pltpu.VMEM((seq, qb), F32)],
        compiler_params=_cparams(("arbitrary", "arbitrary")),
        name="dsa",
    )(saq, saz, idxq, small_t, k, v, kidx)


def _sel_kernel(pt_ref, idxq_ref, small_ref, kinew_ref, *rest, npg, n_past, n_sel, n_real, n_bisect):
    page_refs = rest[:npg]
    sel_ref, x_s = rest[npg:]
    g = pl.program_id(1)
    rws = SAMPLE_ROWS
    top = rws - SEL_ROWS
    pk = n_past + LANES
    gk = npg * PAGE_SIZE

    qs = jnp.concatenate([idxq_ref[:, h * IDX_D:(h + 1) * IDX_D] for h in range(IDX_H)], axis=0)
    small = small_ref[...]

    def scores(d):
        sc = jnp.zeros((SEL_ROWS, d.shape[1]), F32)
        for h in range(IDX_H):
            w = small[top:, SMALL_W + h:SMALL_W + h + 1] * IDX_SCALE
            sc = sc + jnp.maximum(d[h * rws + top:(h + 1) * rws, :], 0.0) * w
        return sc

    kp_t = jnp.concatenate([r[0] for r in page_refs], axis=1).astype(BF16)
    x_s[:, pl.ds(pl.multiple_of(g * gk, LANES), gk)] = scores(jnp.dot(qs, kp_t, preferred_element_type=F32))

    @pl.when(g == pl.num_programs(1) - 1)
    def _():
        knew = jnp.concatenate([kinew_ref[...], jnp.zeros((LANES - rws, IDX_D), F32)], axis=0).astype(BF16)
        row = top + lax.broadcasted_iota(jnp.int32, (SEL_ROWS, LANES), 0)
        col = lax.broadcasted_iota(jnp.int32, (SEL_ROWS, LANES), 1)
        ok = (col >= rws - n_real) & (col < rws) & (col <= row)
        d_new = lax.dot_general(qs, knew, _NT, preferred_element_type=F32)
        x_s[:, n_past:pk] = jnp.where(ok, scores(d_new), NEG_INF)

        kf = float(n_sel)
        thr = _kth_largest(x_s, n_sel, 1, n_bisect, quarter_steps=True)
        x = x_s[...]
        need = kf - _count(x > thr, 1)
        n_tie = _count(x == thr, 1)
        sel_ref[0, 0:top, :] = jnp.ones((top, pk), F32)
        sel_ref[0, top:rws, :] = jnp.where(x >= thr, 1.0, 0.0)
        real = lax.broadcasted_iota(jnp.int32, (SEL_ROWS, 1), 0) >= SEL_ROWS - n_real
        excess = jnp.max(jnp.where((n_tie > need) & real, 1.0, 0.0))

        @pl.when(excess > 0.5)
        def _():
            r_i = lax.broadcasted_iota(jnp.int32, (LANES, LANES), 0)
            c_i = lax.broadcasted_iota(jnp.int32, (LANES, LANES), 1)
            upper = jnp.where(r_i < c_i, 1.0, 0.0).astype(BF16)

            def blk(i, carry):
                cols = pl.ds(pl.multiple_of(i * LANES, LANES), LANES)
                xb = x_s[:, cols]
                tie = jnp.where(xb == thr, 1.0, 0.0)
                rank = jnp.dot(tie.astype(BF16), upper, preferred_element_type=F32) + carry
                keep = (xb > thr) | ((xb == thr) & (rank < need))
                sel_ref[0, top:rws, cols] = jnp.where(keep, 1.0, 0.0)
                return carry + jnp.sum(tie, axis=1, keepdims=True)

            lax.fori_loop(0, pk // LANES, blk, jnp.zeros((SEL_ROWS, 1), F32))


def _dsa_select(page_table, idxq, small, kidx_new, cache_kidx, n_sel, n_real):
    nreq, n_pages = page_table.shape
    n_past = n_pages * PAGE_SIZE
    pk = n_past + LANES
    npg = min(SEL_PAGES_PER_STEP, n_pages)
    req = lambda b, g, pt: (b, 0)

    def page_map(i):
        return lambda b, g, pt: (pt[b, g * npg + i], 0, 0)

    grid_spec = pltpu.PrefetchScalarGridSpec(
        num_scalar_prefetch=1,
        grid=(nreq, n_pages // npg),
        in_specs=[pl.BlockSpec((SAMPLE_ROWS, IDX_H * IDX_D), req),
                  pl.BlockSpec((SAMPLE_ROWS, LANES), req),
                  pl.BlockSpec((SAMPLE_ROWS, IDX_D), req)]
                 + [pl.BlockSpec((1, IDX_D, PAGE_SIZE), page_map(i)) for i in range(npg)],
        out_specs=pl.BlockSpec((1, SAMPLE_ROWS, pk), lambda b, g, pt: (b, 0, 0)),
        scratch_shapes=[pltpu.VMEM((SEL_ROWS, pk), F32)],
    )
    assert n_real <= SEL_ROWS
    return pl.pallas_call(
        functools.partial(_sel_kernel, npg=npg, n_past=n_past, n_sel=n_sel, n_real=n_real, n_bisect=20),
        out_shape=jax.ShapeDtypeStruct((nreq, SAMPLE_ROWS, pk), F32),
        grid_spec=grid_spec,
        compiler_params=_cparams(("arbitrary", "arbitrary")),
        name="dsa_sel",
    )(page_table, idxq, small, kidx_new, *([cache_kidx] * npg))


def _att_kernel(pt_ref, q_ref, z_ref, knew_ref, vnew_ref, sel_ref, seltail_ref, *rest, npg):
    k_refs = rest[:npg]
    v_refs = rest[npg:2 * npg]
    o_ref = rest[2 * npg]
    m_s, l_s, acc_s = rest[2 * npg + 1:]
    g = pl.program_id(1)
    rws = SAMPLE_ROWS
    floor = -1e30

    @pl.when(g == 0)
    def _():
        m_s[...] = jnp.full(m_s.shape, floor, F32)
        l_s[...] = jnp.zeros(l_s.shape, F32)
        acc_s[...] = jnp.zeros(acc_s.shape, F32)

    heads = range(SA_H)
    hsl = [slice(h * SA_DH, (h + 1) * SA_DH) for h in heads]

    def update(kbs, vbs, keep):
        m_old = [m_s[h][:, 0:1] for h in heads]
        l_old = [l_s[h][:, 0:1] for h in heads]
        acc_old = [acc_s[:, hsl[h]] for h in heads]
        s = [lax.dot_general(q_ref[:, hsl[h]], kbs[h], _NT, preferred_element_type=F32) for h in heads]
        m_new = [jnp.maximum(m_old[h], jnp.max(jnp.where(keep, s[h], floor), axis=1, keepdims=True)) for h in heads]
        p = [jnp.where(keep, jnp.exp2(s[h] - m_new[h]), 0.0) for h in heads]
        pv = [jnp.dot(p[h].astype(BF16), vbs[h], preferred_element_type=F32) for h in heads]
        alpha = [jnp.exp2(m_old[h] - m_new[h]) for h in heads]
        l_new = [alpha[h] * l_old[h] + jnp.sum(p[h], axis=1, keepdims=True) for h in heads]
        for h in heads:
            acc_s[:, hsl[h]] = alpha[h] * acc_old[h] + pv[h]
            l_s[h] = jnp.broadcast_to(l_new[h], (rws, LANES))
            m_s[h] = jnp.broadcast_to(m_new[h], (rws, LANES))

    def head_rows(refs, h):
        return jnp.concatenate([r[0, pl.ds(h, PAGE_SIZE, stride=SA_H), :] for r in refs], axis=0).astype(BF16)

    update([head_rows(k_refs, h) for h in heads], [head_rows(v_refs, h) for h in heads], sel_ref[0] > 0.5)

    @pl.when(g == pl.num_programs(1) - 1)
    def _():
        update([knew_ref[pl.ds(h, rws, stride=SA_H), :].astype(BF16) for h in heads],
               [vnew_ref[pl.ds(h, rws, stride=SA_H), :].astype(BF16) for h in heads],
               seltail_ref[0][:, :rws] > 0.5)
        for h in heads:
            o_ref[:, hsl[h]] = (acc_s[:, hsl[h]] / l_s[h][:, 0:1] * _silu(z_ref[:, hsl[h]])).astype(BF16)


def _dsa_attend(page_table, saq, saz, k_new, v_new, sel, cache_k, cache_v):
    nreq, n_pages = page_table.shape
    n_past = n_pages * PAGE_SIZE
    npg = min(ATT_PAGES_PER_STEP, n_pages)
    prow = PAGE_SIZE * SA_H
    req = lambda b, g, pt: (b, 0)

    def page_map(i):
        return lambda b, g, pt: (pt[b, g * npg + i], 0, 0)

    page_specs = [pl.BlockSpec((1, prow, SA_DH), page_map(i)) for i in range(npg)]
    grid_spec = pltpu.PrefetchScalarGridSpec(
        num_scalar_prefetch=1,
        grid=(nreq, n_pages // npg),
        in_specs=[pl.BlockSpec((SAMPLE_ROWS, SA_W), req), pl.BlockSpec((SAMPLE_ROWS, SA_W), req),
                  pl.BlockSpec((SAMPLE_ROWS * SA_H, SA_DH), req), pl.BlockSpec((SAMPLE_ROWS * SA_H, SA_DH), req),
                  pl.BlockSpec((1, SAMPLE_ROWS, npg * PAGE_SIZE), lambda b, g, pt: (b, 0, g)),
                  pl.BlockSpec((1, SAMPLE_ROWS, LANES), lambda b, g, pt: (b, 0, n_past // LANES))]
                 + page_specs + page_specs,
        out_specs=pl.BlockSpec((SAMPLE_ROWS, SA_W), req),
        scratch_shapes=[pltpu.VMEM((SA_H, SAMPLE_ROWS, LANES), F32), pltpu.VMEM((SA_H, SAMPLE_ROWS, LANES), F32),
                        pltpu.VMEM((SAMPLE_ROWS, SA_W), F32)],
    )
    return pl.pallas_call(
        functools.partial(_att_kernel, npg=npg),
        out_shape=jax.ShapeDtypeStruct((nreq * SAMPLE_ROWS, SA_W), BF16),
        grid_spec=grid_spec,
        compiler_params=_cparams(("arbitrary", "arbitrary")),
        name="dsa_att",
    )(page_table, saq, saz, k_new, v_new, sel, sel, *([cache_k] * npg), *([cache_v] * npg))


def _mem_attend(q_ref, z_ref, mk_ref, mv_ref):
    n_mem = mk_ref.shape[0] // MEM_H
    outs = []
    for h in range(MEM_H):
        hs = slice(h * MEM_DH, (h + 1) * MEM_DH)
        mk = mk_ref[pl.ds(h, n_mem, stride=MEM_H), :].astype(BF16)
        mv = mv_ref[pl.ds(h, n_mem, stride=MEM_H), :].astype(BF16)
        s = lax.dot_general(q_ref[:, hs], mk, _NT, preferred_element_type=F32)
        p = jnp.exp(s - jnp.max(s, axis=1, keepdims=True))
        l = jnp.sum(p, axis=1, keepdims=True)
        o = jnp.dot(p.astype(BF16), mv, preferred_element_type=F32) / l
        outs.append((o * _silu(z_ref[:, hs])).astype(BF16))
    return jnp.concatenate(outs, axis=1)


def _memattn_kernel(q_ref, z_ref, mk_ref, mv_ref, o_ref):
    o_ref[...] = _mem_attend(q_ref, z_ref, mk_ref, mv_ref)


def _memattn(memq, memz, mk, mv, nb, tq):
    rows = memq.shape[0]
    n_mem = mk.shape[0] // (nb * MEM_H)
    nq = rows // (nb * tq)
    qblk = lambda b, i: (b * nq + i, 0)
    per_b = lambda b, i: (b, 0)
    return pl.pallas_call(
        _memattn_kernel,
        out_shape=jax.ShapeDtypeStruct((rows, MEM_W), BF16),
        grid=(nb, nq),
        in_specs=[pl.BlockSpec((tq, MEM_W), qblk), pl.BlockSpec((tq, MEM_W), qblk),
                  pl.BlockSpec((n_mem * MEM_H, MEM_DH), per_b), pl.BlockSpec((n_mem * MEM_H, MEM_DH), per_b)],
        out_specs=pl.BlockSpec((tq, MEM_W), qblk),
        compiler_params=_cparams(("arbitrary", "arbitrary")),
        name="memattn",
    )(memq, memz, mk, mv)


def _mixout_kernel(x_ref, a_ref, b_ref, c_ref, wa_ref, wb_ref, wc_ref, g_ref, y_ref):
    acc = (jnp.dot(a_ref[...], wa_ref[...], preferred_element_type=F32)
           + jnp.dot(b_ref[...], wb_ref[...], preferred_element_type=F32)
           + jnp.dot(c_ref[...], wc_ref[...], preferred_element_type=F32))
    y = acc * lax.rsqrt(jnp.mean(acc * acc, axis=-1, keepdims=True) + EPS) * g_ref[...]
    y_ref[...] = x_ref[...] + y


def _mixout_mem_kernel(x_ref, a_ref, b_ref, q_ref, z_ref, mk_ref, mv_ref, wa_ref, wb_ref, wc_ref, g_ref, y_ref):
    c = _mem_attend(q_ref, z_ref, mk_ref, mv_ref)
    acc = (jnp.dot(a_ref[...], wa_ref[...], preferred_element_type=F32)
           + jnp.dot(b_ref[...], wb_ref[...], preferred_element_type=F32)
           + jnp.dot(c, wc_ref[...], preferred_element_type=F32))
    y = acc * lax.rsqrt(jnp.mean(acc * acc, axis=-1, keepdims=True) + EPS) * g_ref[...]
    y_ref[...] = x_ref[...] + y


def _mixout_mem(x2d, a, b, memq, memz, mk, mv, w_out, g_post, tm, seq):
    rows = x2d.shape[0]
    n_mem = mk.shape[0] // (rows // seq * MEM_H)
    wb16 = w_out.astype(BF16)
    row = lambda i: (i, 0)
    const = lambda i: (0, 0)
    per_req = lambda i: (i // (seq // tm), 0)
    return pl.pallas_call(
        _mixout_mem_kernel,
        out_shape=jax.ShapeDtypeStruct((rows, D_MODEL), F32),
        grid=(rows // tm,),
        in_specs=[pl.BlockSpec((tm, D_MODEL), row), pl.BlockSpec((tm, ML_W), row),
                  pl.BlockSpec((tm, SA_W), row), pl.BlockSpec((tm, MEM_W), row), pl.BlockSpec((tm, MEM_W), row),
                  pl.BlockSpec((n_mem * MEM_H, MEM_DH), per_req), pl.BlockSpec((n_mem * MEM_H, MEM_DH), per_req),
                  pl.BlockSpec((ML_W, D_MODEL), const), pl.BlockSpec((SA_W, D_MODEL), const),
                  pl.BlockSpec((MEM_W, D_MODEL), const), pl.BlockSpec((1, D_MODEL), const)],
        out_specs=pl.BlockSpec((tm, D_MODEL), row),
        compiler_params=_cparams(("arbitrary",)),
        name="mixout_mem",
    )(x2d, a, b, memq, memz, mk, mv, wb16[:ML_W], wb16[ML_W:ML_W + SA_W], wb16[ML_W + SA_W:],
      g_post.reshape(1, D_MODEL))


def _mixout(x2d, a, b, c, w_out, g_post, tm):
    rows = x2d.shape[0]
    wb16 = w_out.astype(BF16)
    row = lambda i: (i, 0)
    const = lambda i: (0, 0)
    return pl.pallas_call(
        _mixout_kernel,
        out_shape=jax.ShapeDtypeStruct((rows, D_MODEL), F32),
        grid=(rows // tm,),
        in_specs=[pl.BlockSpec((tm, D_MODEL), row), pl.BlockSpec((tm, ML_W), row),
                  pl.BlockSpec((tm, SA_W), row), pl.BlockSpec((tm, MEM_W), row),
                  pl.BlockSpec((ML_W, D_MODEL), const), pl.BlockSpec((SA_W, D_MODEL), const),
                  pl.BlockSpec((MEM_W, D_MODEL), const), pl.BlockSpec((1, D_MODEL), const)],
        out_specs=pl.BlockSpec((tm, D_MODEL), row),
        compiler_params=_cparams(("arbitrary",)),
        name="mixout",
    )(x2d, a, b, c, wb16[:ML_W], wb16[ML_W:ML_W + SA_W], wb16[ML_W + SA_W:], g_post.reshape(1, D_MODEL))


def _layer(x_p, x_s, st_c, st_n, st_m, c_k, c_v, c_kidx, c_mk, c_mv, page_table, mem_prompt,
           g_pre, w_in, b_gates, g_head, w_mem_k, w_mem_v, g_mem, w_out, g_post):
    nb, seq, _ = x_p.shape
    nreq, t_dec, _ = x_s.shape
    n_mem = mem_prompt.shape[1]
    n_past = page_table.shape[1] * PAGE_SIZE
    weights = _relayout_w_in(w_in)

    tm = min(512, seq)
    tabs = _rope_tables(jnp.arange(seq, dtype=jnp.int32))
    x_p2d = x_p.reshape(nb * seq, D_MODEL)
    (_, _, saq, k, v, saz, memq, memz, idxq, kidx, small, small_t) = _project(
        x_p2d, g_pre, weights, tabs, tm, with_ml=False)
    a_p, p_c, p_n, p_m = _ml_fused(x_p2d, g_pre, weights[0], weights[3], weights[4], b_gates, g_head,
                                   nb, seq, min(ML_FUSED_ROWS, seq), min(ML_CHUNK, seq))
    b_p = _dsa_prompt(saq, saz, idxq, small_t, k, v, kidx, nb, seq)
    mk, mv = _memkv(mem_prompt.reshape(nb * n_mem, D_MODEL), g_mem, w_mem_k, w_mem_v, n_mem)
    y_p = _mixout_mem(x_p.reshape(nb * seq, D_MODEL), a_p, b_p, memq, memz, mk, mv, w_out, g_post,
                      tm, seq).reshape(nb, seq, D_MODEL)

    rws = SAMPLE_ROWS
    n_padrow = rws - t_dec
    xs_pad = jnp.concatenate([jnp.zeros((nreq, n_padrow, D_MODEL), F32), x_s], axis=1).reshape(nreq * rws, D_MODEL)
    pos_s = jnp.tile(jnp.concatenate([jnp.zeros((n_padrow,), jnp.int32),
                                      n_past + jnp.arange(t_dec, dtype=jnp.int32)]), nreq)
    tabs_s = _rope_tables(pos_s)
    (qkv_s, oz_s, saq_s, k_s, v_s, saz_s, memq_s, memz_s, idxq_s, kidx_s, small_s, small_t_s) = _project(
        xs_pad, g_pre, weights, tabs_s, nreq * rws)
    a_s, s_c, s_n, s_m = _mlstm(qkv_s, oz_s, small_s, small_t_s, b_gates, g_head, st_c, st_n, st_m,
                                nreq, rws, n_padrow)
    n_sel = min(TOPK_MAX, (n_past + t_dec) // 4)
    sel = _dsa_select(page_table, idxq_s, small_s, kidx_s, jnp.swapaxes(c_kidx, 1, 2), n_sel, t_dec)
    b_s = _dsa_attend(page_table, saq_s, saz_s, k_s, v_s, sel,
                      c_k.reshape(c_k.shape[0], PAGE_SIZE * SA_H, SA_DH),
                      c_v.reshape(c_v.shape[0], PAGE_SIZE * SA_H, SA_DH))
    c_s = _memattn(memq_s, memz_s, c_mk.reshape(nreq * n_mem * MEM_H, MEM_DH),
                   c_mv.reshape(nreq * n_mem * MEM_H, MEM_DH), nreq, rws)
    y_s = _mixout(xs_pad, a_s, b_s, c_s, w_out, g_post, nreq * rws)

    def real(a2d):
        return a2d.reshape(nreq, rws, -1)[:, n_padrow:]

    new = (p_c, p_n, p_m,
           k.reshape(nb, seq, SA_H, SA_DH), v.reshape(nb, seq, SA_H, SA_DH), kidx.reshape(nb, seq, IDX_D),
           mk.reshape(nb, n_mem, MEM_H, MEM_DH), mv.reshape(nb, n_mem, MEM_H, MEM_DH),
           s_c, s_n, s_m,
           real(k_s).reshape(nreq, t_dec, SA_H, SA_DH), real(v_s).reshape(nreq, t_dec, SA_H, SA_DH), real(kidx_s))
    return y_p, real(y_s), new


def kernel(x_prompt, x_sample, state_mlstm_C, state_mlstm_n, state_mlstm_m, cache_k, cache_v, cache_kidx,
           cache_mem_k, cache_mem_v, page_table, mem_prompt, g_pre, w_in, b_gates, g_head, w_mem_k, w_mem_v,
           g_mem, w_out, g_post):
    xp, xs = x_prompt, x_sample
    per_layer = []
    for l in range(w_in.shape[0]):
        xp, xs, new = _layer(xp, xs, state_mlstm_C[l], state_mlstm_n[l], state_mlstm_m[l],
                             cache_k[l], cache_v[l], cache_kidx[l], cache_mem_k[l], cache_mem_v[l],
                             page_table, mem_prompt, g_pre[l], w_in[l], b_gates[l], g_head[l],
                             w_mem_k[l], w_mem_v[l], g_mem[l], w_out[l], g_post[l])
        per_layer.append(new)
    stacked = [jnp.stack(a) for a in zip(*per_layer)]
    return (xp, xs, *stacked)
```
